```python
import jax, jax.numpy as jnp
from jax import lax
import numpy as np

D_MODEL = 1024
BATCH = 4
SEQ = 4096
DEPTH = 1

HG_WIDTH = D_MODEL // 2
NSA_WIDTH = D_MODEL - HG_WIDTH
HG_EXPAND = 128
HG_HEADS = HG_WIDTH // HG_EXPAND
HG_DK = HG_EXPAND
HG_DV = HG_WIDTH // HG_HEADS
HG_CHUNK = 64
NSA_HEAD_DIM = 64
NSA_HEADS = NSA_WIDTH // NSA_HEAD_DIM
NSA_KV_GROUPS = 2
NSA_GROUP_SIZE = NSA_HEADS // NSA_KV_GROUPS
CMP_BLOCK = 32
CMP_STRIDE = 16
CMP_HIDDEN = 256
SLC_BLOCK = 64
SLC_TOPN = 16
WINDOW = 512
Q_BLOCK = 128
NEG = -1e30
FORCE = 1e4
D_FF = -(-8 * D_MODEL // (3 * 256)) * 256
ALPHA = (2.0 * DEPTH) ** 0.25
BETA = (8.0 * DEPTH) ** -0.25
LN_EPS = 1e-5
RMS_EPS = 1e-6
KV_W = NSA_KV_GROUPS * NSA_HEAD_DIM
IN_SIZES = [HG_WIDTH] * 4 + [NSA_WIDTH] + [KV_W] * 6 + [3 * NSA_HEADS]
IN_COLS = sum(IN_SIZES)

kernel_name = "hybrid_hgrn2_nsa_deepnorm_adaln_block"


def layer_norm(x, g, b):
    xf = x.astype(jnp.float32)
    mu = jnp.mean(xf, -1, keepdims=True)
    var = jnp.mean(jnp.square(xf - mu), -1, keepdims=True)
    return ((xf - mu) * lax.rsqrt(var + LN_EPS) * g + b).astype(x.dtype)


def alibi_slopes():
    s = 2.0 ** (-8.0 * (np.arange(NSA_HEADS) + 1) / NSA_HEADS)
    return jnp.asarray(s, jnp.float32).reshape(NSA_KV_GROUPS, NSA_GROUP_SIZE)


def hgrn2_mixer(q, f_logit, inp, g, lb, norm_g):
    B, T, _ = q.shape
    nc = T // HG_CHUNK
    f = lb + (1.0 - lb) * jax.nn.sigmoid(f_logit)
    log_f = jnp.log(f)
    k = 1.0 - f
    qf = jax.nn.silu(q)

    def to_chunks(a, dh):
        return a.reshape(B, nc, HG_CHUNK, HG_HEADS, dh).transpose(1, 0, 3, 2, 4)

    qc, kc, gc = to_chunks(qf, HG_DK), to_chunks(k, HG_DK), to_chunks(log_f, HG_DK)
    vc = to_chunks(inp, HG_DV)
    causal = jnp.tril(jnp.ones((HG_CHUNK, HG_CHUNK), bool))[:, :, None]

    def step(S, xs):
        qq, kk, vv, ll = xs
        b = jnp.cumsum(ll, axis=2)
        diff = b[:, :, :, None, :] - b[:, :, None, :, :]
        decay = jnp.exp(jnp.where(causal, diff, -jnp.inf))
        attn = jnp.einsum('bhtd,bhsd,bhtsd->bhts', qq, kk, decay)
        o = jnp.einsum('bhts,bhsv->bhtv', attn, vv) + jnp.einsum('bhtd,bhdv->bhtv', qq * jnp.exp(b), S)
        b_last = b[:, :, -1:, :]
        S = jnp.exp(b_last[:, :, 0, :])[..., None] * S + jnp.einsum('bhsd,bhsv->bhdv', kk * jnp.exp(b_last - b), vv)
        return S, o

    S0 = jnp.zeros((B, HG_HEADS, HG_DK, HG_DV), jnp.float32)
    _, o = lax.scan(step, S0, (qc, kc, vc, gc))
    o = o.transpose(1, 0, 3, 2, 4).reshape(B, T, HG_HEADS, HG_DV)
    o = o * lax.rsqrt(jnp.mean(jnp.square(o), -1, keepdims=True) + RMS_EPS)
    return o.reshape(B, T, HG_WIDTH) * norm_g * jax.nn.sigmoid(g)


def nsa_mixer(q, k_c, v_c, k_s, v_s, k_w, v_w, gate_logits,
              pos_k, w1_k, w2_k, pos_v, w1_v, w2_v):
    B, T, _ = q.shape
    G, Hg, dh = NSA_KV_GROUPS, NSA_GROUP_SIZE, NSA_HEAD_DIM
    scale = dh ** -0.5
    slopes = alibi_slopes()
    kv = lambda a: a.reshape(B, T, G, dh)
    k_c, v_c, k_s, v_s, k_w, v_w = map(kv, (k_c, v_c, k_s, v_s, k_w, v_w))

    n_cmp = (T - CMP_BLOCK) // CMP_STRIDE + 1
    tok_idx = jnp.arange(n_cmp)[:, None] * CMP_STRIDE + jnp.arange(CMP_BLOCK)[None, :]

    def compress(a, pos, w1, w2):
        blk = a[:, tok_idx] + pos[None, None, :, None, :]
        blk = blk.transpose(0, 1, 3, 2, 4).reshape(B, n_cmp, G, CMP_BLOCK * dh)
        return jax.nn.gelu(blk @ w1) @ w2

    k_cmp = compress(k_c, pos_k, w1_k, w2_k)
    v_cmp = compress(v_c, pos_v, w1_v, w2_v)
    cmp_end = jnp.arange(n_cmp) * CMP_STRIDE + CMP_BLOCK - 1

    n_slc = T // SLC_BLOCK
    top_n = min(SLC_TOPN, n_slc)
    cs = jnp.arange(n_cmp) * CMP_STRIDE
    ss = jnp.arange(n_slc) * SLC_BLOCK
    overlap = jnp.clip(jnp.minimum(cs[:, None] + CMP_BLOCK, ss[None] + SLC_BLOCK)
                       - jnp.maximum(cs[:, None], ss[None]), 0).astype(jnp.float32) / CMP_BLOCK
    k_slc = k_s.reshape(B, n_slc, SLC_BLOCK, G, dh).transpose(0, 3, 1, 2, 4)
    v_slc = v_s.reshape(B, n_slc, SLC_BLOCK, G, dh).transpose(0, 3, 1, 2, 4)
    gather = jax.vmap(jax.vmap(lambda kb, ix: kb[ix]))

    pad = ((0, 0), (WINDOW, 0), (0, 0), (0, 0))
    kw_pad, vw_pad = jnp.pad(k_w, pad), jnp.pad(v_w, pad)

    nqb = T // Q_BLOCK
    qb = q.reshape(B, nqb, Q_BLOCK, G, Hg, dh).transpose(1, 0, 3, 4, 2, 5)
    gb = jax.nn.sigmoid(gate_logits).reshape(B, nqb, Q_BLOCK, G, Hg, 3).transpose(1, 0, 3, 4, 2, 5)
    j_idx = jnp.arange(n_slc)

    def block_fn(args):
        qi, gi, blk = args
        t = blk * Q_BLOCK + jnp.arange(Q_BLOCK)
        dist_c = (t[:, None] - cmp_end[None, :]).astype(jnp.float32)
        valid_c = dist_c >= 0
        s_c = jnp.einsum('bghtd,bngd->bghtn', qi, k_cmp) * scale - slopes[None, :, :, None, None] * dist_c
        p_c = jax.nn.softmax(jnp.where(valid_c, s_c, NEG), axis=-1) * valid_c
        o_c = jnp.einsum('bghtn,bngd->bghtd', p_c, v_cmp)
        imp = jnp.einsum('bghtn,nj->bgtj', p_c, overlap)
        cur = t[:, None] // SLC_BLOCK
        forced = (j_idx[None] == 0) | (j_idx[None] == cur) | (j_idx[None] == cur - 1)
        imp = jnp.where(forced, FORCE, imp)
        imp = jnp.where(j_idx[None] * SLC_BLOCK <= t[:, None], imp, NEG)
        _, sel = lax.top_k(imp, top_n)
        k_sel = gather(k_slc, sel)
        v_sel = gather(v_slc, sel)
        pos_sel = sel[..., None] * SLC_BLOCK + jnp.arange(SLC_BLOCK)
        dist_s = (t[None, None, :, None, None] - pos_sel)[:, :, None].astype(jnp.float32)
        s_s = jnp.einsum('bghtd,bgtnsd->bghtns', qi, k_sel) * scale - slopes[None, :, :, None, None, None] * dist_s
        s_s = jnp.where(dist_s >= 0, s_s, NEG)
        sh = s_s.shape
        p_s = jax.nn.softmax(s_s.reshape(sh[:4] + (-1,)), axis=-1).reshape(sh)
        o_s = jnp.einsum('bghtns,bgtnsd->bghtd', p_s, v_sel)
        start = blk * Q_BLOCK
        k_win = lax.dynamic_slice_in_dim(kw_pad, start, WINDOW + Q_BLOCK, axis=1)
        v_win = lax.dynamic_slice_in_dim(vw_pad, start, WINDOW + Q_BLOCK, axis=1)
        s_pos = start - WINDOW + jnp.arange(WINDOW + Q_BLOCK)
        dist_w = t[:, None] - s_pos[None]
        valid_w = (dist_w >= 0) & (dist_w < WINDOW) & (s_pos[None] >= 0)
        s_w = jnp.einsum('bghtd,bsgd->bghts', qi, k_win) * scale - slopes[None, :, :, None, None] * dist_w.astype(jnp.float32)
        p_w = jax.nn.softmax(jnp.where(valid_w, s_w, NEG), axis=-1)
        o_w = jnp.einsum('bghts,bsgd->bghtd', p_w, v_win)
        return gi[..., 0:1] * o_c + gi[..., 1:2] * o_s + gi[..., 2:3] * o_w

    out = lax.map(block_fn, (qb, gb, jnp.arange(nqb)))
    return out.transpose(1, 0, 4, 2, 3, 5).reshape(B, T, NSA_WIDTH)


def setup_inputs(seed: int = 0) -> dict:
    key = jax.random.key(seed)
    ks = jax.random.split(key, 24)
    n = lambda k, shape, s: jax.random.normal(k, shape, jnp.float32) * s
    D = D_MODEL
    return {
        "x": n(ks[0], (BATCH, SEQ, D), 1.0),
        "c": n(ks[1], (BATCH, D), 1.0),
        "w_ada": n(ks[2], (DEPTH, D, 6 * D), D ** -0.5),
        "b_ada": n(ks[3], (DEPTH, 6 * D), 0.02),
        "w_in": n(ks[4], (DEPTH, D, IN_COLS), D ** -0.5),
        "hg_lb_logits": n(ks[5], (DEPTH + 1, HG_WIDTH), 1.0),
        "hg_norm_g": 1.0 + n(ks[6], (DEPTH, HG_WIDTH), 0.02),
        "cmp_pos_k": n(ks[7], (DEPTH, CMP_BLOCK, NSA_HEAD_DIM), 0.1),
        "cmp_w1_k": n(ks[8], (DEPTH, CMP_BLOCK * NSA_HEAD_DIM, CMP_HIDDEN), (CMP_BLOCK * NSA_HEAD_DIM) ** -0.5),
        "cmp_w2_k": n(ks[9], (DEPTH, CMP_HIDDEN, NSA_HEAD_DIM), CMP_HIDDEN ** -0.5),
        "cmp_pos_v": n(ks[10], (DEPTH, CMP_BLOCK, NSA_HEAD_DIM), 0.1),
        "cmp_w1_v": n(ks[11], (DEPTH, CMP_BLOCK * NSA_HEAD_DIM, CMP_HIDDEN), (CMP_BLOCK * NSA_HEAD_DIM) ** -0.5),
        "cmp_w2_v": n(ks[12], (DEPTH, CMP_HIDDEN, NSA_HEAD_DIM), CMP_HIDDEN ** -0.5),
        "w_out": n(ks[13], (DEPTH, D, D), D ** -0.5 * BETA),
        "ln1_g": 1.0 + n(ks[14], (DEPTH, D), 0.02),
        "ln1_b": n(ks[15], (DEPTH, D), 0.02),
        "ffn_w1": n(ks[16], (DEPTH, D, D_FF), D ** -0.5),
        "ffn_w3": n(ks[17], (DEPTH, D, D_FF), D ** -0.5),
        "ffn_w2": n(ks[18], (DEPTH, D_FF, D), D_FF ** -0.5 * BETA),
        "ln2_g": 1.0 + n(ks[19], (DEPTH, D), 0.02),
        "ln2_b": n(ks[20], (DEPTH, D), 0.02),
    }


def reference(x, c, w_ada, b_ada, w_in, hg_lb_logits, hg_norm_g,
              cmp_pos_k, cmp_w1_k, cmp_w2_k, cmp_pos_v, cmp_w1_v, cmp_w2_v,
              w_out, ln1_g, ln1_b, ffn_w1, ffn_w3, ffn_w2, ln2_g, ln2_b):
    f32 = jnp.float32
    lower_bounds = jnp.cumsum(jax.nn.softmax(hg_lb_logits.astype(f32), axis=0), axis=0)
    offs = np.cumsum(IN_SIZES)[:-1].tolist()
    for l in range(DEPTH):
        ada = jax.nn.silu(c) @ w_ada[l] + b_ada[l]
        sh1, sc1, g1, sh2, sc2, g2 = jnp.split(ada[:, None, :], 6, axis=-1)
        h = x * (1.0 + sc1) + sh1
        proj = (h @ w_in[l]).astype(f32)
        (hq, hf, hi, hg, nq, nkc, nvc, nks, nvs, nkw, nvw, ngate) = jnp.split(proj, offs, axis=-1)
        y_hg = hgrn2_mixer(hq, hf, hi, hg, lower_bounds[l], hg_norm_g[l].astype(f32))
        y_nsa = nsa_mixer(nq, nkc, nvc, nks, nvs, nkw, nvw, ngate,
                          cmp_pos_k[l].astype(f32), cmp_w1_k[l].astype(f32), cmp_w2_k[l].astype(f32),
                          cmp_pos_v[l].astype(f32), cmp_w1_v[l].astype(f32), cmp_w2_v[l].astype(f32))
        mix = jnp.concatenate([y_hg, y_nsa], axis=-1).astype(x.dtype) @ w_out[l]
        x = layer_norm(ALPHA * x + g1 * mix, ln1_g[l], ln1_b[l])
        h = x * (1.0 + sc2) + sh2
        ffn = (jax.nn.silu(h @ ffn_w1[l]) * (h @ ffn_w3[l])) @ ffn_w2[l]
        x = layer_norm(ALPHA * x + g2 * ffn, ln2_g[l], ln2_b[l])
    return x
```

```python
import functools

import numpy as np
import jax
import jax.numpy as jnp
from jax import lax
from jax.experimental import pallas as pl
from jax.experimental.pallas import tpu as pltpu

F32 = jnp.float32
BF16 = jnp.bfloat16

HG_HEADS = 4
HG_D = 128
NSA_HEADS = 8
NSA_GROUPS = 2
NSA_HG = NSA_HEADS // NSA_GROUPS
NSA_DH = 64
CMP_BLOCK = 32
CMP_STRIDE = 16
CMP_HIDDEN = 256
SLC_BLOCK = 64
SLC_TOPN = 16
WINDOW = 512
Q_BLOCK = 128
NEG = -1e30
FORCE = 1e4
LN_EPS = 1e-5
RMS_EPS = 1e-6

LANES = 128
HG_BLK = 16
VMEM_LIMIT = 48 * 1024 * 1024


def _sigmoid(x):
    return 1.0 / (1.0 + jnp.exp(-x))


def _dot(a, b):
    return jnp.dot(a, b, preferred_element_type=F32)


def _dot_nt(a, b):
    return lax.dot_general(a, b, (((1,), (1,)), ((), ())), preferred_element_type=F32)


def _split3(x):
    hi = x.astype(BF16)
    r = x - hi.astype(F32)
    mid = r.astype(BF16)
    lo = (r - mid.astype(F32)).astype(BF16)
    return hi, mid, lo


def _ada_kernel(c_ref, w_ref, b_ref, o_ref):
    c = c_ref[...]
    s = (c * _sigmoid(c)).astype(BF16)
    o_ref[...] = _dot(s, w_ref[...].astype(BF16)) + b_ref[...]


def _ada(c_pad, w, b):
    rows, d = c_pad.shape
    n = w.shape[1]
    tn = 1024
    return pl.pallas_call(
        _ada_kernel,
        out_shape=jax.ShapeDtypeStruct((rows, n), F32),
        grid=(n // tn,),
        in_specs=[
            pl.BlockSpec((rows, d), lambda j: (0, 0)),
            pl.BlockSpec((d, tn), lambda j: (0, j)),
            pl.BlockSpec((1, tn), lambda j: (0, j)),
        ],
        out_specs=pl.BlockSpec((rows, tn), lambda j: (0, j)),
        compiler_params=pltpu.CompilerParams(
            dimension_semantics=("parallel",), vmem_limit_bytes=VMEM_LIMIT),
        name="ada",
    )(c_pad, w, b)


def _inproj_kernel(x_ref, sc_ref, sh_ref, w_ref, o_ref):
    h = x_ref[0] * (1.0 + sc_ref[0]) + sh_ref[0]
    o_ref[0] = _dot(h.astype(BF16), w_ref[...])


def _inproj(x, ada3, w_bf16, n_mod):
    b, t, d = x.shape
    n = w_bf16.shape[1]
    tm = 256
    return pl.pallas_call(
        _inproj_kernel,
        out_shape=jax.ShapeDtypeStruct((b, t, n), F32),
        grid=(b, t // tm),
        in_specs=[
            pl.BlockSpec((1, tm, d), lambda i, j: (i, j, 0)),
            pl.BlockSpec((1, 1, d), lambda i, j: (i * n_mod + 1, 0, 0)),
            pl.BlockSpec((1, 1, d), lambda i, j: (i * n_mod, 0, 0)),
            pl.BlockSpec((d, n), lambda i, j: (0, 0)),
        ],
        out_specs=pl.BlockSpec((1, tm, n), lambda i, j: (i, j, 0)),
        compiler_params=pltpu.CompilerParams(
            dimension_semantics=("parallel", "parallel"), vmem_limit_bytes=VMEM_LIMIT),
        name="inproj",
    )(x, ada3, ada3, w_bf16)


def _hgrn_kernel(q_ref, f_ref, i_ref, g_ref, lbl_ref, ng_ref, o_ref, st_ref, oi_ref, *, tc, layer):
    blk = HG_BLK
    nb = tc // blk
    per_grp = LANES // blk

    @pl.when(pl.program_id(2) == 0)
    def _():
        st_ref[...] = jnp.zeros_like(st_ref)

    lbl = lbl_ref[...]
    e = jnp.exp(lbl - jnp.max(lbl, axis=0, keepdims=True))
    lb = jnp.sum(e[: layer + 1], axis=0, keepdims=True) / jnp.sum(e, axis=0, keepdims=True)

    q = q_ref[0]
    qs = q * _sigmoid(q)
    f = lb + (1.0 - lb) * _sigmoid(f_ref[0])
    lf = jnp.log(f)
    k = 1.0 - f
    v = i_ref[0]

    r_i = lax.broadcasted_iota(jnp.int32, (LANES, LANES), 0)
    c_i = lax.broadcasted_iota(jnp.int32, (LANES, LANES), 1)
    tri = jnp.where(((r_i // blk) == (c_i // blk)) & (c_i <= r_i), 1.0, 0.0).astype(BF16)
    bs = []
    for g in range(tc // LANES):
        hi, mid, lo = _split3(lf[g * LANES:(g + 1) * LANES])
        bs.append(_dot(tri, hi) + _dot(tri, mid) + _dot(tri, lo))
    b = jnp.concatenate(bs, axis=0)

    b3 = b.reshape(nb, blk, LANES)
    qs3 = qs.reshape(nb, blk, LANES)
    k3 = k.reshape(nb, blk, LANES)
    v3 = v.reshape(nb, blk, LANES)
    bl3 = b3[:, blk - 1:blk, :]
    qd = (qs * jnp.exp(b)).astype(BF16)
    kd = (k3 * jnp.exp(bl3 - b3)).reshape(tc, LANES).astype(BF16)
    dl = jnp.exp(bl3)

    rows = lax.broadcasted_iota(jnp.int32, (1, blk, 1), 1)
    ones = jnp.ones((LANES, LANES), BF16)
    o3 = jnp.zeros((nb, blk, LANES), F32)
    for s in range(blk):
        d = jnp.minimum(b3 - b3[:, s:s + 1, :], 0.0)
        p = jnp.where(rows >= s, qs3 * jnp.exp(d) * k3[:, s:s + 1, :], 0.0)
        a = _dot(p.reshape(tc, LANES).astype(BF16), ones)
        o3 = o3 + a.reshape(nb, blk, LANES) * v3[:, s:s + 1, :]

    lane = lax.broadcasted_iota(jnp.int32, (LANES, LANES), 1)
    uts = []
    for g in range(tc // LANES):
        vt = v[g * LANES:(g + 1) * LANES].T
        vexp = jnp.concatenate(
            [jnp.where((lane // blk) == j, vt, 0.0) for j in range(per_grp)], axis=0).astype(BF16)
        uts.append(_dot(vexp, kd[g * LANES:(g + 1) * LANES]))

    st = st_ref[...]
    for n in range(nb):
        g, j = divmod(n, per_grp)
        oi_ref[n * blk:(n + 1) * blk, :] = _dot_nt(qd[n * blk:(n + 1) * blk], st.astype(BF16))
        st = dl[n] * st + uts[g][j * LANES:(j + 1) * LANES]
    st_ref[...] = st

    o = o3.reshape(tc, LANES) + oi_ref[...]
    o = o * lax.rsqrt(jnp.mean(o * o, axis=-1, keepdims=True) + RMS_EPS)
    o_ref[0] = o * ng_ref[...] * _sigmoid(g_ref[0])


def _hgrn(proj, lb_logits, norm_g, layer):
    b, t, _ = proj.shape
    tc = 512
    h = HG_HEADS
    nslots = lb_logits.shape[0]
    col = lambda seg: (lambda i, j, s: (i, s, seg * h + j))
    return pl.pallas_call(
        functools.partial(_hgrn_kernel, tc=tc, layer=layer),
        out_shape=jax.ShapeDtypeStruct((b, t, h * HG_D), F32),
        grid=(b, h, t // tc),
        in_specs=[
            pl.BlockSpec((1, tc, HG_D), col(0)),
            pl.BlockSpec((1, tc, HG_D), col(1)),
            pl.BlockSpec((1, tc, HG_D), col(2)),
            pl.BlockSpec((1, tc, HG_D), col(3)),
            pl.BlockSpec((nslots, HG_D), lambda i, j, s: (0, j)),
            pl.BlockSpec((1, HG_D), lambda i, j, s: (0, j)),
        ],
        out_specs=pl.BlockSpec((1, tc, HG_D), lambda i, j, s: (i, s, j)),
        scratch_shapes=[pltpu.VMEM((HG_D, HG_D), F32), pltpu.VMEM((tc, HG_D), F32)],
        compiler_params=pltpu.CompilerParams(
            dimension_semantics=("parallel", "parallel", "arbitrary"), vmem_limit_bytes=VMEM_LIMIT),
        name="hgrn",
    )(proj, proj, proj, proj, lb_logits, norm_g)


def _cmp_kernel(h_ref, pos_ref, w1_ref, w2_ref, o_ref):
    h = h_ref[0, 0, 0]
    half = h.shape[1]
    pos = pos_ref[0]
    w1 = w1_ref[0]
    p0 = _dot((h + pos[:, :half]).astype(BF16), w1[:half])
    p1 = _dot((h + pos[:, half:]).astype(BF16), w1[half:])
    ng = h.shape[0]
    pre = p0 + pltpu.roll(p1, ng - 1, 0)
    gl = 0.5 * pre * (1.0 + jnp.tanh(np.sqrt(2.0 / np.pi) * (pre + 0.044715 * (pre * pre * pre))))
    o_ref[0, 0, 0] = _dot(gl.astype(BF16), w2_ref[0])


def _compress(hgrp, pos, w1, w2):
    two, b, g, ng, half = hgrp.shape
    hid = w1.shape[2]
    dh = w2.shape[2]
    return pl.pallas_call(
        _cmp_kernel,
        out_shape=jax.ShapeDtypeStruct((two, b, g, ng, dh), F32),
        grid=(two, b, g),
        in_specs=[
            pl.BlockSpec((1, 1, 1, ng, half), lambda a, i, j: (a, i, j, 0, 0)),
            pl.BlockSpec((1, 1, 2 * half), lambda a, i, j: (a, 0, 0)),
            pl.BlockSpec((1, 2 * half, hid), lambda a, i, j: (a, 0, 0)),
            pl.BlockSpec((1, hid, dh), lambda a, i, j: (a, 0, 0)),
        ],
        out_specs=pl.BlockSpec((1, 1, 1, ng, dh), lambda a, i, j: (a, i, j, 0, 0)),
        compiler_params=pltpu.CompilerParams(
            dimension_semantics=("parallel", "parallel", "parallel"), vmem_limit_bytes=VMEM_LIMIT),
        name="cmp",
    )(hgrp, pos, w1, w2)


def _nsa_kernel(q_ref, gate_ref, kc_ref, vct_ref, ks_ref, vst_ref, kw_ref, vwt_ref, ovt_ref,
                o_ref, x_ref, sel_ref, *, scale):
    qb = pl.program_id(2)
    t0 = qb * Q_BLOCK
    nq = NSA_HG * Q_BLOCK
    qa = (q_ref[0, 0].reshape(nq, 2 * NSA_DH) * scale).astype(BF16)
    tpos = t0 + (lax.broadcasted_iota(jnp.int32, (1, nq), 1) % Q_BLOCK)

    ncp = kc_ref.shape[2]
    zc = _dot_nt(kc_ref[0, 0], qa)
    cend = lax.broadcasted_iota(jnp.int32, (ncp, 1), 0) * CMP_STRIDE + (CMP_BLOCK - 1)
    valid = cend <= tpos
    zm = jnp.where(valid, zc, NEG)
    mc = jnp.max(zm, axis=0, keepdims=True)
    ec = jnp.where(valid, jnp.exp(zm - mc), 0.0)
    lc = jnp.sum(ec, axis=0, keepdims=True)
    pc = ec * jnp.where(lc > 0.0, 1.0 / lc, 0.0)
    o_c = _dot(vct_ref[0, 0], pc.astype(BF16))

    psum = pc[:, 0:Q_BLOCK]
    for hh in range(1, NSA_HG):
        psum = psum + pc[:, hh * Q_BLOCK:(hh + 1) * Q_BLOCK]
    p_hi = psum.astype(BF16)
    p_lo = (psum - p_hi.astype(F32)).astype(BF16)
    imp = _dot(ovt_ref[...], p_hi) + _dot(ovt_ref[...], p_lo)
    ns = imp.shape[0]
    j_i = lax.broadcasted_iota(jnp.int32, (ns, 1), 0)
    t_abs = t0 + lax.broadcasted_iota(jnp.int32, (1, Q_BLOCK), 1)
    cur = t_abs // SLC_BLOCK
    forced = (j_i == 0) | (j_i == cur) | (j_i == cur - 1)
    x = jnp.where(forced, FORCE, imp)
    x = jnp.where(j_i * SLC_BLOCK <= t_abs, x, NEG)
    x_ref[...] = x

    def rank_body(jp, cnt):
        r = x_ref[pl.ds(jp, 1), :]
        ahead = (r > x) | ((r == x) & (j_i > jp))
        return cnt + jnp.where(ahead, 1.0, 0.0)

    n_causal = (t0 + Q_BLOCK) // SLC_BLOCK
    cnt = lax.fori_loop(0, n_causal, rank_body, jnp.zeros((ns, Q_BLOCK), F32))
    sel_ref[...] = jnp.where(cnt < float(SLC_TOPN), 1.0, 0.0)

    spos0 = lax.broadcasted_iota(jnp.int32, (Q_BLOCK, 1), 0)
    per_tile = Q_BLOCK // SLC_BLOCK

    def online(z, vt, carry):
        m, l, acc = carry
        mn = jnp.maximum(m, jnp.max(z, axis=0, keepdims=True))
        alpha = jnp.exp(m - mn)
        p = jnp.exp(z - mn)
        l = alpha * l + jnp.sum(p, axis=0, keepdims=True)
        acc = alpha * acc + _dot(vt, p.astype(BF16))
        return mn, l, acc

    init = (jnp.full((1, nq), NEG, F32), jnp.zeros((1, nq), F32), jnp.zeros((NSA_DH, nq), F32))

    def sel_body(kt, carry):
        z = _dot_nt(ks_ref[0, 0, kt], qa)
        rows = sel_ref[pl.ds(kt * per_tile, per_tile), :]
        mk = jnp.concatenate(
            [jnp.broadcast_to(rows[i:i + 1], (SLC_BLOCK, Q_BLOCK)) for i in range(per_tile)], axis=0)
        mk = jnp.concatenate([mk] * NSA_HG, axis=1)
        ok = (mk > 0.0) & ((kt * Q_BLOCK + spos0) <= tpos)
        return online(jnp.where(ok, z, NEG), vst_ref[0, 0, kt], carry)

    m_s, l_s, acc_s = lax.fori_loop(0, qb + 1, sel_body, init)

    def win_body(kt, carry):
        z = _dot_nt(kw_ref[0, 0, kt], qa)
        dist = tpos - (kt * Q_BLOCK + spos0)
        ok = (dist >= 0) & (dist < WINDOW)
        return online(jnp.where(ok, z, NEG), vwt_ref[0, 0, kt], carry)

    kt_lo = jnp.maximum(qb - WINDOW // Q_BLOCK, 0)
    m_w, l_w, acc_w = lax.fori_loop(kt_lo, qb + 1, win_body, init)

    gts = _sigmoid(gate_ref[0, 0, 0])
    o_ref[0, 0, 0] = (gts[0:1] * o_c + gts[1:2] * (acc_s * (1.0 / l_s))
                      + gts[2:3] * (acc_w * (1.0 / l_w)))


def _nsa(q_aug, gate_t, kc_aug, vc_t, ks_aug, vs_t, kw_aug, vw_t, ov_t, scale):
    b, g, _, t, _ = q_aug.shape
    nqb = t // Q_BLOCK
    ncp = kc_aug.shape[2]
    ns = ov_t.shape[0]
    nq = NSA_HG * Q_BLOCK
    kv_spec = lambda shape: pl.BlockSpec((1, 1) + shape, lambda i, j, s: (i, j) + (0,) * len(shape))
    return pl.pallas_call(
        functools.partial(_nsa_kernel, scale=scale),
        out_shape=jax.ShapeDtypeStruct((b, g, nqb, NSA_DH, nq), F32),
        grid=(b, g, nqb),
        in_specs=[
            pl.BlockSpec((1, 1, NSA_HG, Q_BLOCK, 2 * NSA_DH), lambda i, j, s: (i, j, 0, s, 0)),
            pl.BlockSpec((1, 1, 1, 3, nq), lambda i, j, s: (i, j, s, 0, 0)),
            kv_spec((ncp, 2 * NSA_DH)),
            kv_spec((NSA_DH, ncp)),
            kv_spec((nqb, Q_BLOCK, 2 * NSA_DH)),
            kv_spec((nqb, NSA_DH, Q_BLOCK)),
            kv_spec((nqb, Q_BLOCK, 2 * NSA_DH)),
            kv_spec((nqb, NSA_DH, Q_BLOCK)),
            pl.BlockSpec((ns, ncp), lambda i, j, s: (0, 0)),
        ],
        out_specs=pl.BlockSpec((1, 1, 1, NSA_DH, nq), lambda i, j, s: (i, j, s, 0, 0)),
        scratch_shapes=[pltpu.VMEM((ns, Q_BLOCK), F32), pltpu.VMEM((ns, Q_BLOCK), F32)],
        compiler_params=pltpu.CompilerParams(
            dimension_semantics=("parallel", "parallel", "arbitrary"), vmem_limit_bytes=VMEM_LIMIT),
        name="nsa",
    )(q_aug, gate_t, kc_aug, vc_t, ks_aug, vs_t, kw_aug, vw_t, ov_t)


def _layer_norm(z, g, b):
    mu = jnp.mean(z, axis=-1, keepdims=True)
    zc = z - mu
    var = jnp.mean(zc * zc, axis=-1, keepdims=True)
    return zc * lax.rsqrt(var + LN_EPS) * g + b


def _outln_kernel(x_ref, yh_ref, yn_ref, wh_ref, wn_ref, gate_ref, lg_ref, lb_ref, o_ref, *, alpha):
    mix = _dot(yh_ref[0].astype(BF16), wh_ref[...]) + _dot(yn_ref[0].astype(BF16), wn_ref[...])
    z = alpha * x_ref[0] + gate_ref[0] * mix
    o_ref[0] = _layer_norm(z, lg_ref[...], lb_ref[...])


def _outln(x, y_hg, y_nsa, w_h, w_n, ada3, n_mod, ln_g, ln_b, alpha):
    b, t, d = x.shape
    tm = 512
    wh, wn = y_hg.shape[2], y_nsa.shape[2]
    return pl.pallas_call(
        functools.partial(_outln_kernel, alpha=alpha),
        out_shape=jax.ShapeDtypeStruct((b, t, d), F32),
        grid=(b, t // tm),
        in_specs=[
            pl.BlockSpec((1, tm, d), lambda i, j: (i, j, 0)),
            pl.BlockSpec((1, tm, wh), lambda i, j: (i, j, 0)),
            pl.BlockSpec((1, tm, wn), lambda i, j: (i, j, 0)),
            pl.BlockSpec((wh, d), lambda i, j: (0, 0)),
            pl.BlockSpec((wn, d), lambda i, j: (0, 0)),
            pl.BlockSpec((1, 1, d), lambda i, j: (i * n_mod + 2, 0, 0)),
            pl.BlockSpec((1, d), lambda i, j: (0, 0)),
            pl.BlockSpec((1, d), lambda i, j: (0, 0)),
        ],
        out_specs=pl.BlockSpec((1, tm, d), lambda i, j: (i, j, 0)),
        compiler_params=pltpu.CompilerParams(
            dimension_semantics=("parallel", "parallel"), vmem_limit_bytes=VMEM_LIMIT),
        name="outln",
    )(x, y_hg, y_nsa, w_h, w_n, ada3, ln_g, ln_b)


def _ffn_kernel(x_ref, sc_ref, sh_ref, gate_ref, w1_ref, w3_ref, w2_ref, lg_ref, lb_ref, o_ref,
                h_ref, acc_ref, *, alpha):
    c = pl.program_id(2)

    @pl.when(c == 0)
    def _():
        h_ref[...] = (x_ref[0] * (1.0 + sc_ref[0]) + sh_ref[0]).astype(BF16)
        acc_ref[...] = jnp.zeros_like(acc_ref)

    h = h_ref[...]
    u = _dot(h, w1_ref[...])
    w = _dot(h, w3_ref[...])
    a = (u * _sigmoid(u) * w).astype(BF16)
    acc_ref[...] += _dot(a, w2_ref[...])

    @pl.when(c == pl.num_programs(2) - 1)
    def _():
        z = alpha * x_ref[0] + gate_ref[0] * acc_ref[...]
        o_ref[0] = _layer_norm(z, lg_ref[...], lb_ref[...])


def _ffn(x, ada3, n_mod, w1, w3, w2, ln_g, ln_b, alpha):
    b, t, d = x.shape
    dff = w1.shape[1]
    tm = 512
    tf = dff // 2
    return pl.pallas_call(
        functools.partial(_ffn_kernel, alpha=alpha),
        out_shape=jax.ShapeDtypeStruct((b, t, d), F32),
        grid=(b, t // tm, dff // tf),
        in_specs=[
            pl.BlockSpec((1, tm, d), lambda i, j, c: (i, j, 0)),
            pl.BlockSpec((1, 1, d), lambda i, j, c: (i * n_mod + 4, 0, 0)),
            pl.BlockSpec((1, 1, d), lambda i, j, c: (i * n_mod + 3, 0, 0)),
            pl.BlockSpec((1, 1, d), lambda i, j, c: (i * n_mod + 5, 0, 0)),
            pl.BlockSpec((d, tf), lambda i, j, c: (0, c)),
            pl.BlockSpec((d, tf), lambda i, j, c: (0, c)),
            pl.BlockSpec((tf, d), lambda i, j, c: (c, 0)),
            pl.BlockSpec((1, d), lambda i, j, c: (0, 0)),
            pl.BlockSpec((1, d), lambda i, j, c: (0, 0)),
        ],
        out_specs=pl.BlockSpec((1, tm, d), lambda i, j, c: (i, j, 0)),
        scratch_shapes=[pltpu.VMEM((tm, d), BF16), pltpu.VMEM((tm, d), F32)],
        compiler_params=pltpu.CompilerParams(
            dimension_semantics=("parallel", "parallel", "arbitrary"), vmem_limit_bytes=VMEM_LIMIT),
        name="ffn",
    )(x, ada3, ada3, ada3, w1, w3, w2, ln_g, ln_b)


def _pos_cols(pos, width):
    pos = np.asarray(pos, np.int64)
    cols = np.zeros((pos.shape[0], width), np.float32)
    cols[:, 0] = pos % 256
    cols[:, 1] = pos - pos % 256
    return cols


def _slope_cols(group_slopes, rows_per_head, width, scale):
    g, hg = group_slopes.shape
    cols = np.zeros((g, hg, rows_per_head, width), np.float32)
    cols[..., 0] = (group_slopes / scale)[:, :, None]
    cols[..., 1] = (group_slopes / scale)[:, :, None]
    return cols


def _overlap_t(n_cmp_pad, n_cmp, n_slc):
    cs = np.arange(n_cmp_pad) * CMP_STRIDE
    ss = np.arange(n_slc) * SLC_BLOCK
    ov = np.clip(np.minimum(cs[:, None] + CMP_BLOCK, ss[None] + SLC_BLOCK)
                 - np.maximum(cs[:, None], ss[None]), 0, None).astype(np.float32) / CMP_BLOCK
    ov[n_cmp:] = 0.0
    return ov.T


def kernel(x, c, w_ada, b_ada, w_in, hg_lb_logits, hg_norm_g, cmp_pos_k, cmp_w1_k, cmp_w2_k,
           cmp_pos_v, cmp_w1_v, cmp_w2_v, w_out, ln1_g, ln1_b, ffn_w1, ffn_w3, ffn_w2, ln2_g, ln2_b):
    bsz, t, d = x.shape
    depth = w_ada.shape[0]
    g, hg, dh = NSA_GROUPS, NSA_HG, NSA_DH
    hgw = HG_HEADS * HG_D
    nsw = NSA_HEADS * dh
    kvw = g * dh
    n_mod = 6
    alpha = (2.0 * depth) ** 0.25
    scale = dh ** -0.5
    n_cols = w_in.shape[2]
    n_pad = -(-n_cols // LANES) * LANES
    nqb = t // Q_BLOCK
    n_grp = t // CMP_STRIDE
    n_cmp = (t - CMP_BLOCK) // CMP_STRIDE + 1
    n_slc = t // SLC_BLOCK

    slopes = (2.0 ** (-8.0 * (np.arange(NSA_HEADS) + 1) / NSA_HEADS)).reshape(g, hg)
    q_extra = jnp.asarray(_slope_cols(slopes, 1, dh, scale))
    key_cols = jnp.asarray(_pos_cols(np.arange(t), dh))
    cmp_cols = jnp.asarray(_pos_cols(np.arange(n_grp) * CMP_STRIDE + CMP_BLOCK - 1, dh))
    ov_t = jnp.asarray(_overlap_t(n_grp, n_cmp, n_slc), BF16)

    c_pad = jnp.pad(c, ((0, 8 - bsz), (0, 0)))
    for l in range(depth):
        ada = _ada(c_pad, w_ada[l], b_ada[l][None])
        ada3 = ada[:bsz].reshape(bsz * n_mod, 1, d)

        w_in_p = jnp.pad(w_in[l], ((0, 0), (0, n_pad - n_cols))).astype(BF16)
        proj = _inproj(x, ada3, w_in_p, n_mod)

        y_hg = _hgrn(proj, hg_lb_logits, hg_norm_g[l][None], l)

        o = 4 * hgw
        nq = proj[:, :, o:o + nsw]
        kvs = [proj[:, :, o + nsw + i * kvw: o + nsw + (i + 1) * kvw] for i in range(6)]
        ngate = proj[:, :, o + nsw + 6 * kvw: o + nsw + 6 * kvw + 3 * NSA_HEADS]

        grp = lambda a: a.reshape(bsz, n_grp, CMP_STRIDE, g, dh).transpose(0, 3, 1, 2, 4).reshape(
            bsz, g, n_grp, CMP_STRIDE * dh)
        hgrp = jnp.stack([grp(kvs[0]), grp(kvs[1])])
        pos = jnp.stack([cmp_pos_k[l].reshape(1, -1), cmp_pos_v[l].reshape(1, -1)])
        w1 = jnp.stack([cmp_w1_k[l], cmp_w1_v[l]]).astype(BF16)
        w2 = jnp.stack([cmp_w2_k[l], cmp_w2_v[l]]).astype(BF16)
        kv_cmp = _compress(hgrp, pos, w1, w2)

        def keys_aug(a, cols):
            return jnp.concatenate(
                [a, jnp.broadcast_to(cols, a.shape[:2] + cols.shape)], axis=-1).astype(BF16)

        split = lambda a: a.reshape(bsz, t, g, dh).transpose(0, 2, 1, 3)
        tiles_k = lambda a: keys_aug(split(a), key_cols).reshape(bsz, g, nqb, Q_BLOCK, 2 * dh)
        tiles_vt = lambda a: split(a).reshape(bsz, g, nqb, Q_BLOCK, dh).transpose(0, 1, 2, 4, 3).astype(BF16)
        kc_aug = keys_aug(kv_cmp[0], cmp_cols)
        vc_t = kv_cmp[1].transpose(0, 1, 3, 2).astype(BF16)

        q5 = nq.reshape(bsz, t, g, hg, dh).transpose(0, 2, 3, 1, 4)
        q_aug = jnp.concatenate([q5, jnp.broadcast_to(q_extra[None], q5.shape)], axis=-1)
        gate_t = ngate.reshape(bsz, nqb, Q_BLOCK, g, hg, 3).transpose(0, 3, 1, 5, 4, 2).reshape(
            bsz, g, nqb, 3, hg * Q_BLOCK)

        o_t = _nsa(q_aug, gate_t, kc_aug, vc_t, tiles_k(kvs[2]), tiles_vt(kvs[3]),
                   tiles_k(kvs[4]), tiles_vt(kvs[5]), ov_t, scale)
        y_nsa = o_t.reshape(bsz, g, nqb, dh, hg, Q_BLOCK).transpose(0, 2, 5, 1, 4, 3).reshape(bsz, t, nsw)

        w_o = w_out[l].astype(BF16)
        x = _outln(x, y_hg, y_nsa, w_o[:hgw], w_o[hgw:], ada3, n_mod, ln1_g[l][None], ln1_b[l][None], alpha)
        x = _ffn(x, ada3, n_mod, ffn_w1[l].astype(BF16), ffn_w3[l].astype(BF16), ffn_w2[l].astype(BF16),
                 ln2_g[l][None], ln2_b[l][None], alpha)
    return x
```

```python
import functools

import numpy as np
import jax
import jax.numpy as jnp
from jax import lax
from jax.experimental import pallas as pl
from jax.experimental.pallas import tpu as pltpu

F32 = jnp.float32
BF16 = jnp.bfloat16

HG_HEADS = 4
HG_D = 128
NSA_HEADS = 8
NSA_GROUPS = 2
NSA_HG = NSA_HEADS // NSA_GROUPS
NSA_DH = 64
CMP_BLOCK = 32
CMP_STRIDE = 16
CMP_HIDDEN = 256
SLC_BLOCK = 64
SLC_TOPN = 16
WINDOW = 512
Q_BLOCK = 128
NEG = -1e30
FORCE = 1e4
LN_EPS = 1e-5
RMS_EPS = 1e-6

LANES = 128
HG_BLK = 16
POS_SPLIT = 256
GATE_ROWS = 16
SEL_TILE = 512
VMEM_LIMIT = 48 * 1024 * 1024


def _sigmoid(x):
    return 1.0 / (1.0 + jnp.exp(-x))


def _dot(a, b):
    return jnp.dot(a, b, preferred_element_type=F32)


def _dot_nt(a, b):
    return lax.dot_general(a, b, (((1,), (1,)), ((), ())), preferred_element_type=F32)


def _split3(x):
    hi = x.astype(BF16)
    r = x - hi.astype(F32)
    mid = r.astype(BF16)
    lo = (r - mid.astype(F32)).astype(BF16)
    return hi, mid, lo


def _ada_kernel(c_ref, w_ref, b_ref, o_ref):
    c = c_ref[...]
    s = (c * _sigmoid(c)).astype(BF16)
    o_ref[...] = _dot(s, w_ref[...].astype(BF16)) + b_ref[...]


def _ada(c_pad, w, b):
    rows, d = c_pad.shape
    n = w.shape[1]
    tn = 1024
    return pl.pallas_call(
        _ada_kernel,
        out_shape=jax.ShapeDtypeStruct((rows, n), F32),
        grid=(n // tn,),
        in_specs=[
            pl.BlockSpec((rows, d), lambda j: (0, 0)),
            pl.BlockSpec((d, tn), lambda j: (0, j)),
            pl.BlockSpec((1, tn), lambda j: (0, j)),
        ],
        out_specs=pl.BlockSpec((rows, tn), lambda j: (0, j)),
        compiler_params=pltpu.CompilerParams(
            dimension_semantics=("parallel",), vmem_limit_bytes=VMEM_LIMIT),
        name="ada",
    )(c_pad, w, b)


def _inproj_kernel(x_ref, sc_ref, sh_ref, wm_ref, wt_ref,
                   hg_ref, q_ref, ks_ref, kw_ref, kc_ref, vc_ref, vt_ref, gt_ref, *, tm):
    h = (x_ref[0] * (1.0 + sc_ref[0]) + sh_ref[0]).astype(BF16)
    y = _dot(h, wm_ref[...])
    n_hg = hg_ref.shape[2]
    n_q = q_ref.shape[2]
    n_k = ks_ref.shape[2]
    n_c = kc_ref.shape[2]
    o = 0
    hg_ref[0] = y[:, o:o + n_hg]
    o += n_hg
    q_ref[0] = y[:, o:o + n_q].astype(BF16)
    o += n_q
    row = pl.program_id(1) * tm + lax.broadcasted_iota(jnp.int32, (tm, n_k), 0)
    col = lax.broadcasted_iota(jnp.int32, (tm, n_k), 1) % LANES
    lo = row % POS_SPLIT
    pc = jnp.where(col == NSA_DH, lo, jnp.where(col == NSA_DH + 1, row - lo, 0)).astype(F32)
    ks_ref[0] = (y[:, o:o + n_k] + pc).astype(BF16)
    o += n_k
    kw_ref[0] = (y[:, o:o + n_k] + pc).astype(BF16)
    o += n_k
    kc_ref[0] = y[:, o:o + n_c]
    o += n_c
    vc_ref[0] = y[:, o:o + n_c]
    yt = _dot_nt(wt_ref[...], h)
    n_vt = vt_ref.shape[1]
    vt_ref[0] = yt[:n_vt].astype(BF16)
    gt_ref[0] = yt[n_vt:]


def _inproj(x, ada3, w_main, w_t, n_mod, widths):
    b, t, d = x.shape
    n_hg, n_q, n_k, n_c, n_vt, n_gt = widths
    tm = 256
    row_spec = lambda n: pl.BlockSpec((1, tm, n), lambda i, j: (i, j, 0))
    col_spec = lambda n: pl.BlockSpec((1, n, tm), lambda i, j: (i, 0, j))
    sds = jax.ShapeDtypeStruct
    return pl.pallas_call(
        functools.partial(_inproj_kernel, tm=tm),
        out_shape=(sds((b, t, n_hg), F32), sds((b, t, n_q), BF16), sds((b, t, n_k), BF16),
                   sds((b, t, n_k), BF16), sds((b, t, n_c), F32), sds((b, t, n_c), F32),
                   sds((b, n_vt, t), BF16), sds((b, n_gt, t), F32)),
        grid=(b, t // tm),
        in_specs=[
            pl.BlockSpec((1, tm, d), lambda i, j: (i, j, 0)),
            pl.BlockSpec((1, 1, d), lambda i, j: (i * n_mod + 1, 0, 0)),
            pl.BlockSpec((1, 1, d), lambda i, j: (i * n_mod, 0, 0)),
            pl.BlockSpec(w_main.shape, lambda i, j: (0, 0)),
            pl.BlockSpec(w_t.shape, lambda i, j: (0, 0)),
        ],
        out_specs=(row_spec(n_hg), row_spec(n_q), row_spec(n_k), row_spec(n_k), row_spec(n_c),
                   row_spec(n_c), col_spec(n_vt), col_spec(n_gt)),
        compiler_params=pltpu.CompilerParams(
            dimension_semantics=("parallel", "parallel"), vmem_limit_bytes=VMEM_LIMIT),
        name="inproj",
    )(x, ada3, ada3, w_main, w_t)


def _hgrn_kernel(q_ref, f_ref, i_ref, g_ref, lbl_ref, ng_ref, o_ref, st_ref, oi_ref, *, tc, layer):
    blk = HG_BLK
    nb = tc // blk
    per_grp = LANES // blk

    @pl.when(pl.program_id(2) == 0)
    def _():
        st_ref[...] = jnp.zeros_like(st_ref)

    lbl = lbl_ref[...]
    e = jnp.exp(lbl - jnp.max(lbl, axis=0, keepdims=True))
    lb = jnp.sum(e[: layer + 1], axis=0, keepdims=True) / jnp.sum(e, axis=0, keepdims=True)

    q = q_ref[0]
    qs = q * _sigmoid(q)
    f = lb + (1.0 - lb) * _sigmoid(f_ref[0])
    lf = jnp.log(f)
    k = 1.0 - f
    v = i_ref[0]

    r_i = lax.broadcasted_iota(jnp.int32, (LANES, LANES), 0)
    c_i = lax.broadcasted_iota(jnp.int32, (LANES, LANES), 1)
    tri = jnp.where(((r_i // blk) == (c_i // blk)) & (c_i <= r_i), 1.0, 0.0).astype(BF16)
    bs = []
    for g in range(tc // LANES):
        hi, mid, lo = _split3(lf[g * LANES:(g + 1) * LANES])
        bs.append(_dot(tri, hi) + _dot(tri, mid) + _dot(tri, lo))
    b = jnp.concatenate(bs, axis=0)

    b3 = b.reshape(nb, blk, LANES)
    qs3 = qs.reshape(nb, blk, LANES)
    k3 = k.reshape(nb, blk, LANES)
    v3 = v.reshape(nb, blk, LANES)
    bl3 = b3[:, blk - 1:blk, :]
    qd = (qs * jnp.exp(b)).astype(BF16)
    kd = (k3 * jnp.exp(bl3 - b3)).reshape(tc, LANES).astype(BF16)
    dl = jnp.exp(bl3)

    rows = lax.broadcasted_iota(jnp.int32, (1, blk, 1), 1)
    ones = jnp.ones((LANES, LANES), BF16)
    o3 = jnp.zeros((nb, blk, LANES), F32)
    for s in range(blk):
        d = jnp.minimum(b3 - b3[:, s:s + 1, :], 0.0)
        p = jnp.where(rows >= s, qs3 * jnp.exp(d) * k3[:, s:s + 1, :], 0.0)
        a = _dot(p.reshape(tc, LANES).astype(BF16), ones)
        o3 = o3 + a.reshape(nb, blk, LANES) * v3[:, s:s + 1, :]

    lane = lax.broadcasted_iota(jnp.int32, (LANES, LANES), 1)
    uts = []
    for g in range(tc // LANES):
        vt = v[g * LANES:(g + 1) * LANES].T
        vexp = jnp.concatenate(
            [jnp.where((lane // blk) == j, vt, 0.0) for j in range(per_grp)], axis=0).astype(BF16)
        uts.append(_dot(vexp, kd[g * LANES:(g + 1) * LANES]))

    st = st_ref[...]
    for n in range(nb):
        g, j = divmod(n, per_grp)
        oi_ref[n * blk:(n + 1) * blk, :] = _dot_nt(qd[n * blk:(n + 1) * blk], st.astype(BF16))
        st = dl[n] * st + uts[g][j * LANES:(j + 1) * LANES]
    st_ref[...] = st

    o = o3.reshape(tc, LANES) + oi_ref[...]
    o = o * lax.rsqrt(jnp.mean(o * o, axis=-1, keepdims=True) + RMS_EPS)
    o_ref[0] = (o * ng_ref[...] * _sigmoid(g_ref[0])).astype(o_ref.dtype)


def _hgrn(proj, lb_logits, norm_g, layer):
    b, t, _ = proj.shape
    tc = 512
    h = HG_HEADS
    nslots = lb_logits.shape[0]
    col = lambda seg: (lambda i, j, s: (i, s, seg * h + j))
    return pl.pallas_call(
        functools.partial(_hgrn_kernel, tc=tc, layer=layer),
        out_shape=jax.ShapeDtypeStruct((b, t, h * HG_D), BF16),
        grid=(b, h, t // tc),
        in_specs=[
            pl.BlockSpec((1, tc, HG_D), col(0)),
            pl.BlockSpec((1, tc, HG_D), col(1)),
            pl.BlockSpec((1, tc, HG_D), col(2)),
            pl.BlockSpec((1, tc, HG_D), col(3)),
            pl.BlockSpec((nslots, HG_D), lambda i, j, s: (0, j)),
            pl.BlockSpec((1, HG_D), lambda i, j, s: (0, j)),
        ],
        out_specs=pl.BlockSpec((1, tc, HG_D), lambda i, j, s: (i, s, j)),
        scratch_shapes=[pltpu.VMEM((HG_D, HG_D), F32), pltpu.VMEM((tc, HG_D), F32)],
        compiler_params=pltpu.CompilerParams(
            dimension_semantics=("parallel", "parallel", "arbitrary"), vmem_limit_bytes=VMEM_LIMIT),
        name="hgrn",
    )(proj, proj, proj, proj, lb_logits, norm_g)


def _cmp_kernel(h_ref, pos_ref, w1_ref, w2_ref, o_ref, *, key_side):
    h = h_ref[0]
    half = h.shape[1]
    pos = pos_ref[...]
    p0 = _dot((h + pos[:, :half]).astype(BF16), w1_ref[0, :half])
    p1 = _dot((h + pos[:, half:]).astype(BF16), w1_ref[0, half:])
    ng = h.shape[0]
    pre = p0 + pltpu.roll(p1, ng - 1, 0)
    gl = 0.5 * pre * (1.0 + jnp.tanh(np.sqrt(2.0 / np.pi) * (pre + 0.044715 * (pre * pre * pre))))
    gl = gl.astype(BF16)
    if key_side:
        cend = lax.broadcasted_iota(jnp.int32, (ng, LANES), 0) * CMP_STRIDE + (CMP_BLOCK - 1)
        col = lax.broadcasted_iota(jnp.int32, (ng, LANES), 1)
        lo = cend % POS_SPLIT
        pc = jnp.where(col == NSA_DH, lo, jnp.where(col == NSA_DH + 1, cend - lo, 0)).astype(F32)
        o_ref[0, 0] = (_dot(gl, w2_ref[...]) + pc).astype(BF16)
    else:
        o_ref[0, 0] = _dot_nt(w2_ref[...], gl).astype(BF16)


def _compress(hgrp, pos, w1, w2, key_side):
    b, ng, half = hgrp.shape
    g = w1.shape[0]
    out_tail = (ng, LANES) if key_side else (w2.shape[0], ng)
    return pl.pallas_call(
        functools.partial(_cmp_kernel, key_side=key_side),
        out_shape=jax.ShapeDtypeStruct((b, g) + out_tail, BF16),
        grid=(b, g),
        in_specs=[
            pl.BlockSpec((1, ng, half), lambda i, j: (i, 0, 0)),
            pl.BlockSpec(pos.shape, lambda i, j: (0, 0)),
            pl.BlockSpec((1,) + w1.shape[1:], lambda i, j: (j, 0, 0)),
            pl.BlockSpec(w2.shape, lambda i, j: (0, 0)),
        ],
        out_specs=pl.BlockSpec((1, 1) + out_tail, lambda i, j: (i, j, 0, 0)),
        compiler_params=pltpu.CompilerParams(
            dimension_semantics=("parallel", "parallel"), vmem_limit_bytes=VMEM_LIMIT),
        name="cmp_k" if key_side else "cmp_v",
    )(hgrp, pos, w1, w2)


def _nsa_kernel(q_ref, gate_ref, kc_ref, vct_ref, ks_ref, vst_ref, kw_ref, vwt_ref, ovt_ref, slope_ref,
                o_ref, x_ref, bias_ref, *, scale):
    qb = pl.program_id(2)
    t0 = qb * Q_BLOCK
    nq = NSA_HG * Q_BLOCK
    qblk = q_ref[0]
    qa = jnp.concatenate([qblk[:, h * LANES:(h + 1) * LANES] for h in range(NSA_HG)], axis=0)
    qa = qa * jnp.asarray(scale, BF16) + slope_ref[0]
    tpos = t0 + (lax.broadcasted_iota(jnp.int32, (1, nq), 1) % Q_BLOCK)

    ncp = kc_ref.shape[2]
    zc = _dot_nt(kc_ref[0, 0], qa)
    cend = lax.broadcasted_iota(jnp.int32, (ncp, 1), 0) * CMP_STRIDE + (CMP_BLOCK - 1)
    valid = cend <= tpos
    zm = jnp.where(valid, zc, NEG)
    mc = jnp.max(zm, axis=0, keepdims=True)
    ec = jnp.where(valid, jnp.exp(zm - mc), 0.0)
    lc = jnp.sum(ec, axis=0, keepdims=True)
    pc = ec * jnp.where(lc > 0.0, 1.0 / lc, 0.0)
    o_c = _dot(vct_ref[0, 0], pc.astype(BF16))

    psum = pc[:, 0:Q_BLOCK]
    for hh in range(1, NSA_HG):
        psum = psum + pc[:, hh * Q_BLOCK:(hh + 1) * Q_BLOCK]
    p_hi = psum.astype(BF16)
    p_lo = (psum - p_hi.astype(F32)).astype(BF16)
    imp = _dot(ovt_ref[...], p_hi) + _dot(ovt_ref[...], p_lo)
    ns = imp.shape[0]
    j_i = lax.broadcasted_iota(jnp.int32, (ns, 1), 0)
    t_abs = t0 + lax.broadcasted_iota(jnp.int32, (1, Q_BLOCK), 1)
    cur = t_abs // SLC_BLOCK
    forced = (j_i == 0) | (j_i == cur) | (j_i == cur - 1)
    x = jnp.where(forced, FORCE, imp)
    x = jnp.where(j_i * SLC_BLOCK <= t_abs, x, NEG)
    x_ref[...] = x

    def rank_body(jp, cnt):
        r = x_ref[pl.ds(jp, 1), :]
        ahead = (r > x) | ((r == x) & (j_i > jp))
        return cnt + jnp.where(ahead, 1.0, 0.0)

    n_causal = (t0 + Q_BLOCK) // SLC_BLOCK
    cnt = lax.fori_loop(0, n_causal, rank_body, jnp.zeros((ns, Q_BLOCK), F32))
    bias_ref[...] = jnp.where(cnt < float(SLC_TOPN), 0.0, NEG)

    nkw = WINDOW + Q_BLOCK
    k0 = pl.multiple_of(jnp.maximum(t0 - WINDOW, 0), Q_BLOCK)
    zw = _dot_nt(kw_ref[0, pl.ds(k0, nkw), :], qa)
    dist = tpos - (k0 + lax.broadcasted_iota(jnp.int32, (nkw, 1), 0))
    zw = jnp.where((dist >= 0) & (dist < WINDOW), zw, NEG)
    pw = jnp.exp(zw - jnp.max(zw, axis=0, keepdims=True))
    lw = jnp.sum(pw, axis=0, keepdims=True)
    o_w = _dot(vwt_ref[0, :, pl.ds(k0, nkw)], pw.astype(BF16)) * (1.0 / lw)

    per_step = SEL_TILE // SLC_BLOCK
    spos0 = lax.broadcasted_iota(jnp.int32, (SEL_TILE, 1), 0)

    def sel_step(st, carry, causal):
        m, l, acc = carry
        s0 = pl.multiple_of(st * SEL_TILE, SEL_TILE)
        z = _dot_nt(ks_ref[0, pl.ds(s0, SEL_TILE), :], qa)
        brow = bias_ref[pl.ds(pl.multiple_of(st * per_step, per_step), per_step), :]
        b4 = jnp.concatenate([brow] * NSA_HG, axis=1)
        z = (z.reshape(per_step, SLC_BLOCK, nq) + b4[:, None, :]).reshape(SEL_TILE, nq)
        if causal:
            z = jnp.where((s0 + spos0) <= tpos, z, NEG)
        mn = jnp.maximum(m, jnp.max(z, axis=0, keepdims=True))
        alpha = jnp.exp(m - mn)
        p = jnp.exp(z - mn)
        l = alpha * l + jnp.sum(p, axis=0, keepdims=True)
        acc = alpha * acc + _dot(vst_ref[0, :, pl.ds(s0, SEL_TILE)], p.astype(BF16))
        return mn, l, acc

    init = (jnp.full((1, nq), NEG, F32), jnp.zeros((1, nq), F32), jnp.zeros((NSA_DH, nq), F32))
    last = qb // (SEL_TILE // Q_BLOCK)
    carry = lax.fori_loop(0, last, lambda st, c: sel_step(st, c, False), init)
    _, l_s, acc_s = sel_step(last, carry, True)
    o_s = acc_s * (1.0 / l_s)

    gts = _sigmoid(gate_ref[0])
    outs = []
    for h in range(NSA_HG):
        sl = slice(h * Q_BLOCK, (h + 1) * Q_BLOCK)
        outs.append(gts[3 * h:3 * h + 1] * o_c[:, sl] + gts[3 * h + 1:3 * h + 2] * o_s[:, sl]
                    + gts[3 * h + 2:3 * h + 3] * o_w[:, sl])
    pairs = [jnp.concatenate(outs[h:h + 2], axis=0).T for h in range(0, NSA_HG, 2)]
    o_ref[0] = jnp.concatenate(pairs, axis=1).astype(o_ref.dtype)


def _nsa(q_slots, gate_t, kc_aug, vc_t, ks_aug, kw_aug, v_t, ov_t, slope_tab, scale):
    b, t, _ = q_slots.shape
    g = NSA_GROUPS
    nqb = t // Q_BLOCK
    ncp = kc_aug.shape[2]
    ns = ov_t.shape[0]
    return pl.pallas_call(
        functools.partial(_nsa_kernel, scale=scale),
        out_shape=jax.ShapeDtypeStruct((b, t, NSA_HEADS * NSA_DH), BF16),
        grid=(b, g, nqb),
        in_specs=[
            pl.BlockSpec((1, Q_BLOCK, NSA_HG * LANES), lambda i, j, s: (i, s, j)),
            pl.BlockSpec((1, GATE_ROWS, Q_BLOCK), lambda i, j, s: (i, j, s)),
            pl.BlockSpec((1, 1, ncp, LANES), lambda i, j, s: (i, j, 0, 0)),
            pl.BlockSpec((1, 1, NSA_DH, ncp), lambda i, j, s: (i, j, 0, 0)),
            pl.BlockSpec((1, t, LANES), lambda i, j, s: (i, 0, j)),
            pl.BlockSpec((1, NSA_DH, t), lambda i, j, s: (i, j, 0)),
            pl.BlockSpec((1, t, LANES), lambda i, j, s: (i, 0, j)),
            pl.BlockSpec((1, NSA_DH, t), lambda i, j, s: (i, g + j, 0)),
            pl.BlockSpec((ns, ncp), lambda i, j, s: (0, 0)),
            pl.BlockSpec((1, NSA_HG * Q_BLOCK, LANES), lambda i, j, s: (j, 0, 0)),
        ],
        out_specs=pl.BlockSpec((1, Q_BLOCK, NSA_HG * NSA_DH), lambda i, j, s: (i, s, j)),
        scratch_shapes=[pltpu.VMEM((ns, Q_BLOCK), F32), pltpu.VMEM((ns, Q_BLOCK), F32)],
        compiler_params=pltpu.CompilerParams(
            dimension_semantics=("parallel", "parallel", "arbitrary"), vmem_limit_bytes=VMEM_LIMIT),
        name="nsa",
    )(q_slots, gate_t, kc_aug, vc_t, ks_aug, v_t, kw_aug, v_t, ov_t, slope_tab)


def _layer_norm(z, g, b):
    mu = jnp.mean(z, axis=-1, keepdims=True)
    zc = z - mu
    var = jnp.mean(zc * zc, axis=-1, keepdims=True)
    return zc * lax.rsqrt(var + LN_EPS) * g + b


def _outln_kernel(x_ref, yh_ref, yn_ref, wh_ref, wn_ref, gate_ref, lg_ref, lb_ref, o_ref, *, alpha):
    mix = _dot(yh_ref[0].astype(BF16), wh_ref[...]) + _dot(yn_ref[0].astype(BF16), wn_ref[...])
    z = alpha * x_ref[0] + gate_ref[0] * mix
    o_ref[0] = _layer_norm(z, lg_ref[...], lb_ref[...])


def _outln(x, y_hg, y_nsa, w_h, w_n, ada3, n_mod, ln_g, ln_b, alpha):
    b, t, d = x.shape
    tm = 512
    wh, wn = y_hg.shape[2], y_nsa.shape[2]
    return pl.pallas_call(
        functools.partial(_outln_kernel, alpha=alpha),
        out_shape=jax.ShapeDtypeStruct((b, t, d), F32),
        grid=(b, t // tm),
        in_specs=[
            pl.BlockSpec((1, tm, d), lambda i, j: (i, j, 0)),
            pl.BlockSpec((1, tm, wh), lambda i, j: (i, j, 0)),
            pl.BlockSpec((1, tm, wn), lambda i, j: (i, j, 0)),
            pl.BlockSpec((wh, d), lambda i, j: (0, 0)),
            pl.BlockSpec((wn, d), lambda i, j: (0, 0)),
            pl.BlockSpec((1, 1, d), lambda i, j: (i * n_mod + 2, 0, 0)),
            pl.BlockSpec((1, d), lambda i, j: (0, 0)),
            pl.BlockSpec((1, d), lambda i, j: (0, 0)),
        ],
        out_specs=pl.BlockSpec((1, tm, d), lambda i, j: (i, j, 0)),
        compiler_params=pltpu.CompilerParams(
            dimension_semantics=("parallel", "parallel"), vmem_limit_bytes=VMEM_LIMIT),
        name="outln",
    )(x, y_hg, y_nsa, w_h, w_n, ada3, ln_g, ln_b)


def _ffn_kernel(x_ref, sc_ref, sh_ref, gate_ref, w1_ref, w3_ref, w2_ref, lg_ref, lb_ref, o_ref,
                h_ref, acc_ref, *, alpha):
    c = pl.program_id(2)

    @pl.when(c == 0)
    def _():
        h_ref[...] = (x_ref[0] * (1.0 + sc_ref[0]) + sh_ref[0]).astype(BF16)
        acc_ref[...] = jnp.zeros_like(acc_ref)

    h = h_ref[...]
    u = _dot(h, w1_ref[...])
    w = _dot(h, w3_ref[...])
    a = (u * _sigmoid(u) * w).astype(BF16)
    acc_ref[...] += _dot(a, w2_ref[...])

    @pl.when(c == pl.num_programs(2) - 1)
    def _():
        z = alpha * x_ref[0] + gate_ref[0] * acc_ref[...]
        o_ref[0] = _layer_norm(z, lg_ref[...], lb_ref[...])


def _ffn(x, ada3, n_mod, w1, w3, w2, ln_g, ln_b, alpha):
    b, t, d = x.shape
    dff = w1.shape[1]
    tm = 512
    tf = dff // 2
    return pl.pallas_call(
        functools.partial(_ffn_kernel, alpha=alpha),
        out_shape=jax.ShapeDtypeStruct((b, t, d), F32),
        grid=(b, t // tm, dff // tf),
        in_specs=[
            pl.BlockSpec((1, tm, d), lambda i, j, c: (i, j, 0)),
            pl.BlockSpec((1, 1, d), lambda i, j, c: (i * n_mod + 4, 0, 0)),
            pl.BlockSpec((1, 1, d), lambda i, j, c: (i * n_mod + 3, 0, 0)),
            pl.BlockSpec((1, 1, d), lambda i, j, c: (i * n_mod + 5, 0, 0)),
            pl.BlockSpec((d, tf), lambda i, j, c: (0, c)),
            pl.BlockSpec((d, tf), lambda i, j, c: (0, c)),
            pl.BlockSpec((tf, d), lambda i, j, c: (c, 0)),
            pl.BlockSpec((1, d), lambda i, j, c: (0, 0)),
            pl.BlockSpec((1, d), lambda i, j, c: (0, 0)),
        ],
        out_specs=pl.BlockSpec((1, tm, d), lambda i, j, c: (i, j, 0)),
        scratch_shapes=[pltpu.VMEM((tm, d), BF16), pltpu.VMEM((tm, d), F32)],
        compiler_params=pltpu.CompilerParams(
            dimension_semantics=("parallel", "parallel", "arbitrary"), vmem_limit_bytes=VMEM_LIMIT),
        name="ffn",
    )(x, ada3, ada3, ada3, w1, w3, w2, ln_g, ln_b)


def _slope_table(group_slopes):
    g, hg = group_slopes.shape
    tab = np.zeros((g, hg, Q_BLOCK, LANES), np.float32)
    tab[..., NSA_DH] = group_slopes[:, :, None]
    tab[..., NSA_DH + 1] = group_slopes[:, :, None]
    return tab.reshape(g, hg * Q_BLOCK, LANES)


def _pad_slots(w, n_slots):
    d = w.shape[0]
    w3 = w.reshape(d, n_slots, -1)
    return jnp.pad(w3, ((0, 0), (0, 0), (0, LANES - w3.shape[2]))).reshape(d, n_slots * LANES)


def _cmp_w1_expand(w1, g):
    hid = w1.shape[1]
    w1r = w1.reshape(2, CMP_STRIDE, 1, NSA_DH, hid)
    eye = jnp.eye(g, dtype=w1.dtype).reshape(g, 1, 1, g, 1, 1)
    return (eye * w1r[None]).reshape(g, 2 * CMP_STRIDE * g * NSA_DH, hid)


def _overlap_t(n_cmp_pad, n_cmp, n_slc):
    cs = np.arange(n_cmp_pad) * CMP_STRIDE
    ss = np.arange(n_slc) * SLC_BLOCK
    ov = np.clip(np.minimum(cs[:, None] + CMP_BLOCK, ss[None] + SLC_BLOCK)
                 - np.maximum(cs[:, None], ss[None]), 0, None).astype(np.float32) / CMP_BLOCK
    ov[n_cmp:] = 0.0
    return ov.T


def kernel(x, c, w_ada, b_ada, w_in, hg_lb_logits, hg_norm_g, cmp_pos_k, cmp_w1_k, cmp_w2_k,
           cmp_pos_v, cmp_w1_v, cmp_w2_v, w_out, ln1_g, ln1_b, ffn_w1, ffn_w3, ffn_w2, ln2_g, ln2_b):
    bsz, t, d = x.shape
    depth = w_ada.shape[0]
    g, hg, dh = NSA_GROUPS, NSA_HG, NSA_DH
    hgw = HG_HEADS * HG_D
    nsw = NSA_HEADS * dh
    kvw = g * dh
    n_mod = 6
    alpha = (2.0 * depth) ** 0.25
    scale = dh ** -0.5
    n_grp = t // CMP_STRIDE
    n_cmp = (t - CMP_BLOCK) // CMP_STRIDE + 1
    n_slc = t // SLC_BLOCK
    assert t % SEL_TILE == 0 and t >= WINDOW + Q_BLOCK and 3 * hg <= GATE_ROWS and hg % 2 == 0

    slopes = (2.0 ** (-8.0 * (np.arange(NSA_HEADS) + 1) / NSA_HEADS)).reshape(g, hg)
    slope_tab = jnp.asarray(_slope_table(slopes), BF16)
    ov_t = jnp.asarray(_overlap_t(n_grp, n_cmp, n_slc), BF16)

    c_pad = jnp.pad(c, ((0, 8 - bsz), (0, 0)))
    for l in range(depth):
        ada = _ada(c_pad, w_ada[l], b_ada[l][None])
        ada3 = ada[:bsz].reshape(bsz * n_mod, 1, d)

        w = w_in[l]
        o = 4 * hgw
        w_q = w[:, o:o + nsw]
        w_kv = [w[:, o + nsw + i * kvw: o + nsw + (i + 1) * kvw] for i in range(6)]
        w_g = w[:, o + nsw + 6 * kvw: o + nsw + 6 * kvw + 3 * NSA_HEADS]
        w_main = jnp.concatenate(
            [w[:, :o], _pad_slots(w_q, NSA_HEADS), _pad_slots(w_kv[2], g), _pad_slots(w_kv[4], g),
             w_kv[0], w_kv[1]], axis=1).astype(BF16)
        w_gt = jnp.pad(w_g.reshape(d, g, 3 * hg), ((0, 0), (0, 0), (0, GATE_ROWS - 3 * hg))).reshape(d, -1)
        w_t = jnp.concatenate([w_kv[3], w_kv[5], w_gt], axis=1).T.astype(BF16)
        widths = (o, NSA_HEADS * LANES, g * LANES, kvw, 2 * kvw, g * GATE_ROWS)
        hg_in, q_slots, ks_aug, kw_aug, kc, vc, v_t, gate_t = _inproj(x, ada3, w_main, w_t, n_mod, widths)

        y_hg = _hgrn(hg_in, hg_lb_logits, hg_norm_g[l][None], l)

        grp = lambda a: a.reshape(bsz, n_grp, CMP_STRIDE * kvw)
        pos_e = lambda p: jnp.broadcast_to(
            p.reshape(2, CMP_STRIDE, 1, dh), (2, CMP_STRIDE, g, dh)).reshape(1, -1)
        w2k = jnp.pad(cmp_w2_k[l], ((0, 0), (0, LANES - dh))).astype(BF16)
        kc_aug = _compress(grp(kc), pos_e(cmp_pos_k[l]), _cmp_w1_expand(cmp_w1_k[l], g).astype(BF16),
                           w2k, True)
        vc_t = _compress(grp(vc), pos_e(cmp_pos_v[l]), _cmp_w1_expand(cmp_w1_v[l], g).astype(BF16),
                         cmp_w2_v[l].T.astype(BF16), False)

        y_nsa = _nsa(q_slots, gate_t, kc_aug, vc_t, ks_aug, kw_aug, v_t, ov_t, slope_tab, scale)

        w_o = w_out[l].astype(BF16)
        x = _outln(x, y_hg, y_nsa, w_o[:hgw], w_o[hgw:], ada3, n_mod, ln1_g[l][None], ln1_b[l][None], alpha)
        x = _ffn(x, ada3, n_mod, ffn_w1[l].astype(BF16), ffn_w3[l].astype(BF16), ffn_w2[l].astype(BF16),
                 ln2_g[l][None], ln2_b[l][None], alpha)
    return x
```

```python
import functools

import numpy as np
import jax
import jax.numpy as jnp
from jax import lax
from jax.experimental import pallas as pl
from jax.experimental.pallas import tpu as pltpu

F32 = jnp.float32
BF16 = jnp.bfloat16

HG_HEADS = 4
HG_D = 128
NSA_HEADS = 8
NSA_GROUPS = 2
NSA_HG = NSA_HEADS // NSA_GROUPS
NSA_DH = 64
CMP_BLOCK = 32
CMP_STRIDE = 16
CMP_HIDDEN = 256
SLC_BLOCK = 64
SLC_TOPN = 16
WINDOW = 512
Q_BLOCK = 128
NEG = -1e30
FORCE = 1e4
LN_EPS = 1e-5
RMS_EPS = 1e-6

LANES = 128
HG_BLK = 16
POS_SPLIT = 256
GATE_ROWS = 16
SEL_TILE = 512
V_SLOT = 80
N_SPLIT = 3
LOG2E = 1.4426950408889634
VMEM_LIMIT = 48 * 1024 * 1024


def _sigmoid(x):
    return 1.0 / (1.0 + jnp.exp(-x))


def _dot(a, b):
    return jnp.dot(a, b, preferred_element_type=F32)


def _dot_nt(a, b):
    return lax.dot_general(a, b, (((1,), (1,)), ((), ())), preferred_element_type=F32)


def _pos_columns(pos, col):
    c = col - NSA_DH
    lo = pos % POS_SPLIT
    val = jnp.where(c % 2 == 0, lo, pos - lo)
    return jnp.where((c >= 0) & (c < 2 * N_SPLIT), val, 0).astype(F32)


def _split3(x):
    hi = x.astype(BF16)
    r = x - hi.astype(F32)
    mid = r.astype(BF16)
    lo = (r - mid.astype(F32)).astype(BF16)
    return hi, mid, lo


def _ada_kernel(c_ref, w_ref, b_ref, o_ref):
    c = c_ref[...]
    s = (c * _sigmoid(c)).astype(BF16)
    o_ref[...] = _dot(s, w_ref[...].astype(BF16)) + b_ref[...]


def _ada(c_pad, w, b):
    rows, d = c_pad.shape
    n = w.shape[1]
    tn = 1024
    return pl.pallas_call(
        _ada_kernel,
        out_shape=jax.ShapeDtypeStruct((rows, n), F32),
        grid=(n // tn,),
        in_specs=[
            pl.BlockSpec((rows, d), lambda j: (0, 0)),
            pl.BlockSpec((d, tn), lambda j: (0, j)),
            pl.BlockSpec((1, tn), lambda j: (0, j)),
        ],
        out_specs=pl.BlockSpec((rows, tn), lambda j: (0, j)),
        compiler_params=pltpu.CompilerParams(
            dimension_semantics=("parallel",), vmem_limit_bytes=VMEM_LIMIT),
        name="ada",
    )(c_pad, w, b)


def _inproj_kernel(x_ref, sc_ref, sh_ref, wm_ref, wt_ref,
                   hg_ref, q_ref, ks_ref, kw_ref, kc_ref, vc_ref, vt_ref, gt_ref, *, tm):
    h = (x_ref[0] * (1.0 + sc_ref[0]) + sh_ref[0]).astype(BF16)
    y = _dot(h, wm_ref[...])
    n_hg = hg_ref.shape[2]
    n_q = q_ref.shape[2]
    n_k = ks_ref.shape[2]
    n_c = kc_ref.shape[2]
    o = 0
    hg_ref[0] = y[:, o:o + n_hg]
    o += n_hg
    q_ref[0] = y[:, o:o + n_q].astype(BF16)
    o += n_q
    row = pl.program_id(1) * tm + lax.broadcasted_iota(jnp.int32, (tm, n_k), 0)
    pc = _pos_columns(row, lax.broadcasted_iota(jnp.int32, (tm, n_k), 1) % LANES)
    ks_ref[0] = (y[:, o:o + n_k] + pc).astype(BF16)
    o += n_k
    kw_ref[0] = (y[:, o:o + n_k] + pc).astype(BF16)
    o += n_k
    kc_ref[0] = y[:, o:o + n_c]
    o += n_c
    vc_ref[0] = y[:, o:o + n_c]
    yt = _dot_nt(wt_ref[...], h)
    n_vt = vt_ref.shape[1]
    ri = lax.broadcasted_iota(jnp.int32, (n_vt, tm), 0) % V_SLOT
    vt_ref[0] = (yt[:n_vt] + jnp.where(ri == NSA_DH, 1.0, 0.0)).astype(BF16)
    gt_ref[0] = yt[n_vt:]


def _inproj(x, ada3, w_main, w_t, n_mod, widths):
    b, t, d = x.shape
    n_hg, n_q, n_k, n_c, n_vt, n_gt = widths
    tm = 256
    row_spec = lambda n: pl.BlockSpec((1, tm, n), lambda i, j: (i, j, 0))
    col_spec = lambda n: pl.BlockSpec((1, n, tm), lambda i, j: (i, 0, j))
    sds = jax.ShapeDtypeStruct
    return pl.pallas_call(
        functools.partial(_inproj_kernel, tm=tm),
        out_shape=(sds((b, t, n_hg), F32), sds((b, t, n_q), BF16), sds((b, t, n_k), BF16),
                   sds((b, t, n_k), BF16), sds((b, t, n_c), F32), sds((b, t, n_c), F32),
                   sds((b, n_vt, t), BF16), sds((b, n_gt, t), F32)),
        grid=(b, t // tm),
        in_specs=[
            pl.BlockSpec((1, tm, d), lambda i, j: (i, j, 0)),
            pl.BlockSpec((1, 1, d), lambda i, j: (i * n_mod + 1, 0, 0)),
            pl.BlockSpec((1, 1, d), lambda i, j: (i * n_mod, 0, 0)),
            pl.BlockSpec(w_main.shape, lambda i, j: (0, 0)),
            pl.BlockSpec(w_t.shape, lambda i, j: (0, 0)),
        ],
        out_specs=(row_spec(n_hg), row_spec(n_q), row_spec(n_k), row_spec(n_k), row_spec(n_c),
                   row_spec(n_c), col_spec(n_vt), col_spec(n_gt)),
        compiler_params=pltpu.CompilerParams(
            dimension_semantics=("parallel", "parallel"), vmem_limit_bytes=VMEM_LIMIT),
        name="inproj",
    )(x, ada3, ada3, w_main, w_t)


def _hgrn_kernel(q_ref, f_ref, i_ref, g_ref, lbl_ref, ng_ref, o_ref, st_ref, oi_ref, *, tc, layer):
    blk = HG_BLK
    nb = tc // blk
    per_grp = LANES // blk

    @pl.when(pl.program_id(2) == 0)
    def _():
        st_ref[...] = jnp.zeros_like(st_ref)

    lbl = lbl_ref[...]
    e = jnp.exp(lbl - jnp.max(lbl, axis=0, keepdims=True))
    lb = jnp.sum(e[: layer + 1], axis=0, keepdims=True) / jnp.sum(e, axis=0, keepdims=True)

    q = q_ref[0]
    qs = q * _sigmoid(q)
    f = lb + (1.0 - lb) * _sigmoid(f_ref[0])
    lf = jnp.log(f)
    k = 1.0 - f
    v = i_ref[0]

    r_i = lax.broadcasted_iota(jnp.int32, (LANES, LANES), 0)
    c_i = lax.broadcasted_iota(jnp.int32, (LANES, LANES), 1)
    tri = jnp.where(((r_i // blk) == (c_i // blk)) & (c_i <= r_i), 1.0, 0.0).astype(BF16)
    bs = []
    for g in range(tc // LANES):
        hi, mid, lo = _split3(lf[g * LANES:(g + 1) * LANES])
        bs.append(_dot(tri, hi) + _dot(tri, mid) + _dot(tri, lo))
    b = jnp.concatenate(bs, axis=0)

    b3 = b.reshape(nb, blk, LANES)
    qs3 = qs.reshape(nb, blk, LANES)
    k3 = k.reshape(nb, blk, LANES)
    v3 = v.reshape(nb, blk, LANES)
    bl3 = b3[:, blk - 1:blk, :]
    qd = (qs * jnp.exp(b)).astype(BF16)
    kd = (k3 * jnp.exp(bl3 - b3)).reshape(tc, LANES).astype(BF16)
    dl = jnp.exp(bl3)

    rows = lax.broadcasted_iota(jnp.int32, (1, blk, 1), 1)
    ones = jnp.ones((LANES, LANES), BF16)
    o3 = jnp.zeros((nb, blk, LANES), F32)
    for s in range(blk):
        d = jnp.minimum(b3 - b3[:, s:s + 1, :], 0.0)
        p = jnp.where(rows >= s, qs3 * jnp.exp(d) * k3[:, s:s + 1, :], 0.0)
        a = _dot(p.reshape(tc, LANES).astype(BF16), ones)
        o3 = o3 + a.reshape(nb, blk, LANES) * v3[:, s:s + 1, :]

    lane = lax.broadcasted_iota(jnp.int32, (LANES, LANES), 1)
    uts = []
    for g in range(tc // LANES):
        vt = v[g * LANES:(g + 1) * LANES].T
        vexp = jnp.concatenate(
            [jnp.where((lane // blk) == j, vt, 0.0) for j in range(per_grp)], axis=0).astype(BF16)
        uts.append(_dot(vexp, kd[g * LANES:(g + 1) * LANES]))

    st = st_ref[...]
    for n in range(nb):
        g, j = divmod(n, per_grp)
        oi_ref[n * blk:(n + 1) * blk, :] = _dot_nt(qd[n * blk:(n + 1) * blk], st.astype(BF16))
        st = dl[n] * st + uts[g][j * LANES:(j + 1) * LANES]
    st_ref[...] = st

    o = o3.reshape(tc, LANES) + oi_ref[...]
    o = o * lax.rsqrt(jnp.mean(o * o, axis=-1, keepdims=True) + RMS_EPS)
    o_ref[0] = (o * ng_ref[...] * _sigmoid(g_ref[0])).astype(o_ref.dtype)


def _hgrn(proj, lb_logits, norm_g, layer):
    b, t, _ = proj.shape
    tc = 512
    h = HG_HEADS
    nslots = lb_logits.shape[0]
    col = lambda seg: (lambda i, j, s: (i, s, seg * h + j))
    return pl.pallas_call(
        functools.partial(_hgrn_kernel, tc=tc, layer=layer),
        out_shape=jax.ShapeDtypeStruct((b, t, h * HG_D), BF16),
        grid=(b, h, t // tc),
        in_specs=[
            pl.BlockSpec((1, tc, HG_D), col(0)),
            pl.BlockSpec((1, tc, HG_D), col(1)),
            pl.BlockSpec((1, tc, HG_D), col(2)),
            pl.BlockSpec((1, tc, HG_D), col(3)),
            pl.BlockSpec((nslots, HG_D), lambda i, j, s: (0, j)),
            pl.BlockSpec((1, HG_D), lambda i, j, s: (0, j)),
        ],
        out_specs=pl.BlockSpec((1, tc, HG_D), lambda i, j, s: (i, s, j)),
        scratch_shapes=[pltpu.VMEM((HG_D, HG_D), F32), pltpu.VMEM((tc, HG_D), F32)],
        compiler_params=pltpu.CompilerParams(
            dimension_semantics=("parallel", "parallel", "arbitrary"), vmem_limit_bytes=VMEM_LIMIT),
        name="hgrn",
    )(proj, proj, proj, proj, lb_logits, norm_g)


def _cmp_kernel(h_ref, pos_ref, w1_ref, w2_ref, o_ref, *, key_side):
    h = h_ref[0]
    half = h.shape[1]
    pos = pos_ref[...]
    p0 = _dot((h + pos[:, :half]).astype(BF16), w1_ref[0, :half])
    p1 = _dot((h + pos[:, half:]).astype(BF16), w1_ref[0, half:])
    ng = h.shape[0]
    pre = p0 + pltpu.roll(p1, ng - 1, 0)
    gl = 0.5 * pre * (1.0 + jnp.tanh(np.sqrt(2.0 / np.pi) * (pre + 0.044715 * (pre * pre * pre))))
    gl = gl.astype(BF16)
    if key_side:
        cend = lax.broadcasted_iota(jnp.int32, (ng, LANES), 0) * CMP_STRIDE + (CMP_BLOCK - 1)
        pc = _pos_columns(cend, lax.broadcasted_iota(jnp.int32, (ng, LANES), 1))
        o_ref[0, 0] = (_dot(gl, w2_ref[...]) + pc).astype(BF16)
    else:
        o_ref[0, 0] = _dot_nt(w2_ref[...], gl).astype(BF16)


def _compress(hgrp, pos, w1, w2, key_side):
    b, ng, half = hgrp.shape
    g = w1.shape[0]
    out_tail = (ng, LANES) if key_side else (w2.shape[0], ng)
    return pl.pallas_call(
        functools.partial(_cmp_kernel, key_side=key_side),
        out_shape=jax.ShapeDtypeStruct((b, g) + out_tail, BF16),
        grid=(b, g),
        in_specs=[
            pl.BlockSpec((1, ng, half), lambda i, j: (i, 0, 0)),
            pl.BlockSpec(pos.shape, lambda i, j: (0, 0)),
            pl.BlockSpec((1,) + w1.shape[1:], lambda i, j: (j, 0, 0)),
            pl.BlockSpec(w2.shape, lambda i, j: (0, 0)),
        ],
        out_specs=pl.BlockSpec((1, 1) + out_tail, lambda i, j: (i, j, 0, 0)),
        compiler_params=pltpu.CompilerParams(
            dimension_semantics=("parallel", "parallel"), vmem_limit_bytes=VMEM_LIMIT),
        name="cmp_k" if key_side else "cmp_v",
    )(hgrp, pos, w1, w2)


def _nsa_kernel(q_ref, gate_ref, kc_ref, vct_ref, ks_ref, vst_ref, kw_ref, vwt_ref, ovt_ref, slope_ref,
                o_ref, x_ref, bias_ref, *z_scratch):
    qb = pl.program_id(2)
    t0 = qb * Q_BLOCK
    npair = NSA_HG // 2
    za_refs, zb_refs = z_scratch[:npair], z_scratch[npair:]
    nq = 2 * Q_BLOCK
    qblk = q_ref[0]
    qas = []
    for hp in range(npair):
        qa = jnp.concatenate([qblk[:, h * LANES:(h + 1) * LANES] for h in (2 * hp, 2 * hp + 1)], axis=0)
        qas.append(qa + slope_ref[0, hp * nq:(hp + 1) * nq])
    tpos = t0 + (lax.broadcasted_iota(jnp.int32, (1, nq), 1) % Q_BLOCK)

    def normalised(acc):
        return acc[:NSA_DH] * (1.0 / acc[NSA_DH:NSA_DH + 1])

    per_step = SEL_TILE // SLC_BLOCK
    spos0 = lax.broadcasted_iota(jnp.int32, (SEL_TILE, 1), 0)
    nkw = WINDOW + Q_BLOCK
    k0 = pl.multiple_of(jnp.maximum(t0 - WINDOW, 0), Q_BLOCK)
    kw = kw_ref[0, pl.ds(k0, nkw), :]

    def sel_scores(st, z_refs):
        ks = ks_ref[0, pl.ds(pl.multiple_of(st * SEL_TILE, SEL_TILE), SEL_TILE), :]
        for z_ref, qa in zip(z_refs, qas):
            z_ref[...] = _dot_nt(ks, qa)

    zcs = [_dot_nt(kc_ref[0, 0], qa) for qa in qas]
    zws = [_dot_nt(kw, qa) for qa in qas]
    sel_scores(0, za_refs)

    ncp = kc_ref.shape[2]
    cend = lax.broadcasted_iota(jnp.int32, (ncp, 1), 0) * CMP_STRIDE + (CMP_BLOCK - 1)
    valid = cend <= tpos
    o_c, psum = [], None
    for zc in zcs:
        zm = jnp.where(valid, zc, NEG)
        ec = jnp.where(valid, jnp.exp2(zm - jnp.max(zm, axis=0, keepdims=True)), 0.0)
        lc = jnp.sum(ec, axis=0, keepdims=True)
        pc = ec * jnp.where(lc > 0.0, 1.0 / lc, 0.0)
        o_c.append(_dot(vct_ref[0, 0], pc.astype(BF16)))
        for i in range(2):
            ph = pc[:, i * Q_BLOCK:(i + 1) * Q_BLOCK]
            psum = ph if psum is None else psum + ph

    p_hi = psum.astype(BF16)
    p_lo = (psum - p_hi.astype(F32)).astype(BF16)
    imp = _dot(ovt_ref[...], p_hi) + _dot(ovt_ref[...], p_lo)
    ns = imp.shape[0]
    j_i = lax.broadcasted_iota(jnp.int32, (ns, 1), 0)
    t_abs = t0 + lax.broadcasted_iota(jnp.int32, (1, Q_BLOCK), 1)
    cur = t_abs // SLC_BLOCK
    forced = (j_i == 0) | (j_i == cur) | (j_i == cur - 1)
    x = jnp.where(forced, FORCE, imp)
    x = jnp.where(j_i * SLC_BLOCK <= t_abs, x, NEG)
    x_ref[...] = x

    vw = vwt_ref[0, :, pl.ds(k0, nkw)]
    dist = tpos - (k0 + lax.broadcasted_iota(jnp.int32, (nkw, 1), 0))
    okw = (dist >= 0) & (dist < WINDOW)
    o_w = []
    for zw in zws:
        zw = jnp.where(okw, zw, NEG)
        pw = jnp.exp2(zw - jnp.max(zw, axis=0, keepdims=True))
        o_w.append(normalised(_dot(vw, pw.astype(BF16))))

    def rank_body(jp, cnt):
        r = x_ref[pl.ds(jp, 1), :]
        ahead = (r > x) | ((r == x) & (j_i > jp))
        return cnt + jnp.where(ahead, 1.0, 0.0)

    n_causal = (t0 + Q_BLOCK) // SLC_BLOCK
    cnt = lax.fori_loop(0, n_causal, rank_body, jnp.zeros((ns, Q_BLOCK), F32))
    bias_ref[...] = jnp.where(cnt < float(SLC_TOPN), 0.0, NEG)

    def sel_update(st, z_refs, carry, causal):
        s0 = pl.multiple_of(st * SEL_TILE, SEL_TILE)
        vs = vst_ref[0, :, pl.ds(s0, SEL_TILE)]
        brow = bias_ref[pl.ds(pl.multiple_of(st * per_step, per_step), per_step), :]
        b2 = jnp.concatenate([brow] * 2, axis=1)[:, None, :]
        out = []
        for z_ref, (m, acc) in zip(z_refs, carry):
            z = (z_ref[...].reshape(per_step, SLC_BLOCK, nq) + b2).reshape(SEL_TILE, nq)
            if causal:
                z = jnp.where((s0 + spos0) <= tpos, z, NEG)
            mn = jnp.maximum(m, jnp.max(z, axis=0, keepdims=True))
            p = jnp.exp2(z - mn)
            out.append((mn, jnp.exp2(m - mn) * acc + _dot(vs, p.astype(BF16))))
        return tuple(out)

    def two_steps(i, carry):
        sel_scores(2 * i + 1, zb_refs)
        carry = sel_update(2 * i, za_refs, carry, False)
        sel_scores(2 * i + 2, za_refs)
        return sel_update(2 * i + 1, zb_refs, carry, False)

    init = tuple((jnp.full((1, nq), NEG, F32), jnp.zeros((V_SLOT, nq), F32)) for _ in range(npair))
    last = qb // (SEL_TILE // Q_BLOCK)
    carry = lax.fori_loop(0, last // 2, two_steps, init)

    def odd_tail(carry):
        sel_scores(last, zb_refs)
        carry = sel_update(last - 1, za_refs, carry, False)
        return sel_update(last, zb_refs, carry, True)

    carry = lax.cond(last % 2 == 1, odd_tail, lambda c: sel_update(last, za_refs, c, True), carry)
    o_s = [normalised(acc) for _, acc in carry]

    gts = _sigmoid(gate_ref[0])
    pairs = []
    for hp in range(npair):
        halves = []
        for i in range(2):
            r = 3 * (2 * hp + i)
            sl = slice(i * Q_BLOCK, (i + 1) * Q_BLOCK)
            halves.append(gts[r:r + 1] * o_c[hp][:, sl] + gts[r + 1:r + 2] * o_s[hp][:, sl]
                          + gts[r + 2:r + 3] * o_w[hp][:, sl])
        pairs.append(jnp.concatenate(halves, axis=0).T)
    o_ref[0] = jnp.concatenate(pairs, axis=1).astype(o_ref.dtype)


def _nsa(q_slots, gate_t, kc_aug, vc_t, ks_aug, kw_aug, v_t, ov_t, slope_tab):
    b, t, _ = q_slots.shape
    g = NSA_GROUPS
    nqb = t // Q_BLOCK
    ncp = kc_aug.shape[2]
    ns = ov_t.shape[0]
    return pl.pallas_call(
        _nsa_kernel,
        out_shape=jax.ShapeDtypeStruct((b, t, NSA_HEADS * NSA_DH), BF16),
        grid=(b, g, nqb),
        in_specs=[
            pl.BlockSpec((1, Q_BLOCK, NSA_HG * LANES), lambda i, j, s: (i, s, j)),
            pl.BlockSpec((1, GATE_ROWS, Q_BLOCK), lambda i, j, s: (i, j, s)),
            pl.BlockSpec((1, 1, ncp, LANES), lambda i, j, s: (i, j, 0, 0)),
            pl.BlockSpec((1, 1, NSA_DH, ncp), lambda i, j, s: (i, j, 0, 0)),
            pl.BlockSpec((1, t, LANES), lambda i, j, s: (i, 0, j)),
            pl.BlockSpec((1, V_SLOT, t), lambda i, j, s: (i, j, 0)),
            pl.BlockSpec((1, t, LANES), lambda i, j, s: (i, 0, j)),
            pl.BlockSpec((1, V_SLOT, t), lambda i, j, s: (i, g + j, 0)),
            pl.BlockSpec((ns, ncp), lambda i, j, s: (0, 0)),
            pl.BlockSpec((1, NSA_HG * Q_BLOCK, LANES), lambda i, j, s: (j, 0, 0)),
        ],
        out_specs=pl.BlockSpec((1, Q_BLOCK, NSA_HG * NSA_DH), lambda i, j, s: (i, s, j)),
        scratch_shapes=[pltpu.VMEM((ns, Q_BLOCK), F32), pltpu.VMEM((ns, Q_BLOCK), F32)]
        + [pltpu.VMEM((SEL_TILE, 2 * Q_BLOCK), F32)] * NSA_HG,
        compiler_params=pltpu.CompilerParams(
            dimension_semantics=("parallel", "parallel", "arbitrary"), vmem_limit_bytes=VMEM_LIMIT),
        name="nsa",
    )(q_slots, gate_t, kc_aug, vc_t, ks_aug, v_t, kw_aug, v_t, ov_t, slope_tab)


def _layer_norm(z, g, b):
    mu = jnp.mean(z, axis=-1, keepdims=True)
    zc = z - mu
    var = jnp.mean(zc * zc, axis=-1, keepdims=True)
    return zc * lax.rsqrt(var + LN_EPS) * g + b


def _outln_kernel(x_ref, yh_ref, yn_ref, wh_ref, wn_ref, gate_ref, lg_ref, lb_ref, o_ref, *, alpha):
    mix = _dot(yh_ref[0].astype(BF16), wh_ref[...]) + _dot(yn_ref[0].astype(BF16), wn_ref[...])
    z = alpha * x_ref[0] + gate_ref[0] * mix
    o_ref[0] = _layer_norm(z, lg_ref[...], lb_ref[...])


def _outln(x, y_hg, y_nsa, w_h, w_n, ada3, n_mod, ln_g, ln_b, alpha):
    b, t, d = x.shape
    tm = 512
    wh, wn = y_hg.shape[2], y_nsa.shape[2]
    return pl.pallas_call(
        functools.partial(_outln_kernel, alpha=alpha),
        out_shape=jax.ShapeDtypeStruct((b, t, d), F32),
        grid=(b, t // tm),
        in_specs=[
            pl.BlockSpec((1, tm, d), lambda i, j: (i, j, 0)),
            pl.BlockSpec((1, tm, wh), lambda i, j: (i, j, 0)),
            pl.BlockSpec((1, tm, wn), lambda i, j: (i, j, 0)),
            pl.BlockSpec((wh, d), lambda i, j: (0, 0)),
            pl.BlockSpec((wn, d), lambda i, j: (0, 0)),
            pl.BlockSpec((1, 1, d), lambda i, j: (i * n_mod + 2, 0, 0)),
            pl.BlockSpec((1, d), lambda i, j: (0, 0)),
            pl.BlockSpec((1, d), lambda i, j: (0, 0)),
        ],
        out_specs=pl.BlockSpec((1, tm, d), lambda i, j: (i, j, 0)),
        compiler_params=pltpu.CompilerParams(
            dimension_semantics=("parallel", "parallel"), vmem_limit_bytes=VMEM_LIMIT),
        name="outln",
    )(x, y_hg, y_nsa, w_h, w_n, ada3, ln_g, ln_b)


def _ffn_kernel(x_ref, sc_ref, sh_ref, gate_ref, w1_ref, w3_ref, w2_ref, lg_ref, lb_ref, o_ref,
                h_ref, acc_ref, *, alpha):
    c = pl.program_id(2)

    @pl.when(c == 0)
    def _():
        h_ref[...] = (x_ref[0] * (1.0 + sc_ref[0]) + sh_ref[0]).astype(BF16)
        acc_ref[...] = jnp.zeros_like(acc_ref)

    h = h_ref[...]
    u = _dot(h, w1_ref[...])
    w = _dot(h, w3_ref[...])
    a = (u * _sigmoid(u) * w).astype(BF16)
    acc_ref[...] += _dot(a, w2_ref[...])

    @pl.when(c == pl.num_programs(2) - 1)
    def _():
        z = alpha * x_ref[0] + gate_ref[0] * acc_ref[...]
        o_ref[0] = _layer_norm(z, lg_ref[...], lb_ref[...])


def _ffn(x, ada3, n_mod, w1, w3, w2, ln_g, ln_b, alpha):
    b, t, d = x.shape
    dff = w1.shape[1]
    tm = 512
    tf = dff // 2
    return pl.pallas_call(
        functools.partial(_ffn_kernel, alpha=alpha),
        out_shape=jax.ShapeDtypeStruct((b, t, d), F32),
        grid=(b, t // tm, dff // tf),
        in_specs=[
            pl.BlockSpec((1, tm, d), lambda i, j, c: (i, j, 0)),
            pl.BlockSpec((1, 1, d), lambda i, j, c: (i * n_mod + 4, 0, 0)),
            pl.BlockSpec((1, 1, d), lambda i, j, c: (i * n_mod + 3, 0, 0)),
            pl.BlockSpec((1, 1, d), lambda i, j, c: (i * n_mod + 5, 0, 0)),
            pl.BlockSpec((d, tf), lambda i, j, c: (0, c)),
            pl.BlockSpec((d, tf), lambda i, j, c: (0, c)),
            pl.BlockSpec((tf, d), lambda i, j, c: (c, 0)),
            pl.BlockSpec((1, d), lambda i, j, c: (0, 0)),
            pl.BlockSpec((1, d), lambda i, j, c: (0, 0)),
        ],
        out_specs=pl.BlockSpec((1, tm, d), lambda i, j, c: (i, j, 0)),
        scratch_shapes=[pltpu.VMEM((tm, d), BF16), pltpu.VMEM((tm, d), F32)],
        compiler_params=pltpu.CompilerParams(
            dimension_semantics=("parallel", "parallel", "arbitrary"), vmem_limit_bytes=VMEM_LIMIT),
        name="ffn",
    )(x, ada3, ada3, ada3, w1, w3, w2, ln_g, ln_b)


def _slope_table(group_slopes):
    g, hg = group_slopes.shape
    rest = (group_slopes * LOG2E).astype(np.float32)
    tab = np.zeros((g, hg, Q_BLOCK, LANES), jnp.bfloat16)
    for i in range(N_SPLIT):
        piece = rest.astype(jnp.bfloat16)
        rest = rest - piece.astype(np.float32)
        tab[..., NSA_DH + 2 * i] = piece[:, :, None]
        tab[..., NSA_DH + 2 * i + 1] = piece[:, :, None]
    return tab.reshape(g, hg * Q_BLOCK, LANES)


def _pad_slots(w, n_slots):
    d = w.shape[0]
    w3 = w.reshape(d, n_slots, -1)
    return jnp.pad(w3, ((0, 0), (0, 0), (0, LANES - w3.shape[2]))).reshape(d, n_slots * LANES)


def _cmp_w1_expand(w1, g):
    hid = w1.shape[1]
    w1r = w1.reshape(2, CMP_STRIDE, 1, NSA_DH, hid)
    eye = jnp.eye(g, dtype=w1.dtype).reshape(g, 1, 1, g, 1, 1)
    return (eye * w1r[None]).reshape(g, 2 * CMP_STRIDE * g * NSA_DH, hid)


def _overlap_t(n_cmp_pad, n_cmp, n_slc):
    cs = np.arange(n_cmp_pad) * CMP_STRIDE
    ss = np.arange(n_slc) * SLC_BLOCK
    ov = np.clip(np.minimum(cs[:, None] + CMP_BLOCK, ss[None] + SLC_BLOCK)
                 - np.maximum(cs[:, None], ss[None]), 0, None).astype(np.float32) / CMP_BLOCK
    ov[n_cmp:] = 0.0
    return ov.T


def kernel(x, c, w_ada, b_ada, w_in, hg_lb_logits, hg_norm_g, cmp_pos_k, cmp_w1_k, cmp_w2_k,
           cmp_pos_v, cmp_w1_v, cmp_w2_v, w_out, ln1_g, ln1_b, ffn_w1, ffn_w3, ffn_w2, ln2_g, ln2_b):
    bsz, t, d = x.shape
    depth = w_ada.shape[0]
    g, hg, dh = NSA_GROUPS, NSA_HG, NSA_DH
    hgw = HG_HEADS * HG_D
    nsw = NSA_HEADS * dh
    kvw = g * dh
    n_mod = 6
    alpha = (2.0 * depth) ** 0.25
    scale = dh ** -0.5
    n_grp = t // CMP_STRIDE
    n_cmp = (t - CMP_BLOCK) // CMP_STRIDE + 1
    n_slc = t // SLC_BLOCK
    assert t % SEL_TILE == 0 and t >= WINDOW + Q_BLOCK and 3 * hg <= GATE_ROWS and hg % 2 == 0

    slopes = (2.0 ** (-8.0 * (np.arange(NSA_HEADS) + 1) / NSA_HEADS)).reshape(g, hg)
    slope_tab = jnp.asarray(_slope_table(slopes), BF16)
    ov_t = jnp.asarray(_overlap_t(n_grp, n_cmp, n_slc), BF16)

    c_pad = jnp.pad(c, ((0, 8 - bsz), (0, 0)))
    for l in range(depth):
        ada = _ada(c_pad, w_ada[l], b_ada[l][None])
        ada3 = ada[:bsz].reshape(bsz * n_mod, 1, d)

        w = w_in[l]
        o = 4 * hgw
        w_q = w[:, o:o + nsw]
        w_kv = [w[:, o + nsw + i * kvw: o + nsw + (i + 1) * kvw] for i in range(6)]
        w_g = w[:, o + nsw + 6 * kvw: o + nsw + 6 * kvw + 3 * NSA_HEADS]
        w_main = jnp.concatenate(
            [w[:, :o], _pad_slots(w_q * (scale * LOG2E), NSA_HEADS), _pad_slots(w_kv[2], g),
             _pad_slots(w_kv[4], g), w_kv[0], w_kv[1]], axis=1).astype(BF16)
        w_gt = jnp.pad(w_g.reshape(d, g, 3 * hg), ((0, 0), (0, 0), (0, GATE_ROWS - 3 * hg))).reshape(d, -1)
        v_slots = lambda a: jnp.pad(a.reshape(d, g, dh), ((0, 0), (0, 0), (0, V_SLOT - dh))).reshape(d, -1)
        w_t = jnp.concatenate([v_slots(w_kv[3]), v_slots(w_kv[5]), w_gt], axis=1).T.astype(BF16)
        widths = (o, NSA_HEADS * LANES, g * LANES, kvw, 2 * g * V_SLOT, g * GATE_ROWS)
        hg_in, q_slots, ks_aug, kw_aug, kc, vc, v_t, gate_t = _inproj(x, ada3, w_main, w_t, n_mod, widths)

        y_hg = _hgrn(hg_in, hg_lb_logits, hg_norm_g[l][None], l)

        grp = lambda a: a.reshape(bsz, n_grp, CMP_STRIDE * kvw)
        pos_e = lambda p: jnp.broadcast_to(
            p.reshape(2, CMP_STRIDE, 1, dh), (2, CMP_STRIDE, g, dh)).reshape(1, -1)
        w2k = jnp.pad(cmp_w2_k[l], ((0, 0), (0, LANES - dh))).astype(BF16)
        kc_aug = _compress(grp(kc), pos_e(cmp_pos_k[l]), _cmp_w1_expand(cmp_w1_k[l], g).astype(BF16),
                           w2k, True)
        vc_t = _compress(grp(vc), pos_e(cmp_pos_v[l]), _cmp_w1_expand(cmp_w1_v[l], g).astype(BF16),
                         cmp_w2_v[l].T.astype(BF16), False)

        y_nsa = _nsa(q_slots, gate_t, kc_aug, vc_t, ks_aug, kw_aug, v_t, ov_t, slope_tab)

        w_o = w_out[l].astype(BF16)
        x = _outln(x, y_hg, y_nsa, w_o[:hgw], w_o[hgw:], ada3, n_mod, ln1_g[l][None], ln1_b[l][None], alpha)
        x = _ffn(x, ada3, n_mod, ffn_w1[l].astype(BF16), ffn_w3[l].astype(BF16), ffn_w2[l].astype(BF16),
                 ln2_g[l][None], ln2_b[l][None], alpha)
    return x
```

```python
import functools

import numpy as np
import jax
import jax.numpy as jnp
from jax import lax
from jax.experimental import pallas as pl
from jax.experimental.pallas import tpu as pltpu

F32 = jnp.float32
BF16 = jnp.bfloat16

HG_HEADS = 4
HG_D = 128
NSA_HEADS = 8
NSA_GROUPS = 2
NSA_HG = NSA_HEADS // NSA_GROUPS
NSA_DH = 64
CMP_BLOCK = 32
CMP_STRIDE = 16
CMP_HIDDEN = 256
SLC_BLOCK = 64
SLC_TOPN = 16
WINDOW = 512
Q_BLOCK = 128
NEG = -1e30
FORCE = 1e4
LN_EPS = 1e-5
RMS_EPS = 1e-6

LANES = 128
SUBLANES = 8
HG_BLK = 16
POS_SPLIT = 256
GATE_ROWS = 16
SEL_TILE = 512
V_SLOT = 80
N_SPLIT = 3
LOG2E = 1.4426950408889634
VMEM_LIMIT = 48 * 1024 * 1024


def _sigmoid(x):
    return 1.0 / (1.0 + jnp.exp(-x))


def _dot(a, b):
    return jnp.dot(a, b, preferred_element_type=F32)


def _dot_nt(a, b):
    return lax.dot_general(a, b, (((1,), (1,)), ((), ())), preferred_element_type=F32)


def _pos_columns(pos, col):
    c = col - NSA_DH
    lo = pos % POS_SPLIT
    val = jnp.where(c % 2 == 0, lo, pos - lo)
    return jnp.where((c >= 0) & (c < 2 * N_SPLIT), val, 0).astype(F32)


def _split3(x):
    hi = x.astype(BF16)
    r = x - hi.astype(F32)
    mid = r.astype(BF16)
    lo = (r - mid.astype(F32)).astype(BF16)
    return hi, mid, lo


def _ada_kernel(c_ref, w_ref, b_ref, o_ref):
    c = c_ref[...]
    s = (c * _sigmoid(c)).astype(BF16)
    o_ref[...] = _dot(s, w_ref[...].astype(BF16)) + b_ref[...]


def _ada(c_pad, w, b):
    rows, d = c_pad.shape
    n = w.shape[1]
    tn = 1024
    return pl.pallas_call(
        _ada_kernel,
        out_shape=jax.ShapeDtypeStruct((rows, n), F32),
        grid=(n // tn,),
        in_specs=[
            pl.BlockSpec((rows, d), lambda j: (0, 0)),
            pl.BlockSpec((d, tn), lambda j: (0, j)),
            pl.BlockSpec((1, tn), lambda j: (0, j)),
        ],
        out_specs=pl.BlockSpec((rows, tn), lambda j: (0, j)),
        compiler_params=pltpu.CompilerParams(
            dimension_semantics=("parallel",), vmem_limit_bytes=VMEM_LIMIT),
        name="ada",
    )(c_pad, w, b)


def _inproj_kernel(x_ref, sc_ref, sh_ref, wm_ref, wt_ref,
                   hg_ref, q_ref, ks_ref, kw_ref, kc_ref, vc_ref, vt_ref, gt_ref, *, tm):
    h = (x_ref[0] * (1.0 + sc_ref[0]) + sh_ref[0]).astype(BF16)
    y = _dot(h, wm_ref[...])
    n_hg = hg_ref.shape[2]
    n_q = q_ref.shape[2]
    n_k = ks_ref.shape[2]
    n_c = kc_ref.shape[2]
    o = 0
    hg_ref[0] = y[:, o:o + n_hg]
    o += n_hg
    q_ref[0] = y[:, o:o + n_q].astype(BF16)
    o += n_q
    row = pl.program_id(1) * tm + lax.broadcasted_iota(jnp.int32, (tm, n_k), 0)
    pc = _pos_columns(row, lax.broadcasted_iota(jnp.int32, (tm, n_k), 1) % LANES)
    ks_ref[0] = (y[:, o:o + n_k] + pc).astype(BF16)
    o += n_k
    kw_ref[0] = (y[:, o:o + n_k] + pc).astype(BF16)
    o += n_k
    kc_ref[0] = y[:, o:o + n_c]
    o += n_c
    vc_ref[0] = y[:, o:o + n_c]
    yt = _dot_nt(wt_ref[...], h)
    n_vt = vt_ref.shape[1]
    ri = lax.broadcasted_iota(jnp.int32, (n_vt, tm), 0) % V_SLOT
    vt_ref[0] = (yt[:n_vt] + jnp.where(ri == NSA_DH, 1.0, 0.0)).astype(BF16)
    gt_ref[0] = yt[n_vt:]


def _inproj(x, ada3, w_main, w_t, n_mod, widths):
    b, t, d = x.shape
    n_hg, n_q, n_k, n_c, n_vt, n_gt = widths
    tm = 256
    row_spec = lambda n: pl.BlockSpec((1, tm, n), lambda i, j: (i, j, 0))
    col_spec = lambda n: pl.BlockSpec((1, n, tm), lambda i, j: (i, 0, j))
    sds = jax.ShapeDtypeStruct
    return pl.pallas_call(
        functools.partial(_inproj_kernel, tm=tm),
        out_shape=(sds((b, t, n_hg), F32), sds((b, t, n_q), BF16), sds((b, t, n_k), BF16),
                   sds((b, t, n_k), BF16), sds((b, t, n_c), F32), sds((b, t, n_c), F32),
                   sds((b, n_vt, t), BF16), sds((b, n_gt, t), F32)),
        grid=(b, t // tm),
        in_specs=[
            pl.BlockSpec((1, tm, d), lambda i, j: (i, j, 0)),
            pl.BlockSpec((1, 1, d), lambda i, j: (i * n_mod + 1, 0, 0)),
            pl.BlockSpec((1, 1, d), lambda i, j: (i * n_mod, 0, 0)),
            pl.BlockSpec(w_main.shape, lambda i, j: (0, 0)),
            pl.BlockSpec(w_t.shape, lambda i, j: (0, 0)),
        ],
        out_specs=(row_spec(n_hg), row_spec(n_q), row_spec(n_k), row_spec(n_k), row_spec(n_c),
                   row_spec(n_c), col_spec(n_vt), col_spec(n_gt)),
        compiler_params=pltpu.CompilerParams(
            dimension_semantics=("parallel", "parallel"), vmem_limit_bytes=VMEM_LIMIT),
        name="inproj",
    )(x, ada3, ada3, w_main, w_t)


def _hgrn_kernel(q_ref, f_ref, i_ref, g_ref, lbl_ref, ng_ref, e_ref, o_ref, st_ref, *, tc, layer):
    blk = HG_BLK
    per_grp = LANES // blk
    half = blk // 2

    @pl.when(pl.program_id(2) == 0)
    def _():
        st_ref[...] = jnp.zeros_like(st_ref)

    lbl = lbl_ref[...]
    e = jnp.exp(lbl - jnp.max(lbl, axis=0, keepdims=True))
    lb = jnp.sum(e[: layer + 1], axis=0, keepdims=True) / jnp.sum(e, axis=0, keepdims=True)

    r_i = lax.broadcasted_iota(jnp.int32, (LANES, LANES), 0)
    c_i = lax.broadcasted_iota(jnp.int32, (LANES, LANES), 1)
    same_blk = (r_i // blk) == (c_i // blk)
    tri = jnp.where(same_blk & (c_i <= r_i), 1.0, 0.0).astype(BF16)
    row_blk = r_i // blk
    nh = per_grp * half
    r_h = lax.broadcasted_iota(jnp.int32, (nh, LANES), 0)
    same_half = (r_h // half) == (lax.broadcasted_iota(jnp.int32, (nh, LANES), 1) // blk)
    row_in_half = r_h % half
    blocked = lambda a: a.reshape(per_grp, blk, LANES)

    def gates(g):
        sl = slice(g * LANES, (g + 1) * LANES)
        q = q_ref[0, sl, :]
        qs = q * _sigmoid(q)
        f = lb + (1.0 - lb) * _sigmoid(f_ref[0, sl, :])
        hi, mid, lo = _split3(jnp.log2(f))
        b = _dot(tri, hi) + _dot(tri, mid) + _dot(tri, lo)
        return qs, 1.0 - f, i_ref[0, sl, :], b

    def increments(qs, k, v, b):
        b3 = blocked(b)
        bl3 = b3[:, blk - 1:blk, :]
        qd = (qs * jnp.exp2(b)).astype(BF16)
        kd = (blocked(k) * jnp.exp2(bl3 - b3)).reshape(LANES, LANES)
        kexp = jnp.concatenate(
            [jnp.where(row_blk == j, kd, 0.0) for j in range(per_grp)], axis=1).astype(BF16)
        return qd, jnp.exp2(bl3), _dot(v.T.astype(BF16), kexp)

    def recur(st, qd, dl, ut):
        ois = []
        for j in range(per_grp):
            ois.append(_dot_nt(qd[j * blk:(j + 1) * blk], st.astype(BF16)))
            st = dl[j] * st + ut[:, j * LANES:(j + 1) * LANES]
        return st, jnp.concatenate(ois, axis=0)

    def pairwise(qs, k, b):
        c = b - jnp.log2(k)
        tiles = lambda a, off: jnp.concatenate(
            [a[j * blk + off:j * blk + off + half] for j in range(per_grp)], axis=0)
        q_lo, q_hi, b_lo, b_hi = tiles(qs, 0), tiles(qs, half), tiles(b, 0), tiles(b, half)
        p_lo, p_hi = [], []
        for s in range(blk):
            cs = jnp.concatenate([jnp.broadcast_to(c[j * blk + s:j * blk + s + 1], (half, LANES))
                                  for j in range(per_grp)], axis=0)
            if s < half:
                p_lo.append(jnp.where(row_in_half >= s, q_lo * jnp.exp2(b_lo - cs), 0.0).astype(BF16))
                p_hi.append((q_hi * jnp.exp2(b_hi - cs)).astype(BF16))
            else:
                p_hi.append(jnp.where(row_in_half >= s - half, q_hi * jnp.exp2(b_hi - cs), 0.0).astype(BF16))
        return jnp.concatenate(p_lo, axis=1), jnp.concatenate(p_hi, axis=1)

    def finish(g, v, a_lo, a_hi, o_inter):
        vb = v.astype(BF16)
        o_lo = _dot(jnp.where(same_half, a_lo, 0.0).astype(BF16), vb)
        o_hi = _dot(jnp.where(same_half, a_hi, 0.0).astype(BF16), vb)
        o = jnp.concatenate([x[j * half:(j + 1) * half] for j in range(per_grp) for x in (o_lo, o_hi)],
                            axis=0) + o_inter
        o = o * lax.rsqrt(jnp.mean(o * o, axis=-1, keepdims=True) + RMS_EPS)
        sl = slice(g * LANES, (g + 1) * LANES)
        o_ref[0, sl, :] = (o * ng_ref[...] * _sigmoid(g_ref[0, sl, :])).astype(o_ref.dtype)

    ngrp = tc // LANES
    gs = [gates(g) for g in range(ngrp)]
    incs = [increments(*gt) for gt in gs]
    st = st_ref[...]
    o_inters = []
    for inc in incs:
        st, oi = recur(st, *inc)
        o_inters.append(oi)
    st_ref[...] = st
    pws = [pairwise(qs, k, b) for qs, k, _, b in gs]
    a_lo = _dot(jnp.concatenate([p[0] for p in pws], axis=0), e_ref[:half * LANES, :])
    a_hi = _dot(jnp.concatenate([p[1] for p in pws], axis=0), e_ref[...])
    for g in range(ngrp):
        rows = slice(g * nh, (g + 1) * nh)
        finish(g, gs[g][2], a_lo[rows], a_hi[rows], o_inters[g])


def _hgrn(proj, lb_logits, norm_g, layer):
    b, t, _ = proj.shape
    tc = 512
    h = HG_HEADS
    nslots = lb_logits.shape[0]
    col = lambda seg: (lambda i, j, s: (i, s, seg * h + j))
    fold = np.zeros((HG_BLK, LANES, LANES), np.float32)
    for s in range(HG_BLK):
        fold[s, :, s::HG_BLK] = 1.0
    fold = jnp.asarray(fold.reshape(HG_BLK * LANES, LANES), BF16)
    return pl.pallas_call(
        functools.partial(_hgrn_kernel, tc=tc, layer=layer),
        out_shape=jax.ShapeDtypeStruct((b, t, h * HG_D), BF16),
        grid=(b, h, t // tc),
        in_specs=[
            pl.BlockSpec((1, tc, HG_D), col(0)),
            pl.BlockSpec((1, tc, HG_D), col(1)),
            pl.BlockSpec((1, tc, HG_D), col(2)),
            pl.BlockSpec((1, tc, HG_D), col(3)),
            pl.BlockSpec((nslots, HG_D), lambda i, j, s: (0, j)),
            pl.BlockSpec((1, HG_D), lambda i, j, s: (0, j)),
            pl.BlockSpec(fold.shape, lambda i, j, s: (0, 0)),
        ],
        out_specs=pl.BlockSpec((1, tc, HG_D), lambda i, j, s: (i, s, j)),
        scratch_shapes=[pltpu.VMEM((HG_D, HG_D), F32)],
        compiler_params=pltpu.CompilerParams(
            dimension_semantics=("parallel", "parallel", "arbitrary"), vmem_limit_bytes=VMEM_LIMIT),
        name="hgrn",
    )(proj, proj, proj, proj, lb_logits, norm_g, fold)


def _cmp_kernel(h_ref, pos_ref, w1_ref, w2_ref, o_ref, *, key_side):
    h = h_ref[0]
    half = h.shape[1]
    pos = pos_ref[...]
    p0 = _dot((h + pos[:, :half]).astype(BF16), w1_ref[0, :half])
    p1 = _dot((h + pos[:, half:]).astype(BF16), w1_ref[0, half:])
    ng = h.shape[0]
    pre = p0 + pltpu.roll(p1, ng - 1, 0)
    gl = 0.5 * pre * (1.0 + jnp.tanh(np.sqrt(2.0 / np.pi) * (pre + 0.044715 * (pre * pre * pre))))
    gl = gl.astype(BF16)
    if key_side:
        cend = lax.broadcasted_iota(jnp.int32, (ng, LANES), 0) * CMP_STRIDE + (CMP_BLOCK - 1)
        pc = _pos_columns(cend, lax.broadcasted_iota(jnp.int32, (ng, LANES), 1))
        o_ref[0, 0] = (_dot(gl, w2_ref[...]) + pc).astype(BF16)
    else:
        o_ref[0, 0] = _dot_nt(w2_ref[...], gl).astype(BF16)


def _compress(hgrp, pos, w1, w2, key_side):
    b, ng, half = hgrp.shape
    g = w1.shape[0]
    out_tail = (ng, LANES) if key_side else (w2.shape[0], ng)
    return pl.pallas_call(
        functools.partial(_cmp_kernel, key_side=key_side),
        out_shape=jax.ShapeDtypeStruct((b, g) + out_tail, BF16),
        grid=(b, g),
        in_specs=[
            pl.BlockSpec((1, ng, half), lambda i, j: (i, 0, 0)),
            pl.BlockSpec(pos.shape, lambda i, j: (0, 0)),
            pl.BlockSpec((1,) + w1.shape[1:], lambda i, j: (j, 0, 0)),
            pl.BlockSpec(w2.shape, lambda i, j: (0, 0)),
        ],
        out_specs=pl.BlockSpec((1, 1) + out_tail, lambda i, j: (i, j, 0, 0)),
        compiler_params=pltpu.CompilerParams(
            dimension_semantics=("parallel", "parallel"), vmem_limit_bytes=VMEM_LIMIT),
        name="cmp_k" if key_side else "cmp_v",
    )(hgrp, pos, w1, w2)


def _nsa_kernel(q_ref, gate_ref, kc_ref, vct_ref, ks_ref, vst_ref, kw_ref, vwt_ref, ovt_ref, slope_ref,
                o_ref, bias_ref, *z_scratch):
    qb = pl.program_id(2)
    t0 = qb * Q_BLOCK
    npair = NSA_HG // 2
    za_refs, zb_refs = z_scratch[:npair], z_scratch[npair:]
    nq = 2 * Q_BLOCK
    qblk = q_ref[0]
    qas = []
    for hp in range(npair):
        qa = jnp.concatenate([qblk[:, h * LANES:(h + 1) * LANES] for h in (2 * hp, 2 * hp + 1)], axis=0)
        qas.append(qa + slope_ref[0, hp * nq:(hp + 1) * nq])
    tpos = t0 + (lax.broadcasted_iota(jnp.int32, (1, nq), 1) % Q_BLOCK)

    def normalised(acc):
        return acc[:NSA_DH] * (1.0 / acc[NSA_DH:NSA_DH + 1])

    per_step = SEL_TILE // SLC_BLOCK
    spos0 = lax.broadcasted_iota(jnp.int32, (SEL_TILE, 1), 0)
    nkw = WINDOW + Q_BLOCK
    k0 = pl.multiple_of(jnp.maximum(t0 - WINDOW, 0), Q_BLOCK)
    kw = kw_ref[0, pl.ds(k0, nkw), :]

    def sel_scores(st, z_refs):
        ks = ks_ref[0, pl.ds(pl.multiple_of(st * SEL_TILE, SEL_TILE), SEL_TILE), :]
        for z_ref, qa in zip(z_refs, qas):
            z_ref[...] = _dot_nt(ks, qa)

    zcs = [_dot_nt(kc_ref[0, 0], qa) for qa in qas]
    zws = [_dot_nt(kw, qa) for qa in qas]
    sel_scores(0, za_refs)

    ncp = kc_ref.shape[2]
    cend = lax.broadcasted_iota(jnp.int32, (ncp, 1), 0) * CMP_STRIDE + (CMP_BLOCK - 1)
    valid = cend <= tpos
    o_c, psum = [], None
    for zc in zcs:
        zm = jnp.where(valid, zc, NEG)
        ec = jnp.where(valid, jnp.exp2(zm - jnp.max(zm, axis=0, keepdims=True)), 0.0)
        lc = jnp.sum(ec, axis=0, keepdims=True)
        pc = ec * jnp.where(lc > 0.0, 1.0 / lc, 0.0)
        o_c.append(_dot(vct_ref[0, 0], pc.astype(BF16)))
        for i in range(2):
            ph = pc[:, i * Q_BLOCK:(i + 1) * Q_BLOCK]
            psum = ph if psum is None else psum + ph

    p_hi = psum.astype(BF16)
    p_lo = (psum - p_hi.astype(F32)).astype(BF16)
    imp = _dot(ovt_ref[...], p_hi) + _dot(ovt_ref[...], p_lo)
    ns = imp.shape[0]
    j_i = lax.broadcasted_iota(jnp.int32, (ns, 1), 0)
    t_abs = t0 + lax.broadcasted_iota(jnp.int32, (1, Q_BLOCK), 1)
    cur = t_abs // SLC_BLOCK
    forced = (j_i == 0) | (j_i == cur) | (j_i == cur - 1)
    x = jnp.where(forced, FORCE, imp)
    x = jnp.where(j_i * SLC_BLOCK <= t_abs, x, NEG)

    vw = vwt_ref[0, :, pl.ds(k0, nkw)]
    dist = tpos - (k0 + lax.broadcasted_iota(jnp.int32, (nkw, 1), 0))
    okw = (dist >= 0) & (dist < WINDOW)
    o_w = []
    for zw in zws:
        zw = jnp.where(okw, zw, NEG)
        pw = jnp.exp2(zw - jnp.max(zw, axis=0, keepdims=True))
        o_w.append(normalised(_dot(vw, pw.astype(BF16))))

    sub = lax.broadcasted_iota(jnp.int32, (SUBLANES, Q_BLOCK), 0)
    tiles = [x[v * SUBLANES:(v + 1) * SUBLANES] for v in range(ns // SUBLANES)]
    cnts = [jnp.zeros((SUBLANES, Q_BLOCK), F32) for _ in tiles]
    for jp in range(ns):
        r = x[jp:jp + 1, :]
        vj, rj = divmod(jp, SUBLANES)
        for v, xt in enumerate(tiles):
            if v < vj:
                ahead = r > xt
            elif v > vj:
                ahead = r >= xt
            else:
                ahead = (r > xt) | ((r == xt) & (sub > rj))
            cnts[v] = cnts[v] + jnp.where(ahead, 1.0, 0.0)
    for v, cnt in enumerate(cnts):
        bias_ref[v * SUBLANES:(v + 1) * SUBLANES, :] = jnp.where(cnt < float(SLC_TOPN), 0.0, NEG)

    def sel_update(st, z_refs, carry, causal):
        s0 = pl.multiple_of(st * SEL_TILE, SEL_TILE)
        vs = vst_ref[0, :, pl.ds(s0, SEL_TILE)]
        brow = bias_ref[pl.ds(pl.multiple_of(st * per_step, per_step), per_step), :]
        b2 = jnp.concatenate([brow] * 2, axis=1)[:, None, :]
        out = []
        for z_ref, (m, acc) in zip(z_refs, carry):
            z = (z_ref[...].reshape(per_step, SLC_BLOCK, nq) + b2).reshape(SEL_TILE, nq)
            if causal:
                z = jnp.where((s0 + spos0) <= tpos, z, NEG)
            mn = jnp.maximum(m, jnp.max(z, axis=0, keepdims=True))
            p = jnp.exp2(z - mn)
            out.append((mn, jnp.exp2(m - mn) * acc + _dot(vs, p.astype(BF16))))
        return tuple(out)

    def two_steps(i, carry):
        sel_scores(2 * i + 1, zb_refs)
        carry = sel_update(2 * i, za_refs, carry, False)
        sel_scores(2 * i + 2, za_refs)
        return sel_update(2 * i + 1, zb_refs, carry, False)

    init = tuple((jnp.full((1, nq), NEG, F32), jnp.zeros((V_SLOT, nq), F32)) for _ in range(npair))
    last = qb // (SEL_TILE // Q_BLOCK)
    carry = lax.fori_loop(0, last // 2, two_steps, init)

    def odd_tail(carry):
        sel_scores(last, zb_refs)
        carry = sel_update(last - 1, za_refs, carry, False)
        return sel_update(last, zb_refs, carry, True)

    carry = lax.cond(last % 2 == 1, odd_tail, lambda c: sel_update(last, za_refs, c, True), carry)
    o_s = [normalised(acc) for _, acc in carry]

    gts = _sigmoid(gate_ref[0])
    pairs = []
    for hp in range(npair):
        halves = []
        for i in range(2):
            r = 3 * (2 * hp + i)
            sl = slice(i * Q_BLOCK, (i + 1) * Q_BLOCK)
            halves.append(gts[r:r + 1] * o_c[hp][:, sl] + gts[r + 1:r + 2] * o_s[hp][:, sl]
                          + gts[r + 2:r + 3] * o_w[hp][:, sl])
        pairs.append(jnp.concatenate(halves, axis=0).T)
    o_ref[0] = jnp.concatenate(pairs, axis=1).astype(o_ref.dtype)


def _nsa(q_slots, gate_t, kc_aug, vc_t, ks_aug, kw_aug, v_t, ov_t, slope_tab):
    b, t, _ = q_slots.shape
    g = NSA_GROUPS
    nqb = t // Q_BLOCK
    ncp = kc_aug.shape[2]
    ns = ov_t.shape[0]
    return pl.pallas_call(
        _nsa_kernel,
        out_shape=jax.ShapeDtypeStruct((b, t, NSA_HEADS * NSA_DH), BF16),
        grid=(b, g, nqb),
        in_specs=[
            pl.BlockSpec((1, Q_BLOCK, NSA_HG * LANES), lambda i, j, s: (i, s, j)),
            pl.BlockSpec((1, GATE_ROWS, Q_BLOCK), lambda i, j, s: (i, j, s)),
            pl.BlockSpec((1, 1, ncp, LANES), lambda i, j, s: (i, j, 0, 0)),
            pl.BlockSpec((1, 1, NSA_DH, ncp), lambda i, j, s: (i, j, 0, 0)),
            pl.BlockSpec((1, t, LANES), lambda i, j, s: (i, 0, j)),
            pl.BlockSpec((1, V_SLOT, t), lambda i, j, s: (i, j, 0)),
            pl.BlockSpec((1, t, LANES), lambda i, j, s: (i, 0, j)),
            pl.BlockSpec((1, V_SLOT, t), lambda i, j, s: (i, g + j, 0)),
            pl.BlockSpec((ns, ncp), lambda i, j, s: (0, 0)),
            pl.BlockSpec((1, NSA_HG * Q_BLOCK, LANES), lambda i, j, s: (j, 0, 0)),
        ],
        out_specs=pl.BlockSpec((1, Q_BLOCK, NSA_HG * NSA_DH), lambda i, j, s: (i, s, j)),
        scratch_shapes=[pltpu.VMEM((ns, Q_BLOCK), F32)]
        + [pltpu.VMEM((SEL_TILE, 2 * Q_BLOCK), F32)] * NSA_HG,
        compiler_params=pltpu.CompilerParams(
            dimension_semantics=("parallel", "parallel", "arbitrary"), vmem_limit_bytes=VMEM_LIMIT),
        name="nsa",
    )(q_slots, gate_t, kc_aug, vc_t, ks_aug, v_t, kw_aug, v_t, ov_t, slope_tab)


def _layer_norm(z, g, b):
    mu = jnp.mean(z, axis=-1, keepdims=True)
    zc = z - mu
    var = jnp.mean(zc * zc, axis=-1, keepdims=True)
    return zc * lax.rsqrt(var + LN_EPS) * g + b


def _outln_kernel(x_ref, yh_ref, yn_ref, wh_ref, wn_ref, gate_ref, lg_ref, lb_ref, o_ref, *, alpha):
    mix = _dot(yh_ref[0].astype(BF16), wh_ref[...]) + _dot(yn_ref[0].astype(BF16), wn_ref[...])
    z = alpha * x_ref[0] + gate_ref[0] * mix
    o_ref[0] = _layer_norm(z, lg_ref[...], lb_ref[...])


def _outln(x, y_hg, y_nsa, w_h, w_n, ada3, n_mod, ln_g, ln_b, alpha):
    b, t, d = x.shape
    tm = 512
    wh, wn = y_hg.shape[2], y_nsa.shape[2]
    return pl.pallas_call(
        functools.partial(_outln_kernel, alpha=alpha),
        out_shape=jax.ShapeDtypeStruct((b, t, d), F32),
        grid=(b, t // tm),
        in_specs=[
            pl.BlockSpec((1, tm, d), lambda i, j: (i, j, 0)),
            pl.BlockSpec((1, tm, wh), lambda i, j: (i, j, 0)),
            pl.BlockSpec((1, tm, wn), lambda i, j: (i, j, 0)),
            pl.BlockSpec((wh, d), lambda i, j: (0, 0)),
            pl.BlockSpec((wn, d), lambda i, j: (0, 0)),
            pl.BlockSpec((1, 1, d), lambda i, j: (i * n_mod + 2, 0, 0)),
            pl.BlockSpec((1, d), lambda i, j: (0, 0)),
            pl.BlockSpec((1, d), lambda i, j: (0, 0)),
        ],
        out_specs=pl.BlockSpec((1, tm, d), lambda i, j: (i, j, 0)),
        compiler_params=pltpu.CompilerParams(
            dimension_semantics=("parallel", "parallel"), vmem_limit_bytes=VMEM_LIMIT),
        name="outln",
    )(x, y_hg, y_nsa, w_h, w_n, ada3, ln_g, ln_b)


def _ffn_kernel(x_ref, sc_ref, sh_ref, gate_ref, w1_ref, w3_ref, w2_ref, lg_ref, lb_ref, o_ref,
                h_ref, acc_ref, *, alpha):
    c = pl.program_id(2)

    @pl.when(c == 0)
    def _():
        h_ref[...] = (x_ref[0] * (1.0 + sc_ref[0]) + sh_ref[0]).astype(BF16)
        acc_ref[...] = jnp.zeros_like(acc_ref)

    h = h_ref[...]
    u = _dot(h, w1_ref[...])
    w = _dot(h, w3_ref[...])
    a = (u * _sigmoid(u) * w).astype(BF16)
    acc_ref[...] += _dot(a, w2_ref[...])

    @pl.when(c == pl.num_programs(2) - 1)
    def _():
        z = alpha * x_ref[0] + gate_ref[0] * acc_ref[...]
        o_ref[0] = _layer_norm(z, lg_ref[...], lb_ref[...])


def _ffn(x, ada3, n_mod, w1, w3, w2, ln_g, ln_b, alpha):
    b, t, d = x.shape
    dff = w1.shape[1]
    tm = 512
    tf = dff // 2
    return pl.pallas_call(
        functools.partial(_ffn_kernel, alpha=alpha),
        out_shape=jax.ShapeDtypeStruct((b, t, d), F32),
        grid=(b, t // tm, dff // tf),
        in_specs=[
            pl.BlockSpec((1, tm, d), lambda i, j, c: (i, j, 0)),
            pl.BlockSpec((1, 1, d), lambda i, j, c: (i * n_mod + 4, 0, 0)),
            pl.BlockSpec((1, 1, d), lambda i, j, c: (i * n_mod + 3, 0, 0)),
            pl.BlockSpec((1, 1, d), lambda i, j, c: (i * n_mod + 5, 0, 0)),
            pl.BlockSpec((d, tf), lambda i, j, c: (0, c)),
            pl.BlockSpec((d, tf), lambda i, j, c: (0, c)),
            pl.BlockSpec((tf, d), lambda i, j, c: (c, 0)),
            pl.BlockSpec((1, d), lambda i, j, c: (0, 0)),
            pl.BlockSpec((1, d), lambda i, j, c: (0, 0)),
        ],
        out_specs=pl.BlockSpec((1, tm, d), lambda i, j, c: (i, j, 0)),
        scratch_shapes=[pltpu.VMEM((tm, d), BF16), pltpu.VMEM((tm, d), F32)],
        compiler_params=pltpu.CompilerParams(
            dimension_semantics=("parallel", "parallel", "arbitrary"), vmem_limit_bytes=VMEM_LIMIT),
        name="ffn",
    )(x, ada3, ada3, ada3, w1, w3, w2, ln_g, ln_b)


def _slope_table(group_slopes):
    g, hg = group_slopes.shape
    rest = (group_slopes * LOG2E).astype(np.float32)
    tab = np.zeros((g, hg, Q_BLOCK, LANES), jnp.bfloat16)
    for i in range(N_SPLIT):
        piece = rest.astype(jnp.bfloat16)
        rest = rest - piece.astype(np.float32)
        tab[..., NSA_DH + 2 * i] = piece[:, :, None]
        tab[..., NSA_DH + 2 * i + 1] = piece[:, :, None]
    return tab.reshape(g, hg * Q_BLOCK, LANES)


def _pad_slots(w, n_slots):
    d = w.shape[0]
    w3 = w.reshape(d, n_slots, -1)
    return jnp.pad(w3, ((0, 0), (0, 0), (0, LANES - w3.shape[2]))).reshape(d, n_slots * LANES)


def _cmp_w1_expand(w1, g):
    hid = w1.shape[1]
    w1r = w1.reshape(2, CMP_STRIDE, 1, NSA_DH, hid)
    eye = jnp.eye(g, dtype=w1.dtype).reshape(g, 1, 1, g, 1, 1)
    return (eye * w1r[None]).reshape(g, 2 * CMP_STRIDE * g * NSA_DH, hid)


def _overlap_t(n_cmp_pad, n_cmp, n_slc):
    cs = np.arange(n_cmp_pad) * CMP_STRIDE
    ss = np.arange(n_slc) * SLC_BLOCK
    ov = np.clip(np.minimum(cs[:, None] + CMP_BLOCK, ss[None] + SLC_BLOCK)
                 - np.maximum(cs[:, None], ss[None]), 0, None).astype(np.float32) / CMP_BLOCK
    ov[n_cmp:] = 0.0
    return ov.T


def kernel(x, c, w_ada, b_ada, w_in, hg_lb_logits, hg_norm_g, cmp_pos_k, cmp_w1_k, cmp_w2_k,
           cmp_pos_v, cmp_w1_v, cmp_w2_v, w_out, ln1_g, ln1_b, ffn_w1, ffn_w3, ffn_w2, ln2_g, ln2_b):
    bsz, t, d = x.shape
    depth = w_ada.shape[0]
    g, hg, dh = NSA_GROUPS, NSA_HG, NSA_DH
    hgw = HG_HEADS * HG_D
    nsw = NSA_HEADS * dh
    kvw = g * dh
    n_mod = 6
    alpha = (2.0 * depth) ** 0.25
    scale = dh ** -0.5
    n_grp = t // CMP_STRIDE
    n_cmp = (t - CMP_BLOCK) // CMP_STRIDE + 1
    n_slc = t // SLC_BLOCK
    assert t % SEL_TILE == 0 and t >= WINDOW + Q_BLOCK and 3 * hg <= GATE_ROWS and hg % 2 == 0

    slopes = (2.0 ** (-8.0 * (np.arange(NSA_HEADS) + 1) / NSA_HEADS)).reshape(g, hg)
    slope_tab = jnp.asarray(_slope_table(slopes), BF16)
    ov_t = jnp.asarray(_overlap_t(n_grp, n_cmp, n_slc), BF16)

    c_pad = jnp.pad(c, ((0, 8 - bsz), (0, 0)))
    for l in range(depth):
        ada = _ada(c_pad, w_ada[l], b_ada[l][None])
        ada3 = ada[:bsz].reshape(bsz * n_mod, 1, d)

        w = w_in[l]
        o = 4 * hgw
        w_q = w[:, o:o + nsw]
        w_kv = [w[:, o + nsw + i * kvw: o + nsw + (i + 1) * kvw] for i in range(6)]
        w_g = w[:, o + nsw + 6 * kvw: o + nsw + 6 * kvw + 3 * NSA_HEADS]
        w_main = jnp.concatenate(
            [w[:, :o], _pad_slots(w_q * (scale * LOG2E), NSA_HEADS), _pad_slots(w_kv[2], g),
             _pad_slots(w_kv[4], g), w_kv[0], w_kv[1]], axis=1).astype(BF16)
        w_gt = jnp.pad(w_g.reshape(d, g, 3 * hg), ((0, 0), (0, 0), (0, GATE_ROWS - 3 * hg))).reshape(d, -1)
        v_slots = lambda a: jnp.pad(a.reshape(d, g, dh), ((0, 0), (0, 0), (0, V_SLOT - dh))).reshape(d, -1)
        w_t = jnp.concatenate([v_slots(w_kv[3]), v_slots(w_kv[5]), w_gt], axis=1).T.astype(BF16)
        widths = (o, NSA_HEADS * LANES, g * LANES, kvw, 2 * g * V_SLOT, g * GATE_ROWS)
        hg_in, q_slots, ks_aug, kw_aug, kc, vc, v_t, gate_t = _inproj(x, ada3, w_main, w_t, n_mod, widths)

        y_hg = _hgrn(hg_in, hg_lb_logits, hg_norm_g[l][None], l)

        grp = lambda a: a.reshape(bsz, n_grp, CMP_STRIDE * kvw)
        pos_e = lambda p: jnp.broadcast_to(
            p.reshape(2, CMP_STRIDE, 1, dh), (2, CMP_STRIDE, g, dh)).reshape(1, -1)
        w2k = jnp.pad(cmp_w2_k[l], ((0, 0), (0, LANES - dh))).astype(BF16)
        kc_aug = _compress(grp(kc), pos_e(cmp_pos_k[l]), _cmp_w1_expand(cmp_w1_k[l], g).astype(BF16),
                           w2k, True)
        vc_t = _compress(grp(vc), pos_e(cmp_pos_v[l]), _cmp_w1_expand(cmp_w1_v[l], g).astype(BF16),
                         cmp_w2_v[l].T.astype(BF16), False)

        y_nsa = _nsa(q_slots, gate_t, kc_aug, vc_t, ks_aug, kw_aug, v_t, ov_t, slope_tab)

        w_o = w_out[l].astype(BF16)
        x = _outln(x, y_hg, y_nsa, w_o[:hgw], w_o[hgw:], ada3, n_mod, ln1_g[l][None], ln1_b[l][None], alpha)
        x = _ffn(x, ada3, n_mod, ffn_w1[l].astype(BF16), ffn_w3[l].astype(BF16), ffn_w2[l].astype(BF16),
                 ln2_g[l][None], ln2_b[l][None], alpha)
    return x
```

```python
import functools

import numpy as np
import jax
import jax.numpy as jnp
from jax import lax
from jax.experimental import pallas as pl
from jax.experimental.pallas import tpu as pltpu

F32 = jnp.float32
BF16 = jnp.bfloat16

HG_HEADS = 4
HG_D = 128
NSA_HEADS = 8
NSA_GROUPS = 2
NSA_HG = NSA_HEADS // NSA_GROUPS
NSA_DH = 64
CMP_BLOCK = 32
CMP_STRIDE = 16
CMP_HIDDEN = 256
SLC_BLOCK = 64
SLC_TOPN = 16
WINDOW = 512
Q_BLOCK = 128
NEG = -1e30
FORCE = 1e4
LN_EPS = 1e-5
RMS_EPS = 1e-6

LANES = 128
SUBLANES = 8
HG_BLK = 16
POS_SPLIT = 256
GATE_ROWS = 16
SEL_TILE = 512
V_SLOT = 80
N_SPLIT = 3
LOG2E = 1.4426950408889634
MXU_DIM = 256
VMEM_LIMIT = 48 * 1024 * 1024
FFN_VMEM_LIMIT = 56 * 1024 * 1024


def _sigmoid(x):
    return 1.0 / (1.0 + jnp.exp(-x))


def _dot(a, b):
    return jnp.dot(a, b, preferred_element_type=F32)


def _dot_nt(a, b):
    return lax.dot_general(a, b, (((1,), (1,)), ((), ())), preferred_element_type=F32)


def _pos_columns(pos, col):
    c = col - NSA_DH
    lo = pos % POS_SPLIT
    val = jnp.where(c % 2 == 0, lo, pos - lo)
    return jnp.where((c >= 0) & (c < 2 * N_SPLIT), val, 0).astype(F32)


def _split3(x):
    hi = x.astype(BF16)
    r = x - hi.astype(F32)
    mid = r.astype(BF16)
    lo = (r - mid.astype(F32)).astype(BF16)
    return hi, mid, lo


def _ada_kernel(c_ref, w_ref, b_ref, o_ref):
    c = c_ref[...]
    s = (c * _sigmoid(c)).astype(BF16)
    o_ref[...] = _dot(s, w_ref[...].astype(BF16)) + b_ref[...]


def _ada(c_pad, w, b):
    rows, d = c_pad.shape
    n = w.shape[1]
    tn = 1024
    return pl.pallas_call(
        _ada_kernel,
        out_shape=jax.ShapeDtypeStruct((rows, n), F32),
        grid=(n // tn,),
        in_specs=[
            pl.BlockSpec((rows, d), lambda j: (0, 0)),
            pl.BlockSpec((d, tn), lambda j: (0, j)),
            pl.BlockSpec((1, tn), lambda j: (0, j)),
        ],
        out_specs=pl.BlockSpec((rows, tn), lambda j: (0, j)),
        compiler_params=pltpu.CompilerParams(
            dimension_semantics=("parallel",), vmem_limit_bytes=VMEM_LIMIT),
        name="ada",
    )(c_pad, w, b)


def _inproj_kernel(x_ref, sc_ref, sh_ref, wm_ref, wt_ref,
                   hg_ref, q_ref, ks_ref, kw_ref, kc_ref, vc_ref, vt_ref, gt_ref, *, tm):
    h = (x_ref[0] * (1.0 + sc_ref[0]) + sh_ref[0]).astype(BF16)
    y = _dot(h, wm_ref[...])
    n_hg = hg_ref.shape[2]
    n_q = q_ref.shape[2]
    n_k = ks_ref.shape[2]
    n_c = kc_ref.shape[2]
    o = 0
    hg_ref[0] = y[:, o:o + n_hg]
    o += n_hg
    q_ref[0] = y[:, o:o + n_q].astype(BF16)
    o += n_q
    row = pl.program_id(1) * tm + lax.broadcasted_iota(jnp.int32, (tm, n_k), 0)
    pc = _pos_columns(row, lax.broadcasted_iota(jnp.int32, (tm, n_k), 1) % LANES)
    ks_ref[0] = (y[:, o:o + n_k] + pc).astype(BF16)
    o += n_k
    kw_ref[0] = (y[:, o:o + n_k] + pc).astype(BF16)
    o += n_k
    kc_ref[0] = y[:, o:o + n_c]
    o += n_c
    vc_ref[0] = y[:, o:o + n_c]
    yt = _dot_nt(wt_ref[...], h)
    n_vt = vt_ref.shape[1]
    ri = lax.broadcasted_iota(jnp.int32, (n_vt, tm), 0) % V_SLOT
    vt_ref[0] = (yt[:n_vt] + jnp.where(ri == NSA_DH, 1.0, 0.0)).astype(BF16)
    gt_ref[0] = yt[n_vt:]


def _inproj(x, ada3, w_main, w_t, n_mod, widths):
    b, t, d = x.shape
    n_hg, n_q, n_k, n_c, n_vt, n_gt = widths
    tm = 256
    row_spec = lambda n: pl.BlockSpec((1, tm, n), lambda i, j: (i, j, 0))
    col_spec = lambda n: pl.BlockSpec((1, n, tm), lambda i, j: (i, 0, j))
    sds = jax.ShapeDtypeStruct
    return pl.pallas_call(
        functools.partial(_inproj_kernel, tm=tm),
        out_shape=(sds((b, t, n_hg), F32), sds((b, t, n_q), BF16), sds((b, t, n_k), BF16),
                   sds((b, t, n_k), BF16), sds((b, t, n_c), F32), sds((b, t, n_c), F32),
                   sds((b, n_vt, t), BF16), sds((b, n_gt, t), F32)),
        grid=(b, t // tm),
        in_specs=[
            pl.BlockSpec((1, tm, d), lambda i, j: (i, j, 0)),
            pl.BlockSpec((1, 1, d), lambda i, j: (i * n_mod + 1, 0, 0)),
            pl.BlockSpec((1, 1, d), lambda i, j: (i * n_mod, 0, 0)),
            pl.BlockSpec(w_main.shape, lambda i, j: (0, 0)),
            pl.BlockSpec(w_t.shape, lambda i, j: (0, 0)),
        ],
        out_specs=(row_spec(n_hg), row_spec(n_q), row_spec(n_k), row_spec(n_k), row_spec(n_c),
                   row_spec(n_c), col_spec(n_vt), col_spec(n_gt)),
        compiler_params=pltpu.CompilerParams(
            dimension_semantics=("parallel", "parallel"), vmem_limit_bytes=VMEM_LIMIT),
        name="inproj",
    )(x, ada3, ada3, w_main, w_t)


def _hgrn_kernel(q_ref, f_ref, i_ref, g_ref, lbl_ref, ng_ref, e_ref, o_ref, st_ref, *, tc, layer):
    blk = HG_BLK
    per_grp = LANES // blk
    half = blk // 2

    @pl.when(pl.program_id(2) == 0)
    def _():
        st_ref[...] = jnp.zeros_like(st_ref)

    lbl = lbl_ref[...]
    e = jnp.exp(lbl - jnp.max(lbl, axis=0, keepdims=True))
    lb = jnp.sum(e[: layer + 1], axis=0, keepdims=True) / jnp.sum(e, axis=0, keepdims=True)

    r_i = lax.broadcasted_iota(jnp.int32, (LANES, LANES), 0)
    c_i = lax.broadcasted_iota(jnp.int32, (LANES, LANES), 1)
    same_blk = (r_i // blk) == (c_i // blk)
    tri = jnp.where(same_blk & (c_i <= r_i), 1.0, 0.0).astype(BF16)
    row_blk = r_i // blk
    nh = per_grp * half
    r_h = lax.broadcasted_iota(jnp.int32, (nh, LANES), 0)
    same_half = (r_h // half) == (lax.broadcasted_iota(jnp.int32, (nh, LANES), 1) // blk)
    row_in_half = r_h % half
    blocked = lambda a: a.reshape(per_grp, blk, LANES)

    def gates(g):
        sl = slice(g * LANES, (g + 1) * LANES)
        q = q_ref[0, sl, :]
        qs = q * _sigmoid(q)
        f = lb + (1.0 - lb) * _sigmoid(f_ref[0, sl, :])
        hi, mid, lo = _split3(jnp.log2(f))
        b = _dot(tri, hi) + _dot(tri, mid) + _dot(tri, lo)
        return qs, 1.0 - f, i_ref[0, sl, :], b

    def increments(qs, k, v, b):
        b3 = blocked(b)
        bl3 = b3[:, blk - 1:blk, :]
        qd = (qs * jnp.exp2(b)).astype(BF16)
        kd = (blocked(k) * jnp.exp2(bl3 - b3)).reshape(LANES, LANES)
        kexp = jnp.concatenate(
            [jnp.where(row_blk == j, kd, 0.0) for j in range(per_grp)], axis=1).astype(BF16)
        return qd, jnp.exp2(bl3), _dot(v.T.astype(BF16), kexp)

    def recur(st, qd, dl, ut):
        ois = []
        for j in range(per_grp):
            ois.append(_dot_nt(qd[j * blk:(j + 1) * blk], st.astype(BF16)))
            st = dl[j] * st + ut[:, j * LANES:(j + 1) * LANES]
        return st, jnp.concatenate(ois, axis=0)

    def pairwise(qs, k, b):
        c = b - jnp.log2(k)
        tiles = lambda a, off: jnp.concatenate(
            [a[j * blk + off:j * blk + off + half] for j in range(per_grp)], axis=0)
        q_lo, q_hi, b_lo, b_hi = tiles(qs, 0), tiles(qs, half), tiles(b, 0), tiles(b, half)
        p_lo, p_hi = [], []
        for s in range(blk):
            cs = jnp.concatenate([jnp.broadcast_to(c[j * blk + s:j * blk + s + 1], (half, LANES))
                                  for j in range(per_grp)], axis=0)
            if s < half:
                p_lo.append(jnp.where(row_in_half >= s, q_lo * jnp.exp2(b_lo - cs), 0.0).astype(BF16))
                p_hi.append((q_hi * jnp.exp2(b_hi - cs)).astype(BF16))
            else:
                p_hi.append(jnp.where(row_in_half >= s - half, q_hi * jnp.exp2(b_hi - cs), 0.0).astype(BF16))
        return jnp.concatenate(p_lo, axis=1), jnp.concatenate(p_hi, axis=1)

    def finish(g, v, a_lo, a_hi, o_inter):
        vb = v.astype(BF16)
        o_lo = _dot(jnp.where(same_half, a_lo, 0.0).astype(BF16), vb)
        o_hi = _dot(jnp.where(same_half, a_hi, 0.0).astype(BF16), vb)
        o = jnp.concatenate([x[j * half:(j + 1) * half] for j in range(per_grp) for x in (o_lo, o_hi)],
                            axis=0) + o_inter
        o = o * lax.rsqrt(jnp.mean(o * o, axis=-1, keepdims=True) + RMS_EPS)
        sl = slice(g * LANES, (g + 1) * LANES)
        o_ref[0, sl, :] = (o * ng_ref[...] * _sigmoid(g_ref[0, sl, :])).astype(o_ref.dtype)

    ngrp = tc // LANES
    gs = [gates(g) for g in range(ngrp)]
    incs = [increments(*gt) for gt in gs]
    st = st_ref[...]
    o_inters = []
    for inc in incs:
        st, oi = recur(st, *inc)
        o_inters.append(oi)
    st_ref[...] = st
    pws = [pairwise(qs, k, b) for qs, k, _, b in gs]
    a_lo = _dot(jnp.concatenate([p[0] for p in pws], axis=0), e_ref[:half * LANES, :])
    a_hi = _dot(jnp.concatenate([p[1] for p in pws], axis=0), e_ref[...])
    for g in range(ngrp):
        rows = slice(g * nh, (g + 1) * nh)
        finish(g, gs[g][2], a_lo[rows], a_hi[rows], o_inters[g])


def _hgrn(proj, lb_logits, norm_g, layer):
    b, t, _ = proj.shape
    tc = 512
    h = HG_HEADS
    nslots = lb_logits.shape[0]
    col = lambda seg: (lambda i, j, s: (i, s, seg * h + j))
    fold = np.zeros((HG_BLK, LANES, LANES), np.float32)
    for s in range(HG_BLK):
        fold[s, :, s::HG_BLK] = 1.0
    fold = jnp.asarray(fold.reshape(HG_BLK * LANES, LANES), BF16)
    return pl.pallas_call(
        functools.partial(_hgrn_kernel, tc=tc, layer=layer),
        out_shape=jax.ShapeDtypeStruct((b, t, h * HG_D), BF16),
        grid=(b, h, t // tc),
        in_specs=[
            pl.BlockSpec((1, tc, HG_D), col(0)),
            pl.BlockSpec((1, tc, HG_D), col(1)),
            pl.BlockSpec((1, tc, HG_D), col(2)),
            pl.BlockSpec((1, tc, HG_D), col(3)),
            pl.BlockSpec((nslots, HG_D), lambda i, j, s: (0, j)),
            pl.BlockSpec((1, HG_D), lambda i, j, s: (0, j)),
            pl.BlockSpec(fold.shape, lambda i, j, s: (0, 0)),
        ],
        out_specs=pl.BlockSpec((1, tc, HG_D), lambda i, j, s: (i, s, j)),
        scratch_shapes=[pltpu.VMEM((HG_D, HG_D), F32)],
        compiler_params=pltpu.CompilerParams(
            dimension_semantics=("parallel", "parallel", "arbitrary"), vmem_limit_bytes=VMEM_LIMIT),
        name="hgrn",
    )(proj, proj, proj, proj, lb_logits, norm_g, fold)


def _cmp_kernel(h_ref, pos_ref, w1_ref, w2_ref, o_ref, *, key_side):
    h = h_ref[0]
    half = h.shape[1]
    pos = pos_ref[...]
    p0 = _dot((h + pos[:, :half]).astype(BF16), w1_ref[0, :half])
    p1 = _dot((h + pos[:, half:]).astype(BF16), w1_ref[0, half:])
    ng = h.shape[0]
    pre = p0 + pltpu.roll(p1, ng - 1, 0)
    gl = 0.5 * pre * (1.0 + jnp.tanh(np.sqrt(2.0 / np.pi) * (pre + 0.044715 * (pre * pre * pre))))
    gl = gl.astype(BF16)
    if key_side:
        cend = lax.broadcasted_iota(jnp.int32, (ng, LANES), 0) * CMP_STRIDE + (CMP_BLOCK - 1)
        pc = _pos_columns(cend, lax.broadcasted_iota(jnp.int32, (ng, LANES), 1))
        o_ref[0, 0] = (_dot(gl, w2_ref[...]) + pc).astype(BF16)
    else:
        o_ref[0, 0] = _dot_nt(w2_ref[...], gl).astype(BF16)


def _compress(hgrp, pos, w1, w2, key_side):
    b, ng, half = hgrp.shape
    g = w1.shape[0]
    out_tail = (ng, LANES) if key_side else (w2.shape[0], ng)
    return pl.pallas_call(
        functools.partial(_cmp_kernel, key_side=key_side),
        out_shape=jax.ShapeDtypeStruct((b, g) + out_tail, BF16),
        grid=(b, g),
        in_specs=[
            pl.BlockSpec((1, ng, half), lambda i, j: (i, 0, 0)),
            pl.BlockSpec(pos.shape, lambda i, j: (0, 0)),
            pl.BlockSpec((1,) + w1.shape[1:], lambda i, j: (j, 0, 0)),
            pl.BlockSpec(w2.shape, lambda i, j: (0, 0)),
        ],
        out_specs=pl.BlockSpec((1, 1) + out_tail, lambda i, j: (i, j, 0, 0)),
        compiler_params=pltpu.CompilerParams(
            dimension_semantics=("parallel", "parallel"), vmem_limit_bytes=VMEM_LIMIT),
        name="cmp_k" if key_side else "cmp_v",
    )(hgrp, pos, w1, w2)


def _nsa_kernel(q_ref, gate_ref, kc_ref, vct_ref, ks_ref, vst_ref, kw_ref, vwt_ref, ovt_ref, slope_ref,
                o_ref, bias_ref, *z_scratch):
    qb = pl.program_id(2)
    t0 = qb * Q_BLOCK
    npair = NSA_HG // 2
    za_refs, zb_refs = z_scratch[:npair], z_scratch[npair:]
    nq = 2 * Q_BLOCK
    qblk = q_ref[0]
    qas = []
    for hp in range(npair):
        qa = jnp.concatenate([qblk[:, h * LANES:(h + 1) * LANES] for h in (2 * hp, 2 * hp + 1)], axis=0)
        qas.append(qa + slope_ref[0, hp * nq:(hp + 1) * nq])
    tpos = t0 + (lax.broadcasted_iota(jnp.int32, (1, nq), 1) % Q_BLOCK)

    def normalised(acc):
        return acc[:NSA_DH] * (1.0 / acc[NSA_DH:NSA_DH + 1])

    per_step = SEL_TILE // SLC_BLOCK
    spos0 = lax.broadcasted_iota(jnp.int32, (SEL_TILE, 1), 0)
    nkw = WINDOW + Q_BLOCK
    k0 = pl.multiple_of(jnp.maximum(t0 - WINDOW, 0), Q_BLOCK)
    kw = kw_ref[0, pl.ds(k0, nkw), :]

    def sel_scores(st, z_refs):
        ks = ks_ref[0, pl.ds(pl.multiple_of(st * SEL_TILE, SEL_TILE), SEL_TILE), :]
        for z_ref, qa in zip(z_refs, qas):
            z_ref[...] = _dot_nt(ks, qa)

    zcs = [_dot_nt(kc_ref[0, 0], qa) for qa in qas]
    zws = [_dot_nt(kw, qa) for qa in qas]
    sel_scores(0, za_refs)

    ncp = kc_ref.shape[2]
    cend = lax.broadcasted_iota(jnp.int32, (ncp, 1), 0) * CMP_STRIDE + (CMP_BLOCK - 1)
    valid = cend <= tpos
    o_c, psum = [], None
    for zc in zcs:
        zm = jnp.where(valid, zc, NEG)
        ec = jnp.where(valid, jnp.exp2(zm - jnp.max(zm, axis=0, keepdims=True)), 0.0)
        lc = jnp.sum(ec, axis=0, keepdims=True)
        pc = ec * jnp.where(lc > 0.0, 1.0 / lc, 0.0)
        o_c.append(_dot(vct_ref[0, 0], pc.astype(BF16)))
        for i in range(2):
            ph = pc[:, i * Q_BLOCK:(i + 1) * Q_BLOCK]
            psum = ph if psum is None else psum + ph

    p_hi = psum.astype(BF16)
    p_lo = (psum - p_hi.astype(F32)).astype(BF16)
    imp = _dot(ovt_ref[...], p_hi) + _dot(ovt_ref[...], p_lo)
    ns = imp.shape[0]
    j_i = lax.broadcasted_iota(jnp.int32, (ns, 1), 0)
    t_abs = t0 + lax.broadcasted_iota(jnp.int32, (1, Q_BLOCK), 1)
    cur = t_abs // SLC_BLOCK
    forced = (j_i == 0) | (j_i == cur) | (j_i == cur - 1)
    x = jnp.where(forced, FORCE, imp)
    x = jnp.where(j_i * SLC_BLOCK <= t_abs, x, NEG)

    vw = vwt_ref[0, :, pl.ds(k0, nkw)]
    dist = tpos - (k0 + lax.broadcasted_iota(jnp.int32, (nkw, 1), 0))
    okw = (dist >= 0) & (dist < WINDOW)
    o_w = []
    for zw in zws:
        zw = jnp.where(okw, zw, NEG)
        pw = jnp.exp2(zw - jnp.max(zw, axis=0, keepdims=True))
        o_w.append(normalised(_dot(vw, pw.astype(BF16))))

    sub = lax.broadcasted_iota(jnp.int32, (SUBLANES, Q_BLOCK), 0)
    tiles = [x[v * SUBLANES:(v + 1) * SUBLANES] for v in range(ns // SUBLANES)]
    cnts = [jnp.zeros((SUBLANES, Q_BLOCK), F32) for _ in tiles]
    for jp in range(ns):
        r = x[jp:jp + 1, :]
        vj, rj = divmod(jp, SUBLANES)
        for v, xt in enumerate(tiles):
            if v < vj:
                ahead = r > xt
            elif v > vj:
                ahead = r >= xt
            else:
                ahead = (r > xt) | ((r == xt) & (sub > rj))
            cnts[v] = cnts[v] + jnp.where(ahead, 1.0, 0.0)
    for v, cnt in enumerate(cnts):
        bias_ref[v * SUBLANES:(v + 1) * SUBLANES, :] = jnp.where(cnt < float(SLC_TOPN), 0.0, NEG)

    def sel_update(st, z_refs, carry, causal):
        s0 = pl.multiple_of(st * SEL_TILE, SEL_TILE)
        vs = vst_ref[0, :, pl.ds(s0, SEL_TILE)]
        brow = bias_ref[pl.ds(pl.multiple_of(st * per_step, per_step), per_step), :]
        b2 = jnp.concatenate([brow] * 2, axis=1)[:, None, :]
        out = []
        for z_ref, (m, acc) in zip(z_refs, carry):
            z = (z_ref[...].reshape(per_step, SLC_BLOCK, nq) + b2).reshape(SEL_TILE, nq)
            if causal:
                z = jnp.where((s0 + spos0) <= tpos, z, NEG)
            mn = jnp.maximum(m, jnp.max(z, axis=0, keepdims=True))
            p = jnp.exp2(z - mn)
            out.append((mn, jnp.exp2(m - mn) * acc + _dot(vs, p.astype(BF16))))
        return tuple(out)

    def two_steps(i, carry):
        sel_scores(2 * i + 1, zb_refs)
        carry = sel_update(2 * i, za_refs, carry, False)
        sel_scores(2 * i + 2, za_refs)
        return sel_update(2 * i + 1, zb_refs, carry, False)

    init = tuple((jnp.full((1, nq), NEG, F32), jnp.zeros((V_SLOT, nq), F32)) for _ in range(npair))
    last = qb // (SEL_TILE // Q_BLOCK)
    carry = lax.fori_loop(0, last // 2, two_steps, init)

    def odd_tail(carry):
        sel_scores(last, zb_refs)
        carry = sel_update(last - 1, za_refs, carry, False)
        return sel_update(last, zb_refs, carry, True)

    carry = lax.cond(last % 2 == 1, odd_tail, lambda c: sel_update(last, za_refs, c, True), carry)
    o_s = [normalised(acc) for _, acc in carry]

    gts = _sigmoid(gate_ref[0])
    pairs = []
    for hp in range(npair):
        halves = []
        for i in range(2):
            r = 3 * (2 * hp + i)
            sl = slice(i * Q_BLOCK, (i + 1) * Q_BLOCK)
            halves.append(gts[r:r + 1] * o_c[hp][:, sl] + gts[r + 1:r + 2] * o_s[hp][:, sl]
                          + gts[r + 2:r + 3] * o_w[hp][:, sl])
        pairs.append(jnp.concatenate(halves, axis=0).T)
    o_ref[0] = jnp.concatenate(pairs, axis=1).astype(o_ref.dtype)


def _nsa(q_slots, gate_t, kc_aug, vc_t, ks_aug, kw_aug, v_t, ov_t, slope_tab):
    b, t, _ = q_slots.shape
    g = NSA_GROUPS
    nqb = t // Q_BLOCK
    ncp = kc_aug.shape[2]
    ns = ov_t.shape[0]
    return pl.pallas_call(
        _nsa_kernel,
        out_shape=jax.ShapeDtypeStruct((b, t, NSA_HEADS * NSA_DH), BF16),
        grid=(b, g, nqb),
        in_specs=[
            pl.BlockSpec((1, Q_BLOCK, NSA_HG * LANES), lambda i, j, s: (i, s, j)),
            pl.BlockSpec((1, GATE_ROWS, Q_BLOCK), lambda i, j, s: (i, j, s)),
            pl.BlockSpec((1, 1, ncp, LANES), lambda i, j, s: (i, j, 0, 0)),
            pl.BlockSpec((1, 1, NSA_DH, ncp), lambda i, j, s: (i, j, 0, 0)),
            pl.BlockSpec((1, t, LANES), lambda i, j, s: (i, 0, j)),
            pl.BlockSpec((1, V_SLOT, t), lambda i, j, s: (i, j, 0)),
            pl.BlockSpec((1, t, LANES), lambda i, j, s: (i, 0, j)),
            pl.BlockSpec((1, V_SLOT, t), lambda i, j, s: (i, g + j, 0)),
            pl.BlockSpec((ns, ncp), lambda i, j, s: (0, 0)),
            pl.BlockSpec((1, NSA_HG * Q_BLOCK, LANES), lambda i, j, s: (j, 0, 0)),
        ],
        out_specs=pl.BlockSpec((1, Q_BLOCK, NSA_HG * NSA_DH), lambda i, j, s: (i, s, j)),
        scratch_shapes=[pltpu.VMEM((ns, Q_BLOCK), F32)]
        + [pltpu.VMEM((SEL_TILE, 2 * Q_BLOCK), F32)] * NSA_HG,
        compiler_params=pltpu.CompilerParams(
            dimension_semantics=("parallel", "parallel", "arbitrary"), vmem_limit_bytes=VMEM_LIMIT),
        name="nsa",
    )(q_slots, gate_t, kc_aug, vc_t, ks_aug, v_t, kw_aug, v_t, ov_t, slope_tab)


def _layer_norm(z, g, b):
    mu = jnp.mean(z, axis=-1, keepdims=True)
    zc = z - mu
    var = jnp.mean(zc * zc, axis=-1, keepdims=True)
    return zc * lax.rsqrt(var + LN_EPS) * g + b


def _post_kernel(x_ref, yh_ref, yn_ref, mod_ref, wh_ref, wn_ref, w1_ref, w3_ref, w2_ref,
                 l1g_ref, l1b_ref, l2g_ref, l2b_ref, o_ref, *, alpha, chunks):
    g1, sh2, sc2, g2 = mod_ref[0, 2], mod_ref[0, 3], mod_ref[0, 4], mod_ref[0, 5]
    mix = _dot(yh_ref[0], wh_ref[...]) + _dot(yn_ref[0], wn_ref[...])
    x1 = _layer_norm(alpha * x_ref[0] + g1 * mix, l1g_ref[...], l1b_ref[...])
    h = (x1 * (1.0 + sc2) + sh2).astype(BF16)
    ups = []
    for lo, hi in chunks:
        ups.append((_dot(h, w1_ref[:, lo:hi]), _dot(h, w3_ref[:, lo:hi])))
    a = jnp.concatenate([(u * _sigmoid(u) * w).astype(BF16) for u, w in ups], axis=1)
    z = alpha * x1 + g2 * _dot(a, w2_ref[...])
    o_ref[0] = _layer_norm(z, l2g_ref[...], l2b_ref[...])


def _post(x, y_hg, y_nsa, ada4, w_h, w_n, w1, w3, w2, ln1_g, ln1_b, ln2_g, ln2_b, alpha):
    b, t, d = x.shape
    dff = w1.shape[1]
    tm = 512
    step = 3 * MXU_DIM
    chunks = tuple((lo, min(lo + step, dff)) for lo in range(0, dff, step))
    resident = lambda a: pl.BlockSpec(a.shape, lambda i, j: (0,) * a.ndim, pipeline_mode=pl.Buffered(1))
    rows = lambda a: pl.BlockSpec((1, tm, a.shape[2]), lambda i, j: (i, j, 0))
    vec = pl.BlockSpec((1, d), lambda i, j: (0, 0))
    return pl.pallas_call(
        functools.partial(_post_kernel, alpha=alpha, chunks=chunks),
        out_shape=jax.ShapeDtypeStruct((b, t, d), F32),
        grid=(b, t // tm),
        in_specs=[rows(x), rows(y_hg), rows(y_nsa),
                  pl.BlockSpec((1,) + ada4.shape[1:], lambda i, j: (i, 0, 0, 0)),
                  resident(w_h), resident(w_n), resident(w1), resident(w3), resident(w2),
                  vec, vec, vec, vec],
        out_specs=rows(x),
        compiler_params=pltpu.CompilerParams(
            dimension_semantics=("parallel", "parallel"), vmem_limit_bytes=FFN_VMEM_LIMIT),
        name="post",
    )(x, y_hg, y_nsa, ada4, w_h, w_n, w1, w3, w2, ln1_g, ln1_b, ln2_g, ln2_b)


def _slope_table(group_slopes):
    g, hg = group_slopes.shape
    rest = (group_slopes * LOG2E).astype(np.float32)
    tab = np.zeros((g, hg, Q_BLOCK, LANES), jnp.bfloat16)
    for i in range(N_SPLIT):
        piece = rest.astype(jnp.bfloat16)
        rest = rest - piece.astype(np.float32)
        tab[..., NSA_DH + 2 * i] = piece[:, :, None]
        tab[..., NSA_DH + 2 * i + 1] = piece[:, :, None]
    return tab.reshape(g, hg * Q_BLOCK, LANES)


def _pad_slots(w, n_slots):
    d = w.shape[0]
    w3 = w.reshape(d, n_slots, -1)
    return jnp.pad(w3, ((0, 0), (0, 0), (0, LANES - w3.shape[2]))).reshape(d, n_slots * LANES)


def _cmp_w1_expand(w1, g):
    hid = w1.shape[1]
    w1r = w1.reshape(2, CMP_STRIDE, 1, NSA_DH, hid)
    eye = jnp.eye(g, dtype=w1.dtype).reshape(g, 1, 1, g, 1, 1)
    return (eye * w1r[None]).reshape(g, 2 * CMP_STRIDE * g * NSA_DH, hid)


def _overlap_t(n_cmp_pad, n_cmp, n_slc):
    cs = np.arange(n_cmp_pad) * CMP_STRIDE
    ss = np.arange(n_slc) * SLC_BLOCK
    ov = np.clip(np.minimum(cs[:, None] + CMP_BLOCK, ss[None] + SLC_BLOCK)
                 - np.maximum(cs[:, None], ss[None]), 0, None).astype(np.float32) / CMP_BLOCK
    ov[n_cmp:] = 0.0
    return ov.T


def kernel(x, c, w_ada, b_ada, w_in, hg_lb_logits, hg_norm_g, cmp_pos_k, cmp_w1_k, cmp_w2_k,
           cmp_pos_v, cmp_w1_v, cmp_w2_v, w_out, ln1_g, ln1_b, ffn_w1, ffn_w3, ffn_w2, ln2_g, ln2_b):
    bsz, t, d = x.shape
    depth = w_ada.shape[0]
    g, hg, dh = NSA_GROUPS, NSA_HG, NSA_DH
    hgw = HG_HEADS * HG_D
    nsw = NSA_HEADS * dh
    kvw = g * dh
    n_mod = 6
    alpha = (2.0 * depth) ** 0.25
    scale = dh ** -0.5
    n_grp = t // CMP_STRIDE
    n_cmp = (t - CMP_BLOCK) // CMP_STRIDE + 1
    n_slc = t // SLC_BLOCK
    assert t % SEL_TILE == 0 and t >= WINDOW + Q_BLOCK and 3 * hg <= GATE_ROWS and hg % 2 == 0

    slopes = (2.0 ** (-8.0 * (np.arange(NSA_HEADS) + 1) / NSA_HEADS)).reshape(g, hg)
    slope_tab = jnp.asarray(_slope_table(slopes), BF16)
    ov_t = jnp.asarray(_overlap_t(n_grp, n_cmp, n_slc), BF16)

    c_pad = jnp.pad(c, ((0, 8 - bsz), (0, 0)))
    for l in range(depth):
        ada = _ada(c_pad, w_ada[l], b_ada[l][None])
        ada3 = ada[:bsz].reshape(bsz * n_mod, 1, d)

        w = w_in[l]
        o = 4 * hgw
        w_q = w[:, o:o + nsw]
        w_kv = [w[:, o + nsw + i * kvw: o + nsw + (i + 1) * kvw] for i in range(6)]
        w_g = w[:, o + nsw + 6 * kvw: o + nsw + 6 * kvw + 3 * NSA_HEADS]
        w_main = jnp.concatenate(
            [w[:, :o], _pad_slots(w_q * (scale * LOG2E), NSA_HEADS), _pad_slots(w_kv[2], g),
             _pad_slots(w_kv[4], g), w_kv[0], w_kv[1]], axis=1).astype(BF16)
        w_gt = jnp.pad(w_g.reshape(d, g, 3 * hg), ((0, 0), (0, 0), (0, GATE_ROWS - 3 * hg))).reshape(d, -1)
        v_slots = lambda a: jnp.pad(a.reshape(d, g, dh), ((0, 0), (0, 0), (0, V_SLOT - dh))).reshape(d, -1)
        w_t = jnp.concatenate([v_slots(w_kv[3]), v_slots(w_kv[5]), w_gt], axis=1).T.astype(BF16)
        widths = (o, NSA_HEADS * LANES, g * LANES, kvw, 2 * g * V_SLOT, g * GATE_ROWS)
        hg_in, q_slots, ks_aug, kw_aug, kc, vc, v_t, gate_t = _inproj(x, ada3, w_main, w_t, n_mod, widths)

        y_hg = _hgrn(hg_in, hg_lb_logits, hg_norm_g[l][None], l)

        grp = lambda a: a.reshape(bsz, n_grp, CMP_STRIDE * kvw)
        pos_e = lambda p: jnp.broadcast_to(
            p.reshape(2, CMP_STRIDE, 1, dh), (2, CMP_STRIDE, g, dh)).reshape(1, -1)
        w2k = jnp.pad(cmp_w2_k[l], ((0, 0), (0, LANES - dh))).astype(BF16)
        kc_aug = _compress(grp(kc), pos_e(cmp_pos_k[l]), _cmp_w1_expand(cmp_w1_k[l], g).astype(BF16),
                           w2k, True)
        vc_t = _compress(grp(vc), pos_e(cmp_pos_v[l]), _cmp_w1_expand(cmp_w1_v[l], g).astype(BF16),
                         cmp_w2_v[l].T.astype(BF16), False)

        y_nsa = _nsa(q_slots, gate_t, kc_aug, vc_t, ks_aug, kw_aug, v_t, ov_t, slope_tab)

        w_o = w_out[l].astype(BF16)
        x = _post(x, y_hg, y_nsa, ada3.reshape(bsz, n_mod, 1, d), w_o[:hgw], w_o[hgw:],
                  ffn_w1[l].astype(BF16), ffn_w3[l].astype(BF16), ffn_w2[l].astype(BF16),
                  ln1_g[l][None], ln1_b[l][None], ln2_g[l][None], ln2_b[l][None], alpha)
    return x
```

```python
import functools

import numpy as np
import jax
import jax.numpy as jnp
from jax import lax
from jax.experimental import pallas as pl
from jax.experimental.pallas import tpu as pltpu

F32 = jnp.float32
BF16 = jnp.bfloat16

HG_HEADS = 4
HG_D = 128
NSA_HEADS = 8
NSA_GROUPS = 2
NSA_HG = NSA_HEADS // NSA_GROUPS
NSA_DH = 64
CMP_BLOCK = 32
CMP_STRIDE = 16
CMP_HIDDEN = 256
SLC_BLOCK = 64
SLC_TOPN = 16
WINDOW = 512
Q_BLOCK = 128
NEG = -1e30
FORCE = 1e4
LN_EPS = 1e-5
RMS_EPS = 1e-6

LANES = 128
SUBLANES = 8
HG_BLK = 16
POS_SPLIT = 256
GATE_ROWS = 16
SEL_TILE = 512
SEL_UNROLL = 4
V_SLOT = 80
N_SPLIT = 3
LOG2E = 1.4426950408889634
MXU_DIM = 256
VMEM_LIMIT = 48 * 1024 * 1024
FFN_VMEM_LIMIT = 56 * 1024 * 1024


def _sigmoid(x):
    return 1.0 / (1.0 + jnp.exp(-x))


def _dot(a, b):
    return jnp.dot(a, b, preferred_element_type=F32)


def _dot_nt(a, b):
    return lax.dot_general(a, b, (((1,), (1,)), ((), ())), preferred_element_type=F32)


def _pos_columns(pos, col):
    c = col - NSA_DH
    lo = pos % POS_SPLIT
    val = jnp.where(c % 2 == 0, lo, pos - lo)
    return jnp.where((c >= 0) & (c < 2 * N_SPLIT), val, 0).astype(F32)


def _split3(x):
    hi = x.astype(BF16)
    r = x - hi.astype(F32)
    mid = r.astype(BF16)
    lo = (r - mid.astype(F32)).astype(BF16)
    return hi, mid, lo


def _ada_kernel(c_ref, w_ref, b_ref, o_ref):
    c = c_ref[...]
    s = (c * _sigmoid(c)).astype(BF16)
    o_ref[...] = _dot(s, w_ref[...].astype(BF16)) + b_ref[...]


def _ada(c_pad, w, b):
    rows, d = c_pad.shape
    n = w.shape[1]
    tn = 1024
    return pl.pallas_call(
        _ada_kernel,
        out_shape=jax.ShapeDtypeStruct((rows, n), F32),
        grid=(n // tn,),
        in_specs=[
            pl.BlockSpec((rows, d), lambda j: (0, 0)),
            pl.BlockSpec((d, tn), lambda j: (0, j)),
            pl.BlockSpec((1, tn), lambda j: (0, j)),
        ],
        out_specs=pl.BlockSpec((rows, tn), lambda j: (0, j)),
        compiler_params=pltpu.CompilerParams(
            dimension_semantics=("parallel",), vmem_limit_bytes=VMEM_LIMIT),
        name="ada",
    )(c_pad, w, b)


def _inproj_kernel(x_ref, sc_ref, sh_ref, wm_ref, wt_ref,
                   hg_ref, q_ref, ks_ref, kw_ref, kc_ref, vc_ref, vt_ref, gt_ref, *, tm):
    h = (x_ref[0] * (1.0 + sc_ref[0]) + sh_ref[0]).astype(BF16)
    y = _dot(h, wm_ref[...])
    n_hg = hg_ref.shape[2]
    n_q = q_ref.shape[2]
    n_k = ks_ref.shape[2]
    n_c = kc_ref.shape[2]
    o = 0
    hg_ref[0] = y[:, o:o + n_hg]
    o += n_hg
    q_ref[0] = y[:, o:o + n_q].astype(BF16)
    o += n_q
    row = pl.program_id(1) * tm + lax.broadcasted_iota(jnp.int32, (tm, n_k), 0)
    pc = _pos_columns(row, lax.broadcasted_iota(jnp.int32, (tm, n_k), 1) % LANES)
    ks_ref[0] = (y[:, o:o + n_k] + pc).astype(BF16)
    o += n_k
    kw_ref[0] = (y[:, o:o + n_k] + pc).astype(BF16)
    o += n_k
    kc_ref[0] = y[:, o:o + n_c]
    o += n_c
    vc_ref[0] = y[:, o:o + n_c]
    yt = _dot_nt(wt_ref[...], h)
    n_vt = vt_ref.shape[1]
    ri = lax.broadcasted_iota(jnp.int32, (n_vt, tm), 0) % V_SLOT
    vt_ref[0] = (yt[:n_vt] + jnp.where(ri == NSA_DH, 1.0, 0.0)).astype(BF16)
    gt_ref[0] = yt[n_vt:]


def _inproj(x, ada3, w_main, w_t, n_mod, widths):
    b, t, d = x.shape
    n_hg, n_q, n_k, n_c, n_vt, n_gt = widths
    tm = 256
    row_spec = lambda n: pl.BlockSpec((1, tm, n), lambda i, j: (i, j, 0))
    col_spec = lambda n: pl.BlockSpec((1, n, tm), lambda i, j: (i, 0, j))
    sds = jax.ShapeDtypeStruct
    return pl.pallas_call(
        functools.partial(_inproj_kernel, tm=tm),
        out_shape=(sds((b, t, n_hg), F32), sds((b, t, n_q), BF16), sds((b, t, n_k), BF16),
                   sds((b, t, n_k), BF16), sds((b, t, n_c), F32), sds((b, t, n_c), F32),
                   sds((b, n_vt, t), BF16), sds((b, n_gt, t), F32)),
        grid=(b, t // tm),
        in_specs=[
            pl.BlockSpec((1, tm, d), lambda i, j: (i, j, 0)),
            pl.BlockSpec((1, 1, d), lambda i, j: (i * n_mod + 1, 0, 0)),
            pl.BlockSpec((1, 1, d), lambda i, j: (i * n_mod, 0, 0)),
            pl.BlockSpec(w_main.shape, lambda i, j: (0, 0)),
            pl.BlockSpec(w_t.shape, lambda i, j: (0, 0)),
        ],
        out_specs=(row_spec(n_hg), row_spec(n_q), row_spec(n_k), row_spec(n_k), row_spec(n_c),
                   row_spec(n_c), col_spec(n_vt), col_spec(n_gt)),
        compiler_params=pltpu.CompilerParams(
            dimension_semantics=("parallel", "parallel"), vmem_limit_bytes=VMEM_LIMIT),
        name="inproj",
    )(x, ada3, ada3, w_main, w_t)


def _hgrn_kernel(q_ref, f_ref, i_ref, g_ref, lbl_ref, ng_ref, e_ref, o_ref, st_ref, *, tc, layer):
    blk = HG_BLK
    per_grp = LANES // blk
    half = blk // 2

    @pl.when(pl.program_id(2) == 0)
    def _():
        st_ref[...] = jnp.zeros_like(st_ref)

    lbl = lbl_ref[...]
    e = jnp.exp(lbl - jnp.max(lbl, axis=0, keepdims=True))
    lb = jnp.sum(e[: layer + 1], axis=0, keepdims=True) / jnp.sum(e, axis=0, keepdims=True)

    r_i = lax.broadcasted_iota(jnp.int32, (LANES, LANES), 0)
    c_i = lax.broadcasted_iota(jnp.int32, (LANES, LANES), 1)
    same_blk = (r_i // blk) == (c_i // blk)
    tri = jnp.where(same_blk & (c_i <= r_i), 1.0, 0.0).astype(BF16)
    row_blk = r_i // blk
    nh = per_grp * half
    r_h = lax.broadcasted_iota(jnp.int32, (nh, LANES), 0)
    same_half = (r_h // half) == (lax.broadcasted_iota(jnp.int32, (nh, LANES), 1) // blk)
    row_in_half = r_h % half
    blocked = lambda a: a.reshape(per_grp, blk, LANES)

    def gates(g):
        sl = slice(g * LANES, (g + 1) * LANES)
        q = q_ref[0, sl, :]
        qs = q * _sigmoid(q)
        f = lb + (1.0 - lb) * _sigmoid(f_ref[0, sl, :])
        hi, mid, lo = _split3(jnp.log2(f))
        b = _dot(tri, hi) + _dot(tri, mid) + _dot(tri, lo)
        return qs, 1.0 - f, i_ref[0, sl, :], b

    def increments(qs, k, v, b):
        b3 = blocked(b)
        bl3 = b3[:, blk - 1:blk, :]
        qd = (qs * jnp.exp2(b)).astype(BF16)
        kd = (blocked(k) * jnp.exp2(bl3 - b3)).reshape(LANES, LANES)
        kexp = jnp.concatenate(
            [jnp.where(row_blk == j, kd, 0.0) for j in range(per_grp)], axis=1).astype(BF16)
        return qd, jnp.exp2(bl3), _dot(v.T.astype(BF16), kexp)

    def recur(st, qd, dl, ut):
        ois = []
        for j in range(per_grp):
            ois.append(_dot_nt(qd[j * blk:(j + 1) * blk], st.astype(BF16)))
            st = dl[j] * st + ut[:, j * LANES:(j + 1) * LANES]
        return st, jnp.concatenate(ois, axis=0)

    def pairwise(qs, k, b):
        c = b - jnp.log2(k)
        tiles = lambda a, off: jnp.concatenate(
            [a[j * blk + off:j * blk + off + half] for j in range(per_grp)], axis=0)
        q_lo, q_hi, b_lo, b_hi = tiles(qs, 0), tiles(qs, half), tiles(b, 0), tiles(b, half)
        p_lo, p_hi = [], []
        for s in range(blk):
            cs = jnp.concatenate([jnp.broadcast_to(c[j * blk + s:j * blk + s + 1], (half, LANES))
                                  for j in range(per_grp)], axis=0)
            if s < half:
                p_lo.append(jnp.where(row_in_half >= s, q_lo * jnp.exp2(b_lo - cs), 0.0).astype(BF16))
                p_hi.append((q_hi * jnp.exp2(b_hi - cs)).astype(BF16))
            else:
                p_hi.append(jnp.where(row_in_half >= s - half, q_hi * jnp.exp2(b_hi - cs), 0.0).astype(BF16))
        return jnp.concatenate(p_lo, axis=1), jnp.concatenate(p_hi, axis=1)

    def finish(g, v, a_lo, a_hi, o_inter):
        vb = v.astype(BF16)
        o_lo = _dot(jnp.where(same_half, a_lo, 0.0).astype(BF16), vb)
        o_hi = _dot(jnp.where(same_half, a_hi, 0.0).astype(BF16), vb)
        o = jnp.concatenate([x[j * half:(j + 1) * half] for j in range(per_grp) for x in (o_lo, o_hi)],
                            axis=0) + o_inter
        o = o * lax.rsqrt(jnp.mean(o * o, axis=-1, keepdims=True) + RMS_EPS)
        sl = slice(g * LANES, (g + 1) * LANES)
        o_ref[0, sl, :] = (o * ng_ref[...] * _sigmoid(g_ref[0, sl, :])).astype(o_ref.dtype)

    ngrp = tc // LANES
    gs = [gates(g) for g in range(ngrp)]
    incs = [increments(*gt) for gt in gs]
    st = st_ref[...]
    o_inters = []
    for inc in incs:
        st, oi = recur(st, *inc)
        o_inters.append(oi)
    st_ref[...] = st
    pws = [pairwise(qs, k, b) for qs, k, _, b in gs]
    a_lo = _dot(jnp.concatenate([p[0] for p in pws], axis=0), e_ref[:half * LANES, :])
    a_hi = _dot(jnp.concatenate([p[1] for p in pws], axis=0), e_ref[...])
    for g in range(ngrp):
        rows = slice(g * nh, (g + 1) * nh)
        finish(g, gs[g][2], a_lo[rows], a_hi[rows], o_inters[g])


def _hgrn(proj, lb_logits, norm_g, layer):
    b, t, _ = proj.shape
    tc = 1024
    h = HG_HEADS
    nslots = lb_logits.shape[0]
    col = lambda seg: (lambda i, j, s: (i, s, seg * h + j))
    fold = np.zeros((HG_BLK, LANES, LANES), np.float32)
    for s in range(HG_BLK):
        fold[s, :, s::HG_BLK] = 1.0
    fold = jnp.asarray(fold.reshape(HG_BLK * LANES, LANES), BF16)
    return pl.pallas_call(
        functools.partial(_hgrn_kernel, tc=tc, layer=layer),
        out_shape=jax.ShapeDtypeStruct((b, t, h * HG_D), BF16),
        grid=(b, h, t // tc),
        in_specs=[
            pl.BlockSpec((1, tc, HG_D), col(0)),
            pl.BlockSpec((1, tc, HG_D), col(1)),
            pl.BlockSpec((1, tc, HG_D), col(2)),
            pl.BlockSpec((1, tc, HG_D), col(3)),
            pl.BlockSpec((nslots, HG_D), lambda i, j, s: (0, j)),
            pl.BlockSpec((1, HG_D), lambda i, j, s: (0, j)),
            pl.BlockSpec(fold.shape, lambda i, j, s: (0, 0)),
        ],
        out_specs=pl.BlockSpec((1, tc, HG_D), lambda i, j, s: (i, s, j)),
        scratch_shapes=[pltpu.VMEM((HG_D, HG_D), F32)],
        compiler_params=pltpu.CompilerParams(
            dimension_semantics=("parallel", "parallel", "arbitrary"), vmem_limit_bytes=VMEM_LIMIT),
        name="hgrn",
    )(proj, proj, proj, proj, lb_logits, norm_g, fold)


def _cmp_kernel(h_ref, pos_ref, w1_ref, w2_ref, o_ref, *, key_side):
    h = h_ref[0]
    half = h.shape[1]
    pos = pos_ref[...]
    p0 = _dot((h + pos[:, :half]).astype(BF16), w1_ref[0, :half])
    p1 = _dot((h + pos[:, half:]).astype(BF16), w1_ref[0, half:])
    ng = h.shape[0]
    pre = p0 + pltpu.roll(p1, ng - 1, 0)
    gl = 0.5 * pre * (1.0 + jnp.tanh(np.sqrt(2.0 / np.pi) * (pre + 0.044715 * (pre * pre * pre))))
    gl = gl.astype(BF16)
    if key_side:
        cend = lax.broadcasted_iota(jnp.int32, (ng, LANES), 0) * CMP_STRIDE + (CMP_BLOCK - 1)
        pc = _pos_columns(cend, lax.broadcasted_iota(jnp.int32, (ng, LANES), 1))
        o_ref[0, 0] = (_dot(gl, w2_ref[...]) + pc).astype(BF16)
    else:
        o_ref[0, 0] = _dot_nt(w2_ref[...], gl).astype(BF16)


def _compress(hgrp, pos, w1, w2, key_side):
    b, ng, half = hgrp.shape
    g = w1.shape[0]
    out_tail = (ng, LANES) if key_side else (w2.shape[0], ng)
    return pl.pallas_call(
        functools.partial(_cmp_kernel, key_side=key_side),
        out_shape=jax.ShapeDtypeStruct((b, g) + out_tail, BF16),
        grid=(b, g),
        in_specs=[
            pl.BlockSpec((1, ng, half), lambda i, j: (i, 0, 0)),
            pl.BlockSpec(pos.shape, lambda i, j: (0, 0)),
            pl.BlockSpec((1,) + w1.shape[1:], lambda i, j: (j, 0, 0)),
            pl.BlockSpec(w2.shape, lambda i, j: (0, 0)),
        ],
        out_specs=pl.BlockSpec((1, 1) + out_tail, lambda i, j: (i, j, 0, 0)),
        compiler_params=pltpu.CompilerParams(
            dimension_semantics=("parallel", "parallel"), vmem_limit_bytes=VMEM_LIMIT),
        name="cmp_k" if key_side else "cmp_v",
    )(hgrp, pos, w1, w2)


def _nsa_kernel(q_ref, gate_ref, kc_ref, vct_ref, ks_ref, vst_ref, kw_ref, vwt_ref, ovt_ref, slope_ref,
                o_ref, bias_ref, *z_scratch):
    qb = pl.program_id(2)
    t0 = qb * Q_BLOCK
    npair = NSA_HG // 2
    za_refs, zb_refs = z_scratch[:npair], z_scratch[npair:]
    nq = 2 * Q_BLOCK
    qblk = q_ref[0]
    qas = []
    for hp in range(npair):
        qa = jnp.concatenate([qblk[:, h * LANES:(h + 1) * LANES] for h in (2 * hp, 2 * hp + 1)], axis=0)
        qas.append(qa + slope_ref[0, hp * nq:(hp + 1) * nq])
    tpos = t0 + (lax.broadcasted_iota(jnp.int32, (1, nq), 1) % Q_BLOCK)

    def normalised(acc):
        return acc[:NSA_DH] * (1.0 / acc[NSA_DH:NSA_DH + 1])

    per_step = SEL_TILE // SLC_BLOCK
    spos0 = lax.broadcasted_iota(jnp.int32, (SEL_TILE, 1), 0)
    n_slab = WINDOW // Q_BLOCK + 1
    slab_kt = [qb - (n_slab - 1) + i for i in range(n_slab)]
    slab_at = [pl.multiple_of(jnp.maximum(kt, 0) * Q_BLOCK, Q_BLOCK) for kt in slab_kt]
    kw = jnp.concatenate([kw_ref[0, pl.ds(s, Q_BLOCK), :] for s in slab_at], axis=0)

    def sel_scores(st, z_refs):
        ks = ks_ref[0, pl.ds(pl.multiple_of(st * SEL_TILE, SEL_TILE), SEL_TILE), :]
        for z_ref, qa in zip(z_refs, qas):
            z_ref[...] = _dot_nt(ks, qa)

    zcs = [_dot_nt(kc_ref[0, 0], qa) for qa in qas]
    zws = [_dot_nt(kw, qa) for qa in qas]
    sel_scores(0, za_refs)

    ncp = kc_ref.shape[2]
    cend = lax.broadcasted_iota(jnp.int32, (ncp, 1), 0) * CMP_STRIDE + (CMP_BLOCK - 1)
    valid = cend <= tpos
    o_c, psum = [], None
    for zc in zcs:
        zm = jnp.where(valid, zc, NEG)
        ec = jnp.where(valid, jnp.exp2(zm - jnp.max(zm, axis=0, keepdims=True)), 0.0)
        lc = jnp.sum(ec, axis=0, keepdims=True)
        pc = ec * jnp.where(lc > 0.0, 1.0 / lc, 0.0)
        o_c.append(_dot(vct_ref[0, 0], pc.astype(BF16)))
        for i in range(2):
            ph = pc[:, i * Q_BLOCK:(i + 1) * Q_BLOCK]
            psum = ph if psum is None else psum + ph

    p_hi = psum.astype(BF16)
    p_lo = (psum - p_hi.astype(F32)).astype(BF16)
    imp = _dot(ovt_ref[...], p_hi) + _dot(ovt_ref[...], p_lo)
    ns = imp.shape[0]
    j_i = lax.broadcasted_iota(jnp.int32, (ns, 1), 0)
    t_abs = t0 + lax.broadcasted_iota(jnp.int32, (1, Q_BLOCK), 1)
    cur = t_abs // SLC_BLOCK
    forced = (j_i == 0) | (j_i == cur) | (j_i == cur - 1)
    x = jnp.where(forced, FORCE, imp)
    x = jnp.where(j_i * SLC_BLOCK <= t_abs, x, NEG)

    vw = jnp.concatenate([vwt_ref[0, :, pl.ds(s, Q_BLOCK)] for s in slab_at], axis=1)
    r_k = lax.broadcasted_iota(jnp.int32, (Q_BLOCK, nq), 0)
    r_q = lax.broadcasted_iota(jnp.int32, (Q_BLOCK, nq), 1) % Q_BLOCK
    o_w = []
    for zw in zws:
        slabs = []
        for i, kt in enumerate(slab_kt):
            zi = zw[i * Q_BLOCK:(i + 1) * Q_BLOCK]
            if i == 0:
                zi = jnp.where(r_k > r_q, zi, NEG)
            if i == n_slab - 1:
                zi = jnp.where(r_k <= r_q, zi, NEG)
            else:
                zi = zi + jnp.where(kt < 0, NEG, 0.0)
            slabs.append(zi)
        zw = jnp.concatenate(slabs, axis=0)
        pw = jnp.exp2(zw - jnp.max(zw, axis=0, keepdims=True))
        o_w.append(normalised(_dot(vw, pw.astype(BF16))))

    sub = lax.broadcasted_iota(jnp.int32, (SUBLANES, Q_BLOCK), 0)
    tiles = [x[v * SUBLANES:(v + 1) * SUBLANES] for v in range(ns // SUBLANES)]
    cnts = [jnp.zeros((SUBLANES, Q_BLOCK), F32) for _ in tiles]
    for jp in range(ns):
        r = x[jp:jp + 1, :]
        vj, rj = divmod(jp, SUBLANES)
        for v, xt in enumerate(tiles):
            if v < vj:
                ahead = r > xt
            elif v > vj:
                ahead = r >= xt
            else:
                ahead = (r > xt) | ((r == xt) & (sub > rj))
            cnts[v] = cnts[v] + jnp.where(ahead, 1.0, 0.0)
    for v, cnt in enumerate(cnts):
        bias_ref[v * SUBLANES:(v + 1) * SUBLANES, :] = jnp.where(cnt < float(SLC_TOPN), 0.0, NEG)

    def sel_update(st, z_refs, carry, causal):
        s0 = pl.multiple_of(st * SEL_TILE, SEL_TILE)
        vs = vst_ref[0, :, pl.ds(s0, SEL_TILE)]
        brow = bias_ref[pl.ds(pl.multiple_of(st * per_step, per_step), per_step), :]
        b2 = jnp.concatenate([brow] * 2, axis=1)[:, None, :]
        out = []
        for z_ref, (m, acc) in zip(z_refs, carry):
            z = (z_ref[...].reshape(per_step, SLC_BLOCK, nq) + b2).reshape(SEL_TILE, nq)
            if causal:
                z = jnp.where((s0 + spos0) <= tpos, z, NEG)
            mn = jnp.maximum(m, jnp.max(z, axis=0, keepdims=True))
            p = jnp.exp2(z - mn)
            out.append((mn, jnp.exp2(m - mn) * acc + _dot(vs, p.astype(BF16))))
        return tuple(out)

    def run_steps(first, count, carry, ends_causal):
        bufs = (za_refs, zb_refs)
        for i in range(count):
            if i + 1 < count or not ends_causal:
                sel_scores(first + i + 1, bufs[(i + 1) % 2])
            carry = sel_update(first + i, bufs[i % 2], carry, ends_causal and i + 1 == count)
        return carry

    init = tuple((jnp.full((1, nq), NEG, F32), jnp.zeros((V_SLOT, nq), F32)) for _ in range(npair))
    last = qb // (SEL_TILE // Q_BLOCK)
    n_loop = last // SEL_UNROLL
    carry = lax.fori_loop(0, n_loop, lambda i, c: run_steps(SEL_UNROLL * i, SEL_UNROLL, c, False), init)
    tails = [functools.partial(run_steps, SEL_UNROLL * n_loop, r + 1, ends_causal=True)
             for r in range(SEL_UNROLL)]
    carry = lax.switch(last % SEL_UNROLL, tails, carry)
    o_s = [normalised(acc) for _, acc in carry]

    gts = _sigmoid(gate_ref[0])
    pairs = []
    for hp in range(npair):
        halves = []
        for i in range(2):
            r = 3 * (2 * hp + i)
            sl = slice(i * Q_BLOCK, (i + 1) * Q_BLOCK)
            halves.append(gts[r:r + 1] * o_c[hp][:, sl] + gts[r + 1:r + 2] * o_s[hp][:, sl]
                          + gts[r + 2:r + 3] * o_w[hp][:, sl])
        pairs.append(jnp.concatenate(halves, axis=0).T)
    o_ref[0] = jnp.concatenate(pairs, axis=1).astype(o_ref.dtype)


def _nsa(q_slots, gate_t, kc_aug, vc_t, ks_aug, kw_aug, v_t, ov_t, slope_tab):
    b, t, _ = q_slots.shape
    g = NSA_GROUPS
    nqb = t // Q_BLOCK
    ncp = kc_aug.shape[2]
    ns = ov_t.shape[0]
    return pl.pallas_call(
        _nsa_kernel,
        out_shape=jax.ShapeDtypeStruct((b, t, NSA_HEADS * NSA_DH), BF16),
        grid=(b, g, nqb),
        in_specs=[
            pl.BlockSpec((1, Q_BLOCK, NSA_HG * LANES), lambda i, j, s: (i, s, j)),
            pl.BlockSpec((1, GATE_ROWS, Q_BLOCK), lambda i, j, s: (i, j, s)),
            pl.BlockSpec((1, 1, ncp, LANES), lambda i, j, s: (i, j, 0, 0)),
            pl.BlockSpec((1, 1, NSA_DH, ncp), lambda i, j, s: (i, j, 0, 0)),
            pl.BlockSpec((1, t, LANES), lambda i, j, s: (i, 0, j)),
            pl.BlockSpec((1, V_SLOT, t), lambda i, j, s: (i, j, 0)),
            pl.BlockSpec((1, t, LANES), lambda i, j, s: (i, 0, j)),
            pl.BlockSpec((1, V_SLOT, t), lambda i, j, s: (i, g + j, 0)),
            pl.BlockSpec((ns, ncp), lambda i, j, s: (0, 0)),
            pl.BlockSpec((1, NSA_HG * Q_BLOCK, LANES), lambda i, j, s: (j, 0, 0)),
        ],
        out_specs=pl.BlockSpec((1, Q_BLOCK, NSA_HG * NSA_DH), lambda i, j, s: (i, s, j)),
        scratch_shapes=[pltpu.VMEM((ns, Q_BLOCK), F32)]
        + [pltpu.VMEM((SEL_TILE, 2 * Q_BLOCK), F32)] * NSA_HG,
        compiler_params=pltpu.CompilerParams(
            dimension_semantics=("parallel", "parallel", "arbitrary"), vmem_limit_bytes=VMEM_LIMIT),
        name="nsa",
    )(q_slots, gate_t, kc_aug, vc_t, ks_aug, v_t, kw_aug, v_t, ov_t, slope_tab)


def _layer_norm(z, g, b):
    mu = jnp.mean(z, axis=-1, keepdims=True)
    zc = z - mu
    var = jnp.mean(zc * zc, axis=-1, keepdims=True)
    return zc * lax.rsqrt(var + LN_EPS) * g + b


def _post_kernel(x_ref, yh_ref, yn_ref, mod_ref, wh_ref, wn_ref, w1_ref, w3_ref, w2_ref,
                 l1g_ref, l1b_ref, l2g_ref, l2b_ref, o_ref, *, alpha, chunks):
    g1, sh2, sc2, g2 = mod_ref[0, 2], mod_ref[0, 3], mod_ref[0, 4], mod_ref[0, 5]
    mix = _dot(yh_ref[0], wh_ref[...]) + _dot(yn_ref[0], wn_ref[...])
    x1 = _layer_norm(alpha * x_ref[0] + g1 * mix, l1g_ref[...], l1b_ref[...])
    h = (x1 * (1.0 + sc2) + sh2).astype(BF16)
    ups = []
    for lo, hi in chunks:
        ups.append((_dot(h, w1_ref[:, lo:hi]), _dot(h, w3_ref[:, lo:hi])))
    a = jnp.concatenate([(u * _sigmoid(u) * w).astype(BF16) for u, w in ups], axis=1)
    z = alpha * x1 + g2 * _dot(a, w2_ref[...])
    o_ref[0] = _layer_norm(z, l2g_ref[...], l2b_ref[...])


def _post(x, y_hg, y_nsa, ada4, w_h, w_n, w1, w3, w2, ln1_g, ln1_b, ln2_g, ln2_b, alpha):
    b, t, d = x.shape
    dff = w1.shape[1]
    tm = 512
    step = 3 * MXU_DIM
    chunks = tuple((lo, min(lo + step, dff)) for lo in range(0, dff, step))
    resident = lambda a: pl.BlockSpec(a.shape, lambda i, j: (0,) * a.ndim, pipeline_mode=pl.Buffered(1))
    rows = lambda a: pl.BlockSpec((1, tm, a.shape[2]), lambda i, j: (i, j, 0))
    vec = pl.BlockSpec((1, d), lambda i, j: (0, 0))
    return pl.pallas_call(
        functools.partial(_post_kernel, alpha=alpha, chunks=chunks),
        out_shape=jax.ShapeDtypeStruct((b, t, d), F32),
        grid=(b, t // tm),
        in_specs=[rows(x), rows(y_hg), rows(y_nsa),
                  pl.BlockSpec((1,) + ada4.shape[1:], lambda i, j: (i, 0, 0, 0)),
                  resident(w_h), resident(w_n), resident(w1), resident(w3), resident(w2),
                  vec, vec, vec, vec],
        out_specs=rows(x),
        compiler_params=pltpu.CompilerParams(
            dimension_semantics=("parallel", "parallel"), vmem_limit_bytes=FFN_VMEM_LIMIT),
        name="post",
    )(x, y_hg, y_nsa, ada4, w_h, w_n, w1, w3, w2, ln1_g, ln1_b, ln2_g, ln2_b)


def _slope_table(group_slopes):
    g, hg = group_slopes.shape
    rest = (group_slopes * LOG2E).astype(np.float32)
    tab = np.zeros((g, hg, Q_BLOCK, LANES), jnp.bfloat16)
    for i in range(N_SPLIT):
        piece = rest.astype(jnp.bfloat16)
        rest = rest - piece.astype(np.float32)
        tab[..., NSA_DH + 2 * i] = piece[:, :, None]
        tab[..., NSA_DH + 2 * i + 1] = piece[:, :, None]
    return tab.reshape(g, hg * Q_BLOCK, LANES)


def _pad_slots(w, n_slots):
    d = w.shape[0]
    w3 = w.reshape(d, n_slots, -1)
    return jnp.pad(w3, ((0, 0), (0, 0), (0, LANES - w3.shape[2]))).reshape(d, n_slots * LANES)


def _cmp_w1_expand(w1, g):
    hid = w1.shape[1]
    w1r = w1.reshape(2, CMP_STRIDE, 1, NSA_DH, hid)
    eye = jnp.eye(g, dtype=w1.dtype).reshape(g, 1, 1, g, 1, 1)
    return (eye * w1r[None]).reshape(g, 2 * CMP_STRIDE * g * NSA_DH, hid)


def _overlap_t(n_cmp_pad, n_cmp, n_slc):
    cs = np.arange(n_cmp_pad) * CMP_STRIDE
    ss = np.arange(n_slc) * SLC_BLOCK
    ov = np.clip(np.minimum(cs[:, None] + CMP_BLOCK, ss[None] + SLC_BLOCK)
                 - np.maximum(cs[:, None], ss[None]), 0, None).astype(np.float32) / CMP_BLOCK
    ov[n_cmp:] = 0.0
    return ov.T


def kernel(x, c, w_ada, b_ada, w_in, hg_lb_logits, hg_norm_g, cmp_pos_k, cmp_w1_k, cmp_w2_k,
           cmp_pos_v, cmp_w1_v, cmp_w2_v, w_out, ln1_g, ln1_b, ffn_w1, ffn_w3, ffn_w2, ln2_g, ln2_b):
    bsz, t, d = x.shape
    depth = w_ada.shape[0]
    g, hg, dh = NSA_GROUPS, NSA_HG, NSA_DH
    hgw = HG_HEADS * HG_D
    nsw = NSA_HEADS * dh
    kvw = g * dh
    n_mod = 6
    alpha = (2.0 * depth) ** 0.25
    scale = dh ** -0.5
    n_grp = t // CMP_STRIDE
    n_cmp = (t - CMP_BLOCK) // CMP_STRIDE + 1
    n_slc = t // SLC_BLOCK
    assert t % SEL_TILE == 0 and t >= WINDOW + Q_BLOCK and 3 * hg <= GATE_ROWS and hg % 2 == 0

    slopes = (2.0 ** (-8.0 * (np.arange(NSA_HEADS) + 1) / NSA_HEADS)).reshape(g, hg)
    slope_tab = jnp.asarray(_slope_table(slopes), BF16)
    ov_t = jnp.asarray(_overlap_t(n_grp, n_cmp, n_slc), BF16)

    c_pad = jnp.pad(c, ((0, 8 - bsz), (0, 0)))
    for l in range(depth):
        ada = _ada(c_pad, w_ada[l], b_ada[l][None])
        ada3 = ada[:bsz].reshape(bsz * n_mod, 1, d)

        w = w_in[l]
        o = 4 * hgw
        w_q = w[:, o:o + nsw]
        w_kv = [w[:, o + nsw + i * kvw: o + nsw + (i + 1) * kvw] for i in range(6)]
        w_g = w[:, o + nsw + 6 * kvw: o + nsw + 6 * kvw + 3 * NSA_HEADS]
        w_main = jnp.concatenate(
            [w[:, :o], _pad_slots(w_q * (scale * LOG2E), NSA_HEADS), _pad_slots(w_kv[2], g),
             _pad_slots(w_kv[4], g), w_kv[0], w_kv[1]], axis=1).astype(BF16)
        w_gt = jnp.pad(w_g.reshape(d, g, 3 * hg), ((0, 0), (0, 0), (0, GATE_ROWS - 3 * hg))).reshape(d, -1)
        v_slots = lambda a: jnp.pad(a.reshape(d, g, dh), ((0, 0), (0, 0), (0, V_SLOT - dh))).reshape(d, -1)
        w_t = jnp.concatenate([v_slots(w_kv[3]), v_slots(w_kv[5]), w_gt], axis=1).T.astype(BF16)
        widths = (o, NSA_HEADS * LANES, g * LANES, kvw, 2 * g * V_SLOT, g * GATE_ROWS)
        hg_in, q_slots, ks_aug, kw_aug, kc, vc, v_t, gate_t = _inproj(x, ada3, w_main, w_t, n_mod, widths)

        y_hg = _hgrn(hg_in, hg_lb_logits, hg_norm_g[l][None], l)

        grp = lambda a: a.reshape(bsz, n_grp, CMP_STRIDE * kvw)
        pos_e = lambda p: jnp.broadcast_to(
            p.reshape(2, CMP_STRIDE, 1, dh), (2, CMP_STRIDE, g, dh)).reshape(1, -1)
        w2k = jnp.pad(cmp_w2_k[l], ((0, 0), (0, LANES - dh))).astype(BF16)
        kc_aug = _compress(grp(kc), pos_e(cmp_pos_k[l]), _cmp_w1_expand(cmp_w1_k[l], g).astype(BF16),
                           w2k, True)
        vc_t = _compress(grp(vc), pos_e(cmp_pos_v[l]), _cmp_w1_expand(cmp_w1_v[l], g).astype(BF16),
                         cmp_w2_v[l].T.astype(BF16), False)

        y_nsa = _nsa(q_slots, gate_t, kc_aug, vc_t, ks_aug, kw_aug, v_t, ov_t, slope_tab)

        w_o = w_out[l].astype(BF16)
        x = _post(x, y_hg, y_nsa, ada3.reshape(bsz, n_mod, 1, d), w_o[:hgw], w_o[hgw:],
                  ffn_w1[l].astype(BF16), ffn_w3[l].astype(BF16), ffn_w2[l].astype(BF16),
                  ln1_g[l][None], ln1_b[l][None], ln2_g[l][None], ln2_b[l][None], alpha)
    return x
```

```python
import functools

import numpy as np
import jax
import jax.numpy as jnp
from jax import lax
from jax.experimental import pallas as pl
from jax.experimental.pallas import tpu as pltpu

F32 = jnp.float32
BF16 = jnp.bfloat16

HG_HEADS = 4
HG_D = 128
NSA_HEADS = 8
NSA_GROUPS = 2
NSA_HG = NSA_HEADS // NSA_GROUPS
NSA_DH = 64
CMP_BLOCK = 32
CMP_STRIDE = 16
CMP_HIDDEN = 256
SLC_BLOCK = 64
SLC_TOPN = 16
WINDOW = 512
Q_BLOCK = 128
NEG = -1e30
FORCE = 1e4
LN_EPS = 1e-5
RMS_EPS = 1e-6

LANES = 128
SUBLANES = 8
HG_BLK = 16
POS_SPLIT = 256
GATE_ROWS = 16
SEL_TILE = 512
SEL_UNROLL = 4
V_SLOT = 80
N_SPLIT = 3
LOG2E = 1.4426950408889634
MXU_DIM = 256
VMEM_LIMIT = 48 * 1024 * 1024
FFN_VMEM_LIMIT = 56 * 1024 * 1024


def _sigmoid(x):
    return 1.0 / (1.0 + jnp.exp(-x))


def _dot(a, b):
    return jnp.dot(a, b, preferred_element_type=F32)


def _dot_nt(a, b):
    return lax.dot_general(a, b, (((1,), (1,)), ((), ())), preferred_element_type=F32)


def _pos_columns(pos, col):
    c = col - NSA_DH
    lo = pos % POS_SPLIT
    val = jnp.where(c % 2 == 0, lo, pos - lo)
    return jnp.where((c >= 0) & (c < 2 * N_SPLIT), val, 0).astype(F32)


def _split3(x):
    hi = x.astype(BF16)
    r = x - hi.astype(F32)
    mid = r.astype(BF16)
    lo = (r - mid.astype(F32)).astype(BF16)
    return hi, mid, lo


def _ada_kernel(c_ref, w_ref, b_ref, o_ref):
    c = c_ref[...]
    s = (c * _sigmoid(c)).astype(BF16)
    o_ref[...] = _dot(s, w_ref[...].astype(BF16)) + b_ref[...]


def _ada(c_pad, w, b):
    rows, d = c_pad.shape
    n = w.shape[1]
    tn = 1024
    return pl.pallas_call(
        _ada_kernel,
        out_shape=jax.ShapeDtypeStruct((rows, n), F32),
        grid=(n // tn,),
        in_specs=[
            pl.BlockSpec((rows, d), lambda j: (0, 0)),
            pl.BlockSpec((d, tn), lambda j: (0, j)),
            pl.BlockSpec((1, tn), lambda j: (0, j)),
        ],
        out_specs=pl.BlockSpec((rows, tn), lambda j: (0, j)),
        compiler_params=pltpu.CompilerParams(
            dimension_semantics=("parallel",), vmem_limit_bytes=VMEM_LIMIT),
        name="ada",
    )(c_pad, w, b)


def _inproj_kernel(x_ref, sc_ref, sh_ref, wm_ref, wt_ref,
                   hg_ref, q_ref, ks_ref, kw_ref, kc_ref, vc_ref, vt_ref, gt_ref, *, tm):
    h = (x_ref[0] * (1.0 + sc_ref[0]) + sh_ref[0]).astype(BF16)
    y = _dot(h, wm_ref[...])
    n_hg = hg_ref.shape[2]
    n_q = q_ref.shape[2]
    n_k = ks_ref.shape[2]
    n_c = kc_ref.shape[2]
    o = 0
    hg_ref[0] = y[:, o:o + n_hg]
    o += n_hg
    q_ref[0] = y[:, o:o + n_q].astype(BF16)
    o += n_q
    row = pl.program_id(1) * tm + lax.broadcasted_iota(jnp.int32, (tm, n_k), 0)
    pc = _pos_columns(row, lax.broadcasted_iota(jnp.int32, (tm, n_k), 1) % LANES)
    ks_ref[0] = (y[:, o:o + n_k] + pc).astype(BF16)
    o += n_k
    kw_ref[0] = (y[:, o:o + n_k] + pc).astype(BF16)
    o += n_k
    kc_ref[0] = y[:, o:o + n_c]
    o += n_c
    vc_ref[0] = y[:, o:o + n_c]
    yt = _dot_nt(wt_ref[...], h)
    n_vt = vt_ref.shape[1]
    ri = lax.broadcasted_iota(jnp.int32, (n_vt, tm), 0) % V_SLOT
    vt_ref[0] = (yt[:n_vt] + jnp.where(ri == NSA_DH, 1.0, 0.0)).astype(BF16)
    gt_ref[0] = yt[n_vt:]


def _inproj(x, ada3, w_main, w_t, n_mod, widths):
    b, t, d = x.shape
    n_hg, n_q, n_k, n_c, n_vt, n_gt = widths
    tm = 256
    row_spec = lambda n: pl.BlockSpec((1, tm, n), lambda i, j: (i, j, 0))
    col_spec = lambda n: pl.BlockSpec((1, n, tm), lambda i, j: (i, 0, j))
    sds = jax.ShapeDtypeStruct
    return pl.pallas_call(
        functools.partial(_inproj_kernel, tm=tm),
        out_shape=(sds((b, t, n_hg), F32), sds((b, t, n_q), BF16), sds((b, t, n_k), BF16),
                   sds((b, t, n_k), BF16), sds((b, t, n_c), F32), sds((b, t, n_c), F32),
                   sds((b, n_vt, t), BF16), sds((b, n_gt, t), F32)),
        grid=(b, t // tm),
        in_specs=[
            pl.BlockSpec((1, tm, d), lambda i, j: (i, j, 0)),
            pl.BlockSpec((1, 1, d), lambda i, j: (i * n_mod + 1, 0, 0)),
            pl.BlockSpec((1, 1, d), lambda i, j: (i * n_mod, 0, 0)),
            pl.BlockSpec(w_main.shape, lambda i, j: (0, 0)),
            pl.BlockSpec(w_t.shape, lambda i, j: (0, 0)),
        ],
        out_specs=(row_spec(n_hg), row_spec(n_q), row_spec(n_k), row_spec(n_k), row_spec(n_c),
                   row_spec(n_c), col_spec(n_vt), col_spec(n_gt)),
        compiler_params=pltpu.CompilerParams(
            dimension_semantics=("parallel", "parallel"), vmem_limit_bytes=VMEM_LIMIT),
        name="inproj",
    )(x, ada3, ada3, w_main, w_t)


def _hgrn_kernel(q_ref, f_ref, i_ref, g_ref, lbl_ref, ng_ref, e_ref, o_ref, st_ref, *, tc, layer):
    blk = HG_BLK
    per_grp = LANES // blk
    half = blk // 2

    @pl.when(pl.program_id(2) == 0)
    def _():
        st_ref[...] = jnp.zeros_like(st_ref)

    lbl = lbl_ref[...]
    e = jnp.exp(lbl - jnp.max(lbl, axis=0, keepdims=True))
    lb = jnp.sum(e[: layer + 1], axis=0, keepdims=True) / jnp.sum(e, axis=0, keepdims=True)

    r_i = lax.broadcasted_iota(jnp.int32, (LANES, LANES), 0)
    c_i = lax.broadcasted_iota(jnp.int32, (LANES, LANES), 1)
    same_blk = (r_i // blk) == (c_i // blk)
    tri = jnp.where(same_blk & (c_i <= r_i), 1.0, 0.0).astype(BF16)
    row_blk = r_i // blk
    nh = per_grp * half
    r_h = lax.broadcasted_iota(jnp.int32, (nh, LANES), 0)
    same_half = (r_h // half) == (lax.broadcasted_iota(jnp.int32, (nh, LANES), 1) // blk)
    row_in_half = r_h % half
    blocked = lambda a: a.reshape(per_grp, blk, LANES)

    def gates(g):
        sl = slice(g * LANES, (g + 1) * LANES)
        q = q_ref[0, sl, :]
        qs = q * _sigmoid(q)
        f = lb + (1.0 - lb) * _sigmoid(f_ref[0, sl, :])
        hi, mid, lo = _split3(jnp.log2(f))
        b = _dot(tri, hi) + _dot(tri, mid) + _dot(tri, lo)
        return qs, 1.0 - f, i_ref[0, sl, :], b

    def increments(qs, k, v, b):
        b3 = blocked(b)
        bl3 = b3[:, blk - 1:blk, :]
        qd = (qs * jnp.exp2(b)).astype(BF16)
        kd = (blocked(k) * jnp.exp2(bl3 - b3)).reshape(LANES, LANES)
        kexp = jnp.concatenate(
            [jnp.where(row_blk == j, kd, 0.0) for j in range(per_grp)], axis=1).astype(BF16)
        return qd, jnp.exp2(bl3), _dot(v.T.astype(BF16), kexp)

    def recur(st, qd, dl, ut):
        ois = []
        for j in range(per_grp):
            ois.append(_dot_nt(qd[j * blk:(j + 1) * blk], st.astype(BF16)))
            st = dl[j] * st + ut[:, j * LANES:(j + 1) * LANES]
        return st, jnp.concatenate(ois, axis=0)

    def pairwise(qs, k, b):
        c = b - jnp.log2(k)
        tiles = lambda a, off: jnp.concatenate(
            [a[j * blk + off:j * blk + off + half] for j in range(per_grp)], axis=0)
        q_lo, q_hi, b_lo, b_hi = tiles(qs, 0), tiles(qs, half), tiles(b, 0), tiles(b, half)
        p_lo, p_hi = [], []
        for s in range(blk):
            cs = jnp.concatenate([jnp.broadcast_to(c[j * blk + s:j * blk + s + 1], (half, LANES))
                                  for j in range(per_grp)], axis=0)
            if s < half:
                p_lo.append(jnp.where(row_in_half >= s, q_lo * jnp.exp2(b_lo - cs), 0.0).astype(BF16))
                p_hi.append((q_hi * jnp.exp2(b_hi - cs)).astype(BF16))
            else:
                p_hi.append(jnp.where(row_in_half >= s - half, q_hi * jnp.exp2(b_hi - cs), 0.0).astype(BF16))
        return jnp.concatenate(p_lo, axis=1), jnp.concatenate(p_hi, axis=1)

    def finish(g, v, a_lo, a_hi, o_inter):
        vb = v.astype(BF16)
        o_lo = _dot(jnp.where(same_half, a_lo, 0.0).astype(BF16), vb)
        o_hi = _dot(jnp.where(same_half, a_hi, 0.0).astype(BF16), vb)
        o = jnp.concatenate([x[j * half:(j + 1) * half] for j in range(per_grp) for x in (o_lo, o_hi)],
                            axis=0) + o_inter
        o = o * lax.rsqrt(jnp.mean(o * o, axis=-1, keepdims=True) + RMS_EPS)
        sl = slice(g * LANES, (g + 1) * LANES)
        o_ref[0, sl, :] = (o * ng_ref[...] * _sigmoid(g_ref[0, sl, :])).astype(o_ref.dtype)

    ngrp = tc // LANES
    gs = [gates(g) for g in range(ngrp)]
    incs = [increments(*gt) for gt in gs]
    st = st_ref[...]
    o_inters = []
    for inc in incs:
        st, oi = recur(st, *inc)
        o_inters.append(oi)
    st_ref[...] = st
    pws = [pairwise(qs, k, b) for qs, k, _, b in gs]
    a_lo = _dot(jnp.concatenate([p[0] for p in pws], axis=0), e_ref[:half * LANES, :])
    a_hi = _dot(jnp.concatenate([p[1] for p in pws], axis=0), e_ref[...])
    for g in range(ngrp):
        rows = slice(g * nh, (g + 1) * nh)
        finish(g, gs[g][2], a_lo[rows], a_hi[rows], o_inters[g])


def _hgrn(proj, lb_logits, norm_g, layer):
    b, t, _ = proj.shape
    tc = 1024
    h = HG_HEADS
    nslots = lb_logits.shape[0]
    col = lambda seg: (lambda i, j, s: (i, s, seg * h + j))
    fold = np.zeros((HG_BLK, LANES, LANES), np.float32)
    for s in range(HG_BLK):
        fold[s, :, s::HG_BLK] = 1.0
    fold = jnp.asarray(fold.reshape(HG_BLK * LANES, LANES), BF16)
    return pl.pallas_call(
        functools.partial(_hgrn_kernel, tc=tc, layer=layer),
        out_shape=jax.ShapeDtypeStruct((b, t, h * HG_D), BF16),
        grid=(b, h, t // tc),
        in_specs=[
            pl.BlockSpec((1, tc, HG_D), col(0)),
            pl.BlockSpec((1, tc, HG_D), col(1)),
            pl.BlockSpec((1, tc, HG_D), col(2)),
            pl.BlockSpec((1, tc, HG_D), col(3)),
            pl.BlockSpec((nslots, HG_D), lambda i, j, s: (0, j)),
            pl.BlockSpec((1, HG_D), lambda i, j, s: (0, j)),
            pl.BlockSpec(fold.shape, lambda i, j, s: (0, 0)),
        ],
        out_specs=pl.BlockSpec((1, tc, HG_D), lambda i, j, s: (i, s, j)),
        scratch_shapes=[pltpu.VMEM((HG_D, HG_D), F32)],
        compiler_params=pltpu.CompilerParams(
            dimension_semantics=("parallel", "parallel", "arbitrary"), vmem_limit_bytes=VMEM_LIMIT),
        name="hgrn",
    )(proj, proj, proj, proj, lb_logits, norm_g, fold)


def _cmp_kernel(h_ref, pos_ref, w1_ref, w2_ref, o_ref, *, key_side):
    h = h_ref[0]
    half = h.shape[1]
    pos = pos_ref[...]
    p0 = _dot((h + pos[:, :half]).astype(BF16), w1_ref[0, :half])
    p1 = _dot((h + pos[:, half:]).astype(BF16), w1_ref[0, half:])
    ng = h.shape[0]
    pre = p0 + pltpu.roll(p1, ng - 1, 0)
    gl = 0.5 * pre * (1.0 + jnp.tanh(np.sqrt(2.0 / np.pi) * (pre + 0.044715 * (pre * pre * pre))))
    gl = gl.astype(BF16)
    if key_side:
        cend = lax.broadcasted_iota(jnp.int32, (ng, LANES), 0) * CMP_STRIDE + (CMP_BLOCK - 1)
        pc = _pos_columns(cend, lax.broadcasted_iota(jnp.int32, (ng, LANES), 1))
        o_ref[0, 0] = (_dot(gl, w2_ref[...]) + pc).astype(BF16)
    else:
        o_ref[0, 0] = _dot_nt(w2_ref[...], gl).astype(BF16)


def _compress(hgrp, pos, w1, w2, key_side):
    b, ng, half = hgrp.shape
    g = w1.shape[0]
    out_tail = (ng, LANES) if key_side else (w2.shape[0], ng)
    return pl.pallas_call(
        functools.partial(_cmp_kernel, key_side=key_side),
        out_shape=jax.ShapeDtypeStruct((b, g) + out_tail, BF16),
        grid=(b, g),
        in_specs=[
            pl.BlockSpec((1, ng, half), lambda i, j: (i, 0, 0)),
            pl.BlockSpec(pos.shape, lambda i, j: (0, 0)),
            pl.BlockSpec((1,) + w1.shape[1:], lambda i, j: (j, 0, 0)),
            pl.BlockSpec(w2.shape, lambda i, j: (0, 0)),
        ],
        out_specs=pl.BlockSpec((1, 1) + out_tail, lambda i, j: (i, j, 0, 0)),
        compiler_params=pltpu.CompilerParams(
            dimension_semantics=("parallel", "parallel"), vmem_limit_bytes=VMEM_LIMIT),
        name="cmp_k" if key_side else "cmp_v",
    )(hgrp, pos, w1, w2)


def _nsa_kernel(q_ref, gate_ref, kc_ref, vct_ref, ks_ref, vst_ref, kw_ref, vwt_ref, ovt_ref, slope_ref,
                o_ref, bias_ref, *z_scratch):
    qb = pl.program_id(2)
    t0 = qb * Q_BLOCK
    npair = NSA_HG // 2
    za_refs, zb_refs = z_scratch[:npair], z_scratch[npair:]
    nq = 2 * Q_BLOCK
    qblk = q_ref[0].astype(F32)
    low = lax.broadcasted_iota(jnp.int32, (Q_BLOCK, LANES), 1) < NSA_DH
    qas = []
    for hp in range(npair):
        pair = qblk[:, hp * LANES:(hp + 1) * LANES]
        heads = [jnp.where(low, x, 0.0) for x in (pair, pltpu.roll(pair, NSA_DH, 1))]
        qas.append(jnp.concatenate(heads, axis=0).astype(BF16) + slope_ref[0, hp * nq:(hp + 1) * nq])
    tpos = t0 + (lax.broadcasted_iota(jnp.int32, (1, nq), 1) % Q_BLOCK)

    def normalised(acc):
        return acc[:NSA_DH] * (1.0 / acc[NSA_DH:NSA_DH + 1])

    per_step = SEL_TILE // SLC_BLOCK
    spos0 = lax.broadcasted_iota(jnp.int32, (SEL_TILE, 1), 0)
    n_slab = WINDOW // Q_BLOCK + 1
    slab_kt = [qb - (n_slab - 1) + i for i in range(n_slab)]
    slab_at = [pl.multiple_of(jnp.maximum(kt, 0) * Q_BLOCK, Q_BLOCK) for kt in slab_kt]
    kw = jnp.concatenate([kw_ref[0, pl.ds(s, Q_BLOCK), :] for s in slab_at], axis=0)

    def sel_scores(st, z_refs):
        ks = ks_ref[0, pl.ds(pl.multiple_of(st * SEL_TILE, SEL_TILE), SEL_TILE), :]
        for z_ref, qa in zip(z_refs, qas):
            z_ref[...] = _dot_nt(ks, qa)

    zcs = [_dot_nt(kc_ref[0, 0], qa) for qa in qas]
    zws = [_dot_nt(kw, qa) for qa in qas]
    sel_scores(0, za_refs)

    ncp = kc_ref.shape[2]
    cend = lax.broadcasted_iota(jnp.int32, (ncp, 1), 0) * CMP_STRIDE + (CMP_BLOCK - 1)
    valid = cend <= tpos
    o_c, psum = [], None
    for zc in zcs:
        zm = jnp.where(valid, zc, NEG)
        ec = jnp.where(valid, jnp.exp2(zm - jnp.max(zm, axis=0, keepdims=True)), 0.0)
        lc = jnp.sum(ec, axis=0, keepdims=True)
        pc = ec * jnp.where(lc > 0.0, 1.0 / lc, 0.0)
        o_c.append(_dot(vct_ref[0, 0], pc.astype(BF16)))
        for i in range(2):
            ph = pc[:, i * Q_BLOCK:(i + 1) * Q_BLOCK]
            psum = ph if psum is None else psum + ph

    p_hi = psum.astype(BF16)
    p_lo = (psum - p_hi.astype(F32)).astype(BF16)
    imp = _dot(ovt_ref[...], p_hi) + _dot(ovt_ref[...], p_lo)
    ns = imp.shape[0]
    j_i = lax.broadcasted_iota(jnp.int32, (ns, 1), 0)
    t_abs = t0 + lax.broadcasted_iota(jnp.int32, (1, Q_BLOCK), 1)
    cur = t_abs // SLC_BLOCK
    forced = (j_i == 0) | (j_i == cur) | (j_i == cur - 1)
    x = jnp.where(forced, FORCE, imp)
    x = jnp.where(j_i * SLC_BLOCK <= t_abs, x, NEG)

    vw = jnp.concatenate([vwt_ref[0, :, pl.ds(s, Q_BLOCK)] for s in slab_at], axis=1)
    r_k = lax.broadcasted_iota(jnp.int32, (Q_BLOCK, nq), 0)
    r_q = lax.broadcasted_iota(jnp.int32, (Q_BLOCK, nq), 1) % Q_BLOCK
    o_w = []
    for zw in zws:
        slabs = []
        for i, kt in enumerate(slab_kt):
            zi = zw[i * Q_BLOCK:(i + 1) * Q_BLOCK]
            if i == 0:
                zi = jnp.where(r_k > r_q, zi, NEG)
            if i == n_slab - 1:
                zi = jnp.where(r_k <= r_q, zi, NEG)
            else:
                zi = zi + jnp.where(kt < 0, NEG, 0.0)
            slabs.append(zi)
        zw = jnp.concatenate(slabs, axis=0)
        pw = jnp.exp2(zw - jnp.max(zw, axis=0, keepdims=True))
        o_w.append(normalised(_dot(vw, pw.astype(BF16))))

    sub = lax.broadcasted_iota(jnp.int32, (SUBLANES, Q_BLOCK), 0)
    tiles = [x[v * SUBLANES:(v + 1) * SUBLANES] for v in range(ns // SUBLANES)]
    cnts = [jnp.zeros((SUBLANES, Q_BLOCK), F32) for _ in tiles]
    for jp in range(ns):
        r = x[jp:jp + 1, :]
        vj, rj = divmod(jp, SUBLANES)
        for v, xt in enumerate(tiles):
            if v < vj:
                ahead = r > xt
            elif v > vj:
                ahead = r >= xt
            else:
                ahead = (r > xt) | ((r == xt) & (sub > rj))
            cnts[v] = cnts[v] + jnp.where(ahead, 1.0, 0.0)
    for v, cnt in enumerate(cnts):
        bias_ref[v * SUBLANES:(v + 1) * SUBLANES, :] = jnp.where(cnt < float(SLC_TOPN), 0.0, NEG)

    def sel_update(st, z_refs, carry, causal):
        s0 = pl.multiple_of(st * SEL_TILE, SEL_TILE)
        vs = vst_ref[0, :, pl.ds(s0, SEL_TILE)]
        brow = bias_ref[pl.ds(pl.multiple_of(st * per_step, per_step), per_step), :]
        b2 = jnp.concatenate([brow] * 2, axis=1)[:, None, :]
        out = []
        for z_ref, (m, acc) in zip(z_refs, carry):
            z = (z_ref[...].reshape(per_step, SLC_BLOCK, nq) + b2).reshape(SEL_TILE, nq)
            if causal:
                z = jnp.where((s0 + spos0) <= tpos, z, NEG)
            mn = jnp.maximum(m, jnp.max(z, axis=0, keepdims=True))
            p = jnp.exp2(z - mn)
            out.append((mn, jnp.exp2(m - mn) * acc + _dot(vs, p.astype(BF16))))
        return tuple(out)

    def run_steps(first, count, carry, ends_causal):
        bufs = (za_refs, zb_refs)
        for i in range(count):
            if i + 1 < count or not ends_causal:
                sel_scores(first + i + 1, bufs[(i + 1) % 2])
            carry = sel_update(first + i, bufs[i % 2], carry, ends_causal and i + 1 == count)
        return carry

    init = tuple((jnp.full((1, nq), NEG, F32), jnp.zeros((V_SLOT, nq), F32)) for _ in range(npair))
    last = qb // (SEL_TILE // Q_BLOCK)
    n_loop = last // SEL_UNROLL
    carry = lax.fori_loop(0, n_loop, lambda i, c: run_steps(SEL_UNROLL * i, SEL_UNROLL, c, False), init)
    tails = [functools.partial(run_steps, SEL_UNROLL * n_loop, r + 1, ends_causal=True)
             for r in range(SEL_UNROLL)]
    carry = lax.switch(last % SEL_UNROLL, tails, carry)
    o_s = [normalised(acc) for _, acc in carry]

    gts = _sigmoid(gate_ref[0])
    pairs = []
    for hp in range(npair):
        halves = []
        for i in range(2):
            r = 3 * (2 * hp + i)
            sl = slice(i * Q_BLOCK, (i + 1) * Q_BLOCK)
            halves.append(gts[r:r + 1] * o_c[hp][:, sl] + gts[r + 1:r + 2] * o_s[hp][:, sl]
                          + gts[r + 2:r + 3] * o_w[hp][:, sl])
        pairs.append(jnp.concatenate(halves, axis=0).T)
    o_ref[0] = jnp.concatenate(pairs, axis=1).astype(o_ref.dtype)


def _nsa(q_slots, gate_t, kc_aug, vc_t, ks_aug, kw_aug, v_t, ov_t, slope_tab):
    b, t, _ = q_slots.shape
    g = NSA_GROUPS
    nqb = t // Q_BLOCK
    ncp = kc_aug.shape[2]
    ns = ov_t.shape[0]
    return pl.pallas_call(
        _nsa_kernel,
        out_shape=jax.ShapeDtypeStruct((b, t, NSA_HEADS * NSA_DH), BF16),
        grid=(b, g, nqb),
        in_specs=[
            pl.BlockSpec((1, Q_BLOCK, NSA_HG * NSA_DH), lambda i, j, s: (i, s, j)),
            pl.BlockSpec((1, GATE_ROWS, Q_BLOCK), lambda i, j, s: (i, j, s)),
            pl.BlockSpec((1, 1, ncp, LANES), lambda i, j, s: (i, j, 0, 0)),
            pl.BlockSpec((1, 1, NSA_DH, ncp), lambda i, j, s: (i, j, 0, 0)),
            pl.BlockSpec((1, t, LANES), lambda i, j, s: (i, 0, j)),
            pl.BlockSpec((1, V_SLOT, t), lambda i, j, s: (i, j, 0)),
            pl.BlockSpec((1, t, LANES), lambda i, j, s: (i, 0, j)),
            pl.BlockSpec((1, V_SLOT, t), lambda i, j, s: (i, g + j, 0)),
            pl.BlockSpec((ns, ncp), lambda i, j, s: (0, 0)),
            pl.BlockSpec((1, NSA_HG * Q_BLOCK, LANES), lambda i, j, s: (j, 0, 0)),
        ],
        out_specs=pl.BlockSpec((1, Q_BLOCK, NSA_HG * NSA_DH), lambda i, j, s: (i, s, j)),
        scratch_shapes=[pltpu.VMEM((ns, Q_BLOCK), F32)]
        + [pltpu.VMEM((SEL_TILE, 2 * Q_BLOCK), F32)] * NSA_HG,
        compiler_params=pltpu.CompilerParams(
            dimension_semantics=("parallel", "parallel", "arbitrary"), vmem_limit_bytes=VMEM_LIMIT),
        name="nsa",
    )(q_slots, gate_t, kc_aug, vc_t, ks_aug, v_t, kw_aug, v_t, ov_t, slope_tab)


def _layer_norm(z, g, b):
    mu = jnp.mean(z, axis=-1, keepdims=True)
    zc = z - mu
    var = jnp.mean(zc * zc, axis=-1, keepdims=True)
    return zc * lax.rsqrt(var + LN_EPS) * g + b


def _post_kernel(x_ref, yh_ref, yn_ref, mod_ref, wh_ref, wn_ref, w1_ref, w3_ref, w2_ref,
                 l1g_ref, l1b_ref, l2g_ref, l2b_ref, o_ref, *, alpha, chunks, row_parts):
    g1, sh2, sc2, g2 = mod_ref[0, 2], mod_ref[0, 3], mod_ref[0, 4], mod_ref[0, 5]
    tm = x_ref.shape[1]
    parts = [slice(i * tm // row_parts, (i + 1) * tm // row_parts) for i in range(row_parts)]
    mixes = [_dot(yh_ref[0, r, :], wh_ref[...]) + _dot(yn_ref[0, r, :], wn_ref[...]) for r in parts]
    x1s, ups = [], []
    for r, mix in zip(parts, mixes):
        x1 = _layer_norm(alpha * x_ref[0, r, :] + g1 * mix, l1g_ref[...], l1b_ref[...])
        h = (x1 * (1.0 + sc2) + sh2).astype(BF16)
        x1s.append(x1)
        ups.append([(_dot(h, w1_ref[:, lo:hi]), _dot(h, w3_ref[:, lo:hi])) for lo, hi in chunks])
    downs = []
    for up in ups:
        a = jnp.concatenate([(u * _sigmoid(u) * w).astype(BF16) for u, w in up], axis=1)
        downs.append(_dot(a, w2_ref[...]))
    for r, x1, down in zip(parts, x1s, downs):
        o_ref[0, r, :] = _layer_norm(alpha * x1 + g2 * down, l2g_ref[...], l2b_ref[...])


def _post(x, y_hg, y_nsa, ada4, w_h, w_n, w1, w3, w2, ln1_g, ln1_b, ln2_g, ln2_b, alpha):
    b, t, d = x.shape
    dff = w1.shape[1]
    tm = 512
    step = 3 * MXU_DIM
    chunks = tuple((lo, min(lo + step, dff)) for lo in range(0, dff, step))
    resident = lambda a: pl.BlockSpec(a.shape, lambda i, j: (0,) * a.ndim, pipeline_mode=pl.Buffered(1))
    rows = lambda a: pl.BlockSpec((1, tm, a.shape[2]), lambda i, j: (i, j, 0))
    vec = pl.BlockSpec((1, d), lambda i, j: (0, 0))
    return pl.pallas_call(
        functools.partial(_post_kernel, alpha=alpha, chunks=chunks, row_parts=2),
        out_shape=jax.ShapeDtypeStruct((b, t, d), F32),
        grid=(b, t // tm),
        in_specs=[rows(x), rows(y_hg), rows(y_nsa),
                  pl.BlockSpec((1,) + ada4.shape[1:], lambda i, j: (i, 0, 0, 0)),
                  resident(w_h), resident(w_n), resident(w1), resident(w3), resident(w2),
                  vec, vec, vec, vec],
        out_specs=rows(x),
        compiler_params=pltpu.CompilerParams(
            dimension_semantics=("parallel", "parallel"), vmem_limit_bytes=FFN_VMEM_LIMIT),
        name="post",
    )(x, y_hg, y_nsa, ada4, w_h, w_n, w1, w3, w2, ln1_g, ln1_b, ln2_g, ln2_b)


def _slope_table(group_slopes):
    g, hg = group_slopes.shape
    rest = (group_slopes * LOG2E).astype(np.float32)
    tab = np.zeros((g, hg, Q_BLOCK, LANES), jnp.bfloat16)
    for i in range(N_SPLIT):
        piece = rest.astype(jnp.bfloat16)
        rest = rest - piece.astype(np.float32)
        tab[..., NSA_DH + 2 * i] = piece[:, :, None]
        tab[..., NSA_DH + 2 * i + 1] = piece[:, :, None]
    return tab.reshape(g, hg * Q_BLOCK, LANES)


def _pad_slots(w, n_slots):
    d = w.shape[0]
    w3 = w.reshape(d, n_slots, -1)
    return jnp.pad(w3, ((0, 0), (0, 0), (0, LANES - w3.shape[2]))).reshape(d, n_slots * LANES)


def _cmp_w1_expand(w1, g):
    hid = w1.shape[1]
    w1r = w1.reshape(2, CMP_STRIDE, 1, NSA_DH, hid)
    eye = jnp.eye(g, dtype=w1.dtype).reshape(g, 1, 1, g, 1, 1)
    return (eye * w1r[None]).reshape(g, 2 * CMP_STRIDE * g * NSA_DH, hid)


def _overlap_t(n_cmp_pad, n_cmp, n_slc):
    cs = np.arange(n_cmp_pad) * CMP_STRIDE
    ss = np.arange(n_slc) * SLC_BLOCK
    ov = np.clip(np.minimum(cs[:, None] + CMP_BLOCK, ss[None] + SLC_BLOCK)
                 - np.maximum(cs[:, None], ss[None]), 0, None).astype(np.float32) / CMP_BLOCK
    ov[n_cmp:] = 0.0
    return ov.T


def kernel(x, c, w_ada, b_ada, w_in, hg_lb_logits, hg_norm_g, cmp_pos_k, cmp_w1_k, cmp_w2_k,
           cmp_pos_v, cmp_w1_v, cmp_w2_v, w_out, ln1_g, ln1_b, ffn_w1, ffn_w3, ffn_w2, ln2_g, ln2_b):
    bsz, t, d = x.shape
    depth = w_ada.shape[0]
    g, hg, dh = NSA_GROUPS, NSA_HG, NSA_DH
    hgw = HG_HEADS * HG_D
    nsw = NSA_HEADS * dh
    kvw = g * dh
    n_mod = 6
    alpha = (2.0 * depth) ** 0.25
    scale = dh ** -0.5
    n_grp = t // CMP_STRIDE
    n_cmp = (t - CMP_BLOCK) // CMP_STRIDE + 1
    n_slc = t // SLC_BLOCK
    assert t % SEL_TILE == 0 and t >= WINDOW + Q_BLOCK and 3 * hg <= GATE_ROWS and hg % 2 == 0

    slopes = (2.0 ** (-8.0 * (np.arange(NSA_HEADS) + 1) / NSA_HEADS)).reshape(g, hg)
    slope_tab = jnp.asarray(_slope_table(slopes), BF16)
    ov_t = jnp.asarray(_overlap_t(n_grp, n_cmp, n_slc), BF16)

    c_pad = jnp.pad(c, ((0, 8 - bsz), (0, 0)))
    for l in range(depth):
        ada = _ada(c_pad, w_ada[l], b_ada[l][None])
        ada3 = ada[:bsz].reshape(bsz * n_mod, 1, d)

        w = w_in[l]
        o = 4 * hgw
        w_q = w[:, o:o + nsw]
        w_kv = [w[:, o + nsw + i * kvw: o + nsw + (i + 1) * kvw] for i in range(6)]
        w_g = w[:, o + nsw + 6 * kvw: o + nsw + 6 * kvw + 3 * NSA_HEADS]
        w_main = jnp.concatenate(
            [w[:, :o], w_q * (scale * LOG2E), _pad_slots(w_kv[2], g),
             _pad_slots(w_kv[4], g), w_kv[0], w_kv[1]], axis=1).astype(BF16)
        w_gt = jnp.pad(w_g.reshape(d, g, 3 * hg), ((0, 0), (0, 0), (0, GATE_ROWS - 3 * hg))).reshape(d, -1)
        v_slots = lambda a: jnp.pad(a.reshape(d, g, dh), ((0, 0), (0, 0), (0, V_SLOT - dh))).reshape(d, -1)
        w_t = jnp.concatenate([v_slots(w_kv[3]), v_slots(w_kv[5]), w_gt], axis=1).T.astype(BF16)
        widths = (o, nsw, g * LANES, kvw, 2 * g * V_SLOT, g * GATE_ROWS)
        hg_in, q_slots, ks_aug, kw_aug, kc, vc, v_t, gate_t = _inproj(x, ada3, w_main, w_t, n_mod, widths)

        y_hg = _hgrn(hg_in, hg_lb_logits, hg_norm_g[l][None], l)

        grp = lambda a: a.reshape(bsz, n_grp, CMP_STRIDE * kvw)
        pos_e = lambda p: jnp.broadcast_to(
            p.reshape(2, CMP_STRIDE, 1, dh), (2, CMP_STRIDE, g, dh)).reshape(1, -1)
        w2k = jnp.pad(cmp_w2_k[l], ((0, 0), (0, LANES - dh))).astype(BF16)
        kc_aug = _compress(grp(kc), pos_e(cmp_pos_k[l]), _cmp_w1_expand(cmp_w1_k[l], g).astype(BF16),
                           w2k, True)
        vc_t = _compress(grp(vc), pos_e(cmp_pos_v[l]), _cmp_w1_expand(cmp_w1_v[l], g).astype(BF16),
                         cmp_w2_v[l].T.astype(BF16), False)

        y_nsa = _nsa(q_slots, gate_t, kc_aug, vc_t, ks_aug, kw_aug, v_t, ov_t, slope_tab)

        w_o = w_out[l].astype(BF16)
        x = _post(x, y_hg, y_nsa, ada3.reshape(bsz, n_mod, 1, d), w_o[:hgw], w_o[hgw:],
                  ffn_w1[l].astype(BF16), ffn_w3[l].astype(BF16), ffn_w2[l].astype(BF16),
                  ln1_g[l][None], ln1_b[l][None], ln2_g[l][None], ln2_b[l][None], alpha)
    return x
```

```python
import functools

import numpy as np
import jax
import jax.numpy as jnp
from jax import lax
from jax.experimental import pallas as pl
from jax.experimental.pallas import tpu as pltpu

F32 = jnp.float32
BF16 = jnp.bfloat16

HG_HEADS = 4
HG_D = 128
NSA_HEADS = 8
NSA_GROUPS = 2
NSA_HG = NSA_HEADS // NSA_GROUPS
NSA_DH = 64
CMP_BLOCK = 32
CMP_STRIDE = 16
CMP_HIDDEN = 256
SLC_BLOCK = 64
SLC_TOPN = 16
WINDOW = 512
Q_BLOCK = 128
NEG = -1e30
FORCE = 1e4
LN_EPS = 1e-5
RMS_EPS = 1e-6

LANES = 128
SUBLANES = 8
HG_BLK = 16
HG_SAFE_LOG2 = 60.0
POS_SPLIT = 256
GATE_ROWS = 16
SEL_TILE = 512
SEL_UNROLL = 8
V_SLOT = 80
N_SPLIT = 3
LOG2E = 1.4426950408889634
MXU_DIM = 256
VMEM_LIMIT = 48 * 1024 * 1024
FFN_VMEM_LIMIT = 56 * 1024 * 1024


def _sigmoid(x):
    return 1.0 / (1.0 + jnp.exp(-x))


def _dot(a, b):
    return jnp.dot(a, b, preferred_element_type=F32)


def _dot_nt(a, b):
    return lax.dot_general(a, b, (((1,), (1,)), ((), ())), preferred_element_type=F32)


def _pos_columns(pos, col):
    c = col - NSA_DH
    lo = pos % POS_SPLIT
    val = jnp.where(c % 2 == 0, lo, pos - lo)
    return jnp.where((c >= 0) & (c < 2 * N_SPLIT), val, 0).astype(F32)


def _split2(x):
    hi = x.astype(BF16)
    return hi, (x - hi.astype(F32)).astype(BF16)


def _ada_kernel(c_ref, w_ref, b_ref, o_ref):
    c = c_ref[...]
    s = (c * _sigmoid(c)).astype(BF16)
    o_ref[...] = _dot(s, w_ref[...].astype(BF16)) + b_ref[...]


def _ada(c_pad, w, b):
    rows, d = c_pad.shape
    n = w.shape[1]
    tn = 1024
    return pl.pallas_call(
        _ada_kernel,
        out_shape=jax.ShapeDtypeStruct((rows, n), F32),
        grid=(n // tn,),
        in_specs=[
            pl.BlockSpec((rows, d), lambda j: (0, 0)),
            pl.BlockSpec((d, tn), lambda j: (0, j)),
            pl.BlockSpec((1, tn), lambda j: (0, j)),
        ],
        out_specs=pl.BlockSpec((rows, tn), lambda j: (0, j)),
        compiler_params=pltpu.CompilerParams(
            dimension_semantics=("parallel",), vmem_limit_bytes=VMEM_LIMIT),
        name="ada",
    )(c_pad, w, b)


def _inproj_kernel(x_ref, sc_ref, sh_ref, wm_ref, wt_ref,
                   hg_ref, q_ref, ks_ref, kw_ref, kc_ref, vc_ref, vt_ref, gt_ref, *, tm):
    h = (x_ref[0] * (1.0 + sc_ref[0]) + sh_ref[0]).astype(BF16)
    y = _dot(h, wm_ref[...])
    n_hg = hg_ref.shape[2]
    n_q = q_ref.shape[2]
    n_k = ks_ref.shape[2]
    n_c = kc_ref.shape[2]
    o = 0
    hg_ref[0] = y[:, o:o + n_hg]
    o += n_hg
    q_ref[0] = y[:, o:o + n_q].astype(BF16)
    o += n_q
    row = pl.program_id(1) * tm + lax.broadcasted_iota(jnp.int32, (tm, n_k), 0)
    pc = _pos_columns(row, lax.broadcasted_iota(jnp.int32, (tm, n_k), 1) % LANES)
    ks_ref[0] = (y[:, o:o + n_k] + pc).astype(BF16)
    o += n_k
    kw_ref[0] = (y[:, o:o + n_k] + pc).astype(BF16)
    o += n_k
    kc_ref[0] = y[:, o:o + n_c]
    o += n_c
    vc_ref[0] = y[:, o:o + n_c]
    yt = _dot_nt(wt_ref[...], h)
    n_vt = vt_ref.shape[1]
    ri = lax.broadcasted_iota(jnp.int32, (n_vt, tm), 0) % V_SLOT
    vt_ref[0] = (yt[:n_vt] + jnp.where(ri == NSA_DH, 1.0, 0.0)).astype(BF16)
    gt_ref[0] = yt[n_vt:]


def _inproj(x, ada3, w_main, w_t, n_mod, widths):
    b, t, d = x.shape
    n_hg, n_q, n_k, n_c, n_vt, n_gt = widths
    tm = 256
    row_spec = lambda n: pl.BlockSpec((1, tm, n), lambda i, j: (i, j, 0))
    col_spec = lambda n: pl.BlockSpec((1, n, tm), lambda i, j: (i, 0, j))
    sds = jax.ShapeDtypeStruct
    return pl.pallas_call(
        functools.partial(_inproj_kernel, tm=tm),
        out_shape=(sds((b, t, n_hg), F32), sds((b, t, n_q), BF16), sds((b, t, n_k), BF16),
                   sds((b, t, n_k), BF16), sds((b, t, n_c), F32), sds((b, t, n_c), F32),
                   sds((b, n_vt, t), BF16), sds((b, n_gt, t), F32)),
        grid=(b, t // tm),
        in_specs=[
            pl.BlockSpec((1, tm, d), lambda i, j: (i, j, 0)),
            pl.BlockSpec((1, 1, d), lambda i, j: (i * n_mod + 1, 0, 0)),
            pl.BlockSpec((1, 1, d), lambda i, j: (i * n_mod, 0, 0)),
            pl.BlockSpec(w_main.shape, lambda i, j: (0, 0)),
            pl.BlockSpec(w_t.shape, lambda i, j: (0, 0)),
        ],
        out_specs=(row_spec(n_hg), row_spec(n_q), row_spec(n_k), row_spec(n_k), row_spec(n_c),
                   row_spec(n_c), col_spec(n_vt), col_spec(n_gt)),
        compiler_params=pltpu.CompilerParams(
            dimension_semantics=("parallel", "parallel"), vmem_limit_bytes=VMEM_LIMIT),
        name="inproj",
    )(x, ada3, ada3, w_main, w_t)


def _hgrn_kernel(q_ref, f_ref, i_ref, g_ref, lbl_ref, ng_ref, e_ref, o_ref, st_ref, *, tc, layer):
    blk = HG_BLK
    per_grp = LANES // blk
    half = blk // 2

    @pl.when(pl.program_id(2) == 0)
    def _():
        st_ref[...] = jnp.zeros_like(st_ref)

    lbl = lbl_ref[...]
    e = jnp.exp(lbl - jnp.max(lbl, axis=0, keepdims=True))
    lb = jnp.sum(e[: layer + 1], axis=0, keepdims=True) / jnp.sum(e, axis=0, keepdims=True)

    r_i = lax.broadcasted_iota(jnp.int32, (LANES, LANES), 0)
    c_i = lax.broadcasted_iota(jnp.int32, (LANES, LANES), 1)
    same_blk = (r_i // blk) == (c_i // blk)
    causal_blk = same_blk & (c_i <= r_i)
    tri = jnp.where(causal_blk, 1.0, 0.0).astype(BF16)
    row_blk = r_i // blk
    nh = per_grp * half
    r_h = lax.broadcasted_iota(jnp.int32, (nh, LANES), 0)
    same_half = (r_h // half) == (lax.broadcasted_iota(jnp.int32, (nh, LANES), 1) // blk)
    row_in_half = r_h % half
    blocked = lambda a: a.reshape(per_grp, blk, LANES)

    def gates(g):
        sl = slice(g * LANES, (g + 1) * LANES)
        q = q_ref[0, sl, :]
        qs = q * _sigmoid(q)
        f = lb + (1.0 - lb) * _sigmoid(f_ref[0, sl, :])
        hi, lo = _split2(jnp.log2(f))
        b = _dot(tri, hi) + _dot(tri, lo)
        return qs, 1.0 - f, i_ref[0, sl, :], b

    def increments(qs, k, v, b):
        b3 = blocked(b)
        bl3 = b3[:, blk - 1:blk, :]
        qd = (qs * jnp.exp2(b)).astype(BF16)
        kd = (blocked(k) * jnp.exp2(bl3 - b3)).reshape(LANES, LANES)
        kexp = jnp.concatenate(
            [jnp.where(row_blk == j, kd, 0.0) for j in range(per_grp)], axis=1).astype(BF16)
        return qd, jnp.exp2(bl3), _dot(v.T.astype(BF16), kexp)

    def recur(st, qd, dl, ut):
        ois = []
        for j in range(per_grp):
            ois.append(_dot_nt(qd[j * blk:(j + 1) * blk], st.astype(BF16)))
            st = dl[j] * st + ut[:, j * LANES:(j + 1) * LANES]
        return st, jnp.concatenate(ois, axis=0)

    def pairwise(qs, k, b):
        c = b - jnp.log2(k)
        tiles = lambda a, off: jnp.concatenate(
            [a[j * blk + off:j * blk + off + half] for j in range(per_grp)], axis=0)
        q_lo, q_hi, b_lo, b_hi = tiles(qs, 0), tiles(qs, half), tiles(b, 0), tiles(b, half)
        p_lo, p_hi = [], []
        for s in range(blk):
            cs = jnp.concatenate([jnp.broadcast_to(c[j * blk + s:j * blk + s + 1], (half, LANES))
                                  for j in range(per_grp)], axis=0)
            if s < half:
                p_lo.append(jnp.where(row_in_half >= s, q_lo * jnp.exp2(b_lo - cs), 0.0).astype(BF16))
                p_hi.append((q_hi * jnp.exp2(b_hi - cs)).astype(BF16))
            else:
                p_hi.append(jnp.where(row_in_half >= s - half, q_hi * jnp.exp2(b_hi - cs), 0.0).astype(BF16))
        return jnp.concatenate(p_lo, axis=1), jnp.concatenate(p_hi, axis=1)

    def emit(g, o):
        o = o * lax.rsqrt(jnp.mean(o * o, axis=-1, keepdims=True) + RMS_EPS)
        sl = slice(g * LANES, (g + 1) * LANES)
        o_ref[0, sl, :] = (o * ng_ref[...] * _sigmoid(g_ref[0, sl, :])).astype(o_ref.dtype)

    ngrp = tc // LANES
    gs = [gates(g) for g in range(ngrp)]
    incs = [increments(*gt) for gt in gs]
    st = st_ref[...]
    o_inters = []
    for inc in incs:
        st, oi = recur(st, *inc)
        o_inters.append(oi)
    st_ref[...] = st

    b_min = functools.reduce(jnp.minimum, [jnp.min(gt[3]) for gt in gs])

    @pl.when(b_min >= -HG_SAFE_LOG2)
    def _():
        a_s = [_dot_nt(qd, (k * jnp.exp2(-b)).astype(BF16))
               for (_, k, _, b), (qd, _, _) in zip(gs, incs)]
        o_s = [_dot(jnp.where(causal_blk, a, 0.0).astype(BF16), gt[2].astype(BF16)) for a, gt in zip(a_s, gs)]
        for g, o in enumerate(o_s):
            emit(g, o + o_inters[g])

    @pl.when(b_min < -HG_SAFE_LOG2)
    def _():
        pws = [pairwise(qs, k, b) for qs, k, _, b in gs]
        a_lo = _dot(jnp.concatenate([p[0] for p in pws], axis=0), e_ref[:half * LANES, :])
        a_hi = _dot(jnp.concatenate([p[1] for p in pws], axis=0), e_ref[...])
        for g in range(ngrp):
            rows = slice(g * nh, (g + 1) * nh)
            vb = gs[g][2].astype(BF16)
            o_lo = _dot(jnp.where(same_half, a_lo[rows], 0.0).astype(BF16), vb)
            o_hi = _dot(jnp.where(same_half, a_hi[rows], 0.0).astype(BF16), vb)
            emit(g, jnp.concatenate([x[j * half:(j + 1) * half] for j in range(per_grp) for x in (o_lo, o_hi)],
                                    axis=0) + o_inters[g])


def _hgrn(proj, lb_logits, norm_g, layer):
    b, t, _ = proj.shape
    tc = 1024
    h = HG_HEADS
    nslots = lb_logits.shape[0]
    col = lambda seg: (lambda i, j, s: (i, s, seg * h + j))
    fold = np.zeros((HG_BLK, LANES, LANES), np.float32)
    for s in range(HG_BLK):
        fold[s, :, s::HG_BLK] = 1.0
    fold = jnp.asarray(fold.reshape(HG_BLK * LANES, LANES), BF16)
    return pl.pallas_call(
        functools.partial(_hgrn_kernel, tc=tc, layer=layer),
        out_shape=jax.ShapeDtypeStruct((b, t, h * HG_D), BF16),
        grid=(b, h, t // tc),
        in_specs=[
            pl.BlockSpec((1, tc, HG_D), col(0)),
            pl.BlockSpec((1, tc, HG_D), col(1)),
            pl.BlockSpec((1, tc, HG_D), col(2)),
            pl.BlockSpec((1, tc, HG_D), col(3)),
            pl.BlockSpec((nslots, HG_D), lambda i, j, s: (0, j)),
            pl.BlockSpec((1, HG_D), lambda i, j, s: (0, j)),
            pl.BlockSpec(fold.shape, lambda i, j, s: (0, 0)),
        ],
        out_specs=pl.BlockSpec((1, tc, HG_D), lambda i, j, s: (i, s, j)),
        scratch_shapes=[pltpu.VMEM((HG_D, HG_D), F32)],
        compiler_params=pltpu.CompilerParams(
            dimension_semantics=("parallel", "parallel", "arbitrary"), vmem_limit_bytes=VMEM_LIMIT),
        name="hgrn",
    )(proj, proj, proj, proj, lb_logits, norm_g, fold)


def _cmp_kernel(h_ref, pos_ref, w1_ref, w2_ref, o_ref, *, key_side):
    h = h_ref[0]
    half = h.shape[1]
    pos = pos_ref[...]
    p0 = _dot((h + pos[:, :half]).astype(BF16), w1_ref[0, :half])
    p1 = _dot((h + pos[:, half:]).astype(BF16), w1_ref[0, half:])
    ng = h.shape[0]
    pre = p0 + pltpu.roll(p1, ng - 1, 0)
    gl = 0.5 * pre * (1.0 + jnp.tanh(np.sqrt(2.0 / np.pi) * (pre + 0.044715 * (pre * pre * pre))))
    gl = gl.astype(BF16)
    if key_side:
        cend = lax.broadcasted_iota(jnp.int32, (ng, LANES), 0) * CMP_STRIDE + (CMP_BLOCK - 1)
        pc = _pos_columns(cend, lax.broadcasted_iota(jnp.int32, (ng, LANES), 1))
        o_ref[0, 0] = (_dot(gl, w2_ref[...]) + pc).astype(BF16)
    else:
        o_ref[0, 0] = _dot_nt(w2_ref[...], gl).astype(BF16)


def _compress(hgrp, pos, w1, w2, key_side):
    b, ng, half = hgrp.shape
    g = w1.shape[0]
    out_tail = (ng, LANES) if key_side else (w2.shape[0], ng)
    return pl.pallas_call(
        functools.partial(_cmp_kernel, key_side=key_side),
        out_shape=jax.ShapeDtypeStruct((b, g) + out_tail, BF16),
        grid=(b, g),
        in_specs=[
            pl.BlockSpec((1, ng, half), lambda i, j: (i, 0, 0)),
            pl.BlockSpec(pos.shape, lambda i, j: (0, 0)),
            pl.BlockSpec((1,) + w1.shape[1:], lambda i, j: (j, 0, 0)),
            pl.BlockSpec(w2.shape, lambda i, j: (0, 0)),
        ],
        out_specs=pl.BlockSpec((1, 1) + out_tail, lambda i, j: (i, j, 0, 0)),
        compiler_params=pltpu.CompilerParams(
            dimension_semantics=("parallel", "parallel"), vmem_limit_bytes=VMEM_LIMIT),
        name="cmp_k" if key_side else "cmp_v",
    )(hgrp, pos, w1, w2)


def _nsa_kernel(q_ref, gate_ref, kc_ref, vct_ref, ks_ref, vst_ref, kw_ref, vwt_ref, ovt_ref, slope_ref,
                o_ref, bias_ref, *z_scratch):
    qb = pl.program_id(2)
    t0 = qb * Q_BLOCK
    npair = NSA_HG // 2
    za_refs, zb_refs = z_scratch[:npair], z_scratch[npair:]
    nq = 2 * Q_BLOCK
    qblk = q_ref[0].astype(F32)
    low = lax.broadcasted_iota(jnp.int32, (Q_BLOCK, LANES), 1) < NSA_DH
    qas = []
    for hp in range(npair):
        pair = qblk[:, hp * LANES:(hp + 1) * LANES]
        heads = [jnp.where(low, x, 0.0) for x in (pair, pltpu.roll(pair, NSA_DH, 1))]
        qas.append(jnp.concatenate(heads, axis=0).astype(BF16) + slope_ref[0, hp * nq:(hp + 1) * nq])
    tpos = t0 + (lax.broadcasted_iota(jnp.int32, (1, nq), 1) % Q_BLOCK)

    def normalised(acc):
        return acc[:NSA_DH] * (1.0 / acc[NSA_DH:NSA_DH + 1])

    per_step = SEL_TILE // SLC_BLOCK
    spos0 = lax.broadcasted_iota(jnp.int32, (SEL_TILE, 1), 0)
    n_slab = WINDOW // Q_BLOCK + 1
    slab_kt = [qb - (n_slab - 1) + i for i in range(n_slab)]
    slab_at = [pl.multiple_of(jnp.maximum(kt, 0) * Q_BLOCK, Q_BLOCK) for kt in slab_kt]
    kw = jnp.concatenate([kw_ref[0, pl.ds(s, Q_BLOCK), :] for s in slab_at], axis=0)

    def sel_scores(st, z_refs):
        ks = ks_ref[0, pl.ds(pl.multiple_of(st * SEL_TILE, SEL_TILE), SEL_TILE), :]
        for z_ref, qa in zip(z_refs, qas):
            z_ref[...] = _dot_nt(ks, qa)

    zcs = [_dot_nt(kc_ref[0, 0], qa) for qa in qas]
    zws = [_dot_nt(kw, qa) for qa in qas]
    sel_scores(0, za_refs)

    ncp = kc_ref.shape[2]
    cend = lax.broadcasted_iota(jnp.int32, (ncp, 1), 0) * CMP_STRIDE + (CMP_BLOCK - 1)
    valid = cend <= tpos
    o_c, psum = [], None
    for zc in zcs:
        zm = jnp.where(valid, zc, NEG)
        ec = jnp.where(valid, jnp.exp2(zm - jnp.max(zm, axis=0, keepdims=True)), 0.0)
        lc = jnp.sum(ec, axis=0, keepdims=True)
        pc = ec * jnp.where(lc > 0.0, 1.0 / lc, 0.0)
        o_c.append(_dot(vct_ref[0, 0], pc.astype(BF16)))
        for i in range(2):
            ph = pc[:, i * Q_BLOCK:(i + 1) * Q_BLOCK]
            psum = ph if psum is None else psum + ph

    p_hi = psum.astype(BF16)
    p_lo = (psum - p_hi.astype(F32)).astype(BF16)
    imp = _dot(ovt_ref[...], p_hi) + _dot(ovt_ref[...], p_lo)
    ns = imp.shape[0]
    j_i = lax.broadcasted_iota(jnp.int32, (ns, 1), 0)
    t_abs = t0 + lax.broadcasted_iota(jnp.int32, (1, Q_BLOCK), 1)
    cur = t_abs // SLC_BLOCK
    forced = (j_i == 0) | (j_i == cur) | (j_i == cur - 1)
    x = jnp.where(forced, FORCE, imp)
    x = jnp.where(j_i * SLC_BLOCK <= t_abs, x, NEG)

    vw = jnp.concatenate([vwt_ref[0, :, pl.ds(s, Q_BLOCK)] for s in slab_at], axis=1)
    r_k = lax.broadcasted_iota(jnp.int32, (Q_BLOCK, nq), 0)
    r_q = lax.broadcasted_iota(jnp.int32, (Q_BLOCK, nq), 1) % Q_BLOCK
    o_w = []
    for zw in zws:
        slabs = []
        for i, kt in enumerate(slab_kt):
            zi = zw[i * Q_BLOCK:(i + 1) * Q_BLOCK]
            if i == 0:
                zi = jnp.where(r_k > r_q, zi, NEG)
            if i == n_slab - 1:
                zi = jnp.where(r_k <= r_q, zi, NEG)
            else:
                zi = zi + jnp.where(kt < 0, NEG, 0.0)
            slabs.append(zi)
        zw = jnp.concatenate(slabs, axis=0)
        pw = jnp.exp2(zw - jnp.max(zw, axis=0, keepdims=True))
        o_w.append(normalised(_dot(vw, pw.astype(BF16))))

    sub = lax.broadcasted_iota(jnp.int32, (SUBLANES, Q_BLOCK), 0)
    tiles = [x[v * SUBLANES:(v + 1) * SUBLANES] for v in range(ns // SUBLANES)]
    cnts = [jnp.zeros((SUBLANES, Q_BLOCK), F32) for _ in tiles]
    for jp in range(ns):
        r = x[jp:jp + 1, :]
        vj, rj = divmod(jp, SUBLANES)
        for v, xt in enumerate(tiles):
            if v < vj:
                ahead = r > xt
            elif v > vj:
                ahead = r >= xt
            else:
                ahead = (r > xt) | ((r == xt) & (sub > rj))
            cnts[v] = cnts[v] + jnp.where(ahead, 1.0, 0.0)
    for v, cnt in enumerate(cnts):
        bias_ref[v * SUBLANES:(v + 1) * SUBLANES, :] = jnp.where(cnt < float(SLC_TOPN), 0.0, NEG)

    def sel_update(st, z_refs, carry, causal):
        s0 = pl.multiple_of(st * SEL_TILE, SEL_TILE)
        vs = vst_ref[0, :, pl.ds(s0, SEL_TILE)]
        brow = bias_ref[pl.ds(pl.multiple_of(st * per_step, per_step), per_step), :]
        b2 = jnp.concatenate([brow] * 2, axis=1)[:, None, :]
        out = []
        for z_ref, (m, acc) in zip(z_refs, carry):
            z = (z_ref[...].reshape(per_step, SLC_BLOCK, nq) + b2).reshape(SEL_TILE, nq)
            if causal:
                z = jnp.where((s0 + spos0) <= tpos, z, NEG)
            mn = jnp.maximum(m, jnp.max(z, axis=0, keepdims=True))
            p = jnp.exp2(z - mn)
            out.append((mn, jnp.exp2(m - mn) * acc + _dot(vs, p.astype(BF16))))
        return tuple(out)

    def run_steps(first, count, carry, ends_causal):
        bufs = (za_refs, zb_refs)
        for i in range(count):
            if i + 1 < count or not ends_causal:
                sel_scores(first + i + 1, bufs[(i + 1) % 2])
            carry = sel_update(first + i, bufs[i % 2], carry, ends_causal and i + 1 == count)
        return carry

    init = tuple((jnp.full((1, nq), NEG, F32), jnp.zeros((V_SLOT, nq), F32)) for _ in range(npair))
    last = qb // (SEL_TILE // Q_BLOCK)
    n_loop = last // SEL_UNROLL
    carry = lax.fori_loop(0, n_loop, lambda i, c: run_steps(SEL_UNROLL * i, SEL_UNROLL, c, False), init)
    tails = [functools.partial(run_steps, SEL_UNROLL * n_loop, r + 1, ends_causal=True)
             for r in range(SEL_UNROLL)]
    carry = lax.switch(last % SEL_UNROLL, tails, carry)
    o_s = [normalised(acc) for _, acc in carry]

    gts = _sigmoid(gate_ref[0])
    pairs = []
    for hp in range(npair):
        halves = []
        for i in range(2):
            r = 3 * (2 * hp + i)
            sl = slice(i * Q_BLOCK, (i + 1) * Q_BLOCK)
            halves.append(gts[r:r + 1] * o_c[hp][:, sl] + gts[r + 1:r + 2] * o_s[hp][:, sl]
                          + gts[r + 2:r + 3] * o_w[hp][:, sl])
        pairs.append(jnp.concatenate(halves, axis=0).T)
    o_ref[0] = jnp.concatenate(pairs, axis=1).astype(o_ref.dtype)


def _nsa(q_slots, gate_t, kc_aug, vc_t, ks_aug, kw_aug, v_t, ov_t, slope_tab):
    b, t, _ = q_slots.shape
    g = NSA_GROUPS
    nqb = t // Q_BLOCK
    ncp = kc_aug.shape[2]
    ns = ov_t.shape[0]
    return pl.pallas_call(
        _nsa_kernel,
        out_shape=jax.ShapeDtypeStruct((b, t, NSA_HEADS * NSA_DH), BF16),
        grid=(b, g, nqb),
        in_specs=[
            pl.BlockSpec((1, Q_BLOCK, NSA_HG * NSA_DH), lambda i, j, s: (i, s, j)),
            pl.BlockSpec((1, GATE_ROWS, Q_BLOCK), lambda i, j, s: (i, j, s)),
            pl.BlockSpec((1, 1, ncp, LANES), lambda i, j, s: (i, j, 0, 0)),
            pl.BlockSpec((1, 1, NSA_DH, ncp), lambda i, j, s: (i, j, 0, 0)),
            pl.BlockSpec((1, t, LANES), lambda i, j, s: (i, 0, j)),
            pl.BlockSpec((1, V_SLOT, t), lambda i, j, s: (i, j, 0)),
            pl.BlockSpec((1, t, LANES), lambda i, j, s: (i, 0, j)),
            pl.BlockSpec((1, V_SLOT, t), lambda i, j, s: (i, g + j, 0)),
            pl.BlockSpec((ns, ncp), lambda i, j, s: (0, 0)),
            pl.BlockSpec((1, NSA_HG * Q_BLOCK, LANES), lambda i, j, s: (j, 0, 0)),
        ],
        out_specs=pl.BlockSpec((1, Q_BLOCK, NSA_HG * NSA_DH), lambda i, j, s: (i, s, j)),
        scratch_shapes=[pltpu.VMEM((ns, Q_BLOCK), F32)]
        + [pltpu.VMEM((SEL_TILE, 2 * Q_BLOCK), F32)] * NSA_HG,
        compiler_params=pltpu.CompilerParams(
            dimension_semantics=("parallel", "parallel", "arbitrary"), vmem_limit_bytes=VMEM_LIMIT),
        name="nsa",
    )(q_slots, gate_t, kc_aug, vc_t, ks_aug, v_t, kw_aug, v_t, ov_t, slope_tab)


def _layer_norm(z, g, b):
    mu = jnp.mean(z, axis=-1, keepdims=True)
    zc = z - mu
    var = jnp.mean(zc * zc, axis=-1, keepdims=True)
    return zc * lax.rsqrt(var + LN_EPS) * g + b


def _post_kernel(x_ref, yh_ref, yn_ref, mod_ref, wh_ref, wn_ref, w1_ref, w3_ref, w2_ref,
                 l1g_ref, l1b_ref, l2g_ref, l2b_ref, o_ref, *, alpha, chunks, row_parts):
    g1, sh2, sc2, g2 = mod_ref[0, 2], mod_ref[0, 3], mod_ref[0, 4], mod_ref[0, 5]
    tm = x_ref.shape[1]
    parts = [slice(i * tm // row_parts, (i + 1) * tm // row_parts) for i in range(row_parts)]
    mixes = [_dot(yh_ref[0, r, :], wh_ref[...]) + _dot(yn_ref[0, r, :], wn_ref[...]) for r in parts]
    x1s, ups = [], []
    for r, mix in zip(parts, mixes):
        x1 = _layer_norm(alpha * x_ref[0, r, :] + g1 * mix, l1g_ref[...], l1b_ref[...])
        h = (x1 * (1.0 + sc2) + sh2).astype(BF16)
        x1s.append(x1)
        ups.append([(_dot(h, w1_ref[:, lo:hi]), _dot(h, w3_ref[:, lo:hi])) for lo, hi in chunks])
    downs = []
    for up in ups:
        a = jnp.concatenate([(u * _sigmoid(u) * w).astype(BF16) for u, w in up], axis=1)
        downs.append(_dot(a, w2_ref[...]))
    for r, x1, down in zip(parts, x1s, downs):
        o_ref[0, r, :] = _layer_norm(alpha * x1 + g2 * down, l2g_ref[...], l2b_ref[...])


def _post(x, y_hg, y_nsa, ada4, w_h, w_n, w1, w3, w2, ln1_g, ln1_b, ln2_g, ln2_b, alpha):
    b, t, d = x.shape
    dff = w1.shape[1]
    tm = 512
    step = 3 * MXU_DIM
    chunks = tuple((lo, min(lo + step, dff)) for lo in range(0, dff, step))
    resident = lambda a: pl.BlockSpec(a.shape, lambda i, j: (0,) * a.ndim, pipeline_mode=pl.Buffered(1))
    rows = lambda a: pl.BlockSpec((1, tm, a.shape[2]), lambda i, j: (i, j, 0))
    vec = pl.BlockSpec((1, d), lambda i, j: (0, 0))
    return pl.pallas_call(
        functools.partial(_post_kernel, alpha=alpha, chunks=chunks, row_parts=2),
        out_shape=jax.ShapeDtypeStruct((b, t, d), F32),
        grid=(b, t // tm),
        in_specs=[rows(x), rows(y_hg), rows(y_nsa),
                  pl.BlockSpec((1,) + ada4.shape[1:], lambda i, j: (i, 0, 0, 0)),
                  resident(w_h), resident(w_n), resident(w1), resident(w3), resident(w2),
                  vec, vec, vec, vec],
        out_specs=rows(x),
        compiler_params=pltpu.CompilerParams(
            dimension_semantics=("parallel", "parallel"), vmem_limit_bytes=FFN_VMEM_LIMIT),
        name="post",
    )(x, y_hg, y_nsa, ada4, w_h, w_n, w1, w3, w2, ln1_g, ln1_b, ln2_g, ln2_b)


def _slope_table(group_slopes):
    g, hg = group_slopes.shape
    rest = (group_slopes * LOG2E).astype(np.float32)
    tab = np.zeros((g, hg, Q_BLOCK, LANES), jnp.bfloat16)
    for i in range(N_SPLIT):
        piece = rest.astype(jnp.bfloat16)
        rest = rest - piece.astype(np.float32)
        tab[..., NSA_DH + 2 * i] = piece[:, :, None]
        tab[..., NSA_DH + 2 * i + 1] = piece[:, :, None]
    return tab.reshape(g, hg * Q_BLOCK, LANES)


def _pad_slots(w, n_slots):
    d = w.shape[0]
    w3 = w.reshape(d, n_slots, -1)
    return jnp.pad(w3, ((0, 0), (0, 0), (0, LANES - w3.shape[2]))).reshape(d, n_slots * LANES)


def _cmp_w1_expand(w1, g):
    hid = w1.shape[1]
    w1r = w1.reshape(2, CMP_STRIDE, 1, NSA_DH, hid)
    eye = jnp.eye(g, dtype=w1.dtype).reshape(g, 1, 1, g, 1, 1)
    return (eye * w1r[None]).reshape(g, 2 * CMP_STRIDE * g * NSA_DH, hid)


def _overlap_t(n_cmp_pad, n_cmp, n_slc):
    cs = np.arange(n_cmp_pad) * CMP_STRIDE
    ss = np.arange(n_slc) * SLC_BLOCK
    ov = np.clip(np.minimum(cs[:, None] + CMP_BLOCK, ss[None] + SLC_BLOCK)
                 - np.maximum(cs[:, None], ss[None]), 0, None).astype(np.float32) / CMP_BLOCK
    ov[n_cmp:] = 0.0
    return ov.T


def kernel(x, c, w_ada, b_ada, w_in, hg_lb_logits, hg_norm_g, cmp_pos_k, cmp_w1_k, cmp_w2_k,
           cmp_pos_v, cmp_w1_v, cmp_w2_v, w_out, ln1_g, ln1_b, ffn_w1, ffn_w3, ffn_w2, ln2_g, ln2_b):
    bsz, t, d = x.shape
    depth = w_ada.shape[0]
    g, hg, dh = NSA_GROUPS, NSA_HG, NSA_DH
    hgw = HG_HEADS * HG_D
    nsw = NSA_HEADS * dh
    kvw = g * dh
    n_mod = 6
    alpha = (2.0 * depth) ** 0.25
    scale = dh ** -0.5
    n_grp = t // CMP_STRIDE
    n_cmp = (t - CMP_BLOCK) // CMP_STRIDE + 1
    n_slc = t // SLC_BLOCK
    assert t % SEL_TILE == 0 and t >= WINDOW + Q_BLOCK and 3 * hg <= GATE_ROWS and hg % 2 == 0

    slopes = (2.0 ** (-8.0 * (np.arange(NSA_HEADS) + 1) / NSA_HEADS)).reshape(g, hg)
    slope_tab = jnp.asarray(_slope_table(slopes), BF16)
    ov_t = jnp.asarray(_overlap_t(n_grp, n_cmp, n_slc), BF16)

    c_pad = jnp.pad(c, ((0, 8 - bsz), (0, 0)))
    for l in range(depth):
        ada = _ada(c_pad, w_ada[l], b_ada[l][None])
        ada3 = ada[:bsz].reshape(bsz * n_mod, 1, d)

        w = w_in[l]
        o = 4 * hgw
        w_q = w[:, o:o + nsw]
        w_kv = [w[:, o + nsw + i * kvw: o + nsw + (i + 1) * kvw] for i in range(6)]
        w_g = w[:, o + nsw + 6 * kvw: o + nsw + 6 * kvw + 3 * NSA_HEADS]
        w_main = jnp.concatenate(
            [w[:, :o], w_q * (scale * LOG2E), _pad_slots(w_kv[2], g),
             _pad_slots(w_kv[4], g), w_kv[0], w_kv[1]], axis=1).astype(BF16)
        w_gt = jnp.pad(w_g.reshape(d, g, 3 * hg), ((0, 0), (0, 0), (0, GATE_ROWS - 3 * hg))).reshape(d, -1)
        v_slots = lambda a: jnp.pad(a.reshape(d, g, dh), ((0, 0), (0, 0), (0, V_SLOT - dh))).reshape(d, -1)
        w_t = jnp.concatenate([v_slots(w_kv[3]), v_slots(w_kv[5]), w_gt], axis=1).T.astype(BF16)
        widths = (o, nsw, g * LANES, kvw, 2 * g * V_SLOT, g * GATE_ROWS)
        hg_in, q_slots, ks_aug, kw_aug, kc, vc, v_t, gate_t = _inproj(x, ada3, w_main, w_t, n_mod, widths)

        y_hg = _hgrn(hg_in, hg_lb_logits, hg_norm_g[l][None], l)

        grp = lambda a: a.reshape(bsz, n_grp, CMP_STRIDE * kvw)
        pos_e = lambda p: jnp.broadcast_to(
            p.reshape(2, CMP_STRIDE, 1, dh), (2, CMP_STRIDE, g, dh)).reshape(1, -1)
        w2k = jnp.pad(cmp_w2_k[l], ((0, 0), (0, LANES - dh))).astype(BF16)
        kc_aug = _compress(grp(kc), pos_e(cmp_pos_k[l]), _cmp_w1_expand(cmp_w1_k[l], g).astype(BF16),
                           w2k, True)
        vc_t = _compress(grp(vc), pos_e(cmp_pos_v[l]), _cmp_w1_expand(cmp_w1_v[l], g).astype(BF16),
                         cmp_w2_v[l].T.astype(BF16), False)

        y_nsa = _nsa(q_slots, gate_t, kc_aug, vc_t, ks_aug, kw_aug, v_t, ov_t, slope_tab)

        w_o = w_out[l].astype(BF16)
        x = _post(x, y_hg, y_nsa, ada3.reshape(bsz, n_mod, 1, d), w_o[:hgw], w_o[hgw:],
                  ffn_w1[l].astype(BF16), ffn_w3[l].astype(BF16), ffn_w2[l].astype(BF16),
                  ln1_g[l][None], ln1_b[l][None], ln2_g[l][None], ln2_b[l][None], alpha)
    return x
```

```python
import functools

import numpy as np
import jax
import jax.numpy as jnp
from jax import lax
from jax.experimental import pallas as pl
from jax.experimental.pallas import tpu as pltpu

F32 = jnp.float32
BF16 = jnp.bfloat16

HG_HEADS = 4
HG_D = 128
NSA_HEADS = 8
NSA_GROUPS = 2
NSA_HG = NSA_HEADS // NSA_GROUPS
NSA_DH = 64
CMP_BLOCK = 32
CMP_STRIDE = 16
CMP_HIDDEN = 256
SLC_BLOCK = 64
SLC_TOPN = 16
WINDOW = 512
Q_BLOCK = 128
NEG = -1e30
FORCE = 1e4
LN_EPS = 1e-5
RMS_EPS = 1e-6

LANES = 128
SUBLANES = 8
HG_BLK = 32
HG_SAFE_LOG2 = 100.0
POS_SPLIT = 256
GATE_ROWS = 16
SEL_TILE = 512
SEL_UNROLL = 8
V_SLOT = 80
N_SPLIT = 3
LOG2E = 1.4426950408889634
MXU_DIM = 256
VMEM_LIMIT = 48 * 1024 * 1024
FFN_VMEM_LIMIT = 56 * 1024 * 1024


def _sigmoid(x):
    return 1.0 / (1.0 + jnp.exp(-x))


def _dot(a, b):
    return jnp.dot(a, b, preferred_element_type=F32)


def _dot_nt(a, b):
    return lax.dot_general(a, b, (((1,), (1,)), ((), ())), preferred_element_type=F32)


def _pos_columns(pos, col):
    c = col - NSA_DH
    lo = pos % POS_SPLIT
    val = jnp.where(c % 2 == 0, lo, pos - lo)
    return jnp.where((c >= 0) & (c < 2 * N_SPLIT), val, 0).astype(F32)


def _split2(x):
    hi = x.astype(BF16)
    return hi, (x - hi.astype(F32)).astype(BF16)


def _ada_kernel(c_ref, w_ref, b_ref, o_ref):
    c = c_ref[...]
    s = (c * _sigmoid(c)).astype(BF16)
    o_ref[...] = _dot(s, w_ref[...].astype(BF16)) + b_ref[...]


def _ada(c_pad, w, b):
    rows, d = c_pad.shape
    n = w.shape[1]
    tn = 1024
    return pl.pallas_call(
        _ada_kernel,
        out_shape=jax.ShapeDtypeStruct((rows, n), F32),
        grid=(n // tn,),
        in_specs=[
            pl.BlockSpec((rows, d), lambda j: (0, 0)),
            pl.BlockSpec((d, tn), lambda j: (0, j)),
            pl.BlockSpec((1, tn), lambda j: (0, j)),
        ],
        out_specs=pl.BlockSpec((rows, tn), lambda j: (0, j)),
        compiler_params=pltpu.CompilerParams(
            dimension_semantics=("parallel",), vmem_limit_bytes=VMEM_LIMIT),
        name="ada",
    )(c_pad, w, b)


def _inproj_kernel(x_ref, sc_ref, sh_ref, wm_ref, wt_ref,
                   hg_ref, q_ref, ks_ref, kw_ref, kc_ref, vc_ref, vt_ref, gt_ref, *, tm):
    h = (x_ref[0] * (1.0 + sc_ref[0]) + sh_ref[0]).astype(BF16)
    y = _dot(h, wm_ref[...])
    n_hg = hg_ref.shape[2]
    n_q = q_ref.shape[2]
    n_k = ks_ref.shape[2]
    n_c = kc_ref.shape[2]
    o = 0
    hg_ref[0] = y[:, o:o + n_hg]
    o += n_hg
    q_ref[0] = y[:, o:o + n_q].astype(BF16)
    o += n_q
    row = pl.program_id(1) * tm + lax.broadcasted_iota(jnp.int32, (tm, n_k), 0)
    pc = _pos_columns(row, lax.broadcasted_iota(jnp.int32, (tm, n_k), 1) % LANES)
    ks_ref[0] = (y[:, o:o + n_k] + pc).astype(BF16)
    o += n_k
    kw_ref[0] = (y[:, o:o + n_k] + pc).astype(BF16)
    o += n_k
    kc_ref[0] = y[:, o:o + n_c]
    o += n_c
    vc_ref[0] = y[:, o:o + n_c]
    yt = _dot_nt(wt_ref[...], h)
    n_vt = vt_ref.shape[1]
    ri = lax.broadcasted_iota(jnp.int32, (n_vt, tm), 0) % V_SLOT
    vt_ref[0] = (yt[:n_vt] + jnp.where(ri == NSA_DH, 1.0, 0.0)).astype(BF16)
    gt_ref[0] = yt[n_vt:]


def _inproj(x, ada3, w_main, w_t, n_mod, widths):
    b, t, d = x.shape
    n_hg, n_q, n_k, n_c, n_vt, n_gt = widths
    tm = 256
    row_spec = lambda n: pl.BlockSpec((1, tm, n), lambda i, j: (i, j, 0))
    col_spec = lambda n: pl.BlockSpec((1, n, tm), lambda i, j: (i, 0, j))
    sds = jax.ShapeDtypeStruct
    return pl.pallas_call(
        functools.partial(_inproj_kernel, tm=tm),
        out_shape=(sds((b, t, n_hg), F32), sds((b, t, n_q), BF16), sds((b, t, n_k), BF16),
                   sds((b, t, n_k), BF16), sds((b, t, n_c), F32), sds((b, t, n_c), F32),
                   sds((b, n_vt, t), BF16), sds((b, n_gt, t), F32)),
        grid=(b, t // tm),
        in_specs=[
            pl.BlockSpec((1, tm, d), lambda i, j: (i, j, 0)),
            pl.BlockSpec((1, 1, d), lambda i, j: (i * n_mod + 1, 0, 0)),
            pl.BlockSpec((1, 1, d), lambda i, j: (i * n_mod, 0, 0)),
            pl.BlockSpec(w_main.shape, lambda i, j: (0, 0)),
            pl.BlockSpec(w_t.shape, lambda i, j: (0, 0)),
        ],
        out_specs=(row_spec(n_hg), row_spec(n_q), row_spec(n_k), row_spec(n_k), row_spec(n_c),
                   row_spec(n_c), col_spec(n_vt), col_spec(n_gt)),
        compiler_params=pltpu.CompilerParams(
            dimension_semantics=("parallel", "parallel"), vmem_limit_bytes=VMEM_LIMIT),
        name="inproj",
    )(x, ada3, ada3, w_main, w_t)


def _hgrn_kernel(q_ref, f_ref, i_ref, g_ref, lbl_ref, ng_ref, e_ref, o_ref, st_ref, *, tc, layer):
    blk = HG_BLK
    per_grp = LANES // blk
    half = blk // 2

    @pl.when(pl.program_id(2) == 0)
    def _():
        st_ref[...] = jnp.zeros_like(st_ref)

    lbl = lbl_ref[...]
    e = jnp.exp(lbl - jnp.max(lbl, axis=0, keepdims=True))
    lb = jnp.sum(e[: layer + 1], axis=0, keepdims=True) / jnp.sum(e, axis=0, keepdims=True)

    r_i = lax.broadcasted_iota(jnp.int32, (LANES, LANES), 0)
    c_i = lax.broadcasted_iota(jnp.int32, (LANES, LANES), 1)
    same_blk = (r_i // blk) == (c_i // blk)
    causal_blk = same_blk & (c_i <= r_i)
    tri = jnp.where(causal_blk, 1.0, 0.0).astype(BF16)
    row_blk = r_i // blk
    nh = per_grp * half
    r_h = lax.broadcasted_iota(jnp.int32, (nh, LANES), 0)
    same_half = (r_h // half) == (lax.broadcasted_iota(jnp.int32, (nh, LANES), 1) // blk)
    row_in_half = r_h % half
    blocked = lambda a: a.reshape(per_grp, blk, LANES)

    def gates(g):
        sl = slice(g * LANES, (g + 1) * LANES)
        q = q_ref[0, sl, :]
        qs = q * _sigmoid(q)
        f = lb + (1.0 - lb) * _sigmoid(f_ref[0, sl, :])
        hi, lo = _split2(jnp.log2(f))
        b = _dot(tri, hi) + _dot(tri, lo)
        return qs, 1.0 - f, i_ref[0, sl, :], b

    def increments(qs, k, v, b):
        b3 = blocked(b)
        bl3 = b3[:, blk - 1:blk, :]
        qd = (qs * jnp.exp2(b)).astype(BF16)
        kd = (blocked(k) * jnp.exp2(bl3 - b3)).reshape(LANES, LANES)
        kexp = jnp.concatenate(
            [jnp.where(row_blk == j, kd, 0.0) for j in range(per_grp)], axis=1).astype(BF16)
        return qd, jnp.exp2(bl3), _dot(v.T.astype(BF16), kexp)

    def recur(st, qd, dl, ut):
        ois = []
        for j in range(per_grp):
            ois.append(_dot_nt(qd[j * blk:(j + 1) * blk], st.astype(BF16)))
            st = dl[j] * st + ut[:, j * LANES:(j + 1) * LANES]
        return st, jnp.concatenate(ois, axis=0)

    def pairwise(qs, k, b):
        c = b - jnp.log2(k)
        tiles = lambda a, off: jnp.concatenate(
            [a[j * blk + off:j * blk + off + half] for j in range(per_grp)], axis=0)
        q_lo, q_hi, b_lo, b_hi = tiles(qs, 0), tiles(qs, half), tiles(b, 0), tiles(b, half)
        p_lo, p_hi = [], []
        for s in range(blk):
            cs = jnp.concatenate([jnp.broadcast_to(c[j * blk + s:j * blk + s + 1], (half, LANES))
                                  for j in range(per_grp)], axis=0)
            if s < half:
                p_lo.append(jnp.where(row_in_half >= s, q_lo * jnp.exp2(b_lo - cs), 0.0).astype(BF16))
                p_hi.append((q_hi * jnp.exp2(b_hi - cs)).astype(BF16))
            else:
                p_hi.append(jnp.where(row_in_half >= s - half, q_hi * jnp.exp2(b_hi - cs), 0.0).astype(BF16))
        return jnp.concatenate(p_lo, axis=1), jnp.concatenate(p_hi, axis=1)

    def emit(g, o):
        o = o * lax.rsqrt(jnp.mean(o * o, axis=-1, keepdims=True) + RMS_EPS)
        sl = slice(g * LANES, (g + 1) * LANES)
        o_ref[0, sl, :] = (o * ng_ref[...] * _sigmoid(g_ref[0, sl, :])).astype(o_ref.dtype)

    ngrp = tc // LANES
    gs = [gates(g) for g in range(ngrp)]
    incs = [increments(*gt) for gt in gs]
    st = st_ref[...]
    o_inters = []
    for inc in incs:
        st, oi = recur(st, *inc)
        o_inters.append(oi)
    st_ref[...] = st

    b_min = functools.reduce(jnp.minimum, [jnp.min(gt[3]) for gt in gs])

    @pl.when(b_min >= -HG_SAFE_LOG2)
    def _():
        a_s = [_dot_nt(qd, (k * jnp.exp2(-b)).astype(BF16))
               for (_, k, _, b), (qd, _, _) in zip(gs, incs)]
        o_s = [_dot(jnp.where(causal_blk, a, 0.0).astype(BF16), gt[2].astype(BF16)) for a, gt in zip(a_s, gs)]
        for g, o in enumerate(o_s):
            emit(g, o + o_inters[g])

    @pl.when(b_min < -HG_SAFE_LOG2)
    def _():
        pws = [pairwise(qs, k, b) for qs, k, _, b in gs]
        a_lo = _dot(jnp.concatenate([p[0] for p in pws], axis=0), e_ref[:half * LANES, :])
        a_hi = _dot(jnp.concatenate([p[1] for p in pws], axis=0), e_ref[...])
        for g in range(ngrp):
            rows = slice(g * nh, (g + 1) * nh)
            vb = gs[g][2].astype(BF16)
            o_lo = _dot(jnp.where(same_half, a_lo[rows], 0.0).astype(BF16), vb)
            o_hi = _dot(jnp.where(same_half, a_hi[rows], 0.0).astype(BF16), vb)
            emit(g, jnp.concatenate([x[j * half:(j + 1) * half] for j in range(per_grp) for x in (o_lo, o_hi)],
                                    axis=0) + o_inters[g])


def _hgrn(proj, lb_logits, norm_g, layer):
    b, t, _ = proj.shape
    tc = 1024
    h = HG_HEADS
    nslots = lb_logits.shape[0]
    col = lambda seg: (lambda i, j, s: (i, s, seg * h + j))
    fold = np.zeros((HG_BLK, LANES, LANES), np.float32)
    for s in range(HG_BLK):
        fold[s, :, s::HG_BLK] = 1.0
    fold = jnp.asarray(fold.reshape(HG_BLK * LANES, LANES), BF16)
    return pl.pallas_call(
        functools.partial(_hgrn_kernel, tc=tc, layer=layer),
        out_shape=jax.ShapeDtypeStruct((b, t, h * HG_D), BF16),
        grid=(b, h, t // tc),
        in_specs=[
            pl.BlockSpec((1, tc, HG_D), col(0)),
            pl.BlockSpec((1, tc, HG_D), col(1)),
            pl.BlockSpec((1, tc, HG_D), col(2)),
            pl.BlockSpec((1, tc, HG_D), col(3)),
            pl.BlockSpec((nslots, HG_D), lambda i, j, s: (0, j)),
            pl.BlockSpec((1, HG_D), lambda i, j, s: (0, j)),
            pl.BlockSpec(fold.shape, lambda i, j, s: (0, 0)),
        ],
        out_specs=pl.BlockSpec((1, tc, HG_D), lambda i, j, s: (i, s, j)),
        scratch_shapes=[pltpu.VMEM((HG_D, HG_D), F32)],
        compiler_params=pltpu.CompilerParams(
            dimension_semantics=("parallel", "parallel", "arbitrary"), vmem_limit_bytes=VMEM_LIMIT),
        name="hgrn",
    )(proj, proj, proj, proj, lb_logits, norm_g, fold)


def _cmp_kernel(h_ref, pos_ref, w1_ref, w2_ref, o_ref, *, key_side):
    h = h_ref[0]
    half = h.shape[1]
    pos = pos_ref[...]
    p0 = _dot((h + pos[:, :half]).astype(BF16), w1_ref[0, :half])
    p1 = _dot((h + pos[:, half:]).astype(BF16), w1_ref[0, half:])
    ng = h.shape[0]
    pre = p0 + pltpu.roll(p1, ng - 1, 0)
    gl = 0.5 * pre * (1.0 + jnp.tanh(np.sqrt(2.0 / np.pi) * (pre + 0.044715 * (pre * pre * pre))))
    gl = gl.astype(BF16)
    if key_side:
        cend = lax.broadcasted_iota(jnp.int32, (ng, LANES), 0) * CMP_STRIDE + (CMP_BLOCK - 1)
        pc = _pos_columns(cend, lax.broadcasted_iota(jnp.int32, (ng, LANES), 1))
        o_ref[0, 0] = (_dot(gl, w2_ref[...]) + pc).astype(BF16)
    else:
        o_ref[0, 0] = _dot_nt(w2_ref[...], gl).astype(BF16)


def _compress(hgrp, pos, w1, w2, key_side):
    b, ng, half = hgrp.shape
    g = w1.shape[0]
    out_tail = (ng, LANES) if key_side else (w2.shape[0], ng)
    return pl.pallas_call(
        functools.partial(_cmp_kernel, key_side=key_side),
        out_shape=jax.ShapeDtypeStruct((b, g) + out_tail, BF16),
        grid=(b, g),
        in_specs=[
            pl.BlockSpec((1, ng, half), lambda i, j: (i, 0, 0)),
            pl.BlockSpec(pos.shape, lambda i, j: (0, 0)),
            pl.BlockSpec((1,) + w1.shape[1:], lambda i, j: (j, 0, 0)),
            pl.BlockSpec(w2.shape, lambda i, j: (0, 0)),
        ],
        out_specs=pl.BlockSpec((1, 1) + out_tail, lambda i, j: (i, j, 0, 0)),
        compiler_params=pltpu.CompilerParams(
            dimension_semantics=("parallel", "parallel"), vmem_limit_bytes=VMEM_LIMIT),
        name="cmp_k" if key_side else "cmp_v",
    )(hgrp, pos, w1, w2)


def _nsa_kernel(q_ref, gate_ref, kc_ref, vct_ref, ks_ref, vst_ref, kw_ref, vwt_ref, ovt_ref, slope_ref,
                o_ref, bias_ref, *z_scratch):
    qb = pl.program_id(2)
    t0 = qb * Q_BLOCK
    npair = NSA_HG // 2
    za_refs, zb_refs = z_scratch[:npair], z_scratch[npair:]
    nq = 2 * Q_BLOCK
    qblk = q_ref[0].astype(F32)
    low = lax.broadcasted_iota(jnp.int32, (Q_BLOCK, LANES), 1) < NSA_DH
    qas = []
    for hp in range(npair):
        pair = qblk[:, hp * LANES:(hp + 1) * LANES]
        heads = [jnp.where(low, x, 0.0) for x in (pair, pltpu.roll(pair, NSA_DH, 1))]
        qas.append(jnp.concatenate(heads, axis=0).astype(BF16) + slope_ref[0, hp * nq:(hp + 1) * nq])
    tpos = t0 + (lax.broadcasted_iota(jnp.int32, (1, nq), 1) % Q_BLOCK)

    def normalised(acc):
        return acc[:NSA_DH] * (1.0 / acc[NSA_DH:NSA_DH + 1])

    per_step = SEL_TILE // SLC_BLOCK
    spos0 = lax.broadcasted_iota(jnp.int32, (SEL_TILE, 1), 0)
    n_slab = WINDOW // Q_BLOCK + 1
    slab_kt = [qb - (n_slab - 1) + i for i in range(n_slab)]
    slab_at = [pl.multiple_of(jnp.maximum(kt, 0) * Q_BLOCK, Q_BLOCK) for kt in slab_kt]
    kw = jnp.concatenate([kw_ref[0, pl.ds(s, Q_BLOCK), :] for s in slab_at], axis=0)

    def sel_scores(st, z_refs):
        ks = ks_ref[0, pl.ds(pl.multiple_of(st * SEL_TILE, SEL_TILE), SEL_TILE), :]
        for z_ref, qa in zip(z_refs, qas):
            z_ref[...] = _dot_nt(ks, qa)

    zcs = [_dot_nt(kc_ref[0, 0], qa) for qa in qas]
    zws = [_dot_nt(kw, qa) for qa in qas]
    sel_scores(0, za_refs)

    ncp = kc_ref.shape[2]
    cend = lax.broadcasted_iota(jnp.int32, (ncp, 1), 0) * CMP_STRIDE + (CMP_BLOCK - 1)
    valid = cend <= tpos
    o_c, psum = [], None
    for zc in zcs:
        zm = jnp.where(valid, zc, NEG)
        ec = jnp.where(valid, jnp.exp2(zm - jnp.max(zm, axis=0, keepdims=True)), 0.0)
        lc = jnp.sum(ec, axis=0, keepdims=True)
        pc = ec * jnp.where(lc > 0.0, 1.0 / lc, 0.0)
        o_c.append(_dot(vct_ref[0, 0], pc.astype(BF16)))
        for i in range(2):
            ph = pc[:, i * Q_BLOCK:(i + 1) * Q_BLOCK]
            psum = ph if psum is None else psum + ph

    p_hi = psum.astype(BF16)
    p_lo = (psum - p_hi.astype(F32)).astype(BF16)
    imp = _dot(ovt_ref[...], p_hi) + _dot(ovt_ref[...], p_lo)
    ns = imp.shape[0]
    j_i = lax.broadcasted_iota(jnp.int32, (ns, 1), 0)
    t_abs = t0 + lax.broadcasted_iota(jnp.int32, (1, Q_BLOCK), 1)
    cur = t_abs // SLC_BLOCK
    forced = (j_i == 0) | (j_i == cur) | (j_i == cur - 1)
    x = jnp.where(forced, FORCE, imp)
    x = jnp.where(j_i * SLC_BLOCK <= t_abs, x, NEG)

    vw = jnp.concatenate([vwt_ref[0, :, pl.ds(s, Q_BLOCK)] for s in slab_at], axis=1)
    r_k = lax.broadcasted_iota(jnp.int32, (Q_BLOCK, nq), 0)
    r_q = lax.broadcasted_iota(jnp.int32, (Q_BLOCK, nq), 1) % Q_BLOCK
    o_w = []
    for zw in zws:
        slabs = []
        for i, kt in enumerate(slab_kt):
            zi = zw[i * Q_BLOCK:(i + 1) * Q_BLOCK]
            if i == 0:
                zi = jnp.where(r_k > r_q, zi, NEG)
            if i == n_slab - 1:
                zi = jnp.where(r_k <= r_q, zi, NEG)
            else:
                zi = zi + jnp.where(kt < 0, NEG, 0.0)
            slabs.append(zi)
        zw = jnp.concatenate(slabs, axis=0)
        pw = jnp.exp2(zw - jnp.max(zw, axis=0, keepdims=True))
        o_w.append(normalised(_dot(vw, pw.astype(BF16))))

    sub = lax.broadcasted_iota(jnp.int32, (SUBLANES, Q_BLOCK), 0)
    tiles = [x[v * SUBLANES:(v + 1) * SUBLANES] for v in range(ns // SUBLANES)]
    cnts = [jnp.zeros((SUBLANES, Q_BLOCK), F32) for _ in tiles]
    for jp in range(ns):
        r = x[jp:jp + 1, :]
        vj, rj = divmod(jp, SUBLANES)
        for v, xt in enumerate(tiles):
            if v < vj:
                ahead = r > xt
            elif v > vj:
                ahead = r >= xt
            else:
                ahead = (r > xt) | ((r == xt) & (sub > rj))
            cnts[v] = cnts[v] + jnp.where(ahead, 1.0, 0.0)
    for v, cnt in enumerate(cnts):
        bias_ref[v * SUBLANES:(v + 1) * SUBLANES, :] = jnp.where(cnt < float(SLC_TOPN), 0.0, NEG)

    def sel_update(st, z_refs, carry, causal):
        s0 = pl.multiple_of(st * SEL_TILE, SEL_TILE)
        vs = vst_ref[0, :, pl.ds(s0, SEL_TILE)]
        brow = bias_ref[pl.ds(pl.multiple_of(st * per_step, per_step), per_step), :]
        b2 = jnp.concatenate([brow] * 2, axis=1)[:, None, :]
        out = []
        for z_ref, (m, acc) in zip(z_refs, carry):
            z = (z_ref[...].reshape(per_step, SLC_BLOCK, nq) + b2).reshape(SEL_TILE, nq)
            if causal:
                z = jnp.where((s0 + spos0) <= tpos, z, NEG)
            mn = jnp.maximum(m, jnp.max(z, axis=0, keepdims=True))
            p = jnp.exp2(z - mn)
            out.append((mn, jnp.exp2(m - mn) * acc + _dot(vs, p.astype(BF16))))
        return tuple(out)

    def run_steps(first, count, carry, ends_causal):
        bufs = (za_refs, zb_refs)
        for i in range(count):
            if i + 1 < count or not ends_causal:
                sel_scores(first + i + 1, bufs[(i + 1) % 2])
            carry = sel_update(first + i, bufs[i % 2], carry, ends_causal and i + 1 == count)
        return carry

    init = tuple((jnp.full((1, nq), NEG, F32), jnp.zeros((V_SLOT, nq), F32)) for _ in range(npair))
    last = qb // (SEL_TILE // Q_BLOCK)
    n_loop = last // SEL_UNROLL
    carry = lax.fori_loop(0, n_loop, lambda i, c: run_steps(SEL_UNROLL * i, SEL_UNROLL, c, False), init)
    tails = [functools.partial(run_steps, SEL_UNROLL * n_loop, r + 1, ends_causal=True)
             for r in range(SEL_UNROLL)]
    carry = lax.switch(last % SEL_UNROLL, tails, carry)
    o_s = [normalised(acc) for _, acc in carry]

    gts = _sigmoid(gate_ref[0])
    pairs = []
    for hp in range(npair):
        halves = []
        for i in range(2):
            r = 3 * (2 * hp + i)
            sl = slice(i * Q_BLOCK, (i + 1) * Q_BLOCK)
            halves.append(gts[r:r + 1] * o_c[hp][:, sl] + gts[r + 1:r + 2] * o_s[hp][:, sl]
                          + gts[r + 2:r + 3] * o_w[hp][:, sl])
        pairs.append(jnp.concatenate(halves, axis=0).T)
    o_ref[0] = jnp.concatenate(pairs, axis=1).astype(o_ref.dtype)


def _nsa(q_slots, gate_t, kc_aug, vc_t, ks_aug, kw_aug, v_t, ov_t, slope_tab):
    b, t, _ = q_slots.shape
    g = NSA_GROUPS
    nqb = t // Q_BLOCK
    ncp = kc_aug.shape[2]
    ns = ov_t.shape[0]
    return pl.pallas_call(
        _nsa_kernel,
        out_shape=jax.ShapeDtypeStruct((b, t, NSA_HEADS * NSA_DH), BF16),
        grid=(b, g, nqb),
        in_specs=[
            pl.BlockSpec((1, Q_BLOCK, NSA_HG * NSA_DH), lambda i, j, s: (i, s, j)),
            pl.BlockSpec((1, GATE_ROWS, Q_BLOCK), lambda i, j, s: (i, j, s)),
            pl.BlockSpec((1, 1, ncp, LANES), lambda i, j, s: (i, j, 0, 0)),
            pl.BlockSpec((1, 1, NSA_DH, ncp), lambda i, j, s: (i, j, 0, 0)),
            pl.BlockSpec((1, t, LANES), lambda i, j, s: (i, 0, j)),
            pl.BlockSpec((1, V_SLOT, t), lambda i, j, s: (i, j, 0)),
            pl.BlockSpec((1, t, LANES), lambda i, j, s: (i, 0, j)),
            pl.BlockSpec((1, V_SLOT, t), lambda i, j, s: (i, g + j, 0)),
            pl.BlockSpec((ns, ncp), lambda i, j, s: (0, 0)),
            pl.BlockSpec((1, NSA_HG * Q_BLOCK, LANES), lambda i, j, s: (j, 0, 0)),
        ],
        out_specs=pl.BlockSpec((1, Q_BLOCK, NSA_HG * NSA_DH), lambda i, j, s: (i, s, j)),
        scratch_shapes=[pltpu.VMEM((ns, Q_BLOCK), F32)]
        + [pltpu.VMEM((SEL_TILE, 2 * Q_BLOCK), F32)] * NSA_HG,
        compiler_params=pltpu.CompilerParams(
            dimension_semantics=("parallel", "parallel", "arbitrary"), vmem_limit_bytes=VMEM_LIMIT),
        name="nsa",
    )(q_slots, gate_t, kc_aug, vc_t, ks_aug, v_t, kw_aug, v_t, ov_t, slope_tab)


def _layer_norm(z, g, b):
    mu = jnp.mean(z, axis=-1, keepdims=True)
    zc = z - mu
    var = jnp.mean(zc * zc, axis=-1, keepdims=True)
    return zc * lax.rsqrt(var + LN_EPS) * g + b


def _post_kernel(x_ref, yh_ref, yn_ref, mod_ref, wh_ref, wn_ref, w1_ref, w3_ref, w2_ref,
                 l1g_ref, l1b_ref, l2g_ref, l2b_ref, o_ref, *, alpha, chunks, row_parts):
    g1, sh2, sc2, g2 = mod_ref[0, 2], mod_ref[0, 3], mod_ref[0, 4], mod_ref[0, 5]
    tm = x_ref.shape[1]
    parts = [slice(i * tm // row_parts, (i + 1) * tm // row_parts) for i in range(row_parts)]
    mixes = [_dot(yh_ref[0, r, :], wh_ref[...]) + _dot(yn_ref[0, r, :], wn_ref[...]) for r in parts]
    x1s, ups = [], []
    for r, mix in zip(parts, mixes):
        x1 = _layer_norm(alpha * x_ref[0, r, :] + g1 * mix, l1g_ref[...], l1b_ref[...])
        h = (x1 * (1.0 + sc2) + sh2).astype(BF16)
        x1s.append(x1)
        ups.append([(_dot(h, w1_ref[:, lo:hi]), _dot(h, w3_ref[:, lo:hi])) for lo, hi in chunks])
    downs = []
    for up in ups:
        a = jnp.concatenate([(u * _sigmoid(u) * w).astype(BF16) for u, w in up], axis=1)
        downs.append(_dot(a, w2_ref[...]))
    for r, x1, down in zip(parts, x1s, downs):
        o_ref[0, r, :] = _layer_norm(alpha * x1 + g2 * down, l2g_ref[...], l2b_ref[...])


def _post(x, y_hg, y_nsa, ada4, w_h, w_n, w1, w3, w2, ln1_g, ln1_b, ln2_g, ln2_b, alpha):
    b, t, d = x.shape
    dff = w1.shape[1]
    tm = 512
    step = 3 * MXU_DIM
    chunks = tuple((lo, min(lo + step, dff)) for lo in range(0, dff, step))
    resident = lambda a: pl.BlockSpec(a.shape, lambda i, j: (0,) * a.ndim, pipeline_mode=pl.Buffered(1))
    rows = lambda a: pl.BlockSpec((1, tm, a.shape[2]), lambda i, j: (i, j, 0))
    vec = pl.BlockSpec((1, d), lambda i, j: (0, 0))
    return pl.pallas_call(
        functools.partial(_post_kernel, alpha=alpha, chunks=chunks, row_parts=2),
        out_shape=jax.ShapeDtypeStruct((b, t, d), F32),
        grid=(b, t // tm),
        in_specs=[rows(x), rows(y_hg), rows(y_nsa),
                  pl.BlockSpec((1,) + ada4.shape[1:], lambda i, j: (i, 0, 0, 0)),
                  resident(w_h), resident(w_n), resident(w1), resident(w3), resident(w2),
                  vec, vec, vec, vec],
        out_specs=rows(x),
        compiler_params=pltpu.CompilerParams(
            dimension_semantics=("parallel", "parallel"), vmem_limit_bytes=FFN_VMEM_LIMIT),
        name="post",
    )(x, y_hg, y_nsa, ada4, w_h, w_n, w1, w3, w2, ln1_g, ln1_b, ln2_g, ln2_b)


def _slope_table(group_slopes):
    g, hg = group_slopes.shape
    rest = (group_slopes * LOG2E).astype(np.float32)
    tab = np.zeros((g, hg, Q_BLOCK, LANES), jnp.bfloat16)
    for i in range(N_SPLIT):
        piece = rest.astype(jnp.bfloat16)
        rest = rest - piece.astype(np.float32)
        tab[..., NSA_DH + 2 * i] = piece[:, :, None]
        tab[..., NSA_DH + 2 * i + 1] = piece[:, :, None]
    return tab.reshape(g, hg * Q_BLOCK, LANES)


def _pad_slots(w, n_slots):
    d = w.shape[0]
    w3 = w.reshape(d, n_slots, -1)
    return jnp.pad(w3, ((0, 0), (0, 0), (0, LANES - w3.shape[2]))).reshape(d, n_slots * LANES)


def _cmp_w1_expand(w1, g):
    hid = w1.shape[1]
    w1r = w1.reshape(2, CMP_STRIDE, 1, NSA_DH, hid)
    eye = jnp.eye(g, dtype=w1.dtype).reshape(g, 1, 1, g, 1, 1)
    return (eye * w1r[None]).reshape(g, 2 * CMP_STRIDE * g * NSA_DH, hid)


def _overlap_t(n_cmp_pad, n_cmp, n_slc):
    cs = np.arange(n_cmp_pad) * CMP_STRIDE
    ss = np.arange(n_slc) * SLC_BLOCK
    ov = np.clip(np.minimum(cs[:, None] + CMP_BLOCK, ss[None] + SLC_BLOCK)
                 - np.maximum(cs[:, None], ss[None]), 0, None).astype(np.float32) / CMP_BLOCK
    ov[n_cmp:] = 0.0
    return ov.T


def kernel(x, c, w_ada, b_ada, w_in, hg_lb_logits, hg_norm_g, cmp_pos_k, cmp_w1_k, cmp_w2_k,
           cmp_pos_v, cmp_w1_v, cmp_w2_v, w_out, ln1_g, ln1_b, ffn_w1, ffn_w3, ffn_w2, ln2_g, ln2_b):
    bsz, t, d = x.shape
    depth = w_ada.shape[0]
    g, hg, dh = NSA_GROUPS, NSA_HG, NSA_DH
    hgw = HG_HEADS * HG_D
    nsw = NSA_HEADS * dh
    kvw = g * dh
    n_mod = 6
    alpha = (2.0 * depth) ** 0.25
    scale = dh ** -0.5
    n_grp = t // CMP_STRIDE
    n_cmp = (t - CMP_BLOCK) // CMP_STRIDE + 1
    n_slc = t // SLC_BLOCK
    assert t % SEL_TILE == 0 and t >= WINDOW + Q_BLOCK and 3 * hg <= GATE_ROWS and hg % 2 == 0

    slopes = (2.0 ** (-8.0 * (np.arange(NSA_HEADS) + 1) / NSA_HEADS)).reshape(g, hg)
    slope_tab = jnp.asarray(_slope_table(slopes), BF16)
    ov_t = jnp.asarray(_overlap_t(n_grp, n_cmp, n_slc), BF16)

    c_pad = jnp.pad(c, ((0, 8 - bsz), (0, 0)))
    for l in range(depth):
        ada = _ada(c_pad, w_ada[l], b_ada[l][None])
        ada3 = ada[:bsz].reshape(bsz * n_mod, 1, d)

        w = w_in[l]
        o = 4 * hgw
        w_q = w[:, o:o + nsw]
        w_kv = [w[:, o + nsw + i * kvw: o + nsw + (i + 1) * kvw] for i in range(6)]
        w_g = w[:, o + nsw + 6 * kvw: o + nsw + 6 * kvw + 3 * NSA_HEADS]
        w_main = jnp.concatenate(
            [w[:, :o], w_q * (scale * LOG2E), _pad_slots(w_kv[2], g),
             _pad_slots(w_kv[4], g), w_kv[0], w_kv[1]], axis=1).astype(BF16)
        w_gt = jnp.pad(w_g.reshape(d, g, 3 * hg), ((0, 0), (0, 0), (0, GATE_ROWS - 3 * hg))).reshape(d, -1)
        v_slots = lambda a: jnp.pad(a.reshape(d, g, dh), ((0, 0), (0, 0), (0, V_SLOT - dh))).reshape(d, -1)
        w_t = jnp.concatenate([v_slots(w_kv[3]), v_slots(w_kv[5]), w_gt], axis=1).T.astype(BF16)
        widths = (o, nsw, g * LANES, kvw, 2 * g * V_SLOT, g * GATE_ROWS)
        hg_in, q_slots, ks_aug, kw_aug, kc, vc, v_t, gate_t = _inproj(x, ada3, w_main, w_t, n_mod, widths)

        y_hg = _hgrn(hg_in, hg_lb_logits, hg_norm_g[l][None], l)

        grp = lambda a: a.reshape(bsz, n_grp, CMP_STRIDE * kvw)
        pos_e = lambda p: jnp.broadcast_to(
            p.reshape(2, CMP_STRIDE, 1, dh), (2, CMP_STRIDE, g, dh)).reshape(1, -1)
        w2k = jnp.pad(cmp_w2_k[l], ((0, 0), (0, LANES - dh))).astype(BF16)
        kc_aug = _compress(grp(kc), pos_e(cmp_pos_k[l]), _cmp_w1_expand(cmp_w1_k[l], g).astype(BF16),
                           w2k, True)
        vc_t = _compress(grp(vc), pos_e(cmp_pos_v[l]), _cmp_w1_expand(cmp_w1_v[l], g).astype(BF16),
                         cmp_w2_v[l].T.astype(BF16), False)

        y_nsa = _nsa(q_slots, gate_t, kc_aug, vc_t, ks_aug, kw_aug, v_t, ov_t, slope_tab)

        w_o = w_out[l].astype(BF16)
        x = _post(x, y_hg, y_nsa, ada3.reshape(bsz, n_mod, 1, d), w_o[:hgw], w_o[hgw:],
                  ffn_w1[l].astype(BF16), ffn_w3[l].astype(BF16), ffn_w2[l].astype(BF16),
                  ln1_g[l][None], ln1_b[l][None], ln2_g[l][None], ln2_b[l][None], alpha)
    return x
```

```python
import functools

import numpy as np
import jax
import jax.numpy as jnp
from jax import lax
from jax.experimental import pallas as pl
from jax.experimental.pallas import tpu as pltpu

F32 = jnp.float32
BF16 = jnp.bfloat16

HG_HEADS = 4
HG_D = 128
NSA_HEADS = 8
NSA_GROUPS = 2
NSA_HG = NSA_HEADS // NSA_GROUPS
NSA_DH = 64
CMP_BLOCK = 32
CMP_STRIDE = 16
CMP_HIDDEN = 256
SLC_BLOCK = 64
SLC_TOPN = 16
WINDOW = 512
Q_BLOCK = 256
NEG = -1e30
FORCE = 1e4
LN_EPS = 1e-5
RMS_EPS = 1e-6

LANES = 128
SUBLANES = 8
HG_BLK = 32
HG_SAFE_LOG2 = 100.0
POS_SPLIT = 256
GATE_ROWS = 16
SEL_TILE = 512
SEL_UNROLL = 8
V_SLOT = 80
N_SPLIT = 3
LOG2E = 1.4426950408889634
MXU_DIM = 256
VMEM_LIMIT = 48 * 1024 * 1024
FFN_VMEM_LIMIT = 56 * 1024 * 1024


def _sigmoid(x):
    return 1.0 / (1.0 + jnp.exp(-x))


def _dot(a, b):
    return jnp.dot(a, b, preferred_element_type=F32)


def _dot_nt(a, b):
    return lax.dot_general(a, b, (((1,), (1,)), ((), ())), preferred_element_type=F32)


def _pos_columns(pos, col):
    c = col - NSA_DH
    lo = pos % POS_SPLIT
    val = jnp.where(c % 2 == 0, lo, pos - lo)
    return jnp.where((c >= 0) & (c < 2 * N_SPLIT), val, 0).astype(F32)


def _split2(x):
    hi = x.astype(BF16)
    return hi, (x - hi.astype(F32)).astype(BF16)


def _ada_kernel(c_ref, w_ref, b_ref, o_ref):
    c = c_ref[...]
    s = (c * _sigmoid(c)).astype(BF16)
    o_ref[...] = _dot(s, w_ref[...].astype(BF16)) + b_ref[...]


def _ada(c_pad, w, b):
    rows, d = c_pad.shape
    n = w.shape[1]
    tn = 1024
    return pl.pallas_call(
        _ada_kernel,
        out_shape=jax.ShapeDtypeStruct((rows, n), F32),
        grid=(n // tn,),
        in_specs=[
            pl.BlockSpec((rows, d), lambda j: (0, 0)),
            pl.BlockSpec((d, tn), lambda j: (0, j)),
            pl.BlockSpec((1, tn), lambda j: (0, j)),
        ],
        out_specs=pl.BlockSpec((rows, tn), lambda j: (0, j)),
        compiler_params=pltpu.CompilerParams(
            dimension_semantics=("parallel",), vmem_limit_bytes=VMEM_LIMIT),
        name="ada",
    )(c_pad, w, b)


def _inproj_kernel(x_ref, sc_ref, sh_ref, wm_ref, wt_ref,
                   hg_ref, q_ref, ks_ref, kw_ref, kc_ref, vc_ref, vt_ref, gt_ref, *, tm):
    h = (x_ref[0] * (1.0 + sc_ref[0]) + sh_ref[0]).astype(BF16)
    y = _dot(h, wm_ref[...])
    n_hg = hg_ref.shape[2]
    n_q = q_ref.shape[2]
    n_k = ks_ref.shape[2]
    n_c = kc_ref.shape[2]
    o = 0
    hg_ref[0] = y[:, o:o + n_hg]
    o += n_hg
    q_ref[0] = y[:, o:o + n_q].astype(BF16)
    o += n_q
    row = pl.program_id(1) * tm + lax.broadcasted_iota(jnp.int32, (tm, n_k), 0)
    pc = _pos_columns(row, lax.broadcasted_iota(jnp.int32, (tm, n_k), 1) % LANES)
    ks_ref[0] = (y[:, o:o + n_k] + pc).astype(BF16)
    o += n_k
    kw_ref[0] = (y[:, o:o + n_k] + pc).astype(BF16)
    o += n_k
    kc_ref[0] = y[:, o:o + n_c]
    o += n_c
    vc_ref[0] = y[:, o:o + n_c]
    yt = _dot_nt(wt_ref[...], h)
    n_vt = vt_ref.shape[1]
    ri = lax.broadcasted_iota(jnp.int32, (n_vt, tm), 0) % V_SLOT
    vt_ref[0] = (yt[:n_vt] + jnp.where(ri == NSA_DH, 1.0, 0.0)).astype(BF16)
    gt_ref[0] = yt[n_vt:]


def _inproj(x, ada3, w_main, w_t, n_mod, widths):
    b, t, d = x.shape
    n_hg, n_q, n_k, n_c, n_vt, n_gt = widths
    tm = 256
    row_spec = lambda n: pl.BlockSpec((1, tm, n), lambda i, j: (i, j, 0))
    col_spec = lambda n: pl.BlockSpec((1, n, tm), lambda i, j: (i, 0, j))
    sds = jax.ShapeDtypeStruct
    return pl.pallas_call(
        functools.partial(_inproj_kernel, tm=tm),
        out_shape=(sds((b, t, n_hg), F32), sds((b, t, n_q), BF16), sds((b, t, n_k), BF16),
                   sds((b, t, n_k), BF16), sds((b, t, n_c), F32), sds((b, t, n_c), F32),
                   sds((b, n_vt, t), BF16), sds((b, n_gt, t), F32)),
        grid=(b, t // tm),
        in_specs=[
            pl.BlockSpec((1, tm, d), lambda i, j: (i, j, 0)),
            pl.BlockSpec((1, 1, d), lambda i, j: (i * n_mod + 1, 0, 0)),
            pl.BlockSpec((1, 1, d), lambda i, j: (i * n_mod, 0, 0)),
            pl.BlockSpec(w_main.shape, lambda i, j: (0, 0)),
            pl.BlockSpec(w_t.shape, lambda i, j: (0, 0)),
        ],
        out_specs=(row_spec(n_hg), row_spec(n_q), row_spec(n_k), row_spec(n_k), row_spec(n_c),
                   row_spec(n_c), col_spec(n_vt), col_spec(n_gt)),
        compiler_params=pltpu.CompilerParams(
            dimension_semantics=("parallel", "parallel"), vmem_limit_bytes=VMEM_LIMIT),
        name="inproj",
    )(x, ada3, ada3, w_main, w_t)


def _hgrn_kernel(q_ref, f_ref, i_ref, g_ref, lbl_ref, ng_ref, e_ref, o_ref, st_ref, *, tc, layer):
    blk = HG_BLK
    per_grp = LANES // blk
    half = blk // 2

    @pl.when(pl.program_id(2) == 0)
    def _():
        st_ref[...] = jnp.zeros_like(st_ref)

    lbl = lbl_ref[...]
    e = jnp.exp(lbl - jnp.max(lbl, axis=0, keepdims=True))
    lb = jnp.sum(e[: layer + 1], axis=0, keepdims=True) / jnp.sum(e, axis=0, keepdims=True)

    r_i = lax.broadcasted_iota(jnp.int32, (LANES, LANES), 0)
    c_i = lax.broadcasted_iota(jnp.int32, (LANES, LANES), 1)
    same_blk = (r_i // blk) == (c_i // blk)
    causal_blk = same_blk & (c_i <= r_i)
    tri = jnp.where(causal_blk, 1.0, 0.0).astype(BF16)
    row_blk = r_i // blk
    nh = per_grp * half
    r_h = lax.broadcasted_iota(jnp.int32, (nh, LANES), 0)
    same_half = (r_h // half) == (lax.broadcasted_iota(jnp.int32, (nh, LANES), 1) // blk)
    row_in_half = r_h % half
    blocked = lambda a: a.reshape(per_grp, blk, LANES)

    def gates(g):
        sl = slice(g * LANES, (g + 1) * LANES)
        q = q_ref[0, sl, :]
        qs = q * _sigmoid(q)
        f = lb + (1.0 - lb) * _sigmoid(f_ref[0, sl, :])
        hi, lo = _split2(jnp.log2(f))
        b = _dot(tri, hi) + _dot(tri, lo)
        return qs, 1.0 - f, i_ref[0, sl, :], b

    def increments(qs, k, v, b):
        b3 = blocked(b)
        bl3 = b3[:, blk - 1:blk, :]
        qd = (qs * jnp.exp2(b)).astype(BF16)
        kd = (blocked(k) * jnp.exp2(bl3 - b3)).reshape(LANES, LANES)
        kexp = jnp.concatenate(
            [jnp.where(row_blk == j, kd, 0.0) for j in range(per_grp)], axis=1).astype(BF16)
        return qd, jnp.exp2(bl3), _dot(v.T.astype(BF16), kexp)

    def recur(st, qd, dl, ut):
        ois = []
        for j in range(per_grp):
            ois.append(_dot_nt(qd[j * blk:(j + 1) * blk], st.astype(BF16)))
            st = dl[j] * st + ut[:, j * LANES:(j + 1) * LANES]
        return st, jnp.concatenate(ois, axis=0)

    def pairwise(qs, k, b):
        c = b - jnp.log2(k)
        tiles = lambda a, off: jnp.concatenate(
            [a[j * blk + off:j * blk + off + half] for j in range(per_grp)], axis=0)
        q_lo, q_hi, b_lo, b_hi = tiles(qs, 0), tiles(qs, half), tiles(b, 0), tiles(b, half)
        p_lo, p_hi = [], []
        for s in range(blk):
            cs = jnp.concatenate([jnp.broadcast_to(c[j * blk + s:j * blk + s + 1], (half, LANES))
                                  for j in range(per_grp)], axis=0)
            if s < half:
                p_lo.append(jnp.where(row_in_half >= s, q_lo * jnp.exp2(b_lo - cs), 0.0).astype(BF16))
                p_hi.append((q_hi * jnp.exp2(b_hi - cs)).astype(BF16))
            else:
                p_hi.append(jnp.where(row_in_half >= s - half, q_hi * jnp.exp2(b_hi - cs), 0.0).astype(BF16))
        return jnp.concatenate(p_lo, axis=1), jnp.concatenate(p_hi, axis=1)

    def emit(g, o):
        o = o * lax.rsqrt(jnp.mean(o * o, axis=-1, keepdims=True) + RMS_EPS)
        sl = slice(g * LANES, (g + 1) * LANES)
        o_ref[0, sl, :] = (o * ng_ref[...] * _sigmoid(g_ref[0, sl, :])).astype(o_ref.dtype)

    ngrp = tc // LANES
    gs = [gates(g) for g in range(ngrp)]
    incs = [increments(*gt) for gt in gs]
    st = st_ref[...]
    o_inters = []
    for inc in incs:
        st, oi = recur(st, *inc)
        o_inters.append(oi)
    st_ref[...] = st

    b_min = functools.reduce(jnp.minimum, [jnp.min(gt[3]) for gt in gs])

    @pl.when(b_min >= -HG_SAFE_LOG2)
    def _():
        a_s = [_dot_nt(qd, (k * jnp.exp2(-b)).astype(BF16))
               for (_, k, _, b), (qd, _, _) in zip(gs, incs)]
        o_s = [_dot(jnp.where(causal_blk, a, 0.0).astype(BF16), gt[2].astype(BF16)) for a, gt in zip(a_s, gs)]
        for g, o in enumerate(o_s):
            emit(g, o + o_inters[g])

    @pl.when(b_min < -HG_SAFE_LOG2)
    def _():
        pws = [pairwise(qs, k, b) for qs, k, _, b in gs]
        a_lo = _dot(jnp.concatenate([p[0] for p in pws], axis=0), e_ref[:half * LANES, :])
        a_hi = _dot(jnp.concatenate([p[1] for p in pws], axis=0), e_ref[...])
        for g in range(ngrp):
            rows = slice(g * nh, (g + 1) * nh)
            vb = gs[g][2].astype(BF16)
            o_lo = _dot(jnp.where(same_half, a_lo[rows], 0.0).astype(BF16), vb)
            o_hi = _dot(jnp.where(same_half, a_hi[rows], 0.0).astype(BF16), vb)
            emit(g, jnp.concatenate([x[j * half:(j + 1) * half] for j in range(per_grp) for x in (o_lo, o_hi)],
                                    axis=0) + o_inters[g])


def _hgrn(proj, lb_logits, norm_g, layer):
    b, t, _ = proj.shape
    tc = 1024
    h = HG_HEADS
    nslots = lb_logits.shape[0]
    col = lambda seg: (lambda i, j, s: (i, s, seg * h + j))
    fold = np.zeros((HG_BLK, LANES, LANES), np.float32)
    for s in range(HG_BLK):
        fold[s, :, s::HG_BLK] = 1.0
    fold = jnp.asarray(fold.reshape(HG_BLK * LANES, LANES), BF16)
    return pl.pallas_call(
        functools.partial(_hgrn_kernel, tc=tc, layer=layer),
        out_shape=jax.ShapeDtypeStruct((b, t, h * HG_D), BF16),
        grid=(b, h, t // tc),
        in_specs=[
            pl.BlockSpec((1, tc, HG_D), col(0)),
            pl.BlockSpec((1, tc, HG_D), col(1)),
            pl.BlockSpec((1, tc, HG_D), col(2)),
            pl.BlockSpec((1, tc, HG_D), col(3)),
            pl.BlockSpec((nslots, HG_D), lambda i, j, s: (0, j)),
            pl.BlockSpec((1, HG_D), lambda i, j, s: (0, j)),
            pl.BlockSpec(fold.shape, lambda i, j, s: (0, 0)),
        ],
        out_specs=pl.BlockSpec((1, tc, HG_D), lambda i, j, s: (i, s, j)),
        scratch_shapes=[pltpu.VMEM((HG_D, HG_D), F32)],
        compiler_params=pltpu.CompilerParams(
            dimension_semantics=("parallel", "parallel", "arbitrary"), vmem_limit_bytes=VMEM_LIMIT),
        name="hgrn",
    )(proj, proj, proj, proj, lb_logits, norm_g, fold)


def _cmp_kernel(h_ref, pos_ref, w1_ref, w2_ref, o_ref, *, key_side):
    h = h_ref[0]
    half = h.shape[1]
    pos = pos_ref[...]
    p0 = _dot((h + pos[:, :half]).astype(BF16), w1_ref[0, :half])
    p1 = _dot((h + pos[:, half:]).astype(BF16), w1_ref[0, half:])
    ng = h.shape[0]
    pre = p0 + pltpu.roll(p1, ng - 1, 0)
    gl = 0.5 * pre * (1.0 + jnp.tanh(np.sqrt(2.0 / np.pi) * (pre + 0.044715 * (pre * pre * pre))))
    gl = gl.astype(BF16)
    if key_side:
        cend = lax.broadcasted_iota(jnp.int32, (ng, LANES), 0) * CMP_STRIDE + (CMP_BLOCK - 1)
        pc = _pos_columns(cend, lax.broadcasted_iota(jnp.int32, (ng, LANES), 1))
        o_ref[0, 0] = (_dot(gl, w2_ref[...]) + pc).astype(BF16)
    else:
        o_ref[0, 0] = _dot_nt(w2_ref[...], gl).astype(BF16)


def _compress(hgrp, pos, w1, w2, key_side):
    b, ng, half = hgrp.shape
    g = w1.shape[0]
    out_tail = (ng, LANES) if key_side else (w2.shape[0], ng)
    return pl.pallas_call(
        functools.partial(_cmp_kernel, key_side=key_side),
        out_shape=jax.ShapeDtypeStruct((b, g) + out_tail, BF16),
        grid=(b, g),
        in_specs=[
            pl.BlockSpec((1, ng, half), lambda i, j: (i, 0, 0)),
            pl.BlockSpec(pos.shape, lambda i, j: (0, 0)),
            pl.BlockSpec((1,) + w1.shape[1:], lambda i, j: (j, 0, 0)),
            pl.BlockSpec(w2.shape, lambda i, j: (0, 0)),
        ],
        out_specs=pl.BlockSpec((1, 1) + out_tail, lambda i, j: (i, j, 0, 0)),
        compiler_params=pltpu.CompilerParams(
            dimension_semantics=("parallel", "parallel"), vmem_limit_bytes=VMEM_LIMIT),
        name="cmp_k" if key_side else "cmp_v",
    )(hgrp, pos, w1, w2)


def _nsa_kernel(q_ref, gate_ref, kc_ref, vct_ref, ks_ref, vst_ref, kw_ref, vwt_ref, ovt_ref, slope_ref,
                o_ref, bias_ref, *z_scratch):
    qb = pl.program_id(2)
    t0 = qb * Q_BLOCK
    npair = NSA_HG // 2
    za_refs, zb_refs = z_scratch[:npair], z_scratch[npair:]
    nq = 2 * Q_BLOCK
    qblk = q_ref[0].astype(F32)
    low = lax.broadcasted_iota(jnp.int32, (Q_BLOCK, LANES), 1) < NSA_DH
    qas = []
    for hp in range(npair):
        pair = qblk[:, hp * LANES:(hp + 1) * LANES]
        heads = [jnp.where(low, x, 0.0) for x in (pair, pltpu.roll(pair, NSA_DH, 1))]
        qas.append(jnp.concatenate(heads, axis=0).astype(BF16) + slope_ref[0, hp * nq:(hp + 1) * nq])
    tpos = t0 + (lax.broadcasted_iota(jnp.int32, (1, nq), 1) % Q_BLOCK)

    def normalised(acc):
        return acc[:NSA_DH] * (1.0 / acc[NSA_DH:NSA_DH + 1])

    per_step = SEL_TILE // SLC_BLOCK
    spos0 = lax.broadcasted_iota(jnp.int32, (SEL_TILE, 1), 0)
    n_slab = WINDOW // Q_BLOCK + 1
    slab_kt = [qb - (n_slab - 1) + i for i in range(n_slab)]
    slab_at = [pl.multiple_of(jnp.maximum(kt, 0) * Q_BLOCK, Q_BLOCK) for kt in slab_kt]
    kw = jnp.concatenate([kw_ref[0, pl.ds(s, Q_BLOCK), :] for s in slab_at], axis=0)

    def sel_scores(st, z_refs):
        ks = ks_ref[0, pl.ds(pl.multiple_of(st * SEL_TILE, SEL_TILE), SEL_TILE), :]
        for z_ref, qa in zip(z_refs, qas):
            z_ref[...] = _dot_nt(ks, qa)

    zcs = [_dot_nt(kc_ref[0, 0], qa) for qa in qas]
    zws = [_dot_nt(kw, qa) for qa in qas]
    sel_scores(0, za_refs)

    ncp = kc_ref.shape[2]
    cend = lax.broadcasted_iota(jnp.int32, (ncp, 1), 0) * CMP_STRIDE + (CMP_BLOCK - 1)
    valid = cend <= tpos
    o_c, psum = [], None
    for zc in zcs:
        zm = jnp.where(valid, zc, NEG)
        ec = jnp.where(valid, jnp.exp2(zm - jnp.max(zm, axis=0, keepdims=True)), 0.0)
        lc = jnp.sum(ec, axis=0, keepdims=True)
        pc = ec * jnp.where(lc > 0.0, 1.0 / lc, 0.0)
        o_c.append(_dot(vct_ref[0, 0], pc.astype(BF16)))
        for i in range(2):
            ph = pc[:, i * Q_BLOCK:(i + 1) * Q_BLOCK]
            psum = ph if psum is None else psum + ph

    p_hi = psum.astype(BF16)
    p_lo = (psum - p_hi.astype(F32)).astype(BF16)
    imp = _dot(ovt_ref[...], p_hi) + _dot(ovt_ref[...], p_lo)
    ns = imp.shape[0]
    j_i = lax.broadcasted_iota(jnp.int32, (ns, 1), 0)
    t_abs = t0 + lax.broadcasted_iota(jnp.int32, (1, Q_BLOCK), 1)
    cur = t_abs // SLC_BLOCK
    forced = (j_i == 0) | (j_i == cur) | (j_i == cur - 1)
    x = jnp.where(forced, FORCE, imp)
    x = jnp.where(j_i * SLC_BLOCK <= t_abs, x, NEG)

    vw = jnp.concatenate([vwt_ref[0, :, pl.ds(s, Q_BLOCK)] for s in slab_at], axis=1)
    r_k = lax.broadcasted_iota(jnp.int32, (Q_BLOCK, nq), 0)
    r_q = lax.broadcasted_iota(jnp.int32, (Q_BLOCK, nq), 1) % Q_BLOCK
    o_w = []
    for zw in zws:
        slabs = []
        for i, kt in enumerate(slab_kt):
            zi = zw[i * Q_BLOCK:(i + 1) * Q_BLOCK]
            if i == 0:
                zi = jnp.where(r_k > r_q, zi, NEG)
            if i == n_slab - 1:
                zi = jnp.where(r_k <= r_q, zi, NEG)
            else:
                zi = zi + jnp.where(kt < 0, NEG, 0.0)
            slabs.append(zi)
        zw = jnp.concatenate(slabs, axis=0)
        pw = jnp.exp2(zw - jnp.max(zw, axis=0, keepdims=True))
        o_w.append(normalised(_dot(vw, pw.astype(BF16))))

    sub = lax.broadcasted_iota(jnp.int32, (SUBLANES, Q_BLOCK), 0)
    tiles = [x[v * SUBLANES:(v + 1) * SUBLANES] for v in range(ns // SUBLANES)]
    cnts = [jnp.zeros((SUBLANES, Q_BLOCK), F32) for _ in tiles]
    for jp in range(ns):
        r = x[jp:jp + 1, :]
        vj, rj = divmod(jp, SUBLANES)
        for v, xt in enumerate(tiles):
            if v < vj:
                ahead = r > xt
            elif v > vj:
                ahead = r >= xt
            else:
                ahead = (r > xt) | ((r == xt) & (sub > rj))
            cnts[v] = cnts[v] + jnp.where(ahead, 1.0, 0.0)
    for v, cnt in enumerate(cnts):
        bias_ref[v * SUBLANES:(v + 1) * SUBLANES, :] = jnp.where(cnt < float(SLC_TOPN), 0.0, NEG)

    def sel_update(st, z_refs, carry, causal):
        s0 = pl.multiple_of(st * SEL_TILE, SEL_TILE)
        vs = vst_ref[0, :, pl.ds(s0, SEL_TILE)]
        brow = bias_ref[pl.ds(pl.multiple_of(st * per_step, per_step), per_step), :]
        b2 = jnp.concatenate([brow] * 2, axis=1)[:, None, :]
        out = []
        for z_ref, (m, acc) in zip(z_refs, carry):
            z = (z_ref[...].reshape(per_step, SLC_BLOCK, nq) + b2).reshape(SEL_TILE, nq)
            if causal:
                z = jnp.where((s0 + spos0) <= tpos, z, NEG)
            mn = jnp.maximum(m, jnp.max(z, axis=0, keepdims=True))
            p = jnp.exp2(z - mn)
            out.append((mn, jnp.exp2(m - mn) * acc + _dot(vs, p.astype(BF16))))
        return tuple(out)

    def run_steps(first, count, carry, ends_causal):
        bufs = (za_refs, zb_refs)
        for i in range(count):
            if i + 1 < count or not ends_causal:
                sel_scores(first + i + 1, bufs[(i + 1) % 2])
            carry = sel_update(first + i, bufs[i % 2], carry, ends_causal and i + 1 == count)
        return carry

    init = tuple((jnp.full((1, nq), NEG, F32), jnp.zeros((V_SLOT, nq), F32)) for _ in range(npair))
    last = qb // (SEL_TILE // Q_BLOCK)
    n_loop = last // SEL_UNROLL
    carry = lax.fori_loop(0, n_loop, lambda i, c: run_steps(SEL_UNROLL * i, SEL_UNROLL, c, False), init)
    tails = [functools.partial(run_steps, SEL_UNROLL * n_loop, r + 1, ends_causal=True)
             for r in range(SEL_UNROLL)]
    carry = lax.switch(last % SEL_UNROLL, tails, carry)
    o_s = [normalised(acc) for _, acc in carry]

    gts = _sigmoid(gate_ref[0])
    pairs = []
    for hp in range(npair):
        halves = []
        for i in range(2):
            r = 3 * (2 * hp + i)
            sl = slice(i * Q_BLOCK, (i + 1) * Q_BLOCK)
            halves.append(gts[r:r + 1] * o_c[hp][:, sl] + gts[r + 1:r + 2] * o_s[hp][:, sl]
                          + gts[r + 2:r + 3] * o_w[hp][:, sl])
        pairs.append(jnp.concatenate(halves, axis=0).T)
    o_ref[0] = jnp.concatenate(pairs, axis=1).astype(o_ref.dtype)


def _nsa(q_slots, gate_t, kc_aug, vc_t, ks_aug, kw_aug, v_t, ov_t, slope_tab):
    b, t, _ = q_slots.shape
    g = NSA_GROUPS
    nqb = t // Q_BLOCK
    ncp = kc_aug.shape[2]
    ns = ov_t.shape[0]
    return pl.pallas_call(
        _nsa_kernel,
        out_shape=jax.ShapeDtypeStruct((b, t, NSA_HEADS * NSA_DH), BF16),
        grid=(b, g, nqb),
        in_specs=[
            pl.BlockSpec((1, Q_BLOCK, NSA_HG * NSA_DH), lambda i, j, s: (i, s, j)),
            pl.BlockSpec((1, GATE_ROWS, Q_BLOCK), lambda i, j, s: (i, j, s)),
            pl.BlockSpec((1, 1, ncp, LANES), lambda i, j, s: (i, j, 0, 0)),
            pl.BlockSpec((1, 1, NSA_DH, ncp), lambda i, j, s: (i, j, 0, 0)),
            pl.BlockSpec((1, t, LANES), lambda i, j, s: (i, 0, j)),
            pl.BlockSpec((1, V_SLOT, t), lambda i, j, s: (i, j, 0)),
            pl.BlockSpec((1, t, LANES), lambda i, j, s: (i, 0, j)),
            pl.BlockSpec((1, V_SLOT, t), lambda i, j, s: (i, g + j, 0)),
            pl.BlockSpec((ns, ncp), lambda i, j, s: (0, 0)),
            pl.BlockSpec((1, NSA_HG * Q_BLOCK, LANES), lambda i, j, s: (j, 0, 0)),
        ],
        out_specs=pl.BlockSpec((1, Q_BLOCK, NSA_HG * NSA_DH), lambda i, j, s: (i, s, j)),
        scratch_shapes=[pltpu.VMEM((ns, Q_BLOCK), F32)]
        + [pltpu.VMEM((SEL_TILE, 2 * Q_BLOCK), F32)] * NSA_HG,
        compiler_params=pltpu.CompilerParams(
            dimension_semantics=("parallel", "parallel", "arbitrary"), vmem_limit_bytes=VMEM_LIMIT),
        name="nsa",
    )(q_slots, gate_t, kc_aug, vc_t, ks_aug, v_t, kw_aug, v_t, ov_t, slope_tab)


def _layer_norm(z, g, b):
    mu = jnp.mean(z, axis=-1, keepdims=True)
    zc = z - mu
    var = jnp.mean(zc * zc, axis=-1, keepdims=True)
    return zc * lax.rsqrt(var + LN_EPS) * g + b


def _post_kernel(x_ref, yh_ref, yn_ref, mod_ref, wh_ref, wn_ref, w1_ref, w3_ref, w2_ref,
                 l1g_ref, l1b_ref, l2g_ref, l2b_ref, o_ref, *, alpha, chunks, row_parts):
    g1, sh2, sc2, g2 = mod_ref[0, 2], mod_ref[0, 3], mod_ref[0, 4], mod_ref[0, 5]
    tm = x_ref.shape[1]
    parts = [slice(i * tm // row_parts, (i + 1) * tm // row_parts) for i in range(row_parts)]
    mixes = [_dot(yh_ref[0, r, :], wh_ref[...]) + _dot(yn_ref[0, r, :], wn_ref[...]) for r in parts]
    x1s, ups = [], []
    for r, mix in zip(parts, mixes):
        x1 = _layer_norm(alpha * x_ref[0, r, :] + g1 * mix, l1g_ref[...], l1b_ref[...])
        h = (x1 * (1.0 + sc2) + sh2).astype(BF16)
        x1s.append(x1)
        ups.append([(_dot(h, w1_ref[:, lo:hi]), _dot(h, w3_ref[:, lo:hi])) for lo, hi in chunks])
    downs = []
    for up in ups:
        a = jnp.concatenate([(u * _sigmoid(u) * w).astype(BF16) for u, w in up], axis=1)
        downs.append(_dot(a, w2_ref[...]))
    for r, x1, down in zip(parts, x1s, downs):
        o_ref[0, r, :] = _layer_norm(alpha * x1 + g2 * down, l2g_ref[...], l2b_ref[...])


def _post(x, y_hg, y_nsa, ada4, w_h, w_n, w1, w3, w2, ln1_g, ln1_b, ln2_g, ln2_b, alpha):
    b, t, d = x.shape
    dff = w1.shape[1]
    tm = 512
    step = 3 * MXU_DIM
    chunks = tuple((lo, min(lo + step, dff)) for lo in range(0, dff, step))
    resident = lambda a: pl.BlockSpec(a.shape, lambda i, j: (0,) * a.ndim, pipeline_mode=pl.Buffered(1))
    rows = lambda a: pl.BlockSpec((1, tm, a.shape[2]), lambda i, j: (i, j, 0))
    vec = pl.BlockSpec((1, d), lambda i, j: (0, 0))
    return pl.pallas_call(
        functools.partial(_post_kernel, alpha=alpha, chunks=chunks, row_parts=2),
        out_shape=jax.ShapeDtypeStruct((b, t, d), F32),
        grid=(b, t // tm),
        in_specs=[rows(x), rows(y_hg), rows(y_nsa),
                  pl.BlockSpec((1,) + ada4.shape[1:], lambda i, j: (i, 0, 0, 0)),
                  resident(w_h), resident(w_n), resident(w1), resident(w3), resident(w2),
                  vec, vec, vec, vec],
        out_specs=rows(x),
        compiler_params=pltpu.CompilerParams(
            dimension_semantics=("parallel", "parallel"), vmem_limit_bytes=FFN_VMEM_LIMIT),
        name="post",
    )(x, y_hg, y_nsa, ada4, w_h, w_n, w1, w3, w2, ln1_g, ln1_b, ln2_g, ln2_b)


def _slope_table(group_slopes):
    g, hg = group_slopes.shape
    rest = (group_slopes * LOG2E).astype(np.float32)
    tab = np.zeros((g, hg, Q_BLOCK, LANES), jnp.bfloat16)
    for i in range(N_SPLIT):
        piece = rest.astype(jnp.bfloat16)
        rest = rest - piece.astype(np.float32)
        tab[..., NSA_DH + 2 * i] = piece[:, :, None]
        tab[..., NSA_DH + 2 * i + 1] = piece[:, :, None]
    return tab.reshape(g, hg * Q_BLOCK, LANES)


def _pad_slots(w, n_slots):
    d = w.shape[0]
    w3 = w.reshape(d, n_slots, -1)
    return jnp.pad(w3, ((0, 0), (0, 0), (0, LANES - w3.shape[2]))).reshape(d, n_slots * LANES)


def _cmp_w1_expand(w1, g):
    hid = w1.shape[1]
    w1r = w1.reshape(2, CMP_STRIDE, 1, NSA_DH, hid)
    eye = jnp.eye(g, dtype=w1.dtype).reshape(g, 1, 1, g, 1, 1)
    return (eye * w1r[None]).reshape(g, 2 * CMP_STRIDE * g * NSA_DH, hid)


def _overlap_t(n_cmp_pad, n_cmp, n_slc):
    cs = np.arange(n_cmp_pad) * CMP_STRIDE
    ss = np.arange(n_slc) * SLC_BLOCK
    ov = np.clip(np.minimum(cs[:, None] + CMP_BLOCK, ss[None] + SLC_BLOCK)
                 - np.maximum(cs[:, None], ss[None]), 0, None).astype(np.float32) / CMP_BLOCK
    ov[n_cmp:] = 0.0
    return ov.T


def kernel(x, c, w_ada, b_ada, w_in, hg_lb_logits, hg_norm_g, cmp_pos_k, cmp_w1_k, cmp_w2_k,
           cmp_pos_v, cmp_w1_v, cmp_w2_v, w_out, ln1_g, ln1_b, ffn_w1, ffn_w3, ffn_w2, ln2_g, ln2_b):
    bsz, t, d = x.shape
    depth = w_ada.shape[0]
    g, hg, dh = NSA_GROUPS, NSA_HG, NSA_DH
    hgw = HG_HEADS * HG_D
    nsw = NSA_HEADS * dh
    kvw = g * dh
    n_mod = 6
    alpha = (2.0 * depth) ** 0.25
    scale = dh ** -0.5
    n_grp = t // CMP_STRIDE
    n_cmp = (t - CMP_BLOCK) // CMP_STRIDE + 1
    n_slc = t // SLC_BLOCK
    assert t % SEL_TILE == 0 and t >= WINDOW + Q_BLOCK and 3 * hg <= GATE_ROWS and hg % 2 == 0

    slopes = (2.0 ** (-8.0 * (np.arange(NSA_HEADS) + 1) / NSA_HEADS)).reshape(g, hg)
    slope_tab = jnp.asarray(_slope_table(slopes), BF16)
    ov_t = jnp.asarray(_overlap_t(n_grp, n_cmp, n_slc), BF16)

    c_pad = jnp.pad(c, ((0, 8 - bsz), (0, 0)))
    for l in range(depth):
        ada = _ada(c_pad, w_ada[l], b_ada[l][None])
        ada3 = ada[:bsz].reshape(bsz * n_mod, 1, d)

        w = w_in[l]
        o = 4 * hgw
        w_q = w[:, o:o + nsw]
        w_kv = [w[:, o + nsw + i * kvw: o + nsw + (i + 1) * kvw] for i in range(6)]
        w_g = w[:, o + nsw + 6 * kvw: o + nsw + 6 * kvw + 3 * NSA_HEADS]
        w_main = jnp.concatenate(
            [w[:, :o], w_q * (scale * LOG2E), _pad_slots(w_kv[2], g),
             _pad_slots(w_kv[4], g), w_kv[0], w_kv[1]], axis=1).astype(BF16)
        w_gt = jnp.pad(w_g.reshape(d, g, 3 * hg), ((0, 0), (0, 0), (0, GATE_ROWS - 3 * hg))).reshape(d, -1)
        v_slots = lambda a: jnp.pad(a.reshape(d, g, dh), ((0, 0), (0, 0), (0, V_SLOT - dh))).reshape(d, -1)
        w_t = jnp.concatenate([v_slots(w_kv[3]), v_slots(w_kv[5]), w_gt], axis=1).T.astype(BF16)
        widths = (o, nsw, g * LANES, kvw, 2 * g * V_SLOT, g * GATE_ROWS)
        hg_in, q_slots, ks_aug, kw_aug, kc, vc, v_t, gate_t = _inproj(x, ada3, w_main, w_t, n_mod, widths)

        y_hg = _hgrn(hg_in, hg_lb_logits, hg_norm_g[l][None], l)

        grp = lambda a: a.reshape(bsz, n_grp, CMP_STRIDE * kvw)
        pos_e = lambda p: jnp.broadcast_to(
            p.reshape(2, CMP_STRIDE, 1, dh), (2, CMP_STRIDE, g, dh)).reshape(1, -1)
        w2k = jnp.pad(cmp_w2_k[l], ((0, 0), (0, LANES - dh))).astype(BF16)
        kc_aug = _compress(grp(kc), pos_e(cmp_pos_k[l]), _cmp_w1_expand(cmp_w1_k[l], g).astype(BF16),
                           w2k, True)
        vc_t = _compress(grp(vc), pos_e(cmp_pos_v[l]), _cmp_w1_expand(cmp_w1_v[l], g).astype(BF16),
                         cmp_w2_v[l].T.astype(BF16), False)

        y_nsa = _nsa(q_slots, gate_t, kc_aug, vc_t, ks_aug, kw_aug, v_t, ov_t, slope_tab)

        w_o = w_out[l].astype(BF16)
        x = _post(x, y_hg, y_nsa, ada3.reshape(bsz, n_mod, 1, d), w_o[:hgw], w_o[hgw:],
                  ffn_w1[l].astype(BF16), ffn_w3[l].astype(BF16), ffn_w2[l].astype(BF16),
                  ln1_g[l][None], ln1_b[l][None], ln2_g[l][None], ln2_b[l][None], alpha)
    return x
```

```python
import functools

import numpy as np
import jax
import jax.numpy as jnp
from jax import lax
from jax.experimental import pallas as pl
from jax.experimental.pallas import tpu as pltpu

F32 = jnp.float32
BF16 = jnp.bfloat16

HG_HEADS = 4
HG_D = 128
NSA_HEADS = 8
NSA_GROUPS = 2
NSA_HG = NSA_HEADS // NSA_GROUPS
NSA_DH = 64
CMP_BLOCK = 32
CMP_STRIDE = 16
CMP_HIDDEN = 256
SLC_BLOCK = 64
SLC_TOPN = 16
WINDOW = 512
Q_BLOCK = 256
NEG = -1e30
FORCE = 1e4
LN_EPS = 1e-5
RMS_EPS = 1e-6

LANES = 128
SUBLANES = 8
HG_BLK = 32
HG_SAFE_LOG2 = 100.0
POS_SPLIT = 256
GATE_ROWS = 16
SEL_TILE = 512
SEL_UNROLL = 8
V_SLOT = 80
N_SPLIT = 3
LOG2E = 1.4426950408889634
MXU_DIM = 256
VMEM_LIMIT = 48 * 1024 * 1024
FFN_VMEM_LIMIT = 56 * 1024 * 1024


def _sigmoid(x):
    return 1.0 / (1.0 + jnp.exp(-x))


def _dot(a, b):
    return jnp.dot(a, b, preferred_element_type=F32)


def _dot_nt(a, b):
    return lax.dot_general(a, b, (((1,), (1,)), ((), ())), preferred_element_type=F32)


def _pos_columns(pos, col):
    c = col - NSA_DH
    lo = pos % POS_SPLIT
    val = jnp.where(c % 2 == 0, lo, pos - lo)
    return jnp.where((c >= 0) & (c < 2 * N_SPLIT), val, 0).astype(F32)


def _split2(x):
    hi = x.astype(BF16)
    return hi, (x - hi.astype(F32)).astype(BF16)


def _ada_kernel(c_ref, w_ref, b_ref, o_ref):
    c = c_ref[...]
    s = (c * _sigmoid(c)).astype(BF16)
    o_ref[...] = _dot(s, w_ref[...].astype(BF16)) + b_ref[...]


def _ada(c_pad, w, b):
    rows, d = c_pad.shape
    n = w.shape[1]
    tn = 1024
    return pl.pallas_call(
        _ada_kernel,
        out_shape=jax.ShapeDtypeStruct((rows, n), F32),
        grid=(n // tn,),
        in_specs=[
            pl.BlockSpec((rows, d), lambda j: (0, 0)),
            pl.BlockSpec((d, tn), lambda j: (0, j)),
            pl.BlockSpec((1, tn), lambda j: (0, j)),
        ],
        out_specs=pl.BlockSpec((rows, tn), lambda j: (0, j)),
        compiler_params=pltpu.CompilerParams(
            dimension_semantics=("parallel",), vmem_limit_bytes=VMEM_LIMIT),
        name="ada",
    )(c_pad, w, b)


def _inproj_kernel(x_ref, sc_ref, sh_ref, wm_ref, wt_ref,
                   hg_ref, q_ref, ks_ref, kw_ref, kc_ref, vc_ref, vt_ref, gt_ref, *, tm):
    h = (x_ref[0] * (1.0 + sc_ref[0]) + sh_ref[0]).astype(BF16)
    y = _dot(h, wm_ref[...])
    n_hg = hg_ref.shape[2]
    n_q = q_ref.shape[2]
    n_k = ks_ref.shape[2]
    n_c = kc_ref.shape[2]
    o = 0
    hg_ref[0] = y[:, o:o + n_hg]
    o += n_hg
    q_ref[0] = y[:, o:o + n_q].astype(BF16)
    o += n_q
    row = pl.program_id(1) * tm + lax.broadcasted_iota(jnp.int32, (tm, n_k), 0)
    pc = _pos_columns(row, lax.broadcasted_iota(jnp.int32, (tm, n_k), 1) % LANES)
    ks_ref[0] = (y[:, o:o + n_k] + pc).astype(BF16)
    o += n_k
    kw_ref[0] = (y[:, o:o + n_k] + pc).astype(BF16)
    o += n_k
    kc_ref[0] = y[:, o:o + n_c]
    o += n_c
    vc_ref[0] = y[:, o:o + n_c]
    yt = _dot_nt(wt_ref[...], h)
    n_vt = vt_ref.shape[1]
    ri = lax.broadcasted_iota(jnp.int32, (n_vt, tm), 0) % V_SLOT
    vt_ref[0] = (yt[:n_vt] + jnp.where(ri == NSA_DH, 1.0, 0.0)).astype(BF16)
    gt_ref[0] = yt[n_vt:]


def _inproj(x, ada3, w_main, w_t, n_mod, widths):
    b, t, d = x.shape
    n_hg, n_q, n_k, n_c, n_vt, n_gt = widths
    tm = 512
    row_spec = lambda n: pl.BlockSpec((1, tm, n), lambda i, j: (i, j, 0))
    col_spec = lambda n: pl.BlockSpec((1, n, tm), lambda i, j: (i, 0, j))
    sds = jax.ShapeDtypeStruct
    return pl.pallas_call(
        functools.partial(_inproj_kernel, tm=tm),
        out_shape=(sds((b, t, n_hg), F32), sds((b, t, n_q), BF16), sds((b, t, n_k), BF16),
                   sds((b, t, n_k), BF16), sds((b, t, n_c), F32), sds((b, t, n_c), F32),
                   sds((b, n_vt, t), BF16), sds((b, n_gt, t), F32)),
        grid=(b, t // tm),
        in_specs=[
            pl.BlockSpec((1, tm, d), lambda i, j: (i, j, 0)),
            pl.BlockSpec((1, 1, d), lambda i, j: (i * n_mod + 1, 0, 0)),
            pl.BlockSpec((1, 1, d), lambda i, j: (i * n_mod, 0, 0)),
            pl.BlockSpec(w_main.shape, lambda i, j: (0, 0), pipeline_mode=pl.Buffered(1)),
            pl.BlockSpec(w_t.shape, lambda i, j: (0, 0), pipeline_mode=pl.Buffered(1)),
        ],
        out_specs=(row_spec(n_hg), row_spec(n_q), row_spec(n_k), row_spec(n_k), row_spec(n_c),
                   row_spec(n_c), col_spec(n_vt), col_spec(n_gt)),
        compiler_params=pltpu.CompilerParams(
            dimension_semantics=("parallel", "parallel"), vmem_limit_bytes=VMEM_LIMIT),
        name="inproj",
    )(x, ada3, ada3, w_main, w_t)


def _hgrn_kernel(q_ref, f_ref, i_ref, g_ref, lbl_ref, ng_ref, e_ref, o_ref, st_ref, *, tc, layer):
    blk = HG_BLK
    per_grp = LANES // blk
    half = blk // 2

    @pl.when(pl.program_id(2) == 0)
    def _():
        st_ref[...] = jnp.zeros_like(st_ref)

    lbl = lbl_ref[...]
    e = jnp.exp(lbl - jnp.max(lbl, axis=0, keepdims=True))
    lb = jnp.sum(e[: layer + 1], axis=0, keepdims=True) / jnp.sum(e, axis=0, keepdims=True)

    r_i = lax.broadcasted_iota(jnp.int32, (LANES, LANES), 0)
    c_i = lax.broadcasted_iota(jnp.int32, (LANES, LANES), 1)
    same_blk = (r_i // blk) == (c_i // blk)
    causal_blk = same_blk & (c_i <= r_i)
    tri = jnp.where(causal_blk, 1.0, 0.0).astype(BF16)
    row_blk = r_i // blk
    nh = per_grp * half
    r_h = lax.broadcasted_iota(jnp.int32, (nh, LANES), 0)
    same_half = (r_h // half) == (lax.broadcasted_iota(jnp.int32, (nh, LANES), 1) // blk)
    row_in_half = r_h % half
    blocked = lambda a: a.reshape(per_grp, blk, LANES)

    def gates(g):
        sl = slice(g * LANES, (g + 1) * LANES)
        q = q_ref[0, sl, :]
        qs = q * _sigmoid(q)
        f = lb + (1.0 - lb) * _sigmoid(f_ref[0, sl, :])
        hi, lo = _split2(jnp.log2(f))
        b = _dot(tri, hi) + _dot(tri, lo)
        return qs, 1.0 - f, i_ref[0, sl, :], b

    def increments(qs, k, v, b):
        b3 = blocked(b)
        bl3 = b3[:, blk - 1:blk, :]
        qd = (qs * jnp.exp2(b)).astype(BF16)
        kd = (blocked(k) * jnp.exp2(bl3 - b3)).reshape(LANES, LANES)
        kexp = jnp.concatenate(
            [jnp.where(row_blk == j, kd, 0.0) for j in range(per_grp)], axis=1).astype(BF16)
        return qd, jnp.exp2(bl3), _dot(v.T.astype(BF16), kexp)

    def recur(st, qd, dl, ut):
        ois = []
        for j in range(per_grp):
            ois.append(_dot_nt(qd[j * blk:(j + 1) * blk], st.astype(BF16)))
            st = dl[j] * st + ut[:, j * LANES:(j + 1) * LANES]
        return st, jnp.concatenate(ois, axis=0)

    def pairwise(qs, k, b):
        c = b - jnp.log2(k)
        tiles = lambda a, off: jnp.concatenate(
            [a[j * blk + off:j * blk + off + half] for j in range(per_grp)], axis=0)
        q_lo, q_hi, b_lo, b_hi = tiles(qs, 0), tiles(qs, half), tiles(b, 0), tiles(b, half)
        p_lo, p_hi = [], []
        for s in range(blk):
            cs = jnp.concatenate([jnp.broadcast_to(c[j * blk + s:j * blk + s + 1], (half, LANES))
                                  for j in range(per_grp)], axis=0)
            if s < half:
                p_lo.append(jnp.where(row_in_half >= s, q_lo * jnp.exp2(b_lo - cs), 0.0).astype(BF16))
                p_hi.append((q_hi * jnp.exp2(b_hi - cs)).astype(BF16))
            else:
                p_hi.append(jnp.where(row_in_half >= s - half, q_hi * jnp.exp2(b_hi - cs), 0.0).astype(BF16))
        return jnp.concatenate(p_lo, axis=1), jnp.concatenate(p_hi, axis=1)

    def emit(g, o):
        o = o * lax.rsqrt(jnp.mean(o * o, axis=-1, keepdims=True) + RMS_EPS)
        sl = slice(g * LANES, (g + 1) * LANES)
        o_ref[0, sl, :] = (o * ng_ref[...] * _sigmoid(g_ref[0, sl, :])).astype(o_ref.dtype)

    ngrp = tc // LANES
    gs = [gates(g) for g in range(ngrp)]
    incs = [increments(*gt) for gt in gs]
    st = st_ref[...]
    o_inters = []
    for inc in incs:
        st, oi = recur(st, *inc)
        o_inters.append(oi)
    st_ref[...] = st

    b_min = functools.reduce(jnp.minimum, [jnp.min(gt[3]) for gt in gs])

    @pl.when(b_min >= -HG_SAFE_LOG2)
    def _():
        a_s = [_dot_nt(qd, (k * jnp.exp2(-b)).astype(BF16))
               for (_, k, _, b), (qd, _, _) in zip(gs, incs)]
        o_s = [_dot(jnp.where(causal_blk, a, 0.0).astype(BF16), gt[2].astype(BF16)) for a, gt in zip(a_s, gs)]
        for g, o in enumerate(o_s):
            emit(g, o + o_inters[g])

    @pl.when(b_min < -HG_SAFE_LOG2)
    def _():
        pws = [pairwise(qs, k, b) for qs, k, _, b in gs]
        a_lo = _dot(jnp.concatenate([p[0] for p in pws], axis=0), e_ref[:half * LANES, :])
        a_hi = _dot(jnp.concatenate([p[1] for p in pws], axis=0), e_ref[...])
        for g in range(ngrp):
            rows = slice(g * nh, (g + 1) * nh)
            vb = gs[g][2].astype(BF16)
            o_lo = _dot(jnp.where(same_half, a_lo[rows], 0.0).astype(BF16), vb)
            o_hi = _dot(jnp.where(same_half, a_hi[rows], 0.0).astype(BF16), vb)
            emit(g, jnp.concatenate([x[j * half:(j + 1) * half] for j in range(per_grp) for x in (o_lo, o_hi)],
                                    axis=0) + o_inters[g])


def _hgrn(proj, lb_logits, norm_g, layer):
    b, t, _ = proj.shape
    tc = 1024
    h = HG_HEADS
    nslots = lb_logits.shape[0]
    col = lambda seg: (lambda i, j, s: (i, s, seg * h + j))
    fold = np.zeros((HG_BLK, LANES, LANES), np.float32)
    for s in range(HG_BLK):
        fold[s, :, s::HG_BLK] = 1.0
    fold = jnp.asarray(fold.reshape(HG_BLK * LANES, LANES), BF16)
    return pl.pallas_call(
        functools.partial(_hgrn_kernel, tc=tc, layer=layer),
        out_shape=jax.ShapeDtypeStruct((b, t, h * HG_D), BF16),
        grid=(b, h, t // tc),
        in_specs=[
            pl.BlockSpec((1, tc, HG_D), col(0)),
            pl.BlockSpec((1, tc, HG_D), col(1)),
            pl.BlockSpec((1, tc, HG_D), col(2)),
            pl.BlockSpec((1, tc, HG_D), col(3)),
            pl.BlockSpec((nslots, HG_D), lambda i, j, s: (0, j)),
            pl.BlockSpec((1, HG_D), lambda i, j, s: (0, j)),
            pl.BlockSpec(fold.shape, lambda i, j, s: (0, 0)),
        ],
        out_specs=pl.BlockSpec((1, tc, HG_D), lambda i, j, s: (i, s, j)),
        scratch_shapes=[pltpu.VMEM((HG_D, HG_D), F32)],
        compiler_params=pltpu.CompilerParams(
            dimension_semantics=("parallel", "parallel", "arbitrary"), vmem_limit_bytes=VMEM_LIMIT),
        name="hgrn",
    )(proj, proj, proj, proj, lb_logits, norm_g, fold)


def _cmp_kernel(h_ref, pos_ref, w1_ref, w2_ref, o_ref, *, key_side):
    h = h_ref[0]
    half = h.shape[1]
    pos = pos_ref[...]
    p0 = _dot((h + pos[:, :half]).astype(BF16), w1_ref[0, :half])
    p1 = _dot((h + pos[:, half:]).astype(BF16), w1_ref[0, half:])
    ng = h.shape[0]
    pre = p0 + pltpu.roll(p1, ng - 1, 0)
    gl = 0.5 * pre * (1.0 + jnp.tanh(np.sqrt(2.0 / np.pi) * (pre + 0.044715 * (pre * pre * pre))))
    gl = gl.astype(BF16)
    if key_side:
        cend = lax.broadcasted_iota(jnp.int32, (ng, LANES), 0) * CMP_STRIDE + (CMP_BLOCK - 1)
        pc = _pos_columns(cend, lax.broadcasted_iota(jnp.int32, (ng, LANES), 1))
        o_ref[0, 0] = (_dot(gl, w2_ref[...]) + pc).astype(BF16)
    else:
        vt = _dot_nt(w2_ref[...], gl)
        ones_row = lax.broadcasted_iota(jnp.int32, vt.shape, 0) == NSA_DH
        o_ref[0, 0] = jnp.where(ones_row, 1.0, vt).astype(BF16)


def _compress(hgrp, pos, w1, w2, key_side):
    b, ng, half = hgrp.shape
    g = w1.shape[0]
    out_tail = (ng, LANES) if key_side else (w2.shape[0], ng)
    return pl.pallas_call(
        functools.partial(_cmp_kernel, key_side=key_side),
        out_shape=jax.ShapeDtypeStruct((b, g) + out_tail, BF16),
        grid=(b, g),
        in_specs=[
            pl.BlockSpec((1, ng, half), lambda i, j: (i, 0, 0)),
            pl.BlockSpec(pos.shape, lambda i, j: (0, 0)),
            pl.BlockSpec((1,) + w1.shape[1:], lambda i, j: (j, 0, 0)),
            pl.BlockSpec(w2.shape, lambda i, j: (0, 0)),
        ],
        out_specs=pl.BlockSpec((1, 1) + out_tail, lambda i, j: (i, j, 0, 0)),
        compiler_params=pltpu.CompilerParams(
            dimension_semantics=("parallel", "parallel"), vmem_limit_bytes=VMEM_LIMIT),
        name="cmp_k" if key_side else "cmp_v",
    )(hgrp, pos, w1, w2)


def _nsa_kernel(q_ref, gate_ref, kc_ref, vct_ref, ks_ref, vst_ref, kw_ref, vwt_ref, ovt_ref, slope_ref,
                o_ref, bias_ref, *z_scratch):
    qb = pl.program_id(2)
    t0 = qb * Q_BLOCK
    npair = NSA_HG // 2
    za_refs, zb_refs = z_scratch[:npair], z_scratch[npair:]
    nq = 2 * Q_BLOCK
    qblk = q_ref[0].astype(F32)
    low = lax.broadcasted_iota(jnp.int32, (Q_BLOCK, LANES), 1) < NSA_DH
    qas = []
    for hp in range(npair):
        pair = qblk[:, hp * LANES:(hp + 1) * LANES]
        heads = [jnp.where(low, x, 0.0) for x in (pair, pltpu.roll(pair, NSA_DH, 1))]
        qas.append(jnp.concatenate(heads, axis=0).astype(BF16) + slope_ref[0, hp * nq:(hp + 1) * nq])
    tpos = t0 + (lax.broadcasted_iota(jnp.int32, (1, nq), 1) % Q_BLOCK)

    def normalised(acc):
        return acc[:NSA_DH] * (1.0 / acc[NSA_DH:NSA_DH + 1])

    per_step = SEL_TILE // SLC_BLOCK
    spos0 = lax.broadcasted_iota(jnp.int32, (SEL_TILE, 1), 0)
    n_slab = WINDOW // Q_BLOCK + 1
    slab_kt = [qb - (n_slab - 1) + i for i in range(n_slab)]
    slab_at = [pl.multiple_of(jnp.maximum(kt, 0) * Q_BLOCK, Q_BLOCK) for kt in slab_kt]
    kw = jnp.concatenate([kw_ref[0, pl.ds(s, Q_BLOCK), :] for s in slab_at], axis=0)

    def sel_scores(st, z_refs):
        ks = ks_ref[0, pl.ds(pl.multiple_of(st * SEL_TILE, SEL_TILE), SEL_TILE), :]
        for z_ref, qa in zip(z_refs, qas):
            z_ref[...] = _dot_nt(ks, qa)

    zcs = [_dot_nt(kc_ref[0, 0], qa) for qa in qas]
    zws = [_dot_nt(kw, qa) for qa in qas]

    ncp = kc_ref.shape[2]
    cend = lax.broadcasted_iota(jnp.int32, (ncp, 1), 0) * CMP_STRIDE + (CMP_BLOCK - 1)
    valid = cend <= tpos
    o_c, imp = [], None
    for zc in zcs:
        zm = jnp.where(valid, zc, NEG)
        mc = jnp.max(zm, axis=0, keepdims=True)
        eb = jnp.exp2(zm - mc).astype(BF16)
        acc = _dot(vct_ref[0, 0], eb)
        inv = jnp.where(mc > 0.5 * NEG, 1.0 / acc[NSA_DH:NSA_DH + 1], 0.0)
        o_c.append(acc[:NSA_DH] * inv)
        for i in range(2):
            sl = slice(i * Q_BLOCK, (i + 1) * Q_BLOCK)
            ih = _dot(ovt_ref[...], eb[:, sl]) * inv[:, sl]
            imp = ih if imp is None else imp + ih

    sel_scores(0, za_refs)

    ns = imp.shape[0]
    j_i = lax.broadcasted_iota(jnp.int32, (ns, 1), 0)
    t_abs = t0 + lax.broadcasted_iota(jnp.int32, (1, Q_BLOCK), 1)
    cur = t_abs // SLC_BLOCK
    forced = (j_i == 0) | (j_i == cur) | (j_i == cur - 1)
    x = jnp.where(forced, FORCE, imp)
    x = jnp.where(j_i * SLC_BLOCK <= t_abs, x, NEG)

    vw = jnp.concatenate([vwt_ref[0, :, pl.ds(s, Q_BLOCK)] for s in slab_at], axis=1)
    r_k = lax.broadcasted_iota(jnp.int32, (Q_BLOCK, nq), 0)
    r_q = lax.broadcasted_iota(jnp.int32, (Q_BLOCK, nq), 1) % Q_BLOCK
    o_w = []
    for zw in zws:
        slabs = []
        for i, kt in enumerate(slab_kt):
            zi = zw[i * Q_BLOCK:(i + 1) * Q_BLOCK]
            if i == 0:
                zi = jnp.where(r_k > r_q, zi, NEG)
            if i == n_slab - 1:
                zi = jnp.where(r_k <= r_q, zi, NEG)
            else:
                zi = zi + jnp.where(kt < 0, NEG, 0.0)
            slabs.append(zi)
        zw = jnp.concatenate(slabs, axis=0)
        pw = jnp.exp2(zw - jnp.max(zw, axis=0, keepdims=True))
        o_w.append(normalised(_dot(vw, pw.astype(BF16))))

    sub = lax.broadcasted_iota(jnp.int32, (SUBLANES, Q_BLOCK), 0)
    tiles = [x[v * SUBLANES:(v + 1) * SUBLANES] for v in range(ns // SUBLANES)]
    cnts = [jnp.zeros((SUBLANES, Q_BLOCK), F32) for _ in tiles]
    for jp in range(ns):
        r = x[jp:jp + 1, :]
        vj, rj = divmod(jp, SUBLANES)
        for v, xt in enumerate(tiles):
            if v < vj:
                ahead = r > xt
            elif v > vj:
                ahead = r >= xt
            else:
                ahead = (r > xt) | ((r == xt) & (sub > rj))
            cnts[v] = cnts[v] + jnp.where(ahead, 1.0, 0.0)
    for v, cnt in enumerate(cnts):
        bias_ref[v * SUBLANES:(v + 1) * SUBLANES, :] = jnp.where(cnt < float(SLC_TOPN), 0.0, NEG)

    def sel_update(st, z_refs, carry, causal):
        s0 = pl.multiple_of(st * SEL_TILE, SEL_TILE)
        vs = vst_ref[0, :, pl.ds(s0, SEL_TILE)]
        brow = bias_ref[pl.ds(pl.multiple_of(st * per_step, per_step), per_step), :]
        b2 = jnp.concatenate([brow] * 2, axis=1)[:, None, :]
        out = []
        for z_ref, (m, acc) in zip(z_refs, carry):
            z = (z_ref[...].reshape(per_step, SLC_BLOCK, nq) + b2).reshape(SEL_TILE, nq)
            if causal:
                z = jnp.where((s0 + spos0) <= tpos, z, NEG)
            mn = jnp.maximum(m, jnp.max(z, axis=0, keepdims=True))
            p = jnp.exp2(z - mn)
            out.append((mn, jnp.exp2(m - mn) * acc + _dot(vs, p.astype(BF16))))
        return tuple(out)

    def run_steps(first, count, carry, ends_causal):
        bufs = (za_refs, zb_refs)
        for i in range(count):
            if i + 1 < count or not ends_causal:
                sel_scores(first + i + 1, bufs[(i + 1) % 2])
            carry = sel_update(first + i, bufs[i % 2], carry, ends_causal and i + 1 == count)
        return carry

    init = tuple((jnp.full((1, nq), NEG, F32), jnp.zeros((V_SLOT, nq), F32)) for _ in range(npair))
    last = qb // (SEL_TILE // Q_BLOCK)
    n_loop = last // SEL_UNROLL
    carry = lax.fori_loop(0, n_loop, lambda i, c: run_steps(SEL_UNROLL * i, SEL_UNROLL, c, False), init)
    tails = [functools.partial(run_steps, SEL_UNROLL * n_loop, r + 1, ends_causal=True)
             for r in range(SEL_UNROLL)]
    carry = lax.switch(last % SEL_UNROLL, tails, carry)
    o_s = [normalised(acc) for _, acc in carry]

    gts = _sigmoid(gate_ref[0])
    pairs = []
    for hp in range(npair):
        halves = []
        for i in range(2):
            r = 3 * (2 * hp + i)
            sl = slice(i * Q_BLOCK, (i + 1) * Q_BLOCK)
            halves.append(gts[r:r + 1] * o_c[hp][:, sl] + gts[r + 1:r + 2] * o_s[hp][:, sl]
                          + gts[r + 2:r + 3] * o_w[hp][:, sl])
        pairs.append(jnp.concatenate(halves, axis=0).T)
    o_ref[0] = jnp.concatenate(pairs, axis=1).astype(o_ref.dtype)


def _nsa(q_slots, gate_t, kc_aug, vc_t, ks_aug, kw_aug, v_t, ov_t, slope_tab):
    b, t, _ = q_slots.shape
    g = NSA_GROUPS
    nqb = t // Q_BLOCK
    ncp = kc_aug.shape[2]
    ns = ov_t.shape[0]
    return pl.pallas_call(
        _nsa_kernel,
        out_shape=jax.ShapeDtypeStruct((b, t, NSA_HEADS * NSA_DH), BF16),
        grid=(b, g, nqb),
        in_specs=[
            pl.BlockSpec((1, Q_BLOCK, NSA_HG * NSA_DH), lambda i, j, s: (i, s, j)),
            pl.BlockSpec((1, GATE_ROWS, Q_BLOCK), lambda i, j, s: (i, j, s)),
            pl.BlockSpec((1, 1, ncp, LANES), lambda i, j, s: (i, j, 0, 0)),
            pl.BlockSpec((1, 1, V_SLOT, ncp), lambda i, j, s: (i, j, 0, 0)),
            pl.BlockSpec((1, t, LANES), lambda i, j, s: (i, 0, j)),
            pl.BlockSpec((1, V_SLOT, t), lambda i, j, s: (i, j, 0)),
            pl.BlockSpec((1, t, LANES), lambda i, j, s: (i, 0, j)),
            pl.BlockSpec((1, V_SLOT, t), lambda i, j, s: (i, g + j, 0)),
            pl.BlockSpec((ns, ncp), lambda i, j, s: (0, 0)),
            pl.BlockSpec((1, NSA_HG * Q_BLOCK, LANES), lambda i, j, s: (j, 0, 0)),
        ],
        out_specs=pl.BlockSpec((1, Q_BLOCK, NSA_HG * NSA_DH), lambda i, j, s: (i, s, j)),
        scratch_shapes=[pltpu.VMEM((ns, Q_BLOCK), F32)]
        + [pltpu.VMEM((SEL_TILE, 2 * Q_BLOCK), F32)] * NSA_HG,
        compiler_params=pltpu.CompilerParams(
            dimension_semantics=("parallel", "parallel", "arbitrary"), vmem_limit_bytes=VMEM_LIMIT),
        name="nsa",
    )(q_slots, gate_t, kc_aug, vc_t, ks_aug, v_t, kw_aug, v_t, ov_t, slope_tab)


def _layer_norm(z, g, b):
    mu = jnp.mean(z, axis=-1, keepdims=True)
    zc = z - mu
    var = jnp.mean(zc * zc, axis=-1, keepdims=True)
    return zc * lax.rsqrt(var + LN_EPS) * g + b


def _post_kernel(x_ref, yh_ref, yn_ref, mod_ref, wh_ref, wn_ref, w1_ref, w3_ref, w2_ref,
                 l1g_ref, l1b_ref, l2g_ref, l2b_ref, o_ref, *, alpha, chunks, row_parts):
    g1, sh2, sc2, g2 = mod_ref[0, 2], mod_ref[0, 3], mod_ref[0, 4], mod_ref[0, 5]
    tm = x_ref.shape[1]
    parts = [slice(i * tm // row_parts, (i + 1) * tm // row_parts) for i in range(row_parts)]
    mixes = [_dot(yh_ref[0, r, :], wh_ref[...]) + _dot(yn_ref[0, r, :], wn_ref[...]) for r in parts]
    x1s, ups = [], []
    for r, mix in zip(parts, mixes):
        x1 = _layer_norm(alpha * x_ref[0, r, :] + g1 * mix, l1g_ref[...], l1b_ref[...])
        h = (x1 * (1.0 + sc2) + sh2).astype(BF16)
        x1s.append(x1)
        ups.append([(_dot(h, w1_ref[:, lo:hi]), _dot(h, w3_ref[:, lo:hi])) for lo, hi in chunks])
    downs = []
    for up in ups:
        a = jnp.concatenate([(u * _sigmoid(u) * w).astype(BF16) for u, w in up], axis=1)
        downs.append(_dot(a, w2_ref[...]))
    for r, x1, down in zip(parts, x1s, downs):
        o_ref[0, r, :] = _layer_norm(alpha * x1 + g2 * down, l2g_ref[...], l2b_ref[...])


def _post(x, y_hg, y_nsa, ada4, w_h, w_n, w1, w3, w2, ln1_g, ln1_b, ln2_g, ln2_b, alpha):
    b, t, d = x.shape
    dff = w1.shape[1]
    tm = 1024
    step = 3 * MXU_DIM
    chunks = tuple((lo, min(lo + step, dff)) for lo in range(0, dff, step))
    resident = lambda a: pl.BlockSpec(a.shape, lambda i, j: (0,) * a.ndim, pipeline_mode=pl.Buffered(1))
    rows = lambda a: pl.BlockSpec((1, tm, a.shape[2]), lambda i, j: (i, j, 0))
    vec = pl.BlockSpec((1, d), lambda i, j: (0, 0))
    return pl.pallas_call(
        functools.partial(_post_kernel, alpha=alpha, chunks=chunks, row_parts=4),
        out_shape=jax.ShapeDtypeStruct((b, t, d), F32),
        grid=(b, t // tm),
        in_specs=[rows(x), rows(y_hg), rows(y_nsa),
                  pl.BlockSpec((1,) + ada4.shape[1:], lambda i, j: (i, 0, 0, 0)),
                  resident(w_h), resident(w_n), resident(w1), resident(w3), resident(w2),
                  vec, vec, vec, vec],
        out_specs=rows(x),
        compiler_params=pltpu.CompilerParams(
            dimension_semantics=("parallel", "parallel"), vmem_limit_bytes=FFN_VMEM_LIMIT),
        name="post",
    )(x, y_hg, y_nsa, ada4, w_h, w_n, w1, w3, w2, ln1_g, ln1_b, ln2_g, ln2_b)


def _slope_table(group_slopes):
    g, hg = group_slopes.shape
    rest = (group_slopes * LOG2E).astype(np.float32)
    tab = np.zeros((g, hg, Q_BLOCK, LANES), jnp.bfloat16)
    for i in range(N_SPLIT):
        piece = rest.astype(jnp.bfloat16)
        rest = rest - piece.astype(np.float32)
        tab[..., NSA_DH + 2 * i] = piece[:, :, None]
        tab[..., NSA_DH + 2 * i + 1] = piece[:, :, None]
    return tab.reshape(g, hg * Q_BLOCK, LANES)


def _pad_slots(w, n_slots):
    d = w.shape[0]
    w3 = w.reshape(d, n_slots, -1)
    return jnp.pad(w3, ((0, 0), (0, 0), (0, LANES - w3.shape[2]))).reshape(d, n_slots * LANES)


def _cmp_w1_expand(w1, g):
    hid = w1.shape[1]
    w1r = w1.reshape(2, CMP_STRIDE, 1, NSA_DH, hid)
    eye = jnp.eye(g, dtype=w1.dtype).reshape(g, 1, 1, g, 1, 1)
    return (eye * w1r[None]).reshape(g, 2 * CMP_STRIDE * g * NSA_DH, hid)


def _overlap_t(n_cmp_pad, n_cmp, n_slc):
    cs = np.arange(n_cmp_pad) * CMP_STRIDE
    ss = np.arange(n_slc) * SLC_BLOCK
    ov = np.clip(np.minimum(cs[:, None] + CMP_BLOCK, ss[None] + SLC_BLOCK)
                 - np.maximum(cs[:, None], ss[None]), 0, None).astype(np.float32) / CMP_BLOCK
    ov[n_cmp:] = 0.0
    return ov.T


def kernel(x, c, w_ada, b_ada, w_in, hg_lb_logits, hg_norm_g, cmp_pos_k, cmp_w1_k, cmp_w2_k,
           cmp_pos_v, cmp_w1_v, cmp_w2_v, w_out, ln1_g, ln1_b, ffn_w1, ffn_w3, ffn_w2, ln2_g, ln2_b):
    bsz, t, d = x.shape
    depth = w_ada.shape[0]
    g, hg, dh = NSA_GROUPS, NSA_HG, NSA_DH
    hgw = HG_HEADS * HG_D
    nsw = NSA_HEADS * dh
    kvw = g * dh
    n_mod = 6
    alpha = (2.0 * depth) ** 0.25
    scale = dh ** -0.5
    n_grp = t // CMP_STRIDE
    n_cmp = (t - CMP_BLOCK) // CMP_STRIDE + 1
    n_slc = t // SLC_BLOCK
    assert t % SEL_TILE == 0 and t >= WINDOW + Q_BLOCK and 3 * hg <= GATE_ROWS and hg % 2 == 0

    slopes = (2.0 ** (-8.0 * (np.arange(NSA_HEADS) + 1) / NSA_HEADS)).reshape(g, hg)
    slope_tab = jnp.asarray(_slope_table(slopes), BF16)
    ov_t = jnp.asarray(_overlap_t(n_grp, n_cmp, n_slc), BF16)

    c_pad = jnp.pad(c, ((0, 8 - bsz), (0, 0)))
    for l in range(depth):
        ada = _ada(c_pad, w_ada[l], b_ada[l][None])
        ada3 = ada[:bsz].reshape(bsz * n_mod, 1, d)

        w = w_in[l]
        o = 4 * hgw
        w_q = w[:, o:o + nsw]
        w_kv = [w[:, o + nsw + i * kvw: o + nsw + (i + 1) * kvw] for i in range(6)]
        w_g = w[:, o + nsw + 6 * kvw: o + nsw + 6 * kvw + 3 * NSA_HEADS]
        w_main = jnp.concatenate(
            [w[:, :o], w_q * (scale * LOG2E), _pad_slots(w_kv[2], g),
             _pad_slots(w_kv[4], g), w_kv[0], w_kv[1]], axis=1).astype(BF16)
        w_gt = jnp.pad(w_g.reshape(d, g, 3 * hg), ((0, 0), (0, 0), (0, GATE_ROWS - 3 * hg))).reshape(d, -1)
        v_slots = lambda a: jnp.pad(a.reshape(d, g, dh), ((0, 0), (0, 0), (0, V_SLOT - dh))).reshape(d, -1)
        w_t = jnp.concatenate([v_slots(w_kv[3]), v_slots(w_kv[5]), w_gt], axis=1).T.astype(BF16)
        widths = (o, nsw, g * LANES, kvw, 2 * g * V_SLOT, g * GATE_ROWS)
        hg_in, q_slots, ks_aug, kw_aug, kc, vc, v_t, gate_t = _inproj(x, ada3, w_main, w_t, n_mod, widths)

        y_hg = _hgrn(hg_in, hg_lb_logits, hg_norm_g[l][None], l)

        grp = lambda a: a.reshape(bsz, n_grp, CMP_STRIDE * kvw)
        pos_e = lambda p: jnp.broadcast_to(
            p.reshape(2, CMP_STRIDE, 1, dh), (2, CMP_STRIDE, g, dh)).reshape(1, -1)
        w2k = jnp.pad(cmp_w2_k[l], ((0, 0), (0, LANES - dh))).astype(BF16)
        kc_aug = _compress(grp(kc), pos_e(cmp_pos_k[l]), _cmp_w1_expand(cmp_w1_k[l], g).astype(BF16),
                           w2k, True)
        vc_t = _compress(grp(vc), pos_e(cmp_pos_v[l]), _cmp_w1_expand(cmp_w1_v[l], g).astype(BF16),
                         jnp.pad(cmp_w2_v[l].T, ((0, V_SLOT - dh), (0, 0))).astype(BF16), False)

        y_nsa = _nsa(q_slots, gate_t, kc_aug, vc_t, ks_aug, kw_aug, v_t, ov_t, slope_tab)

        w_o = w_out[l].astype(BF16)
        x = _post(x, y_hg, y_nsa, ada3.reshape(bsz, n_mod, 1, d), w_o[:hgw], w_o[hgw:],
                  ffn_w1[l].astype(BF16), ffn_w3[l].astype(BF16), ffn_w2[l].astype(BF16),
                  ln1_g[l][None], ln1_b[l][None], ln2_g[l][None], ln2_b[l][None], alpha)
    return x
```

```python
import functools

import numpy as np
import jax
import jax.numpy as jnp
from jax import lax
from jax.experimental import pallas as pl
from jax.experimental.pallas import tpu as pltpu

F32 = jnp.float32
BF16 = jnp.bfloat16

HG_HEADS = 4
HG_D = 128
NSA_HEADS = 8
NSA_GROUPS = 2
NSA_HG = NSA_HEADS // NSA_GROUPS
NSA_DH = 64
CMP_BLOCK = 32
CMP_STRIDE = 16
CMP_HIDDEN = 256
SLC_BLOCK = 64
SLC_TOPN = 16
WINDOW = 512
Q_BLOCK = 256
NEG = -1e30
FORCE = 1e4
LN_EPS = 1e-5
RMS_EPS = 1e-6

LANES = 128
SUBLANES = 8
HG_BLK = 32
HG_SAFE_LOG2 = 100.0
POS_SPLIT = 256
GATE_ROWS = 16
SEL_TILE = 512
SEL_UNROLL = 8
V_SLOT = 80
N_SPLIT = 3
LOG2E = 1.4426950408889634
MXU_DIM = 256
VMEM_LIMIT = 48 * 1024 * 1024
FFN_VMEM_LIMIT = 56 * 1024 * 1024


def _sigmoid(x):
    return 1.0 / (1.0 + jnp.exp(-x))


def _dot(a, b):
    return jnp.dot(a, b, preferred_element_type=F32)


def _dot_nt(a, b):
    return lax.dot_general(a, b, (((1,), (1,)), ((), ())), preferred_element_type=F32)


def _pos_columns(pos, col):
    c = col - NSA_DH
    lo = pos % POS_SPLIT
    val = jnp.where(c % 2 == 0, lo, pos - lo)
    return jnp.where((c >= 0) & (c < 2 * N_SPLIT), val, 0).astype(F32)


def _split2(x):
    hi = x.astype(BF16)
    return hi, (x - hi.astype(F32)).astype(BF16)


def _ada_kernel(c_ref, w_ref, b_ref, o_ref):
    c = c_ref[...]
    s = (c * _sigmoid(c)).astype(BF16)
    o_ref[...] = _dot(s, w_ref[...].astype(BF16)) + b_ref[...]


def _ada(c_pad, w, b):
    rows, d = c_pad.shape
    n = w.shape[1]
    tn = 1024
    return pl.pallas_call(
        _ada_kernel,
        out_shape=jax.ShapeDtypeStruct((rows, n), F32),
        grid=(n // tn,),
        in_specs=[
            pl.BlockSpec((rows, d), lambda j: (0, 0)),
            pl.BlockSpec((d, tn), lambda j: (0, j)),
            pl.BlockSpec((1, tn), lambda j: (0, j)),
        ],
        out_specs=pl.BlockSpec((rows, tn), lambda j: (0, j)),
        compiler_params=pltpu.CompilerParams(
            dimension_semantics=("parallel",), vmem_limit_bytes=VMEM_LIMIT),
        name="ada",
    )(c_pad, w, b)


def _inproj_kernel(x_ref, sc_ref, sh_ref, wm_ref, wt_ref,
                   hg_ref, q_ref, ks_ref, kw_ref, kc_ref, vc_ref, vt_ref, gt_ref, *, tm):
    h = (x_ref[0] * (1.0 + sc_ref[0]) + sh_ref[0]).astype(BF16)
    y = _dot(h, wm_ref[...])
    n_hg = hg_ref.shape[2]
    n_q = q_ref.shape[2]
    n_k = ks_ref.shape[2]
    n_c = kc_ref.shape[2]
    o = 0
    hg_ref[0] = y[:, o:o + n_hg]
    o += n_hg
    q_ref[0] = y[:, o:o + n_q].astype(BF16)
    o += n_q
    row = pl.program_id(1) * tm + lax.broadcasted_iota(jnp.int32, (tm, n_k), 0)
    pc = _pos_columns(row, lax.broadcasted_iota(jnp.int32, (tm, n_k), 1) % LANES)
    ks_ref[0] = (y[:, o:o + n_k] + pc).astype(BF16)
    o += n_k
    kw_ref[0] = (y[:, o:o + n_k] + pc).astype(BF16)
    o += n_k
    kc_ref[0] = y[:, o:o + n_c]
    o += n_c
    vc_ref[0] = y[:, o:o + n_c]
    yt = _dot_nt(wt_ref[...], h)
    n_vt = vt_ref.shape[1]
    ri = lax.broadcasted_iota(jnp.int32, (n_vt, tm), 0) % V_SLOT
    vt_ref[0] = (yt[:n_vt] + jnp.where(ri == NSA_DH, 1.0, 0.0)).astype(BF16)
    gt_ref[0] = yt[n_vt:]


def _inproj(x, ada3, w_main, w_t, n_mod, widths):
    b, t, d = x.shape
    n_hg, n_q, n_k, n_c, n_vt, n_gt = widths
    tm = 512
    row_spec = lambda n: pl.BlockSpec((1, tm, n), lambda i, j: (i, j, 0))
    col_spec = lambda n: pl.BlockSpec((1, n, tm), lambda i, j: (i, 0, j))
    sds = jax.ShapeDtypeStruct
    return pl.pallas_call(
        functools.partial(_inproj_kernel, tm=tm),
        out_shape=(sds((b, t, n_hg), F32), sds((b, t, n_q), BF16), sds((b, t, n_k), BF16),
                   sds((b, t, n_k), BF16), sds((b, t, n_c), F32), sds((b, t, n_c), F32),
                   sds((b, n_vt, t), BF16), sds((b, n_gt, t), F32)),
        grid=(b, t // tm),
        in_specs=[
            pl.BlockSpec((1, tm, d), lambda i, j: (i, j, 0)),
            pl.BlockSpec((1, 1, d), lambda i, j: (i * n_mod + 1, 0, 0)),
            pl.BlockSpec((1, 1, d), lambda i, j: (i * n_mod, 0, 0)),
            pl.BlockSpec(w_main.shape, lambda i, j: (0, 0), pipeline_mode=pl.Buffered(1)),
            pl.BlockSpec(w_t.shape, lambda i, j: (0, 0), pipeline_mode=pl.Buffered(1)),
        ],
        out_specs=(row_spec(n_hg), row_spec(n_q), row_spec(n_k), row_spec(n_k), row_spec(n_c),
                   row_spec(n_c), col_spec(n_vt), col_spec(n_gt)),
        compiler_params=pltpu.CompilerParams(
            dimension_semantics=("parallel", "parallel"), vmem_limit_bytes=VMEM_LIMIT),
        name="inproj",
    )(x, ada3, ada3, w_main, w_t)


def _hgrn_kernel(q_ref, f_ref, i_ref, g_ref, lbl_ref, ng_ref, e_ref, o_ref, st_ref, *, tc, layer):
    blk = HG_BLK
    per_grp = LANES // blk
    half = blk // 2

    @pl.when(pl.program_id(2) == 0)
    def _():
        st_ref[...] = jnp.zeros_like(st_ref)

    lbl = lbl_ref[...]
    e = jnp.exp(lbl - jnp.max(lbl, axis=0, keepdims=True))
    lb = jnp.sum(e[: layer + 1], axis=0, keepdims=True) / jnp.sum(e, axis=0, keepdims=True)

    r_i = lax.broadcasted_iota(jnp.int32, (LANES, LANES), 0)
    c_i = lax.broadcasted_iota(jnp.int32, (LANES, LANES), 1)
    same_blk = (r_i // blk) == (c_i // blk)
    causal_blk = same_blk & (c_i <= r_i)
    tri = jnp.where(causal_blk, 1.0, 0.0).astype(BF16)
    row_blk = r_i // blk
    nh = per_grp * half
    r_h = lax.broadcasted_iota(jnp.int32, (nh, LANES), 0)
    same_half = (r_h // half) == (lax.broadcasted_iota(jnp.int32, (nh, LANES), 1) // blk)
    row_in_half = r_h % half
    blocked = lambda a: a.reshape(per_grp, blk, LANES)

    def gates(g):
        sl = slice(g * LANES, (g + 1) * LANES)
        q = q_ref[0, sl, :]
        qs = q * _sigmoid(q)
        f = lb + (1.0 - lb) * _sigmoid(f_ref[0, sl, :])
        hi, lo = _split2(jnp.log2(f))
        b = _dot(tri, hi) + _dot(tri, lo)
        return qs, 1.0 - f, i_ref[0, sl, :], b

    def increments(qs, k, v, b):
        b3 = blocked(b)
        bl3 = b3[:, blk - 1:blk, :]
        qd = (qs * jnp.exp2(b)).astype(BF16)
        kd = (blocked(k) * jnp.exp2(bl3 - b3)).reshape(LANES, LANES)
        kexp = jnp.concatenate(
            [jnp.where(row_blk == j, kd, 0.0) for j in range(per_grp)], axis=1).astype(BF16)
        return qd, jnp.exp2(bl3), _dot(v.T.astype(BF16), kexp)

    def recur(st, qd, dl, ut):
        ois = []
        for j in range(per_grp):
            ois.append(_dot_nt(qd[j * blk:(j + 1) * blk], st.astype(BF16)))
            st = dl[j] * st + ut[:, j * LANES:(j + 1) * LANES]
        return st, jnp.concatenate(ois, axis=0)

    def pairwise(qs, k, b):
        c = b - jnp.log2(k)
        tiles = lambda a, off: jnp.concatenate(
            [a[j * blk + off:j * blk + off + half] for j in range(per_grp)], axis=0)
        q_lo, q_hi, b_lo, b_hi = tiles(qs, 0), tiles(qs, half), tiles(b, 0), tiles(b, half)
        p_lo, p_hi = [], []
        for s in range(blk):
            cs = jnp.concatenate([jnp.broadcast_to(c[j * blk + s:j * blk + s + 1], (half, LANES))
                                  for j in range(per_grp)], axis=0)
            if s < half:
                p_lo.append(jnp.where(row_in_half >= s, q_lo * jnp.exp2(b_lo - cs), 0.0).astype(BF16))
                p_hi.append((q_hi * jnp.exp2(b_hi - cs)).astype(BF16))
            else:
                p_hi.append(jnp.where(row_in_half >= s - half, q_hi * jnp.exp2(b_hi - cs), 0.0).astype(BF16))
        return jnp.concatenate(p_lo, axis=1), jnp.concatenate(p_hi, axis=1)

    def emit(g, o):
        o = o * lax.rsqrt(jnp.mean(o * o, axis=-1, keepdims=True) + RMS_EPS)
        sl = slice(g * LANES, (g + 1) * LANES)
        o_ref[0, sl, :] = (o * ng_ref[...] * _sigmoid(g_ref[0, sl, :])).astype(o_ref.dtype)

    ngrp = tc // LANES
    gs = [gates(g) for g in range(ngrp)]
    incs = [increments(*gt) for gt in gs]

    a_s = [_dot_nt(qd, (k * jnp.exp2(-b)).astype(BF16))
           for (_, k, _, b), (qd, _, _) in zip(gs, incs)]
    o_s = [_dot(jnp.where(causal_blk, a, 0.0).astype(BF16), gt[2].astype(BF16)) for a, gt in zip(a_s, gs)]

    st = st_ref[...]
    o_inters = []
    for g, inc in enumerate(incs):
        st, oi = recur(st, *inc)
        o_inters.append(oi)
        emit(g, o_s[g] + oi)
    st_ref[...] = st

    b_min = functools.reduce(jnp.minimum, [jnp.min(gt[3]) for gt in gs])

    @pl.when(b_min < -HG_SAFE_LOG2)
    def _():
        pws = [pairwise(qs, k, b) for qs, k, _, b in gs]
        a_lo = _dot(jnp.concatenate([p[0] for p in pws], axis=0), e_ref[:half * LANES, :])
        a_hi = _dot(jnp.concatenate([p[1] for p in pws], axis=0), e_ref[...])
        for g in range(ngrp):
            rows = slice(g * nh, (g + 1) * nh)
            vb = gs[g][2].astype(BF16)
            o_lo = _dot(jnp.where(same_half, a_lo[rows], 0.0).astype(BF16), vb)
            o_hi = _dot(jnp.where(same_half, a_hi[rows], 0.0).astype(BF16), vb)
            emit(g, jnp.concatenate([x[j * half:(j + 1) * half] for j in range(per_grp) for x in (o_lo, o_hi)],
                                    axis=0) + o_inters[g])


def _hgrn(proj, lb_logits, norm_g, layer):
    b, t, _ = proj.shape
    tc = 1024
    h = HG_HEADS
    nslots = lb_logits.shape[0]
    col = lambda seg: (lambda i, j, s: (i, s, seg * h + j))
    fold = np.zeros((HG_BLK, LANES, LANES), np.float32)
    for s in range(HG_BLK):
        fold[s, :, s::HG_BLK] = 1.0
    fold = jnp.asarray(fold.reshape(HG_BLK * LANES, LANES), BF16)
    return pl.pallas_call(
        functools.partial(_hgrn_kernel, tc=tc, layer=layer),
        out_shape=jax.ShapeDtypeStruct((b, t, h * HG_D), BF16),
        grid=(b, h, t // tc),
        in_specs=[
            pl.BlockSpec((1, tc, HG_D), col(0)),
            pl.BlockSpec((1, tc, HG_D), col(1)),
            pl.BlockSpec((1, tc, HG_D), col(2)),
            pl.BlockSpec((1, tc, HG_D), col(3)),
            pl.BlockSpec((nslots, HG_D), lambda i, j, s: (0, j)),
            pl.BlockSpec((1, HG_D), lambda i, j, s: (0, j)),
            pl.BlockSpec(fold.shape, lambda i, j, s: (0, 0)),
        ],
        out_specs=pl.BlockSpec((1, tc, HG_D), lambda i, j, s: (i, s, j)),
        scratch_shapes=[pltpu.VMEM((HG_D, HG_D), F32)],
        compiler_params=pltpu.CompilerParams(
            dimension_semantics=("parallel", "parallel", "arbitrary"), vmem_limit_bytes=VMEM_LIMIT),
        name="hgrn",
    )(proj, proj, proj, proj, lb_logits, norm_g, fold)


def _cmp_kernel(h_ref, pos_ref, w1_ref, w2_ref, o_ref, *, key_side):
    h = h_ref[0]
    half = h.shape[1]
    pos = pos_ref[...]
    p0 = _dot((h + pos[:, :half]).astype(BF16), w1_ref[0, :half])
    p1 = _dot((h + pos[:, half:]).astype(BF16), w1_ref[0, half:])
    ng = h.shape[0]
    pre = p0 + pltpu.roll(p1, ng - 1, 0)
    gl = 0.5 * pre * (1.0 + jnp.tanh(np.sqrt(2.0 / np.pi) * (pre + 0.044715 * (pre * pre * pre))))
    gl = gl.astype(BF16)
    if key_side:
        cend = lax.broadcasted_iota(jnp.int32, (ng, LANES), 0) * CMP_STRIDE + (CMP_BLOCK - 1)
        pc = _pos_columns(cend, lax.broadcasted_iota(jnp.int32, (ng, LANES), 1))
        o_ref[0, 0] = (_dot(gl, w2_ref[...]) + pc).astype(BF16)
    else:
        vt = _dot_nt(w2_ref[...], gl)
        ones_row = lax.broadcasted_iota(jnp.int32, vt.shape, 0) == NSA_DH
        o_ref[0, 0] = jnp.where(ones_row, 1.0, vt).astype(BF16)


def _compress(hgrp, pos, w1, w2, key_side):
    b, ng, half = hgrp.shape
    g = w1.shape[0]
    out_tail = (ng, LANES) if key_side else (w2.shape[0], ng)
    return pl.pallas_call(
        functools.partial(_cmp_kernel, key_side=key_side),
        out_shape=jax.ShapeDtypeStruct((b, g) + out_tail, BF16),
        grid=(b, g),
        in_specs=[
            pl.BlockSpec((1, ng, half), lambda i, j: (i, 0, 0)),
            pl.BlockSpec(pos.shape, lambda i, j: (0, 0)),
            pl.BlockSpec((1,) + w1.shape[1:], lambda i, j: (j, 0, 0)),
            pl.BlockSpec(w2.shape, lambda i, j: (0, 0)),
        ],
        out_specs=pl.BlockSpec((1, 1) + out_tail, lambda i, j: (i, j, 0, 0)),
        compiler_params=pltpu.CompilerParams(
            dimension_semantics=("parallel", "parallel"), vmem_limit_bytes=VMEM_LIMIT),
        name="cmp_k" if key_side else "cmp_v",
    )(hgrp, pos, w1, w2)


def _nsa_kernel(q_ref, gate_ref, kc_ref, vct_ref, ks_ref, vst_ref, kw_ref, vwt_ref, ovt_ref, slope_ref,
                o_ref, bias_ref, *z_scratch):
    qb = pl.program_id(2)
    t0 = qb * Q_BLOCK
    npair = NSA_HG // 2
    za_refs, zb_refs = z_scratch[:npair], z_scratch[npair:]
    nq = 2 * Q_BLOCK
    qblk = q_ref[0].astype(F32)
    low = lax.broadcasted_iota(jnp.int32, (Q_BLOCK, LANES), 1) < NSA_DH
    qas = []
    for hp in range(npair):
        pair = qblk[:, hp * LANES:(hp + 1) * LANES]
        heads = [jnp.where(low, x, 0.0) for x in (pair, pltpu.roll(pair, NSA_DH, 1))]
        qas.append(jnp.concatenate(heads, axis=0).astype(BF16) + slope_ref[0, hp * nq:(hp + 1) * nq])
    tpos = t0 + (lax.broadcasted_iota(jnp.int32, (1, nq), 1) % Q_BLOCK)

    def normalised(acc):
        return acc[:NSA_DH] * (1.0 / acc[NSA_DH:NSA_DH + 1])

    per_step = SEL_TILE // SLC_BLOCK
    spos0 = lax.broadcasted_iota(jnp.int32, (SEL_TILE, 1), 0)
    n_slab = WINDOW // Q_BLOCK + 1
    slab_kt = [qb - (n_slab - 1) + i for i in range(n_slab)]
    slab_at = [pl.multiple_of(jnp.maximum(kt, 0) * Q_BLOCK, Q_BLOCK) for kt in slab_kt]
    kw = jnp.concatenate([kw_ref[0, pl.ds(s, Q_BLOCK), :] for s in slab_at], axis=0)

    def sel_scores(st, z_refs):
        ks = ks_ref[0, pl.ds(pl.multiple_of(st * SEL_TILE, SEL_TILE), SEL_TILE), :]
        for z_ref, qa in zip(z_refs, qas):
            z_ref[...] = _dot_nt(ks, qa)

    zcs = [_dot_nt(kc_ref[0, 0], qa) for qa in qas]
    zws = [_dot_nt(kw, qa) for qa in qas]

    ncp = kc_ref.shape[2]
    cend = lax.broadcasted_iota(jnp.int32, (ncp, 1), 0) * CMP_STRIDE + (CMP_BLOCK - 1)
    valid = cend <= tpos
    o_c, imp = [], None
    for zc in zcs:
        zm = jnp.where(valid, zc, NEG)
        mc = jnp.max(zm, axis=0, keepdims=True)
        eb = jnp.exp2(zm - mc).astype(BF16)
        acc = _dot(vct_ref[0, 0], eb)
        inv = jnp.where(mc > 0.5 * NEG, 1.0 / acc[NSA_DH:NSA_DH + 1], 0.0)
        o_c.append(acc[:NSA_DH] * inv)
        for i in range(2):
            sl = slice(i * Q_BLOCK, (i + 1) * Q_BLOCK)
            ih = _dot(ovt_ref[...], eb[:, sl]) * inv[:, sl]
            imp = ih if imp is None else imp + ih

    sel_scores(0, za_refs)

    ns = imp.shape[0]
    j_i = lax.broadcasted_iota(jnp.int32, (ns, 1), 0)
    t_abs = t0 + lax.broadcasted_iota(jnp.int32, (1, Q_BLOCK), 1)
    cur = t_abs // SLC_BLOCK
    forced = (j_i == 0) | (j_i == cur) | (j_i == cur - 1)
    x = jnp.where(forced, FORCE, imp)
    x = jnp.where(j_i * SLC_BLOCK <= t_abs, x, NEG)

    vw = jnp.concatenate([vwt_ref[0, :, pl.ds(s, Q_BLOCK)] for s in slab_at], axis=1)
    r_k = lax.broadcasted_iota(jnp.int32, (Q_BLOCK, nq), 0)
    r_q = lax.broadcasted_iota(jnp.int32, (Q_BLOCK, nq), 1) % Q_BLOCK
    o_w = []
    for zw in zws:
        slabs = []
        for i, kt in enumerate(slab_kt):
            zi = zw[i * Q_BLOCK:(i + 1) * Q_BLOCK]
            if i == 0:
                zi = jnp.where(r_k > r_q, zi, NEG)
            if i == n_slab - 1:
                zi = jnp.where(r_k <= r_q, zi, NEG)
            else:
                zi = zi + jnp.where(kt < 0, NEG, 0.0)
            slabs.append(zi)
        zw = jnp.concatenate(slabs, axis=0)
        pw = jnp.exp2(zw - jnp.max(zw, axis=0, keepdims=True))
        o_w.append(normalised(_dot(vw, pw.astype(BF16))))

    sub = lax.broadcasted_iota(jnp.int32, (SUBLANES, Q_BLOCK), 0)
    tiles = [x[v * SUBLANES:(v + 1) * SUBLANES] for v in range(ns // SUBLANES)]
    cnts = [jnp.zeros((SUBLANES, Q_BLOCK), F32) for _ in tiles]
    for jp in range(ns):
        r = x[jp:jp + 1, :]
        vj, rj = divmod(jp, SUBLANES)
        for v, xt in enumerate(tiles):
            if v < vj:
                ahead = r > xt
            elif v > vj:
                ahead = r >= xt
            else:
                ahead = (r > xt) | ((r == xt) & (sub > rj))
            cnts[v] = cnts[v] + jnp.where(ahead, 1.0, 0.0)
    for v, cnt in enumerate(cnts):
        bias_ref[v * SUBLANES:(v + 1) * SUBLANES, :] = jnp.where(cnt < float(SLC_TOPN), 0.0, NEG)

    def sel_update(st, z_refs, carry, causal):
        s0 = pl.multiple_of(st * SEL_TILE, SEL_TILE)
        vs = vst_ref[0, :, pl.ds(s0, SEL_TILE)]
        brow = bias_ref[pl.ds(pl.multiple_of(st * per_step, per_step), per_step), :]
        b2 = jnp.concatenate([brow] * 2, axis=1)[:, None, :]
        out = []
        for z_ref, (m, acc) in zip(z_refs, carry):
            z = (z_ref[...].reshape(per_step, SLC_BLOCK, nq) + b2).reshape(SEL_TILE, nq)
            if causal:
                z = jnp.where((s0 + spos0) <= tpos, z, NEG)
            mn = jnp.maximum(m, jnp.max(z, axis=0, keepdims=True))
            p = jnp.exp2(z - mn)
            out.append((mn, jnp.exp2(m - mn) * acc + _dot(vs, p.astype(BF16))))
        return tuple(out)

    def run_steps(first, count, carry, ends_causal):
        bufs = (za_refs, zb_refs)
        for i in range(count):
            if i + 1 < count or not ends_causal:
                sel_scores(first + i + 1, bufs[(i + 1) % 2])
            carry = sel_update(first + i, bufs[i % 2], carry, ends_causal and i + 1 == count)
        return carry

    init = tuple((jnp.full((1, nq), NEG, F32), jnp.zeros((V_SLOT, nq), F32)) for _ in range(npair))
    last = qb // (SEL_TILE // Q_BLOCK)
    n_loop = last // SEL_UNROLL
    carry = lax.fori_loop(0, n_loop, lambda i, c: run_steps(SEL_UNROLL * i, SEL_UNROLL, c, False), init)
    tails = [functools.partial(run_steps, SEL_UNROLL * n_loop, r + 1, ends_causal=True)
             for r in range(SEL_UNROLL)]
    carry = lax.switch(last % SEL_UNROLL, tails, carry)
    o_s = [normalised(acc) for _, acc in carry]

    gts = _sigmoid(gate_ref[0])
    pairs = []
    for hp in range(npair):
        halves = []
        for i in range(2):
            r = 3 * (2 * hp + i)
            sl = slice(i * Q_BLOCK, (i + 1) * Q_BLOCK)
            halves.append(gts[r:r + 1] * o_c[hp][:, sl] + gts[r + 1:r + 2] * o_s[hp][:, sl]
                          + gts[r + 2:r + 3] * o_w[hp][:, sl])
        pairs.append(jnp.concatenate(halves, axis=0).T)
    o_ref[0] = jnp.concatenate(pairs, axis=1).astype(o_ref.dtype)


def _nsa(q_slots, gate_t, kc_aug, vc_t, ks_aug, kw_aug, v_t, ov_t, slope_tab):
    b, t, _ = q_slots.shape
    g = NSA_GROUPS
    nqb = t // Q_BLOCK
    ncp = kc_aug.shape[2]
    ns = ov_t.shape[0]
    return pl.pallas_call(
        _nsa_kernel,
        out_shape=jax.ShapeDtypeStruct((b, t, NSA_HEADS * NSA_DH), BF16),
        grid=(b, g, nqb),
        in_specs=[
            pl.BlockSpec((1, Q_BLOCK, NSA_HG * NSA_DH), lambda i, j, s: (i, s, j)),
            pl.BlockSpec((1, GATE_ROWS, Q_BLOCK), lambda i, j, s: (i, j, s)),
            pl.BlockSpec((1, 1, ncp, LANES), lambda i, j, s: (i, j, 0, 0)),
            pl.BlockSpec((1, 1, V_SLOT, ncp), lambda i, j, s: (i, j, 0, 0)),
            pl.BlockSpec((1, t, LANES), lambda i, j, s: (i, 0, j)),
            pl.BlockSpec((1, V_SLOT, t), lambda i, j, s: (i, j, 0)),
            pl.BlockSpec((1, t, LANES), lambda i, j, s: (i, 0, j)),
            pl.BlockSpec((1, V_SLOT, t), lambda i, j, s: (i, g + j, 0)),
            pl.BlockSpec((ns, ncp), lambda i, j, s: (0, 0)),
            pl.BlockSpec((1, NSA_HG * Q_BLOCK, LANES), lambda i, j, s: (j, 0, 0)),
        ],
        out_specs=pl.BlockSpec((1, Q_BLOCK, NSA_HG * NSA_DH), lambda i, j, s: (i, s, j)),
        scratch_shapes=[pltpu.VMEM((ns, Q_BLOCK), F32)]
        + [pltpu.VMEM((SEL_TILE, 2 * Q_BLOCK), F32)] * NSA_HG,
        compiler_params=pltpu.CompilerParams(
            dimension_semantics=("parallel", "parallel", "arbitrary"), vmem_limit_bytes=VMEM_LIMIT),
        name="nsa",
    )(q_slots, gate_t, kc_aug, vc_t, ks_aug, v_t, kw_aug, v_t, ov_t, slope_tab)


def _layer_norm(z, g, b):
    mu = jnp.mean(z, axis=-1, keepdims=True)
    zc = z - mu
    var = jnp.mean(zc * zc, axis=-1, keepdims=True)
    return zc * lax.rsqrt(var + LN_EPS) * g + b


def _post_kernel(x_ref, yh_ref, yn_ref, mod_ref, wh_ref, wn_ref, w1_ref, w3_ref, w2_ref,
                 l1g_ref, l1b_ref, l2g_ref, l2b_ref, o_ref, *, alpha, chunks, row_parts):
    g1, sh2, sc2, g2 = mod_ref[0, 2], mod_ref[0, 3], mod_ref[0, 4], mod_ref[0, 5]
    tm = x_ref.shape[1]
    parts = [slice(i * tm // row_parts, (i + 1) * tm // row_parts) for i in range(row_parts)]
    mixes = [_dot(yh_ref[0, r, :], wh_ref[...]) + _dot(yn_ref[0, r, :], wn_ref[...]) for r in parts]
    x1s, ups = [], []
    for r, mix in zip(parts, mixes):
        x1 = _layer_norm(alpha * x_ref[0, r, :] + g1 * mix, l1g_ref[...], l1b_ref[...])
        h = (x1 * (1.0 + sc2) + sh2).astype(BF16)
        x1s.append(x1)
        ups.append([(_dot(h, w1_ref[:, lo:hi]), _dot(h, w3_ref[:, lo:hi])) for lo, hi in chunks])
    downs = []
    for up in ups:
        a = jnp.concatenate([(u * _sigmoid(u) * w).astype(BF16) for u, w in up], axis=1)
        downs.append(_dot(a, w2_ref[...]))
    for r, x1, down in zip(parts, x1s, downs):
        o_ref[0, r, :] = _layer_norm(alpha * x1 + g2 * down, l2g_ref[...], l2b_ref[...])


def _post(x, y_hg, y_nsa, ada4, w_h, w_n, w1, w3, w2, ln1_g, ln1_b, ln2_g, ln2_b, alpha):
    b, t, d = x.shape
    dff = w1.shape[1]
    tm = 1024
    step = 3 * MXU_DIM
    chunks = tuple((lo, min(lo + step, dff)) for lo in range(0, dff, step))
    resident = lambda a: pl.BlockSpec(a.shape, lambda i, j: (0,) * a.ndim, pipeline_mode=pl.Buffered(1))
    rows = lambda a: pl.BlockSpec((1, tm, a.shape[2]), lambda i, j: (i, j, 0))
    vec = pl.BlockSpec((1, d), lambda i, j: (0, 0))
    return pl.pallas_call(
        functools.partial(_post_kernel, alpha=alpha, chunks=chunks, row_parts=4),
        out_shape=jax.ShapeDtypeStruct((b, t, d), F32),
        grid=(b, t // tm),
        in_specs=[rows(x), rows(y_hg), rows(y_nsa),
                  pl.BlockSpec((1,) + ada4.shape[1:], lambda i, j: (i, 0, 0, 0)),
                  resident(w_h), resident(w_n), resident(w1), resident(w3), resident(w2),
                  vec, vec, vec, vec],
        out_specs=rows(x),
        compiler_params=pltpu.CompilerParams(
            dimension_semantics=("parallel", "parallel"), vmem_limit_bytes=FFN_VMEM_LIMIT),
        name="post",
    )(x, y_hg, y_nsa, ada4, w_h, w_n, w1, w3, w2, ln1_g, ln1_b, ln2_g, ln2_b)


def _slope_table(group_slopes):
    g, hg = group_slopes.shape
    rest = (group_slopes * LOG2E).astype(np.float32)
    tab = np.zeros((g, hg, Q_BLOCK, LANES), jnp.bfloat16)
    for i in range(N_SPLIT):
        piece = rest.astype(jnp.bfloat16)
        rest = rest - piece.astype(np.float32)
        tab[..., NSA_DH + 2 * i] = piece[:, :, None]
        tab[..., NSA_DH + 2 * i + 1] = piece[:, :, None]
    return tab.reshape(g, hg * Q_BLOCK, LANES)


def _pad_slots(w, n_slots):
    d = w.shape[0]
    w3 = w.reshape(d, n_slots, -1)
    return jnp.pad(w3, ((0, 0), (0, 0), (0, LANES - w3.shape[2]))).reshape(d, n_slots * LANES)


def _cmp_w1_expand(w1, g):
    hid = w1.shape[1]
    w1r = w1.reshape(2, CMP_STRIDE, 1, NSA_DH, hid)
    eye = jnp.eye(g, dtype=w1.dtype).reshape(g, 1, 1, g, 1, 1)
    return (eye * w1r[None]).reshape(g, 2 * CMP_STRIDE * g * NSA_DH, hid)


def _overlap_t(n_cmp_pad, n_cmp, n_slc):
    cs = np.arange(n_cmp_pad) * CMP_STRIDE
    ss = np.arange(n_slc) * SLC_BLOCK
    ov = np.clip(np.minimum(cs[:, None] + CMP_BLOCK, ss[None] + SLC_BLOCK)
                 - np.maximum(cs[:, None], ss[None]), 0, None).astype(np.float32) / CMP_BLOCK
    ov[n_cmp:] = 0.0
    return ov.T


def kernel(x, c, w_ada, b_ada, w_in, hg_lb_logits, hg_norm_g, cmp_pos_k, cmp_w1_k, cmp_w2_k,
           cmp_pos_v, cmp_w1_v, cmp_w2_v, w_out, ln1_g, ln1_b, ffn_w1, ffn_w3, ffn_w2, ln2_g, ln2_b):
    bsz, t, d = x.shape
    depth = w_ada.shape[0]
    g, hg, dh = NSA_GROUPS, NSA_HG, NSA_DH
    hgw = HG_HEADS * HG_D
    nsw = NSA_HEADS * dh
    kvw = g * dh
    n_mod = 6
    alpha = (2.0 * depth) ** 0.25
    scale = dh ** -0.5
    n_grp = t // CMP_STRIDE
    n_cmp = (t - CMP_BLOCK) // CMP_STRIDE + 1
    n_slc = t // SLC_BLOCK
    assert t % SEL_TILE == 0 and t >= WINDOW + Q_BLOCK and 3 * hg <= GATE_ROWS and hg % 2 == 0

    slopes = (2.0 ** (-8.0 * (np.arange(NSA_HEADS) + 1) / NSA_HEADS)).reshape(g, hg)
    slope_tab = jnp.asarray(_slope_table(slopes), BF16)
    ov_t = jnp.asarray(_overlap_t(n_grp, n_cmp, n_slc), BF16)

    c_pad = jnp.pad(c, ((0, 8 - bsz), (0, 0)))
    for l in range(depth):
        ada = _ada(c_pad, w_ada[l], b_ada[l][None])
        ada3 = ada[:bsz].reshape(bsz * n_mod, 1, d)

        w = w_in[l]
        o = 4 * hgw
        w_q = w[:, o:o + nsw]
        w_kv = [w[:, o + nsw + i * kvw: o + nsw + (i + 1) * kvw] for i in range(6)]
        w_g = w[:, o + nsw + 6 * kvw: o + nsw + 6 * kvw + 3 * NSA_HEADS]
        w_main = jnp.concatenate(
            [w[:, :o], w_q * (scale * LOG2E), _pad_slots(w_kv[2], g),
             _pad_slots(w_kv[4], g), w_kv[0], w_kv[1]], axis=1).astype(BF16)
        w_gt = jnp.pad(w_g.reshape(d, g, 3 * hg), ((0, 0), (0, 0), (0, GATE_ROWS - 3 * hg))).reshape(d, -1)
        v_slots = lambda a: jnp.pad(a.reshape(d, g, dh), ((0, 0), (0, 0), (0, V_SLOT - dh))).reshape(d, -1)
        w_t = jnp.concatenate([v_slots(w_kv[3]), v_slots(w_kv[5]), w_gt], axis=1).T.astype(BF16)
        widths = (o, nsw, g * LANES, kvw, 2 * g * V_SLOT, g * GATE_ROWS)
        hg_in, q_slots, ks_aug, kw_aug, kc, vc, v_t, gate_t = _inproj(x, ada3, w_main, w_t, n_mod, widths)

        y_hg = _hgrn(hg_in, hg_lb_logits, hg_norm_g[l][None], l)

        grp = lambda a: a.reshape(bsz, n_grp, CMP_STRIDE * kvw)
        pos_e = lambda p: jnp.broadcast_to(
            p.reshape(2, CMP_STRIDE, 1, dh), (2, CMP_STRIDE, g, dh)).reshape(1, -1)
        w2k = jnp.pad(cmp_w2_k[l], ((0, 0), (0, LANES - dh))).astype(BF16)
        kc_aug = _compress(grp(kc), pos_e(cmp_pos_k[l]), _cmp_w1_expand(cmp_w1_k[l], g).astype(BF16),
                           w2k, True)
        vc_t = _compress(grp(vc), pos_e(cmp_pos_v[l]), _cmp_w1_expand(cmp_w1_v[l], g).astype(BF16),
                         jnp.pad(cmp_w2_v[l].T, ((0, V_SLOT - dh), (0, 0))).astype(BF16), False)

        y_nsa = _nsa(q_slots, gate_t, kc_aug, vc_t, ks_aug, kw_aug, v_t, ov_t, slope_tab)

        w_o = w_out[l].astype(BF16)
        x = _post(x, y_hg, y_nsa, ada3.reshape(bsz, n_mod, 1, d), w_o[:hgw], w_o[hgw:],
                  ffn_w1[l].astype(BF16), ffn_w3[l].astype(BF16), ffn_w2[l].astype(BF16),
                  ln1_g[l][None], ln1_b[l][None], ln2_g[l][None], ln2_b[l][None], alpha)
    return x
```

```python
import functools

import numpy as np
import jax
import jax.numpy as jnp
from jax import lax
from jax.experimental import pallas as pl
from jax.experimental.pallas import tpu as pltpu

F32 = jnp.float32
BF16 = jnp.bfloat16

HG_HEADS = 4
HG_D = 128
NSA_HEADS = 8
NSA_GROUPS = 2
NSA_HG = NSA_HEADS // NSA_GROUPS
NSA_DH = 64
CMP_BLOCK = 32
CMP_STRIDE = 16
CMP_HIDDEN = 256
SLC_BLOCK = 64
SLC_TOPN = 16
WINDOW = 512
Q_BLOCK = 256
NEG = -1e30
FORCE = 1e4
LN_EPS = 1e-5
RMS_EPS = 1e-6

LANES = 128
SUBLANES = 8
HG_BLK = 32
HG_SAFE_LOG2 = 100.0
POS_SPLIT = 256
GATE_ROWS = 16
SEL_TILE = 512
SEL_UNROLL = 8
RANK_SPLITS = 4
V_SLOT = 80
N_SPLIT = 3
LOG2E = 1.4426950408889634
MXU_DIM = 256
VMEM_LIMIT = 48 * 1024 * 1024
FFN_VMEM_LIMIT = 56 * 1024 * 1024


def _sigmoid(x):
    return 1.0 / (1.0 + jnp.exp(-x))


def _dot(a, b):
    return jnp.dot(a, b, preferred_element_type=F32)


def _dot_nt(a, b):
    return lax.dot_general(a, b, (((1,), (1,)), ((), ())), preferred_element_type=F32)


def _pos_columns(pos, col):
    c = col - NSA_DH
    lo = pos % POS_SPLIT
    val = jnp.where(c % 2 == 0, lo, pos - lo)
    return jnp.where((c >= 0) & (c < 2 * N_SPLIT), val, 0).astype(F32)


def _split2(x):
    hi = x.astype(BF16)
    return hi, (x - hi.astype(F32)).astype(BF16)


def _ada_kernel(c_ref, w_ref, b_ref, o_ref):
    c = c_ref[...]
    s = (c * _sigmoid(c)).astype(BF16)
    o_ref[...] = _dot(s, w_ref[...].astype(BF16)) + b_ref[...]


def _ada(c_pad, w, b):
    rows, d = c_pad.shape
    n = w.shape[1]
    tn = 1024
    return pl.pallas_call(
        _ada_kernel,
        out_shape=jax.ShapeDtypeStruct((rows, n), F32),
        grid=(n // tn,),
        in_specs=[
            pl.BlockSpec((rows, d), lambda j: (0, 0)),
            pl.BlockSpec((d, tn), lambda j: (0, j)),
            pl.BlockSpec((1, tn), lambda j: (0, j)),
        ],
        out_specs=pl.BlockSpec((rows, tn), lambda j: (0, j)),
        compiler_params=pltpu.CompilerParams(
            dimension_semantics=("parallel",), vmem_limit_bytes=VMEM_LIMIT),
        name="ada",
    )(c_pad, w, b)


def _inproj_kernel(x_ref, sc_ref, sh_ref, wm_ref, wt_ref,
                   hg_ref, q_ref, ks_ref, kw_ref, kc_ref, vc_ref, vt_ref, gt_ref, *, tm):
    h = (x_ref[0] * (1.0 + sc_ref[0]) + sh_ref[0]).astype(BF16)
    y = _dot(h, wm_ref[...])
    n_hg = hg_ref.shape[2]
    n_q = q_ref.shape[2]
    n_k = ks_ref.shape[2]
    n_c = kc_ref.shape[2]
    o = 0
    hg_ref[0] = y[:, o:o + n_hg]
    o += n_hg
    q_ref[0] = y[:, o:o + n_q].astype(BF16)
    o += n_q
    row = pl.program_id(1) * tm + lax.broadcasted_iota(jnp.int32, (tm, n_k), 0)
    pc = _pos_columns(row, lax.broadcasted_iota(jnp.int32, (tm, n_k), 1) % LANES)
    ks_ref[0] = (y[:, o:o + n_k] + pc).astype(BF16)
    o += n_k
    kw_ref[0] = (y[:, o:o + n_k] + pc).astype(BF16)
    o += n_k
    kc_ref[0] = y[:, o:o + n_c]
    o += n_c
    vc_ref[0] = y[:, o:o + n_c]
    yt = _dot_nt(wt_ref[...], h)
    n_vt = vt_ref.shape[1]
    ri = lax.broadcasted_iota(jnp.int32, (n_vt, tm), 0) % V_SLOT
    vt_ref[0] = (yt[:n_vt] + jnp.where(ri == NSA_DH, 1.0, 0.0)).astype(BF16)
    gt_ref[0] = yt[n_vt:]


def _inproj(x, ada3, w_main, w_t, n_mod, widths):
    b, t, d = x.shape
    n_hg, n_q, n_k, n_c, n_vt, n_gt = widths
    tm = 512
    row_spec = lambda n: pl.BlockSpec((1, tm, n), lambda i, j: (i, j, 0))
    col_spec = lambda n: pl.BlockSpec((1, n, tm), lambda i, j: (i, 0, j))
    sds = jax.ShapeDtypeStruct
    return pl.pallas_call(
        functools.partial(_inproj_kernel, tm=tm),
        out_shape=(sds((b, t, n_hg), F32), sds((b, t, n_q), BF16), sds((b, t, n_k), BF16),
                   sds((b, t, n_k), BF16), sds((b, t, n_c), F32), sds((b, t, n_c), F32),
                   sds((b, n_vt, t), BF16), sds((b, n_gt, t), F32)),
        grid=(b, t // tm),
        in_specs=[
            pl.BlockSpec((1, tm, d), lambda i, j: (i, j, 0)),
            pl.BlockSpec((1, 1, d), lambda i, j: (i * n_mod + 1, 0, 0)),
            pl.BlockSpec((1, 1, d), lambda i, j: (i * n_mod, 0, 0)),
            pl.BlockSpec(w_main.shape, lambda i, j: (0, 0), pipeline_mode=pl.Buffered(1)),
            pl.BlockSpec(w_t.shape, lambda i, j: (0, 0), pipeline_mode=pl.Buffered(1)),
        ],
        out_specs=(row_spec(n_hg), row_spec(n_q), row_spec(n_k), row_spec(n_k), row_spec(n_c),
                   row_spec(n_c), col_spec(n_vt), col_spec(n_gt)),
        compiler_params=pltpu.CompilerParams(
            dimension_semantics=("parallel", "parallel"), vmem_limit_bytes=VMEM_LIMIT),
        name="inproj",
    )(x, ada3, ada3, w_main, w_t)


def _hgrn_kernel(q_ref, f_ref, i_ref, g_ref, lbl_ref, ng_ref, e_ref, o_ref, st_ref, *, tc, layer):
    blk = HG_BLK
    per_grp = LANES // blk
    half = blk // 2

    @pl.when(pl.program_id(2) == 0)
    def _():
        st_ref[...] = jnp.zeros_like(st_ref)

    lbl = lbl_ref[...]
    e = jnp.exp(lbl - jnp.max(lbl, axis=0, keepdims=True))
    lb = jnp.sum(e[: layer + 1], axis=0, keepdims=True) / jnp.sum(e, axis=0, keepdims=True)

    r_i = lax.broadcasted_iota(jnp.int32, (LANES, LANES), 0)
    c_i = lax.broadcasted_iota(jnp.int32, (LANES, LANES), 1)
    same_blk = (r_i // blk) == (c_i // blk)
    causal_blk = same_blk & (c_i <= r_i)
    tri = jnp.where(causal_blk, 1.0, 0.0).astype(BF16)
    row_blk = r_i // blk
    nh = per_grp * half
    r_h = lax.broadcasted_iota(jnp.int32, (nh, LANES), 0)
    same_half = (r_h // half) == (lax.broadcasted_iota(jnp.int32, (nh, LANES), 1) // blk)
    row_in_half = r_h % half
    blocked = lambda a: a.reshape(per_grp, blk, LANES)

    def gates(g):
        sl = slice(g * LANES, (g + 1) * LANES)
        q = q_ref[0, sl, :]
        qs = q * _sigmoid(q)
        f = lb + (1.0 - lb) * _sigmoid(f_ref[0, sl, :])
        hi, lo = _split2(jnp.log2(f))
        b = _dot(tri, hi) + _dot(tri, lo)
        return qs, 1.0 - f, i_ref[0, sl, :], b

    def increments(qs, k, v, b):
        b3 = blocked(b)
        bl3 = b3[:, blk - 1:blk, :]
        qd = (qs * jnp.exp2(b)).astype(BF16)
        kd = (blocked(k) * jnp.exp2(bl3 - b3)).reshape(LANES, LANES)
        kexp = jnp.concatenate(
            [jnp.where(row_blk == j, kd, 0.0) for j in range(per_grp)], axis=1).astype(BF16)
        return qd, jnp.exp2(bl3), _dot(v.T.astype(BF16), kexp)

    def recur(st, qd, dl, ut):
        ois = []
        for j in range(per_grp):
            ois.append(_dot_nt(qd[j * blk:(j + 1) * blk], st.astype(BF16)))
            st = dl[j] * st + ut[:, j * LANES:(j + 1) * LANES]
        return st, jnp.concatenate(ois, axis=0)

    def pairwise(qs, k, b):
        c = b - jnp.log2(k)
        tiles = lambda a, off: jnp.concatenate(
            [a[j * blk + off:j * blk + off + half] for j in range(per_grp)], axis=0)
        q_lo, q_hi, b_lo, b_hi = tiles(qs, 0), tiles(qs, half), tiles(b, 0), tiles(b, half)
        p_lo, p_hi = [], []
        for s in range(blk):
            cs = jnp.concatenate([jnp.broadcast_to(c[j * blk + s:j * blk + s + 1], (half, LANES))
                                  for j in range(per_grp)], axis=0)
            if s < half:
                p_lo.append(jnp.where(row_in_half >= s, q_lo * jnp.exp2(b_lo - cs), 0.0).astype(BF16))
                p_hi.append((q_hi * jnp.exp2(b_hi - cs)).astype(BF16))
            else:
                p_hi.append(jnp.where(row_in_half >= s - half, q_hi * jnp.exp2(b_hi - cs), 0.0).astype(BF16))
        return jnp.concatenate(p_lo, axis=1), jnp.concatenate(p_hi, axis=1)

    def emit(g, o):
        o = o * lax.rsqrt(jnp.mean(o * o, axis=-1, keepdims=True) + RMS_EPS)
        sl = slice(g * LANES, (g + 1) * LANES)
        o_ref[0, sl, :] = (o * ng_ref[...] * _sigmoid(g_ref[0, sl, :])).astype(o_ref.dtype)

    ngrp = tc // LANES
    gs = [gates(g) for g in range(ngrp)]
    incs = [increments(*gt) for gt in gs]

    a_s = [_dot_nt(qd, (k * jnp.exp2(-b)).astype(BF16))
           for (_, k, _, b), (qd, _, _) in zip(gs, incs)]
    o_s = [_dot(jnp.where(causal_blk, a, 0.0).astype(BF16), gt[2].astype(BF16)) for a, gt in zip(a_s, gs)]

    st = st_ref[...]
    o_inters = []
    for g, inc in enumerate(incs):
        st, oi = recur(st, *inc)
        o_inters.append(oi)
        emit(g, o_s[g] + oi)
    st_ref[...] = st

    b_min = functools.reduce(jnp.minimum, [jnp.min(gt[3]) for gt in gs])

    @pl.when(b_min < -HG_SAFE_LOG2)
    def _():
        pws = [pairwise(qs, k, b) for qs, k, _, b in gs]
        a_lo = _dot(jnp.concatenate([p[0] for p in pws], axis=0), e_ref[:half * LANES, :])
        a_hi = _dot(jnp.concatenate([p[1] for p in pws], axis=0), e_ref[...])
        for g in range(ngrp):
            rows = slice(g * nh, (g + 1) * nh)
            vb = gs[g][2].astype(BF16)
            o_lo = _dot(jnp.where(same_half, a_lo[rows], 0.0).astype(BF16), vb)
            o_hi = _dot(jnp.where(same_half, a_hi[rows], 0.0).astype(BF16), vb)
            emit(g, jnp.concatenate([x[j * half:(j + 1) * half] for j in range(per_grp) for x in (o_lo, o_hi)],
                                    axis=0) + o_inters[g])


def _hgrn(proj, lb_logits, norm_g, layer):
    b, t, _ = proj.shape
    tc = 1024
    h = HG_HEADS
    nslots = lb_logits.shape[0]
    col = lambda seg: (lambda i, j, s: (i, s, seg * h + j))
    fold = np.zeros((HG_BLK, LANES, LANES), np.float32)
    for s in range(HG_BLK):
        fold[s, :, s::HG_BLK] = 1.0
    fold = jnp.asarray(fold.reshape(HG_BLK * LANES, LANES), BF16)
    return pl.pallas_call(
        functools.partial(_hgrn_kernel, tc=tc, layer=layer),
        out_shape=jax.ShapeDtypeStruct((b, t, h * HG_D), BF16),
        grid=(b, h, t // tc),
        in_specs=[
            pl.BlockSpec((1, tc, HG_D), col(0)),
            pl.BlockSpec((1, tc, HG_D), col(1)),
            pl.BlockSpec((1, tc, HG_D), col(2)),
            pl.BlockSpec((1, tc, HG_D), col(3)),
            pl.BlockSpec((nslots, HG_D), lambda i, j, s: (0, j)),
            pl.BlockSpec((1, HG_D), lambda i, j, s: (0, j)),
            pl.BlockSpec(fold.shape, lambda i, j, s: (0, 0)),
        ],
        out_specs=pl.BlockSpec((1, tc, HG_D), lambda i, j, s: (i, s, j)),
        scratch_shapes=[pltpu.VMEM((HG_D, HG_D), F32)],
        compiler_params=pltpu.CompilerParams(
            dimension_semantics=("parallel", "parallel", "arbitrary"), vmem_limit_bytes=VMEM_LIMIT),
        name="hgrn",
    )(proj, proj, proj, proj, lb_logits, norm_g, fold)


def _cmp_kernel(h_ref, pos_ref, w1_ref, w2_ref, o_ref, *, key_side):
    h = h_ref[0]
    half = h.shape[1]
    pos = pos_ref[...]
    p0 = _dot((h + pos[:, :half]).astype(BF16), w1_ref[0, :half])
    p1 = _dot((h + pos[:, half:]).astype(BF16), w1_ref[0, half:])
    ng = h.shape[0]
    pre = p0 + pltpu.roll(p1, ng - 1, 0)
    gl = 0.5 * pre * (1.0 + jnp.tanh(np.sqrt(2.0 / np.pi) * (pre + 0.044715 * (pre * pre * pre))))
    gl = gl.astype(BF16)
    if key_side:
        cend = lax.broadcasted_iota(jnp.int32, (ng, LANES), 0) * CMP_STRIDE + (CMP_BLOCK - 1)
        pc = _pos_columns(cend, lax.broadcasted_iota(jnp.int32, (ng, LANES), 1))
        o_ref[0, 0] = (_dot(gl, w2_ref[...]) + pc).astype(BF16)
    else:
        vt = _dot_nt(w2_ref[...], gl)
        ones_row = lax.broadcasted_iota(jnp.int32, vt.shape, 0) == NSA_DH
        o_ref[0, 0] = jnp.where(ones_row, 1.0, vt).astype(BF16)


def _compress(hgrp, pos, w1, w2, key_side):
    b, ng, half = hgrp.shape
    g = w1.shape[0]
    out_tail = (ng, LANES) if key_side else (w2.shape[0], ng)
    return pl.pallas_call(
        functools.partial(_cmp_kernel, key_side=key_side),
        out_shape=jax.ShapeDtypeStruct((b, g) + out_tail, BF16),
        grid=(b, g),
        in_specs=[
            pl.BlockSpec((1, ng, half), lambda i, j: (i, 0, 0)),
            pl.BlockSpec(pos.shape, lambda i, j: (0, 0)),
            pl.BlockSpec((1,) + w1.shape[1:], lambda i, j: (j, 0, 0)),
            pl.BlockSpec(w2.shape, lambda i, j: (0, 0)),
        ],
        out_specs=pl.BlockSpec((1, 1) + out_tail, lambda i, j: (i, j, 0, 0)),
        compiler_params=pltpu.CompilerParams(
            dimension_semantics=("parallel", "parallel"), vmem_limit_bytes=VMEM_LIMIT),
        name="cmp_k" if key_side else "cmp_v",
    )(hgrp, pos, w1, w2)


def _nsa_kernel(q_ref, gate_ref, kc_ref, vct_ref, ks_ref, vst_ref, kw_ref, vwt_ref, ovt_ref, slope_ref,
                o_ref, bias_ref, *z_scratch):
    qb = pl.program_id(2)
    t0 = qb * Q_BLOCK
    npair = NSA_HG // 2
    za_refs, zb_refs = z_scratch[:npair], z_scratch[npair:]
    nq = 2 * Q_BLOCK
    qblk = q_ref[0].astype(F32)
    low = lax.broadcasted_iota(jnp.int32, (Q_BLOCK, LANES), 1) < NSA_DH
    qas = []
    for hp in range(npair):
        pair = qblk[:, hp * LANES:(hp + 1) * LANES]
        heads = [jnp.where(low, x, 0.0) for x in (pair, pltpu.roll(pair, NSA_DH, 1))]
        qas.append(jnp.concatenate(heads, axis=0).astype(BF16) + slope_ref[0, hp * nq:(hp + 1) * nq])
    tpos = t0 + (lax.broadcasted_iota(jnp.int32, (1, nq), 1) % Q_BLOCK)

    def normalised(acc):
        return acc[:NSA_DH] * (1.0 / acc[NSA_DH:NSA_DH + 1])

    per_step = SEL_TILE // SLC_BLOCK
    spos0 = lax.broadcasted_iota(jnp.int32, (SEL_TILE, 1), 0)
    n_slab = WINDOW // Q_BLOCK + 1
    slab_kt = [qb - (n_slab - 1) + i for i in range(n_slab)]
    slab_at = [pl.multiple_of(jnp.maximum(kt, 0) * Q_BLOCK, Q_BLOCK) for kt in slab_kt]
    kw = jnp.concatenate([kw_ref[0, pl.ds(s, Q_BLOCK), :] for s in slab_at], axis=0)

    def sel_scores(st, z_refs):
        ks = ks_ref[0, pl.ds(pl.multiple_of(st * SEL_TILE, SEL_TILE), SEL_TILE), :]
        for z_ref, qa in zip(z_refs, qas):
            z_ref[...] = _dot_nt(ks, qa)

    zcs = [_dot_nt(kc_ref[0, 0], qa) for qa in qas]
    zws = [_dot_nt(kw, qa) for qa in qas]

    ncp = kc_ref.shape[2]
    cend = lax.broadcasted_iota(jnp.int32, (ncp, 1), 0) * CMP_STRIDE + (CMP_BLOCK - 1)
    valid = cend <= tpos
    o_c, imp = [], None
    for zc in zcs:
        zm = jnp.where(valid, zc, NEG)
        mc = jnp.max(zm, axis=0, keepdims=True)
        eb = jnp.exp2(zm - mc).astype(BF16)
        acc = _dot(vct_ref[0, 0], eb)
        inv = jnp.where(mc > 0.5 * NEG, 1.0 / acc[NSA_DH:NSA_DH + 1], 0.0)
        o_c.append(acc[:NSA_DH] * inv)
        for i in range(2):
            sl = slice(i * Q_BLOCK, (i + 1) * Q_BLOCK)
            ih = _dot(ovt_ref[...], eb[:, sl]) * inv[:, sl]
            imp = ih if imp is None else imp + ih

    sel_scores(0, za_refs)

    ns = imp.shape[0]
    j_i = lax.broadcasted_iota(jnp.int32, (ns, 1), 0)
    t_abs = t0 + lax.broadcasted_iota(jnp.int32, (1, Q_BLOCK), 1)
    cur = t_abs // SLC_BLOCK
    forced = (j_i == 0) | (j_i == cur) | (j_i == cur - 1)
    x = jnp.where(forced, FORCE, imp)
    x = jnp.where(j_i * SLC_BLOCK <= t_abs, x, NEG)

    vw = jnp.concatenate([vwt_ref[0, :, pl.ds(s, Q_BLOCK)] for s in slab_at], axis=1)
    r_k = lax.broadcasted_iota(jnp.int32, (Q_BLOCK, nq), 0)
    r_q = lax.broadcasted_iota(jnp.int32, (Q_BLOCK, nq), 1) % Q_BLOCK
    o_w = []
    for zw in zws:
        slabs = []
        for i, kt in enumerate(slab_kt):
            zi = zw[i * Q_BLOCK:(i + 1) * Q_BLOCK]
            if i == 0:
                zi = jnp.where(r_k > r_q, zi, NEG)
            if i == n_slab - 1:
                zi = jnp.where(r_k <= r_q, zi, NEG)
            else:
                zi = zi + jnp.where(kt < 0, NEG, 0.0)
            slabs.append(zi)
        zw = jnp.concatenate(slabs, axis=0)
        pw = jnp.exp2(zw - jnp.max(zw, axis=0, keepdims=True))
        o_w.append(normalised(_dot(vw, pw.astype(BF16))))

    def rank_rows(n_rows):
        sub = lax.broadcasted_iota(jnp.int32, (SUBLANES, Q_BLOCK), 0)
        tiles = [x[v * SUBLANES:(v + 1) * SUBLANES] for v in range(n_rows // SUBLANES)]
        cnts = [jnp.zeros((SUBLANES, Q_BLOCK), F32) for _ in tiles]
        for jp in range(n_rows):
            r = x[jp:jp + 1, :]
            vj, rj = divmod(jp, SUBLANES)
            for v, xt in enumerate(tiles):
                if v < vj:
                    ahead = r > xt
                elif v > vj:
                    ahead = r >= xt
                else:
                    ahead = (r > xt) | ((r == xt) & (sub > rj))
                cnts[v] = cnts[v] + jnp.where(ahead, 1.0, 0.0)
        for v, cnt in enumerate(cnts):
            bias_ref[v * SUBLANES:(v + 1) * SUBLANES, :] = jnp.where(cnt < float(SLC_TOPN), 0.0, NEG)
        if n_rows < ns:
            bias_ref[n_rows:, :] = jnp.zeros((ns - n_rows, Q_BLOCK), F32)

    n_causal = (t0 + Q_BLOCK) // SLC_BLOCK
    sizes = [ns * (i + 1) // RANK_SPLITS for i in range(RANK_SPLITS)]
    lax.switch((n_causal * RANK_SPLITS - 1) // ns, [functools.partial(rank_rows, n) for n in sizes])

    def sel_update(st, z_refs, carry, causal):
        s0 = pl.multiple_of(st * SEL_TILE, SEL_TILE)
        vs = vst_ref[0, :, pl.ds(s0, SEL_TILE)]
        brow = bias_ref[pl.ds(pl.multiple_of(st * per_step, per_step), per_step), :]
        b2 = jnp.concatenate([brow] * 2, axis=1)[:, None, :]
        out = []
        for z_ref, (m, acc) in zip(z_refs, carry):
            z = (z_ref[...].reshape(per_step, SLC_BLOCK, nq) + b2).reshape(SEL_TILE, nq)
            if causal:
                z = jnp.where((s0 + spos0) <= tpos, z, NEG)
            mn = jnp.maximum(m, jnp.max(z, axis=0, keepdims=True))
            p = jnp.exp2(z - mn)
            out.append((mn, jnp.exp2(m - mn) * acc + _dot(vs, p.astype(BF16))))
        return tuple(out)

    def run_steps(first, count, carry, ends_causal):
        bufs = (za_refs, zb_refs)
        for i in range(count):
            if i + 1 < count or not ends_causal:
                sel_scores(first + i + 1, bufs[(i + 1) % 2])
            carry = sel_update(first + i, bufs[i % 2], carry, ends_causal and i + 1 == count)
        return carry

    init = tuple((jnp.full((1, nq), NEG, F32), jnp.zeros((V_SLOT, nq), F32)) for _ in range(npair))
    last = qb // (SEL_TILE // Q_BLOCK)
    n_loop = last // SEL_UNROLL
    carry = lax.fori_loop(0, n_loop, lambda i, c: run_steps(SEL_UNROLL * i, SEL_UNROLL, c, False), init)
    tails = [functools.partial(run_steps, SEL_UNROLL * n_loop, r + 1, ends_causal=True)
             for r in range(SEL_UNROLL)]
    carry = lax.switch(last % SEL_UNROLL, tails, carry)
    o_s = [normalised(acc) for _, acc in carry]

    gts = _sigmoid(gate_ref[0])
    pairs = []
    for hp in range(npair):
        halves = []
        for i in range(2):
            r = 3 * (2 * hp + i)
            sl = slice(i * Q_BLOCK, (i + 1) * Q_BLOCK)
            halves.append(gts[r:r + 1] * o_c[hp][:, sl] + gts[r + 1:r + 2] * o_s[hp][:, sl]
                          + gts[r + 2:r + 3] * o_w[hp][:, sl])
        pairs.append(jnp.concatenate(halves, axis=0).T)
    o_ref[0] = jnp.concatenate(pairs, axis=1).astype(o_ref.dtype)


def _nsa(q_slots, gate_t, kc_aug, vc_t, ks_aug, kw_aug, v_t, ov_t, slope_tab):
    b, t, _ = q_slots.shape
    g = NSA_GROUPS
    nqb = t // Q_BLOCK
    ncp = kc_aug.shape[2]
    ns = ov_t.shape[0]
    return pl.pallas_call(
        _nsa_kernel,
        out_shape=jax.ShapeDtypeStruct((b, t, NSA_HEADS * NSA_DH), BF16),
        grid=(b, g, nqb),
        in_specs=[
            pl.BlockSpec((1, Q_BLOCK, NSA_HG * NSA_DH), lambda i, j, s: (i, s, j)),
            pl.BlockSpec((1, GATE_ROWS, Q_BLOCK), lambda i, j, s: (i, j, s)),
            pl.BlockSpec((1, 1, ncp, LANES), lambda i, j, s: (i, j, 0, 0)),
            pl.BlockSpec((1, 1, V_SLOT, ncp), lambda i, j, s: (i, j, 0, 0)),
            pl.BlockSpec((1, t, LANES), lambda i, j, s: (i, 0, j)),
            pl.BlockSpec((1, V_SLOT, t), lambda i, j, s: (i, j, 0)),
            pl.BlockSpec((1, t, LANES), lambda i, j, s: (i, 0, j)),
            pl.BlockSpec((1, V_SLOT, t), lambda i, j, s: (i, g + j, 0)),
            pl.BlockSpec((ns, ncp), lambda i, j, s: (0, 0)),
            pl.BlockSpec((1, NSA_HG * Q_BLOCK, LANES), lambda i, j, s: (j, 0, 0)),
        ],
        out_specs=pl.BlockSpec((1, Q_BLOCK, NSA_HG * NSA_DH), lambda i, j, s: (i, s, j)),
        scratch_shapes=[pltpu.VMEM((ns, Q_BLOCK), F32)]
        + [pltpu.VMEM((SEL_TILE, 2 * Q_BLOCK), F32)] * NSA_HG,
        compiler_params=pltpu.CompilerParams(
            dimension_semantics=("parallel", "parallel", "arbitrary"), vmem_limit_bytes=VMEM_LIMIT),
        name="nsa",
    )(q_slots, gate_t, kc_aug, vc_t, ks_aug, v_t, kw_aug, v_t, ov_t, slope_tab)


def _layer_norm(z, g, b):
    mu = jnp.mean(z, axis=-1, keepdims=True)
    zc = z - mu
    var = jnp.mean(zc * zc, axis=-1, keepdims=True)
    return zc * lax.rsqrt(var + LN_EPS) * g + b


def _post_kernel(x_ref, yh_ref, yn_ref, mod_ref, wh_ref, wn_ref, w1_ref, w3_ref, w2_ref,
                 l1g_ref, l1b_ref, l2g_ref, l2b_ref, o_ref, *, alpha, chunks, row_parts):
    g1, sh2, sc2, g2 = mod_ref[0, 2], mod_ref[0, 3], mod_ref[0, 4], mod_ref[0, 5]
    tm = x_ref.shape[1]
    parts = [slice(i * tm // row_parts, (i + 1) * tm // row_parts) for i in range(row_parts)]
    mixes = [_dot(yh_ref[0, r, :], wh_ref[...]) + _dot(yn_ref[0, r, :], wn_ref[...]) for r in parts]
    x1s, ups = [], []
    for r, mix in zip(parts, mixes):
        x1 = _layer_norm(alpha * x_ref[0, r, :] + g1 * mix, l1g_ref[...], l1b_ref[...])
        h = (x1 * (1.0 + sc2) + sh2).astype(BF16)
        x1s.append(x1)
        ups.append([(_dot(h, w1_ref[:, lo:hi]), _dot(h, w3_ref[:, lo:hi])) for lo, hi in chunks])
    downs = []
    for up in ups:
        a = jnp.concatenate([(u * _sigmoid(u) * w).astype(BF16) for u, w in up], axis=1)
        downs.append(_dot(a, w2_ref[...]))
    for r, x1, down in zip(parts, x1s, downs):
        o_ref[0, r, :] = _layer_norm(alpha * x1 + g2 * down, l2g_ref[...], l2b_ref[...])


def _post(x, y_hg, y_nsa, ada4, w_h, w_n, w1, w3, w2, ln1_g, ln1_b, ln2_g, ln2_b, alpha):
    b, t, d = x.shape
    dff = w1.shape[1]
    tm = 1024
    step = 3 * MXU_DIM
    chunks = tuple((lo, min(lo + step, dff)) for lo in range(0, dff, step))
    resident = lambda a: pl.BlockSpec(a.shape, lambda i, j: (0,) * a.ndim, pipeline_mode=pl.Buffered(1))
    rows = lambda a: pl.BlockSpec((1, tm, a.shape[2]), lambda i, j: (i, j, 0))
    vec = pl.BlockSpec((1, d), lambda i, j: (0, 0))
    return pl.pallas_call(
        functools.partial(_post_kernel, alpha=alpha, chunks=chunks, row_parts=4),
        out_shape=jax.ShapeDtypeStruct((b, t, d), F32),
        grid=(b, t // tm),
        in_specs=[rows(x), rows(y_hg), rows(y_nsa),
                  pl.BlockSpec((1,) + ada4.shape[1:], lambda i, j: (i, 0, 0, 0)),
                  resident(w_h), resident(w_n), resident(w1), resident(w3), resident(w2),
                  vec, vec, vec, vec],
        out_specs=rows(x),
        compiler_params=pltpu.CompilerParams(
            dimension_semantics=("parallel", "parallel"), vmem_limit_bytes=FFN_VMEM_LIMIT),
        name="post",
    )(x, y_hg, y_nsa, ada4, w_h, w_n, w1, w3, w2, ln1_g, ln1_b, ln2_g, ln2_b)


def _slope_table(group_slopes):
    g, hg = group_slopes.shape
    rest = (group_slopes * LOG2E).astype(np.float32)
    tab = np.zeros((g, hg, Q_BLOCK, LANES), jnp.bfloat16)
    for i in range(N_SPLIT):
        piece = rest.astype(jnp.bfloat16)
        rest = rest - piece.astype(np.float32)
        tab[..., NSA_DH + 2 * i] = piece[:, :, None]
        tab[..., NSA_DH + 2 * i + 1] = piece[:, :, None]
    return tab.reshape(g, hg * Q_BLOCK, LANES)


def _pad_slots(w, n_slots):
    d = w.shape[0]
    w3 = w.reshape(d, n_slots, -1)
    return jnp.pad(w3, ((0, 0), (0, 0), (0, LANES - w3.shape[2]))).reshape(d, n_slots * LANES)


def _cmp_w1_expand(w1, g):
    hid = w1.shape[1]
    w1r = w1.reshape(2, CMP_STRIDE, 1, NSA_DH, hid)
    eye = jnp.eye(g, dtype=w1.dtype).reshape(g, 1, 1, g, 1, 1)
    return (eye * w1r[None]).reshape(g, 2 * CMP_STRIDE * g * NSA_DH, hid)


def _overlap_t(n_cmp_pad, n_cmp, n_slc):
    cs = np.arange(n_cmp_pad) * CMP_STRIDE
    ss = np.arange(n_slc) * SLC_BLOCK
    ov = np.clip(np.minimum(cs[:, None] + CMP_BLOCK, ss[None] + SLC_BLOCK)
                 - np.maximum(cs[:, None], ss[None]), 0, None).astype(np.float32) / CMP_BLOCK
    ov[n_cmp:] = 0.0
    return ov.T


def kernel(x, c, w_ada, b_ada, w_in, hg_lb_logits, hg_norm_g, cmp_pos_k, cmp_w1_k, cmp_w2_k,
           cmp_pos_v, cmp_w1_v, cmp_w2_v, w_out, ln1_g, ln1_b, ffn_w1, ffn_w3, ffn_w2, ln2_g, ln2_b):
    bsz, t, d = x.shape
    depth = w_ada.shape[0]
    g, hg, dh = NSA_GROUPS, NSA_HG, NSA_DH
    hgw = HG_HEADS * HG_D
    nsw = NSA_HEADS * dh
    kvw = g * dh
    n_mod = 6
    alpha = (2.0 * depth) ** 0.25
    scale = dh ** -0.5
    n_grp = t // CMP_STRIDE
    n_cmp = (t - CMP_BLOCK) // CMP_STRIDE + 1
    n_slc = t // SLC_BLOCK
    assert t % SEL_TILE == 0 and t >= WINDOW + Q_BLOCK and 3 * hg <= GATE_ROWS and hg % 2 == 0
    assert n_slc % (RANK_SPLITS * SUBLANES) == 0

    slopes = (2.0 ** (-8.0 * (np.arange(NSA_HEADS) + 1) / NSA_HEADS)).reshape(g, hg)
    slope_tab = jnp.asarray(_slope_table(slopes), BF16)
    ov_t = jnp.asarray(_overlap_t(n_grp, n_cmp, n_slc), BF16)

    c_pad = jnp.pad(c, ((0, 8 - bsz), (0, 0)))
    for l in range(depth):
        ada = _ada(c_pad, w_ada[l], b_ada[l][None])
        ada3 = ada[:bsz].reshape(bsz * n_mod, 1, d)

        w = w_in[l]
        o = 4 * hgw
        w_q = w[:, o:o + nsw]
        w_kv = [w[:, o + nsw + i * kvw: o + nsw + (i + 1) * kvw] for i in range(6)]
        w_g = w[:, o + nsw + 6 * kvw: o + nsw + 6 * kvw + 3 * NSA_HEADS]
        w_main = jnp.concatenate(
            [w[:, :o], w_q * (scale * LOG2E), _pad_slots(w_kv[2], g),
             _pad_slots(w_kv[4], g), w_kv[0], w_kv[1]], axis=1).astype(BF16)
        w_gt = jnp.pad(w_g.reshape(d, g, 3 * hg), ((0, 0), (0, 0), (0, GATE_ROWS - 3 * hg))).reshape(d, -1)
        v_slots = lambda a: jnp.pad(a.reshape(d, g, dh), ((0, 0), (0, 0), (0, V_SLOT - dh))).reshape(d, -1)
        w_t = jnp.concatenate([v_slots(w_kv[3]), v_slots(w_kv[5]), w_gt], axis=1).T.astype(BF16)
        widths = (o, nsw, g * LANES, kvw, 2 * g * V_SLOT, g * GATE_ROWS)
        hg_in, q_slots, ks_aug, kw_aug, kc, vc, v_t, gate_t = _inproj(x, ada3, w_main, w_t, n_mod, widths)

        y_hg = _hgrn(hg_in, hg_lb_logits, hg_norm_g[l][None], l)

        grp = lambda a: a.reshape(bsz, n_grp, CMP_STRIDE * kvw)
        pos_e = lambda p: jnp.broadcast_to(
            p.reshape(2, CMP_STRIDE, 1, dh), (2, CMP_STRIDE, g, dh)).reshape(1, -1)
        w2k = jnp.pad(cmp_w2_k[l], ((0, 0), (0, LANES - dh))).astype(BF16)
        kc_aug = _compress(grp(kc), pos_e(cmp_pos_k[l]), _cmp_w1_expand(cmp_w1_k[l], g).astype(BF16),
                           w2k, True)
        vc_t = _compress(grp(vc), pos_e(cmp_pos_v[l]), _cmp_w1_expand(cmp_w1_v[l], g).astype(BF16),
                         jnp.pad(cmp_w2_v[l].T, ((0, V_SLOT - dh), (0, 0))).astype(BF16), False)

        y_nsa = _nsa(q_slots, gate_t, kc_aug, vc_t, ks_aug, kw_aug, v_t, ov_t, slope_tab)

        w_o = w_out[l].astype(BF16)
        x = _post(x, y_hg, y_nsa, ada3.reshape(bsz, n_mod, 1, d), w_o[:hgw], w_o[hgw:],
                  ffn_w1[l].astype(BF16), ffn_w3[l].astype(BF16), ffn_w2[l].astype(BF16),
                  ln1_g[l][None], ln1_b[l][None], ln2_g[l][None], ln2_b[l][None], alpha)
    return x
```

```python
import functools

import numpy as np
import jax
import jax.numpy as jnp
from jax import lax
from jax.experimental import pallas as pl
from jax.experimental.pallas import tpu as pltpu

F32 = jnp.float32
BF16 = jnp.bfloat16

HG_HEADS = 4
HG_D = 128
NSA_HEADS = 8
NSA_GROUPS = 2
NSA_HG = NSA_HEADS // NSA_GROUPS
NSA_DH = 64
CMP_BLOCK = 32
CMP_STRIDE = 16
CMP_HIDDEN = 256
SLC_BLOCK = 64
SLC_TOPN = 16
WINDOW = 512
Q_BLOCK = 256
NEG = -1e30
FORCE = 1e4
LN_EPS = 1e-5
RMS_EPS = 1e-6

LANES = 128
SUBLANES = 8
HG_BLK = 32
HG_SAFE_LOG2 = 100.0
POS_SPLIT = 256
GATE_ROWS = 16
SEL_TILE = 512
SEL_UNROLL = 8
RANK_SPLITS = 8
V_SLOT = 80
N_SPLIT = 3
LOG2E = 1.4426950408889634
MXU_DIM = 256
VMEM_LIMIT = 48 * 1024 * 1024
FFN_VMEM_LIMIT = 56 * 1024 * 1024


def _sigmoid(x):
    return 1.0 / (1.0 + jnp.exp(-x))


def _dot(a, b):
    return jnp.dot(a, b, preferred_element_type=F32)


def _dot_nt(a, b):
    return lax.dot_general(a, b, (((1,), (1,)), ((), ())), preferred_element_type=F32)


def _pos_columns(pos, col):
    c = col - NSA_DH
    lo = pos % POS_SPLIT
    val = jnp.where(c % 2 == 0, lo, pos - lo)
    return jnp.where((c >= 0) & (c < 2 * N_SPLIT), val, 0).astype(F32)


def _split2(x):
    hi = x.astype(BF16)
    return hi, (x - hi.astype(F32)).astype(BF16)


def _ada_kernel(c_ref, w_ref, b_ref, o_ref):
    c = c_ref[...]
    s = (c * _sigmoid(c)).astype(BF16)
    o_ref[...] = _dot(s, w_ref[...].astype(BF16)) + b_ref[...]


def _ada(c_pad, w, b):
    rows, d = c_pad.shape
    n = w.shape[1]
    tn = 1024
    return pl.pallas_call(
        _ada_kernel,
        out_shape=jax.ShapeDtypeStruct((rows, n), F32),
        grid=(n // tn,),
        in_specs=[
            pl.BlockSpec((rows, d), lambda j: (0, 0)),
            pl.BlockSpec((d, tn), lambda j: (0, j)),
            pl.BlockSpec((1, tn), lambda j: (0, j)),
        ],
        out_specs=pl.BlockSpec((rows, tn), lambda j: (0, j)),
        compiler_params=pltpu.CompilerParams(
            dimension_semantics=("parallel",), vmem_limit_bytes=VMEM_LIMIT),
        name="ada",
    )(c_pad, w, b)


def _inproj_kernel(x_ref, sc_ref, sh_ref, wm_ref, wt_ref,
                   hg_ref, q_ref, ks_ref, kw_ref, kc_ref, vc_ref, vt_ref, gt_ref, *, tm):
    h = (x_ref[0] * (1.0 + sc_ref[0]) + sh_ref[0]).astype(BF16)
    y = _dot(h, wm_ref[...])
    n_hg = hg_ref.shape[2]
    n_q = q_ref.shape[2]
    n_k = ks_ref.shape[2]
    n_c = kc_ref.shape[2]
    o = 0
    hg_ref[0] = y[:, o:o + n_hg]
    o += n_hg
    q_ref[0] = y[:, o:o + n_q].astype(BF16)
    o += n_q
    row = pl.program_id(1) * tm + lax.broadcasted_iota(jnp.int32, (tm, n_k), 0)
    pc = _pos_columns(row, lax.broadcasted_iota(jnp.int32, (tm, n_k), 1) % LANES)
    ks_ref[0] = (y[:, o:o + n_k] + pc).astype(BF16)
    o += n_k
    kw_ref[0] = (y[:, o:o + n_k] + pc).astype(BF16)
    o += n_k
    kc_ref[0] = y[:, o:o + n_c]
    o += n_c
    vc_ref[0] = y[:, o:o + n_c]
    yt = _dot_nt(wt_ref[...], h)
    n_vt = vt_ref.shape[1]
    ri = lax.broadcasted_iota(jnp.int32, (n_vt, tm), 0) % V_SLOT
    vt_ref[0] = (yt[:n_vt] + jnp.where(ri == NSA_DH, 1.0, 0.0)).astype(BF16)
    gt_ref[0] = yt[n_vt:]


def _inproj(x, ada3, w_main, w_t, n_mod, widths):
    b, t, d = x.shape
    n_hg, n_q, n_k, n_c, n_vt, n_gt = widths
    tm = 512
    row_spec = lambda n: pl.BlockSpec((1, tm, n), lambda i, j: (i, j, 0))
    col_spec = lambda n: pl.BlockSpec((1, n, tm), lambda i, j: (i, 0, j))
    sds = jax.ShapeDtypeStruct
    return pl.pallas_call(
        functools.partial(_inproj_kernel, tm=tm),
        out_shape=(sds((b, t, n_hg), F32), sds((b, t, n_q), BF16), sds((b, t, n_k), BF16),
                   sds((b, t, n_k), BF16), sds((b, t, n_c), F32), sds((b, t, n_c), F32),
                   sds((b, n_vt, t), BF16), sds((b, n_gt, t), F32)),
        grid=(b, t // tm),
        in_specs=[
            pl.BlockSpec((1, tm, d), lambda i, j: (i, j, 0)),
            pl.BlockSpec((1, 1, d), lambda i, j: (i * n_mod + 1, 0, 0)),
            pl.BlockSpec((1, 1, d), lambda i, j: (i * n_mod, 0, 0)),
            pl.BlockSpec(w_main.shape, lambda i, j: (0, 0), pipeline_mode=pl.Buffered(1)),
            pl.BlockSpec(w_t.shape, lambda i, j: (0, 0), pipeline_mode=pl.Buffered(1)),
        ],
        out_specs=(row_spec(n_hg), row_spec(n_q), row_spec(n_k), row_spec(n_k), row_spec(n_c),
                   row_spec(n_c), col_spec(n_vt), col_spec(n_gt)),
        compiler_params=pltpu.CompilerParams(
            dimension_semantics=("parallel", "parallel"), vmem_limit_bytes=VMEM_LIMIT),
        name="inproj",
    )(x, ada3, ada3, w_main, w_t)


def _hgrn_kernel(q_ref, f_ref, i_ref, g_ref, lbl_ref, ng_ref, e_ref, o_ref, st_ref, *, tc, layer):
    blk = HG_BLK
    per_grp = LANES // blk
    half = blk // 2

    @pl.when(pl.program_id(2) == 0)
    def _():
        st_ref[...] = jnp.zeros_like(st_ref)

    lbl = lbl_ref[...]
    e = jnp.exp(lbl - jnp.max(lbl, axis=0, keepdims=True))
    lb = jnp.sum(e[: layer + 1], axis=0, keepdims=True) / jnp.sum(e, axis=0, keepdims=True)

    r_i = lax.broadcasted_iota(jnp.int32, (LANES, LANES), 0)
    c_i = lax.broadcasted_iota(jnp.int32, (LANES, LANES), 1)
    same_blk = (r_i // blk) == (c_i // blk)
    causal_blk = same_blk & (c_i <= r_i)
    tri = jnp.where(causal_blk, 1.0, 0.0).astype(BF16)
    row_blk = r_i // blk
    nh = per_grp * half
    r_h = lax.broadcasted_iota(jnp.int32, (nh, LANES), 0)
    same_half = (r_h // half) == (lax.broadcasted_iota(jnp.int32, (nh, LANES), 1) // blk)
    row_in_half = r_h % half
    blocked = lambda a: a.reshape(per_grp, blk, LANES)

    def gates(g):
        sl = slice(g * LANES, (g + 1) * LANES)
        q = q_ref[0, sl, :]
        qs = q * _sigmoid(q)
        f = lb + (1.0 - lb) * _sigmoid(f_ref[0, sl, :])
        hi, lo = _split2(jnp.log2(f))
        b = _dot(tri, hi) + _dot(tri, lo)
        return qs, 1.0 - f, i_ref[0, sl, :], b

    def increments(qs, k, v, b):
        b3 = blocked(b)
        bl3 = b3[:, blk - 1:blk, :]
        qd = (qs * jnp.exp2(b)).astype(BF16)
        kd = (blocked(k) * jnp.exp2(bl3 - b3)).reshape(LANES, LANES)
        kexp = jnp.concatenate(
            [jnp.where(row_blk == j, kd, 0.0) for j in range(per_grp)], axis=1).astype(BF16)
        return qd, jnp.exp2(bl3), _dot(v.T.astype(BF16), kexp)

    def recur(st, qd, dl, ut):
        ois = []
        for j in range(per_grp):
            ois.append(_dot_nt(qd[j * blk:(j + 1) * blk], st.astype(BF16)))
            st = dl[j] * st + ut[:, j * LANES:(j + 1) * LANES]
        return st, jnp.concatenate(ois, axis=0)

    def pairwise(qs, k, b):
        c = b - jnp.log2(k)
        tiles = lambda a, off: jnp.concatenate(
            [a[j * blk + off:j * blk + off + half] for j in range(per_grp)], axis=0)
        q_lo, q_hi, b_lo, b_hi = tiles(qs, 0), tiles(qs, half), tiles(b, 0), tiles(b, half)
        p_lo, p_hi = [], []
        for s in range(blk):
            cs = jnp.concatenate([jnp.broadcast_to(c[j * blk + s:j * blk + s + 1], (half, LANES))
                                  for j in range(per_grp)], axis=0)
            if s < half:
                p_lo.append(jnp.where(row_in_half >= s, q_lo * jnp.exp2(b_lo - cs), 0.0).astype(BF16))
                p_hi.append((q_hi * jnp.exp2(b_hi - cs)).astype(BF16))
            else:
                p_hi.append(jnp.where(row_in_half >= s - half, q_hi * jnp.exp2(b_hi - cs), 0.0).astype(BF16))
        return jnp.concatenate(p_lo, axis=1), jnp.concatenate(p_hi, axis=1)

    def emit(g, o):
        o = o * lax.rsqrt(jnp.mean(o * o, axis=-1, keepdims=True) + RMS_EPS)
        sl = slice(g * LANES, (g + 1) * LANES)
        o_ref[0, sl, :] = (o * ng_ref[...] * _sigmoid(g_ref[0, sl, :])).astype(o_ref.dtype)

    ngrp = tc // LANES
    gs = [gates(g) for g in range(ngrp)]
    incs = [increments(*gt) for gt in gs]

    a_s = [_dot_nt(qd, (k * jnp.exp2(-b)).astype(BF16))
           for (_, k, _, b), (qd, _, _) in zip(gs, incs)]
    o_s = [_dot(jnp.where(causal_blk, a, 0.0).astype(BF16), gt[2].astype(BF16)) for a, gt in zip(a_s, gs)]

    st = st_ref[...]
    o_inters = []
    for g, inc in enumerate(incs):
        st, oi = recur(st, *inc)
        o_inters.append(oi)
        emit(g, o_s[g] + oi)
    st_ref[...] = st

    b_min = functools.reduce(jnp.minimum, [jnp.min(gt[3]) for gt in gs])

    @pl.when(b_min < -HG_SAFE_LOG2)
    def _():
        pws = [pairwise(qs, k, b) for qs, k, _, b in gs]
        a_lo = _dot(jnp.concatenate([p[0] for p in pws], axis=0), e_ref[:half * LANES, :])
        a_hi = _dot(jnp.concatenate([p[1] for p in pws], axis=0), e_ref[...])
        for g in range(ngrp):
            rows = slice(g * nh, (g + 1) * nh)
            vb = gs[g][2].astype(BF16)
            o_lo = _dot(jnp.where(same_half, a_lo[rows], 0.0).astype(BF16), vb)
            o_hi = _dot(jnp.where(same_half, a_hi[rows], 0.0).astype(BF16), vb)
            emit(g, jnp.concatenate([x[j * half:(j + 1) * half] for j in range(per_grp) for x in (o_lo, o_hi)],
                                    axis=0) + o_inters[g])


def _hgrn(proj, lb_logits, norm_g, layer):
    b, t, _ = proj.shape
    tc = 1024
    h = HG_HEADS
    nslots = lb_logits.shape[0]
    col = lambda seg: (lambda i, j, s: (i, s, seg * h + j))
    fold = np.zeros((HG_BLK, LANES, LANES), np.float32)
    for s in range(HG_BLK):
        fold[s, :, s::HG_BLK] = 1.0
    fold = jnp.asarray(fold.reshape(HG_BLK * LANES, LANES), BF16)
    return pl.pallas_call(
        functools.partial(_hgrn_kernel, tc=tc, layer=layer),
        out_shape=jax.ShapeDtypeStruct((b, t, h * HG_D), BF16),
        grid=(b, h, t // tc),
        in_specs=[
            pl.BlockSpec((1, tc, HG_D), col(0)),
            pl.BlockSpec((1, tc, HG_D), col(1)),
            pl.BlockSpec((1, tc, HG_D), col(2)),
            pl.BlockSpec((1, tc, HG_D), col(3)),
            pl.BlockSpec((nslots, HG_D), lambda i, j, s: (0, j)),
            pl.BlockSpec((1, HG_D), lambda i, j, s: (0, j)),
            pl.BlockSpec(fold.shape, lambda i, j, s: (0, 0)),
        ],
        out_specs=pl.BlockSpec((1, tc, HG_D), lambda i, j, s: (i, s, j)),
        scratch_shapes=[pltpu.VMEM((HG_D, HG_D), F32)],
        compiler_params=pltpu.CompilerParams(
            dimension_semantics=("parallel", "parallel", "arbitrary"), vmem_limit_bytes=VMEM_LIMIT),
        name="hgrn",
    )(proj, proj, proj, proj, lb_logits, norm_g, fold)


def _cmp_kernel(h_ref, pos_ref, w1_ref, w2_ref, o_ref, *, key_side):
    h = h_ref[0]
    half = h.shape[1]
    pos = pos_ref[...]
    p0 = _dot((h + pos[:, :half]).astype(BF16), w1_ref[0, :half])
    p1 = _dot((h + pos[:, half:]).astype(BF16), w1_ref[0, half:])
    ng = h.shape[0]
    pre = p0 + pltpu.roll(p1, ng - 1, 0)
    gl = 0.5 * pre * (1.0 + jnp.tanh(np.sqrt(2.0 / np.pi) * (pre + 0.044715 * (pre * pre * pre))))
    gl = gl.astype(BF16)
    if key_side:
        cend = lax.broadcasted_iota(jnp.int32, (ng, LANES), 0) * CMP_STRIDE + (CMP_BLOCK - 1)
        pc = _pos_columns(cend, lax.broadcasted_iota(jnp.int32, (ng, LANES), 1))
        o_ref[0, 0] = (_dot(gl, w2_ref[...]) + pc).astype(BF16)
    else:
        vt = _dot_nt(w2_ref[...], gl)
        ones_row = lax.broadcasted_iota(jnp.int32, vt.shape, 0) == NSA_DH
        o_ref[0, 0] = jnp.where(ones_row, 1.0, vt).astype(BF16)


def _compress(hgrp, pos, w1, w2, key_side):
    b, ng, half = hgrp.shape
    g = w1.shape[0]
    out_tail = (ng, LANES) if key_side else (w2.shape[0], ng)
    return pl.pallas_call(
        functools.partial(_cmp_kernel, key_side=key_side),
        out_shape=jax.ShapeDtypeStruct((b, g) + out_tail, BF16),
        grid=(b, g),
        in_specs=[
            pl.BlockSpec((1, ng, half), lambda i, j: (i, 0, 0)),
            pl.BlockSpec(pos.shape, lambda i, j: (0, 0)),
            pl.BlockSpec((1,) + w1.shape[1:], lambda i, j: (j, 0, 0)),
            pl.BlockSpec(w2.shape, lambda i, j: (0, 0)),
        ],
        out_specs=pl.BlockSpec((1, 1) + out_tail, lambda i, j: (i, j, 0, 0)),
        compiler_params=pltpu.CompilerParams(
            dimension_semantics=("parallel", "parallel"), vmem_limit_bytes=VMEM_LIMIT),
        name="cmp_k" if key_side else "cmp_v",
    )(hgrp, pos, w1, w2)


def _nsa_kernel(q_ref, gate_ref, kc_ref, vct_ref, ks_ref, vst_ref, kw_ref, vwt_ref, ovt_ref, slope_ref,
                o_ref, bias_ref, *z_scratch):
    qb = pl.program_id(2)
    t0 = qb * Q_BLOCK
    npair = NSA_HG // 2
    za_refs, zb_refs = z_scratch[:npair], z_scratch[npair:]
    nq = 2 * Q_BLOCK
    qblk = q_ref[0].astype(F32)
    low = lax.broadcasted_iota(jnp.int32, (Q_BLOCK, LANES), 1) < NSA_DH
    qas = []
    for hp in range(npair):
        pair = qblk[:, hp * LANES:(hp + 1) * LANES]
        heads = [jnp.where(low, x, 0.0) for x in (pair, pltpu.roll(pair, NSA_DH, 1))]
        qas.append(jnp.concatenate(heads, axis=0).astype(BF16) + slope_ref[0, hp * nq:(hp + 1) * nq])
    tpos = t0 + (lax.broadcasted_iota(jnp.int32, (1, nq), 1) % Q_BLOCK)

    def normalised(acc):
        return acc[:NSA_DH] * (1.0 / acc[NSA_DH:NSA_DH + 1])

    per_step = SEL_TILE // SLC_BLOCK
    spos0 = lax.broadcasted_iota(jnp.int32, (SEL_TILE, 1), 0)
    n_slab = WINDOW // Q_BLOCK + 1
    slab_kt = [qb - (n_slab - 1) + i for i in range(n_slab)]
    slab_at = [pl.multiple_of(jnp.maximum(kt, 0) * Q_BLOCK, Q_BLOCK) for kt in slab_kt]
    kw = jnp.concatenate([kw_ref[0, pl.ds(s, Q_BLOCK), :] for s in slab_at], axis=0)

    def sel_scores(st, z_refs):
        ks = ks_ref[0, pl.ds(pl.multiple_of(st * SEL_TILE, SEL_TILE), SEL_TILE), :]
        for z_ref, qa in zip(z_refs, qas):
            z_ref[...] = _dot_nt(ks, qa)

    zcs = [_dot_nt(kc_ref[0, 0], qa) for qa in qas]
    zws = [_dot_nt(kw, qa) for qa in qas]

    ncp = kc_ref.shape[2]
    cend = lax.broadcasted_iota(jnp.int32, (ncp, 1), 0) * CMP_STRIDE + (CMP_BLOCK - 1)
    valid = cend <= tpos
    o_c, imp = [], None
    for zc in zcs:
        zm = jnp.where(valid, zc, NEG)
        mc = jnp.max(zm, axis=0, keepdims=True)
        eb = jnp.exp2(zm - mc).astype(BF16)
        acc = _dot(vct_ref[0, 0], eb)
        inv = jnp.where(mc > 0.5 * NEG, 1.0 / acc[NSA_DH:NSA_DH + 1], 0.0)
        o_c.append(acc[:NSA_DH] * inv)
        for i in range(2):
            sl = slice(i * Q_BLOCK, (i + 1) * Q_BLOCK)
            ih = _dot(ovt_ref[...], eb[:, sl]) * inv[:, sl]
            imp = ih if imp is None else imp + ih

    sel_scores(0, za_refs)

    ns = imp.shape[0]
    j_i = lax.broadcasted_iota(jnp.int32, (ns, 1), 0)
    t_abs = t0 + lax.broadcasted_iota(jnp.int32, (1, Q_BLOCK), 1)
    cur = t_abs // SLC_BLOCK
    forced = (j_i == 0) | (j_i == cur) | (j_i == cur - 1)
    x = jnp.where(forced, FORCE, imp)
    x = jnp.where(j_i * SLC_BLOCK <= t_abs, x, NEG)

    vw = jnp.concatenate([vwt_ref[0, :, pl.ds(s, Q_BLOCK)] for s in slab_at], axis=1)
    r_k = lax.broadcasted_iota(jnp.int32, (Q_BLOCK, nq), 0)
    r_q = lax.broadcasted_iota(jnp.int32, (Q_BLOCK, nq), 1) % Q_BLOCK
    o_w = []
    for zw in zws:
        slabs = []
        for i, kt in enumerate(slab_kt):
            zi = zw[i * Q_BLOCK:(i + 1) * Q_BLOCK]
            if i == 0:
                zi = jnp.where(r_k > r_q, zi, NEG)
            if i == n_slab - 1:
                zi = jnp.where(r_k <= r_q, zi, NEG)
            else:
                zi = zi + jnp.where(kt < 0, NEG, 0.0)
            slabs.append(zi)
        zw = jnp.concatenate(slabs, axis=0)
        pw = jnp.exp2(zw - jnp.max(zw, axis=0, keepdims=True))
        o_w.append(normalised(_dot(vw, pw.astype(BF16))))

    def rank_rows(n_rows):
        sub = lax.broadcasted_iota(jnp.int32, (SUBLANES, Q_BLOCK), 0)
        tiles = [x[v * SUBLANES:(v + 1) * SUBLANES] for v in range(n_rows // SUBLANES)]
        cnts = [jnp.zeros((SUBLANES, Q_BLOCK), F32) for _ in tiles]
        for jp in range(n_rows):
            r = x[jp:jp + 1, :]
            vj, rj = divmod(jp, SUBLANES)
            for v, xt in enumerate(tiles):
                if v < vj:
                    ahead = r > xt
                elif v > vj:
                    ahead = r >= xt
                else:
                    ahead = (r > xt) | ((r == xt) & (sub > rj))
                cnts[v] = cnts[v] + jnp.where(ahead, 1.0, 0.0)
        for v, cnt in enumerate(cnts):
            bias_ref[v * SUBLANES:(v + 1) * SUBLANES, :] = jnp.where(cnt < float(SLC_TOPN), 0.0, NEG)
        if n_rows < ns:
            bias_ref[n_rows:, :] = jnp.zeros((ns - n_rows, Q_BLOCK), F32)

    n_causal = (t0 + Q_BLOCK) // SLC_BLOCK
    sizes = [ns * (i + 1) // RANK_SPLITS for i in range(RANK_SPLITS)]
    lax.switch((n_causal * RANK_SPLITS - 1) // ns, [functools.partial(rank_rows, n) for n in sizes])

    def sel_update(st, z_refs, carry, causal):
        s0 = pl.multiple_of(st * SEL_TILE, SEL_TILE)
        vs = vst_ref[0, :, pl.ds(s0, SEL_TILE)]
        brow = bias_ref[pl.ds(pl.multiple_of(st * per_step, per_step), per_step), :]
        b2 = jnp.concatenate([brow] * 2, axis=1)[:, None, :]
        out = []
        for z_ref, (m, acc) in zip(z_refs, carry):
            z = (z_ref[...].reshape(per_step, SLC_BLOCK, nq) + b2).reshape(SEL_TILE, nq)
            if causal:
                z = jnp.where((s0 + spos0) <= tpos, z, NEG)
            mn = jnp.maximum(m, jnp.max(z, axis=0, keepdims=True))
            p = jnp.exp2(z - mn)
            out.append((mn, jnp.exp2(m - mn) * acc + _dot(vs, p.astype(BF16))))
        return tuple(out)

    def run_steps(first, count, carry, ends_causal):
        bufs = (za_refs, zb_refs)
        for i in range(count):
            if i + 1 < count or not ends_causal:
                sel_scores(first + i + 1, bufs[(i + 1) % 2])
            carry = sel_update(first + i, bufs[i % 2], carry, ends_causal and i + 1 == count)
        return carry

    init = tuple((jnp.full((1, nq), NEG, F32), jnp.zeros((V_SLOT, nq), F32)) for _ in range(npair))
    last = qb // (SEL_TILE // Q_BLOCK)
    n_loop = last // SEL_UNROLL
    carry = lax.fori_loop(0, n_loop, lambda i, c: run_steps(SEL_UNROLL * i, SEL_UNROLL, c, False), init)
    tails = [functools.partial(run_steps, SEL_UNROLL * n_loop, r + 1, ends_causal=True)
             for r in range(SEL_UNROLL)]
    carry = lax.switch(last % SEL_UNROLL, tails, carry)
    o_s = [normalised(acc) for _, acc in carry]

    gts = _sigmoid(gate_ref[0])
    pairs = []
    for hp in range(npair):
        halves = []
        for i in range(2):
            r = 3 * (2 * hp + i)
            sl = slice(i * Q_BLOCK, (i + 1) * Q_BLOCK)
            halves.append(gts[r:r + 1] * o_c[hp][:, sl] + gts[r + 1:r + 2] * o_s[hp][:, sl]
                          + gts[r + 2:r + 3] * o_w[hp][:, sl])
        pairs.append(jnp.concatenate(halves, axis=0).T)
    o_ref[0] = jnp.concatenate(pairs, axis=1).astype(o_ref.dtype)


def _nsa(q_slots, gate_t, kc_aug, vc_t, ks_aug, kw_aug, v_t, ov_t, slope_tab):
    b, t, _ = q_slots.shape
    g = NSA_GROUPS
    nqb = t // Q_BLOCK
    ncp = kc_aug.shape[2]
    ns = ov_t.shape[0]
    assert t % SEL_TILE == 0 and SEL_TILE % Q_BLOCK == 0 and t >= WINDOW + Q_BLOCK and WINDOW % Q_BLOCK == 0
    assert 3 * NSA_HG <= GATE_ROWS and NSA_HG % 2 == 0 and ns % (RANK_SPLITS * SUBLANES) == 0
    return pl.pallas_call(
        _nsa_kernel,
        out_shape=jax.ShapeDtypeStruct((b, t, NSA_HEADS * NSA_DH), BF16),
        grid=(b, g, nqb),
        in_specs=[
            pl.BlockSpec((1, Q_BLOCK, NSA_HG * NSA_DH), lambda i, j, s: (i, s, j)),
            pl.BlockSpec((1, GATE_ROWS, Q_BLOCK), lambda i, j, s: (i, j, s)),
            pl.BlockSpec((1, 1, ncp, LANES), lambda i, j, s: (i, j, 0, 0)),
            pl.BlockSpec((1, 1, V_SLOT, ncp), lambda i, j, s: (i, j, 0, 0)),
            pl.BlockSpec((1, t, LANES), lambda i, j, s: (i, 0, j)),
            pl.BlockSpec((1, V_SLOT, t), lambda i, j, s: (i, j, 0)),
            pl.BlockSpec((1, t, LANES), lambda i, j, s: (i, 0, j)),
            pl.BlockSpec((1, V_SLOT, t), lambda i, j, s: (i, g + j, 0)),
            pl.BlockSpec((ns, ncp), lambda i, j, s: (0, 0)),
            pl.BlockSpec((1, NSA_HG * Q_BLOCK, LANES), lambda i, j, s: (j, 0, 0)),
        ],
        out_specs=pl.BlockSpec((1, Q_BLOCK, NSA_HG * NSA_DH), lambda i, j, s: (i, s, j)),
        scratch_shapes=[pltpu.VMEM((ns, Q_BLOCK), F32)]
        + [pltpu.VMEM((SEL_TILE, 2 * Q_BLOCK), F32)] * NSA_HG,
        compiler_params=pltpu.CompilerParams(
            dimension_semantics=("parallel", "parallel", "arbitrary"), vmem_limit_bytes=VMEM_LIMIT),
        name="nsa",
    )(q_slots, gate_t, kc_aug, vc_t, ks_aug, v_t, kw_aug, v_t, ov_t, slope_tab)


def _layer_norm(z, g, b):
    mu = jnp.mean(z, axis=-1, keepdims=True)
    zc = z - mu
    var = jnp.mean(zc * zc, axis=-1, keepdims=True)
    return zc * lax.rsqrt(var + LN_EPS) * g + b


def _post_kernel(x_ref, yh_ref, yn_ref, mod_ref, wh_ref, wn_ref, w1_ref, w3_ref, w2_ref,
                 l1g_ref, l1b_ref, l2g_ref, l2b_ref, o_ref, *, alpha, chunks, row_parts):
    g1, sh2, sc2, g2 = mod_ref[0, 2], mod_ref[0, 3], mod_ref[0, 4], mod_ref[0, 5]
    tm = x_ref.shape[1]
    parts = [slice(i * tm // row_parts, (i + 1) * tm // row_parts) for i in range(row_parts)]
    mixes = [_dot(yh_ref[0, r, :], wh_ref[...]) + _dot(yn_ref[0, r, :], wn_ref[...]) for r in parts]
    x1s, ups = [], []
    for r, mix in zip(parts, mixes):
        x1 = _layer_norm(alpha * x_ref[0, r, :] + g1 * mix, l1g_ref[...], l1b_ref[...])
        h = (x1 * (1.0 + sc2) + sh2).astype(BF16)
        x1s.append(x1)
        ups.append([(_dot(h, w1_ref[:, lo:hi]), _dot(h, w3_ref[:, lo:hi])) for lo, hi in chunks])
    downs = []
    for up in ups:
        a = jnp.concatenate([(u * _sigmoid(u) * w).astype(BF16) for u, w in up], axis=1)
        downs.append(_dot(a, w2_ref[...]))
    for r, x1, down in zip(parts, x1s, downs):
        o_ref[0, r, :] = _layer_norm(alpha * x1 + g2 * down, l2g_ref[...], l2b_ref[...])


def _post(x, y_hg, y_nsa, ada4, w_h, w_n, w1, w3, w2, ln1_g, ln1_b, ln2_g, ln2_b, alpha):
    b, t, d = x.shape
    dff = w1.shape[1]
    tm = 1024
    step = 3 * MXU_DIM
    chunks = tuple((lo, min(lo + step, dff)) for lo in range(0, dff, step))
    resident = lambda a: pl.BlockSpec(a.shape, lambda i, j: (0,) * a.ndim, pipeline_mode=pl.Buffered(1))
    rows = lambda a: pl.BlockSpec((1, tm, a.shape[2]), lambda i, j: (i, j, 0))
    vec = pl.BlockSpec((1, d), lambda i, j: (0, 0))
    return pl.pallas_call(
        functools.partial(_post_kernel, alpha=alpha, chunks=chunks, row_parts=4),
        out_shape=jax.ShapeDtypeStruct((b, t, d), F32),
        grid=(b, t // tm),
        in_specs=[rows(x), rows(y_hg), rows(y_nsa),
                  pl.BlockSpec((1,) + ada4.shape[1:], lambda i, j: (i, 0, 0, 0)),
                  resident(w_h), resident(w_n), resident(w1), resident(w3), resident(w2),
                  vec, vec, vec, vec],
        out_specs=rows(x),
        compiler_params=pltpu.CompilerParams(
            dimension_semantics=("parallel", "parallel"), vmem_limit_bytes=FFN_VMEM_LIMIT),
        name="post",
    )(x, y_hg, y_nsa, ada4, w_h, w_n, w1, w3, w2, ln1_g, ln1_b, ln2_g, ln2_b)


def _slope_table(group_slopes):
    g, hg = group_slopes.shape
    rest = (group_slopes * LOG2E).astype(np.float32)
    tab = np.zeros((g, hg, Q_BLOCK, LANES), jnp.bfloat16)
    for i in range(N_SPLIT):
        piece = rest.astype(jnp.bfloat16)
        rest = rest - piece.astype(np.float32)
        tab[..., NSA_DH + 2 * i] = piece[:, :, None]
        tab[..., NSA_DH + 2 * i + 1] = piece[:, :, None]
    return tab.reshape(g, hg * Q_BLOCK, LANES)


def _pad_slots(w, n_slots):
    d = w.shape[0]
    w3 = w.reshape(d, n_slots, -1)
    return jnp.pad(w3, ((0, 0), (0, 0), (0, LANES - w3.shape[2]))).reshape(d, n_slots * LANES)


def _cmp_w1_expand(w1, g):
    hid = w1.shape[1]
    w1r = w1.reshape(2, CMP_STRIDE, 1, NSA_DH, hid)
    eye = jnp.eye(g, dtype=w1.dtype).reshape(g, 1, 1, g, 1, 1)
    return (eye * w1r[None]).reshape(g, 2 * CMP_STRIDE * g * NSA_DH, hid)


def _overlap_t(n_cmp_pad, n_cmp, n_slc):
    cs = np.arange(n_cmp_pad) * CMP_STRIDE
    ss = np.arange(n_slc) * SLC_BLOCK
    ov = np.clip(np.minimum(cs[:, None] + CMP_BLOCK, ss[None] + SLC_BLOCK)
                 - np.maximum(cs[:, None], ss[None]), 0, None).astype(np.float32) / CMP_BLOCK
    ov[n_cmp:] = 0.0
    return ov.T


def kernel(x, c, w_ada, b_ada, w_in, hg_lb_logits, hg_norm_g, cmp_pos_k, cmp_w1_k, cmp_w2_k,
           cmp_pos_v, cmp_w1_v, cmp_w2_v, w_out, ln1_g, ln1_b, ffn_w1, ffn_w3, ffn_w2, ln2_g, ln2_b):
    bsz, t, d = x.shape
    depth = w_ada.shape[0]
    g, hg, dh = NSA_GROUPS, NSA_HG, NSA_DH
    hgw = HG_HEADS * HG_D
    nsw = NSA_HEADS * dh
    kvw = g * dh
    n_mod = 6
    alpha = (2.0 * depth) ** 0.25
    scale = dh ** -0.5
    n_grp = t // CMP_STRIDE
    n_cmp = (t - CMP_BLOCK) // CMP_STRIDE + 1
    n_slc = t // SLC_BLOCK

    slopes = (2.0 ** (-8.0 * (np.arange(NSA_HEADS) + 1) / NSA_HEADS)).reshape(g, hg)
    slope_tab = jnp.asarray(_slope_table(slopes), BF16)
    ov_t = jnp.asarray(_overlap_t(n_grp, n_cmp, n_slc), BF16)

    c_pad = jnp.pad(c, ((0, 8 - bsz), (0, 0)))
    for l in range(depth):
        ada = _ada(c_pad, w_ada[l], b_ada[l][None])
        ada3 = ada[:bsz].reshape(bsz * n_mod, 1, d)

        w = w_in[l]
        o = 4 * hgw
        w_q = w[:, o:o + nsw]
        w_kv = [w[:, o + nsw + i * kvw: o + nsw + (i + 1) * kvw] for i in range(6)]
        w_g = w[:, o + nsw + 6 * kvw: o + nsw + 6 * kvw + 3 * NSA_HEADS]
        w_main = jnp.concatenate(
            [w[:, :o], w_q * (scale * LOG2E), _pad_slots(w_kv[2], g),
             _pad_slots(w_kv[4], g), w_kv[0], w_kv[1]], axis=1).astype(BF16)
        w_gt = jnp.pad(w_g.reshape(d, g, 3 * hg), ((0, 0), (0, 0), (0, GATE_ROWS - 3 * hg))).reshape(d, -1)
        v_slots = lambda a: jnp.pad(a.reshape(d, g, dh), ((0, 0), (0, 0), (0, V_SLOT - dh))).reshape(d, -1)
        w_t = jnp.concatenate([v_slots(w_kv[3]), v_slots(w_kv[5]), w_gt], axis=1).T.astype(BF16)
        widths = (o, nsw, g * LANES, kvw, 2 * g * V_SLOT, g * GATE_ROWS)
        hg_in, q_slots, ks_aug, kw_aug, kc, vc, v_t, gate_t = _inproj(x, ada3, w_main, w_t, n_mod, widths)

        y_hg = _hgrn(hg_in, hg_lb_logits, hg_norm_g[l][None], l)

        grp = lambda a: a.reshape(bsz, n_grp, CMP_STRIDE * kvw)
        pos_e = lambda p: jnp.broadcast_to(
            p.reshape(2, CMP_STRIDE, 1, dh), (2, CMP_STRIDE, g, dh)).reshape(1, -1)
        w2k = jnp.pad(cmp_w2_k[l], ((0, 0), (0, LANES - dh))).astype(BF16)
        kc_aug = _compress(grp(kc), pos_e(cmp_pos_k[l]), _cmp_w1_expand(cmp_w1_k[l], g).astype(BF16),
                           w2k, True)
        vc_t = _compress(grp(vc), pos_e(cmp_pos_v[l]), _cmp_w1_expand(cmp_w1_v[l], g).astype(BF16),
                         jnp.pad(cmp_w2_v[l].T, ((0, V_SLOT - dh), (0, 0))).astype(BF16), False)

        y_nsa = _nsa(q_slots, gate_t, kc_aug, vc_t, ks_aug, kw_aug, v_t, ov_t, slope_tab)

        w_o = w_out[l].astype(BF16)
        x = _post(x, y_hg, y_nsa, ada3.reshape(bsz, n_mod, 1, d), w_o[:hgw], w_o[hgw:],
                  ffn_w1[l].astype(BF16), ffn_w3[l].astype(BF16), ffn_w2[l].astype(BF16),
                  ln1_g[l][None], ln1_b[l][None], ln2_g[l][None], ln2_b[l][None], alpha)
    return x
```

```python
import functools

import numpy as np
import jax
import jax.numpy as jnp
from jax import lax
from jax.experimental import pallas as pl
from jax.experimental.pallas import tpu as pltpu

F32 = jnp.float32
BF16 = jnp.bfloat16

HG_HEADS = 4
HG_D = 128
NSA_HEADS = 8
NSA_GROUPS = 2
NSA_HG = NSA_HEADS // NSA_GROUPS
NSA_DH = 64
CMP_BLOCK = 32
CMP_STRIDE = 16
CMP_HIDDEN = 256
SLC_BLOCK = 64
SLC_TOPN = 16
WINDOW = 512
Q_BLOCK = 256
NEG = -1e30
FORCE = 1e4
LN_EPS = 1e-5
RMS_EPS = 1e-6

LANES = 128
SUBLANES = 8
HG_BLK = 32
HG_SAFE_LOG2 = 100.0
POS_SPLIT = 256
GATE_ROWS = 16
SEL_TILE = 512
SEL_UNROLL = 8
RANK_SPLITS = 8
V_SLOT = 80
N_SPLIT = 3
LOG2E = 1.4426950408889634
MXU_DIM = 256
VMEM_LIMIT = 48 * 1024 * 1024
FFN_VMEM_LIMIT = 56 * 1024 * 1024


def _sigmoid(x):
    return 1.0 / (1.0 + jnp.exp(-x))


def _dot(a, b):
    return jnp.dot(a, b, preferred_element_type=F32)


def _dot_nt(a, b):
    return lax.dot_general(a, b, (((1,), (1,)), ((), ())), preferred_element_type=F32)


def _pos_columns(pos, col):
    c = col - NSA_DH
    lo = pos % POS_SPLIT
    val = jnp.where(c % 2 == 0, lo, pos - lo)
    return jnp.where((c >= 0) & (c < 2 * N_SPLIT), val, 0).astype(F32)


def _split2(x):
    hi = x.astype(BF16)
    return hi, (x - hi.astype(F32)).astype(BF16)


def _ada_kernel(c_ref, w_ref, b_ref, o_ref):
    c = c_ref[...]
    s = (c * _sigmoid(c)).astype(BF16)
    o_ref[...] = _dot(s, w_ref[...].astype(BF16)) + b_ref[...]


def _ada(c_pad, w, b):
    rows, d = c_pad.shape
    n = w.shape[1]
    tn = 1024
    return pl.pallas_call(
        _ada_kernel,
        out_shape=jax.ShapeDtypeStruct((rows, n), F32),
        grid=(n // tn,),
        in_specs=[
            pl.BlockSpec((rows, d), lambda j: (0, 0)),
            pl.BlockSpec((d, tn), lambda j: (0, j)),
            pl.BlockSpec((1, tn), lambda j: (0, j)),
        ],
        out_specs=pl.BlockSpec((rows, tn), lambda j: (0, j)),
        compiler_params=pltpu.CompilerParams(
            dimension_semantics=("parallel",), vmem_limit_bytes=VMEM_LIMIT),
        name="ada",
    )(c_pad, w, b)


def _inproj_kernel(x_ref, sc_ref, sh_ref, wm_ref, wt_ref,
                   hg_ref, q_ref, ks_ref, kw_ref, kc_ref, vc_ref, vt_ref, gt_ref, *, tm):
    h = (x_ref[0] * (1.0 + sc_ref[0]) + sh_ref[0]).astype(BF16)
    y = _dot_nt(h, wm_ref[...])
    n_hg = hg_ref.shape[2]
    n_q = q_ref.shape[2]
    n_k = ks_ref.shape[2]
    n_c = kc_ref.shape[2]
    o = 0
    hg_ref[0] = y[:, o:o + n_hg]
    o += n_hg
    q_ref[0] = y[:, o:o + n_q].astype(BF16)
    o += n_q
    row = pl.program_id(1) * tm + lax.broadcasted_iota(jnp.int32, (tm, n_k), 0)
    pc = _pos_columns(row, lax.broadcasted_iota(jnp.int32, (tm, n_k), 1) % LANES)
    ks_ref[0] = (y[:, o:o + n_k] + pc).astype(BF16)
    o += n_k
    kw_ref[0] = (y[:, o:o + n_k] + pc).astype(BF16)
    o += n_k
    kc_ref[0] = y[:, o:o + n_c]
    o += n_c
    vc_ref[0] = y[:, o:o + n_c]
    yt = _dot_nt(wt_ref[...], h)
    n_vt = vt_ref.shape[1]
    ri = lax.broadcasted_iota(jnp.int32, (n_vt, tm), 0) % V_SLOT
    vt_ref[0] = (yt[:n_vt] + jnp.where(ri == NSA_DH, 1.0, 0.0)).astype(BF16)
    gt_ref[0] = yt[n_vt:]


def _inproj(x, ada3, w_main, w_t, n_mod, widths):
    b, t, d = x.shape
    n_hg, n_q, n_k, n_c, n_vt, n_gt = widths
    tm = 512
    row_spec = lambda n: pl.BlockSpec((1, tm, n), lambda i, j: (i, j, 0))
    col_spec = lambda n: pl.BlockSpec((1, n, tm), lambda i, j: (i, 0, j))
    sds = jax.ShapeDtypeStruct
    return pl.pallas_call(
        functools.partial(_inproj_kernel, tm=tm),
        out_shape=(sds((b, t, n_hg), F32), sds((b, t, n_q), BF16), sds((b, t, n_k), BF16),
                   sds((b, t, n_k), BF16), sds((b, t, n_c), F32), sds((b, t, n_c), F32),
                   sds((b, n_vt, t), BF16), sds((b, n_gt, t), F32)),
        grid=(b, t // tm),
        in_specs=[
            pl.BlockSpec((1, tm, d), lambda i, j: (i, j, 0)),
            pl.BlockSpec((1, 1, d), lambda i, j: (i * n_mod + 1, 0, 0)),
            pl.BlockSpec((1, 1, d), lambda i, j: (i * n_mod, 0, 0)),
            pl.BlockSpec(w_main.shape, lambda i, j: (0, 0), pipeline_mode=pl.Buffered(1)),
            pl.BlockSpec(w_t.shape, lambda i, j: (0, 0), pipeline_mode=pl.Buffered(1)),
        ],
        out_specs=(row_spec(n_hg), row_spec(n_q), row_spec(n_k), row_spec(n_k), row_spec(n_c),
                   row_spec(n_c), col_spec(n_vt), col_spec(n_gt)),
        compiler_params=pltpu.CompilerParams(
            dimension_semantics=("parallel", "parallel"), vmem_limit_bytes=VMEM_LIMIT),
        name="inproj",
    )(x, ada3, ada3, w_main, w_t)


def _hgrn_kernel(q_ref, f_ref, i_ref, g_ref, lbl_ref, ng_ref, e_ref, o_ref, st_ref, *, tc, layer):
    blk = HG_BLK
    per_grp = LANES // blk
    half = blk // 2

    @pl.when(pl.program_id(2) == 0)
    def _():
        st_ref[...] = jnp.zeros_like(st_ref)

    lbl = lbl_ref[...]
    e = jnp.exp(lbl - jnp.max(lbl, axis=0, keepdims=True))
    lb = jnp.sum(e[: layer + 1], axis=0, keepdims=True) / jnp.sum(e, axis=0, keepdims=True)

    r_i = lax.broadcasted_iota(jnp.int32, (LANES, LANES), 0)
    c_i = lax.broadcasted_iota(jnp.int32, (LANES, LANES), 1)
    same_blk = (r_i // blk) == (c_i // blk)
    causal_blk = same_blk & (c_i <= r_i)
    tri = jnp.where(causal_blk, 1.0, 0.0).astype(BF16)
    row_blk = r_i // blk
    nh = per_grp * half
    r_h = lax.broadcasted_iota(jnp.int32, (nh, LANES), 0)
    same_half = (r_h // half) == (lax.broadcasted_iota(jnp.int32, (nh, LANES), 1) // blk)
    row_in_half = r_h % half
    blocked = lambda a: a.reshape(per_grp, blk, LANES)

    def gates(g):
        sl = slice(g * LANES, (g + 1) * LANES)
        q = q_ref[0, sl, :]
        qs = q * _sigmoid(q)
        f = lb + (1.0 - lb) * _sigmoid(f_ref[0, sl, :])
        hi, lo = _split2(jnp.log2(f))
        b = _dot(tri, hi) + _dot(tri, lo)
        return qs, 1.0 - f, i_ref[0, sl, :], b

    def increments(qs, k, v, b):
        b3 = blocked(b)
        bl3 = b3[:, blk - 1:blk, :]
        qd = (qs * jnp.exp2(b)).astype(BF16)
        kd = (blocked(k) * jnp.exp2(bl3 - b3)).reshape(LANES, LANES)
        kexp = jnp.concatenate(
            [jnp.where(row_blk == j, kd, 0.0) for j in range(per_grp)], axis=1).astype(BF16)
        return qd, jnp.exp2(bl3), _dot(v.T.astype(BF16), kexp)

    def recur(st, qd, dl, ut):
        ois = []
        for j in range(per_grp):
            ois.append(_dot_nt(qd[j * blk:(j + 1) * blk], st.astype(BF16)))
            st = dl[j] * st + ut[:, j * LANES:(j + 1) * LANES]
        return st, jnp.concatenate(ois, axis=0)

    def pairwise(qs, k, b):
        c = b - jnp.log2(k)
        tiles = lambda a, off: jnp.concatenate(
            [a[j * blk + off:j * blk + off + half] for j in range(per_grp)], axis=0)
        q_lo, q_hi, b_lo, b_hi = tiles(qs, 0), tiles(qs, half), tiles(b, 0), tiles(b, half)
        p_lo, p_hi = [], []
        for s in range(blk):
            cs = jnp.concatenate([jnp.broadcast_to(c[j * blk + s:j * blk + s + 1], (half, LANES))
                                  for j in range(per_grp)], axis=0)
            if s < half:
                p_lo.append(jnp.where(row_in_half >= s, q_lo * jnp.exp2(b_lo - cs), 0.0).astype(BF16))
                p_hi.append((q_hi * jnp.exp2(b_hi - cs)).astype(BF16))
            else:
                p_hi.append(jnp.where(row_in_half >= s - half, q_hi * jnp.exp2(b_hi - cs), 0.0).astype(BF16))
        return jnp.concatenate(p_lo, axis=1), jnp.concatenate(p_hi, axis=1)

    def emit(g, o):
        o = o * lax.rsqrt(jnp.mean(o * o, axis=-1, keepdims=True) + RMS_EPS)
        sl = slice(g * LANES, (g + 1) * LANES)
        o_ref[0, sl, :] = (o * ng_ref[...] * _sigmoid(g_ref[0, sl, :])).astype(o_ref.dtype)

    ngrp = tc // LANES
    gs = [gates(g) for g in range(ngrp)]
    incs = [increments(*gt) for gt in gs]

    a_s = [_dot_nt(qd, (k * jnp.exp2(-b)).astype(BF16))
           for (_, k, _, b), (qd, _, _) in zip(gs, incs)]
    o_s = [_dot(jnp.where(causal_blk, a, 0.0).astype(BF16), gt[2].astype(BF16)) for a, gt in zip(a_s, gs)]

    st = st_ref[...]
    o_inters = []
    for g, inc in enumerate(incs):
        st, oi = recur(st, *inc)
        o_inters.append(oi)
        emit(g, o_s[g] + oi)
    st_ref[...] = st

    b_min = functools.reduce(jnp.minimum, [jnp.min(gt[3]) for gt in gs])

    @pl.when(b_min < -HG_SAFE_LOG2)
    def _():
        pws = [pairwise(qs, k, b) for qs, k, _, b in gs]
        a_lo = _dot(jnp.concatenate([p[0] for p in pws], axis=0), e_ref[:half * LANES, :])
        a_hi = _dot(jnp.concatenate([p[1] for p in pws], axis=0), e_ref[...])
        for g in range(ngrp):
            rows = slice(g * nh, (g + 1) * nh)
            vb = gs[g][2].astype(BF16)
            o_lo = _dot(jnp.where(same_half, a_lo[rows], 0.0).astype(BF16), vb)
            o_hi = _dot(jnp.where(same_half, a_hi[rows], 0.0).astype(BF16), vb)
            emit(g, jnp.concatenate([x[j * half:(j + 1) * half] for j in range(per_grp) for x in (o_lo, o_hi)],
                                    axis=0) + o_inters[g])


def _hgrn(proj, lb_logits, norm_g, layer):
    b, t, _ = proj.shape
    tc = 1024
    h = HG_HEADS
    nslots = lb_logits.shape[0]
    col = lambda seg: (lambda i, j, s: (i, s, seg * h + j))
    fold = np.zeros((HG_BLK, LANES, LANES), np.float32)
    for s in range(HG_BLK):
        fold[s, :, s::HG_BLK] = 1.0
    fold = jnp.asarray(fold.reshape(HG_BLK * LANES, LANES), BF16)
    return pl.pallas_call(
        functools.partial(_hgrn_kernel, tc=tc, layer=layer),
        out_shape=jax.ShapeDtypeStruct((b, t, h * HG_D), BF16),
        grid=(b, h, t // tc),
        in_specs=[
            pl.BlockSpec((1, tc, HG_D), col(0)),
            pl.BlockSpec((1, tc, HG_D), col(1)),
            pl.BlockSpec((1, tc, HG_D), col(2)),
            pl.BlockSpec((1, tc, HG_D), col(3)),
            pl.BlockSpec((nslots, HG_D), lambda i, j, s: (0, j)),
            pl.BlockSpec((1, HG_D), lambda i, j, s: (0, j)),
            pl.BlockSpec(fold.shape, lambda i, j, s: (0, 0)),
        ],
        out_specs=pl.BlockSpec((1, tc, HG_D), lambda i, j, s: (i, s, j)),
        scratch_shapes=[pltpu.VMEM((HG_D, HG_D), F32)],
        compiler_params=pltpu.CompilerParams(
            dimension_semantics=("parallel", "parallel", "arbitrary"), vmem_limit_bytes=VMEM_LIMIT),
        name="hgrn",
    )(proj, proj, proj, proj, lb_logits, norm_g, fold)


def _cmp_kernel(h_ref, pos_ref, w1_ref, w2_ref, o_ref, *, key_side):
    h = h_ref[0]
    half = h.shape[1]
    pos = pos_ref[...]
    p0 = _dot((h + pos[:, :half]).astype(BF16), w1_ref[0, :half])
    p1 = _dot((h + pos[:, half:]).astype(BF16), w1_ref[0, half:])
    ng = h.shape[0]
    pre = p0 + pltpu.roll(p1, ng - 1, 0)
    gl = 0.5 * pre * (1.0 + jnp.tanh(np.sqrt(2.0 / np.pi) * (pre + 0.044715 * (pre * pre * pre))))
    gl = gl.astype(BF16)
    if key_side:
        cend = lax.broadcasted_iota(jnp.int32, (ng, LANES), 0) * CMP_STRIDE + (CMP_BLOCK - 1)
        pc = _pos_columns(cend, lax.broadcasted_iota(jnp.int32, (ng, LANES), 1))
        o_ref[0, 0] = (_dot(gl, w2_ref[...]) + pc).astype(BF16)
    else:
        vt = _dot_nt(w2_ref[...], gl)
        ones_row = lax.broadcasted_iota(jnp.int32, vt.shape, 0) == NSA_DH
        o_ref[0, 0] = jnp.where(ones_row, 1.0, vt).astype(BF16)


def _compress(hgrp, pos, w1, w2, key_side):
    b, ng, half = hgrp.shape
    g = w1.shape[0]
    out_tail = (ng, LANES) if key_side else (w2.shape[0], ng)
    return pl.pallas_call(
        functools.partial(_cmp_kernel, key_side=key_side),
        out_shape=jax.ShapeDtypeStruct((b, g) + out_tail, BF16),
        grid=(b, g),
        in_specs=[
            pl.BlockSpec((1, ng, half), lambda i, j: (i, 0, 0)),
            pl.BlockSpec(pos.shape, lambda i, j: (0, 0)),
            pl.BlockSpec((1,) + w1.shape[1:], lambda i, j: (j, 0, 0)),
            pl.BlockSpec(w2.shape, lambda i, j: (0, 0)),
        ],
        out_specs=pl.BlockSpec((1, 1) + out_tail, lambda i, j: (i, j, 0, 0)),
        compiler_params=pltpu.CompilerParams(
            dimension_semantics=("parallel", "parallel"), vmem_limit_bytes=VMEM_LIMIT),
        name="cmp_k" if key_side else "cmp_v",
    )(hgrp, pos, w1, w2)


def _nsa_kernel(q_ref, gate_ref, kc_ref, vct_ref, ks_ref, vst_ref, kw_ref, vwt_ref, ovt_ref, slope_ref,
                o_ref, bias_ref, *z_scratch):
    qb = pl.program_id(2)
    t0 = qb * Q_BLOCK
    npair = NSA_HG // 2
    za_refs, zb_refs = z_scratch[:npair], z_scratch[npair:]
    nq = 2 * Q_BLOCK
    qblk = q_ref[0].astype(F32)
    low = lax.broadcasted_iota(jnp.int32, (Q_BLOCK, LANES), 1) < NSA_DH
    qas = []
    for hp in range(npair):
        pair = qblk[:, hp * LANES:(hp + 1) * LANES]
        heads = [jnp.where(low, x, 0.0) for x in (pair, pltpu.roll(pair, NSA_DH, 1))]
        qas.append(jnp.concatenate(heads, axis=0).astype(BF16) + slope_ref[0, hp * nq:(hp + 1) * nq])
    tpos = t0 + (lax.broadcasted_iota(jnp.int32, (1, nq), 1) % Q_BLOCK)

    def normalised(acc):
        return acc[:NSA_DH] * (1.0 / acc[NSA_DH:NSA_DH + 1])

    per_step = SEL_TILE // SLC_BLOCK
    spos0 = lax.broadcasted_iota(jnp.int32, (SEL_TILE, 1), 0)
    n_slab = WINDOW // Q_BLOCK + 1
    slab_kt = [qb - (n_slab - 1) + i for i in range(n_slab)]
    slab_at = [pl.multiple_of(jnp.maximum(kt, 0) * Q_BLOCK, Q_BLOCK) for kt in slab_kt]
    kw = jnp.concatenate([kw_ref[0, pl.ds(s, Q_BLOCK), :] for s in slab_at], axis=0)

    def sel_scores(st, z_refs):
        ks = ks_ref[0, pl.ds(pl.multiple_of(st * SEL_TILE, SEL_TILE), SEL_TILE), :]
        for z_ref, qa in zip(z_refs, qas):
            z_ref[...] = _dot_nt(ks, qa)

    zcs = [_dot_nt(kc_ref[0, 0], qa) for qa in qas]
    zws = [_dot_nt(kw, qa) for qa in qas]

    ncp = kc_ref.shape[2]
    cend = lax.broadcasted_iota(jnp.int32, (ncp, 1), 0) * CMP_STRIDE + (CMP_BLOCK - 1)
    valid = cend <= tpos
    o_c, imp = [], None
    for zc in zcs:
        zm = jnp.where(valid, zc, NEG)
        mc = jnp.max(zm, axis=0, keepdims=True)
        eb = jnp.exp2(zm - mc).astype(BF16)
        acc = _dot(vct_ref[0, 0], eb)
        inv = jnp.where(mc > 0.5 * NEG, 1.0 / acc[NSA_DH:NSA_DH + 1], 0.0)
        o_c.append(acc[:NSA_DH] * inv)
        for i in range(2):
            sl = slice(i * Q_BLOCK, (i + 1) * Q_BLOCK)
            ih = _dot(ovt_ref[...], eb[:, sl]) * inv[:, sl]
            imp = ih if imp is None else imp + ih

    sel_scores(0, za_refs)

    ns = imp.shape[0]
    j_i = lax.broadcasted_iota(jnp.int32, (ns, 1), 0)
    t_abs = t0 + lax.broadcasted_iota(jnp.int32, (1, Q_BLOCK), 1)
    cur = t_abs // SLC_BLOCK
    forced = (j_i == 0) | (j_i == cur) | (j_i == cur - 1)
    x = jnp.where(forced, FORCE, imp)
    x = jnp.where(j_i * SLC_BLOCK <= t_abs, x, NEG)

    vw = jnp.concatenate([vwt_ref[0, :, pl.ds(s, Q_BLOCK)] for s in slab_at], axis=1)
    r_k = lax.broadcasted_iota(jnp.int32, (Q_BLOCK, nq), 0)
    r_q = lax.broadcasted_iota(jnp.int32, (Q_BLOCK, nq), 1) % Q_BLOCK
    o_w = []
    for zw in zws:
        slabs = []
        for i, kt in enumerate(slab_kt):
            zi = zw[i * Q_BLOCK:(i + 1) * Q_BLOCK]
            if i == 0:
                zi = jnp.where(r_k > r_q, zi, NEG)
            if i == n_slab - 1:
                zi = jnp.where(r_k <= r_q, zi, NEG)
            else:
                zi = zi + jnp.where(kt < 0, NEG, 0.0)
            slabs.append(zi)
        zw = jnp.concatenate(slabs, axis=0)
        pw = jnp.exp2(zw - jnp.max(zw, axis=0, keepdims=True))
        o_w.append(normalised(_dot(vw, pw.astype(BF16))))

    def rank_rows(n_rows):
        sub = lax.broadcasted_iota(jnp.int32, (SUBLANES, Q_BLOCK), 0)
        tiles = [x[v * SUBLANES:(v + 1) * SUBLANES] for v in range(n_rows // SUBLANES)]
        cnts = [jnp.zeros((SUBLANES, Q_BLOCK), F32) for _ in tiles]
        for jp in range(n_rows):
            r = x[jp:jp + 1, :]
            vj, rj = divmod(jp, SUBLANES)
            for v, xt in enumerate(tiles):
                if v < vj:
                    ahead = r > xt
                elif v > vj:
                    ahead = r >= xt
                else:
                    ahead = (r > xt) | ((r == xt) & (sub > rj))
                cnts[v] = cnts[v] + jnp.where(ahead, 1.0, 0.0)
        for v, cnt in enumerate(cnts):
            bias_ref[v * SUBLANES:(v + 1) * SUBLANES, :] = jnp.where(cnt < float(SLC_TOPN), 0.0, NEG)
        if n_rows < ns:
            bias_ref[n_rows:, :] = jnp.zeros((ns - n_rows, Q_BLOCK), F32)

    n_causal = (t0 + Q_BLOCK) // SLC_BLOCK
    sizes = [ns * (i + 1) // RANK_SPLITS for i in range(RANK_SPLITS)]
    lax.switch((n_causal * RANK_SPLITS - 1) // ns, [functools.partial(rank_rows, n) for n in sizes])

    def sel_update(st, z_refs, carry, causal):
        s0 = pl.multiple_of(st * SEL_TILE, SEL_TILE)
        vs = vst_ref[0, :, pl.ds(s0, SEL_TILE)]
        brow = bias_ref[pl.ds(pl.multiple_of(st * per_step, per_step), per_step), :]
        b2 = jnp.concatenate([brow] * 2, axis=1)[:, None, :]
        out = []
        for z_ref, (m, acc) in zip(z_refs, carry):
            z = (z_ref[...].reshape(per_step, SLC_BLOCK, nq) + b2).reshape(SEL_TILE, nq)
            if causal:
                z = jnp.where((s0 + spos0) <= tpos, z, NEG)
            mn = jnp.maximum(m, jnp.max(z, axis=0, keepdims=True))
            p = jnp.exp2(z - mn)
            out.append((mn, jnp.exp2(m - mn) * acc + _dot(vs, p.astype(BF16))))
        return tuple(out)

    def run_steps(first, count, carry, ends_causal):
        bufs = (za_refs, zb_refs)
        for i in range(count):
            if i + 1 < count or not ends_causal:
                sel_scores(first + i + 1, bufs[(i + 1) % 2])
            carry = sel_update(first + i, bufs[i % 2], carry, ends_causal and i + 1 == count)
        return carry

    init = tuple((jnp.full((1, nq), NEG, F32), jnp.zeros((V_SLOT, nq), F32)) for _ in range(npair))
    last = qb // (SEL_TILE // Q_BLOCK)
    n_loop = last // SEL_UNROLL
    carry = lax.fori_loop(0, n_loop, lambda i, c: run_steps(SEL_UNROLL * i, SEL_UNROLL, c, False), init)
    tails = [functools.partial(run_steps, SEL_UNROLL * n_loop, r + 1, ends_causal=True)
             for r in range(SEL_UNROLL)]
    carry = lax.switch(last % SEL_UNROLL, tails, carry)
    o_s = [normalised(acc) for _, acc in carry]

    gts = _sigmoid(gate_ref[0])
    pairs = []
    for hp in range(npair):
        halves = []
        for i in range(2):
            r = 3 * (2 * hp + i)
            sl = slice(i * Q_BLOCK, (i + 1) * Q_BLOCK)
            halves.append(gts[r:r + 1] * o_c[hp][:, sl] + gts[r + 1:r + 2] * o_s[hp][:, sl]
                          + gts[r + 2:r + 3] * o_w[hp][:, sl])
        pairs.append(jnp.concatenate(halves, axis=0).T)
    o_ref[0] = jnp.concatenate(pairs, axis=1).astype(o_ref.dtype)


def _nsa(q_slots, gate_t, kc_aug, vc_t, ks_aug, kw_aug, v_t, ov_t, slope_tab):
    b, t, _ = q_slots.shape
    g = NSA_GROUPS
    nqb = t // Q_BLOCK
    ncp = kc_aug.shape[2]
    ns = ov_t.shape[0]
    assert t % SEL_TILE == 0 and SEL_TILE % Q_BLOCK == 0 and t >= WINDOW + Q_BLOCK and WINDOW % Q_BLOCK == 0
    assert 3 * NSA_HG <= GATE_ROWS and NSA_HG % 2 == 0 and ns % (RANK_SPLITS * SUBLANES) == 0
    return pl.pallas_call(
        _nsa_kernel,
        out_shape=jax.ShapeDtypeStruct((b, t, NSA_HEADS * NSA_DH), BF16),
        grid=(b, g, nqb),
        in_specs=[
            pl.BlockSpec((1, Q_BLOCK, NSA_HG * NSA_DH), lambda i, j, s: (i, s, j)),
            pl.BlockSpec((1, GATE_ROWS, Q_BLOCK), lambda i, j, s: (i, j, s)),
            pl.BlockSpec((1, 1, ncp, LANES), lambda i, j, s: (i, j, 0, 0)),
            pl.BlockSpec((1, 1, V_SLOT, ncp), lambda i, j, s: (i, j, 0, 0)),
            pl.BlockSpec((1, t, LANES), lambda i, j, s: (i, 0, j)),
            pl.BlockSpec((1, V_SLOT, t), lambda i, j, s: (i, j, 0)),
            pl.BlockSpec((1, t, LANES), lambda i, j, s: (i, 0, j)),
            pl.BlockSpec((1, V_SLOT, t), lambda i, j, s: (i, g + j, 0)),
            pl.BlockSpec((ns, ncp), lambda i, j, s: (0, 0)),
            pl.BlockSpec((1, NSA_HG * Q_BLOCK, LANES), lambda i, j, s: (j, 0, 0)),
        ],
        out_specs=pl.BlockSpec((1, Q_BLOCK, NSA_HG * NSA_DH), lambda i, j, s: (i, s, j)),
        scratch_shapes=[pltpu.VMEM((ns, Q_BLOCK), F32)]
        + [pltpu.VMEM((SEL_TILE, 2 * Q_BLOCK), F32)] * NSA_HG,
        compiler_params=pltpu.CompilerParams(
            dimension_semantics=("parallel", "parallel", "arbitrary"), vmem_limit_bytes=VMEM_LIMIT),
        name="nsa",
    )(q_slots, gate_t, kc_aug, vc_t, ks_aug, v_t, kw_aug, v_t, ov_t, slope_tab)


def _layer_norm(z, g, b):
    mu = jnp.mean(z, axis=-1, keepdims=True)
    zc = z - mu
    var = jnp.mean(zc * zc, axis=-1, keepdims=True)
    return zc * lax.rsqrt(var + LN_EPS) * g + b


def _post_kernel(x_ref, yh_ref, yn_ref, mod_ref, wh_ref, wn_ref, w1_ref, w3_ref, w2_ref,
                 l1g_ref, l1b_ref, l2g_ref, l2b_ref, o_ref, *, alpha, chunks, row_parts):
    g1, sh2, sc2, g2 = mod_ref[0, 2], mod_ref[0, 3], mod_ref[0, 4], mod_ref[0, 5]
    tm = x_ref.shape[1]
    parts = [slice(i * tm // row_parts, (i + 1) * tm // row_parts) for i in range(row_parts)]
    mixes = [_dot(yh_ref[0, r, :], wh_ref[...]) + _dot(yn_ref[0, r, :], wn_ref[...]) for r in parts]
    x1s, ups = [], []
    for r, mix in zip(parts, mixes):
        x1 = _layer_norm(alpha * x_ref[0, r, :] + g1 * mix, l1g_ref[...], l1b_ref[...])
        h = (x1 * (1.0 + sc2) + sh2).astype(BF16)
        x1s.append(x1)
        ups.append([(_dot(h, w1_ref[:, lo:hi]), _dot(h, w3_ref[:, lo:hi])) for lo, hi in chunks])
    downs = []
    for up in ups:
        a = jnp.concatenate([(u * _sigmoid(u) * w).astype(BF16) for u, w in up], axis=1)
        downs.append(_dot(a, w2_ref[...]))
    for r, x1, down in zip(parts, x1s, downs):
        o_ref[0, r, :] = _layer_norm(alpha * x1 + g2 * down, l2g_ref[...], l2b_ref[...])


def _post(x, y_hg, y_nsa, ada4, w_h, w_n, w1, w3, w2, ln1_g, ln1_b, ln2_g, ln2_b, alpha):
    b, t, d = x.shape
    dff = w1.shape[1]
    tm = 1024
    step = 3 * MXU_DIM
    chunks = tuple((lo, min(lo + step, dff)) for lo in range(0, dff, step))
    resident = lambda a: pl.BlockSpec(a.shape, lambda i, j: (0,) * a.ndim, pipeline_mode=pl.Buffered(1))
    rows = lambda a: pl.BlockSpec((1, tm, a.shape[2]), lambda i, j: (i, j, 0))
    vec = pl.BlockSpec((1, d), lambda i, j: (0, 0))
    return pl.pallas_call(
        functools.partial(_post_kernel, alpha=alpha, chunks=chunks, row_parts=4),
        out_shape=jax.ShapeDtypeStruct((b, t, d), F32),
        grid=(b, t // tm),
        in_specs=[rows(x), rows(y_hg), rows(y_nsa),
                  pl.BlockSpec((1,) + ada4.shape[1:], lambda i, j: (i, 0, 0, 0)),
                  resident(w_h), resident(w_n), resident(w1), resident(w3), resident(w2),
                  vec, vec, vec, vec],
        out_specs=rows(x),
        compiler_params=pltpu.CompilerParams(
            dimension_semantics=("parallel", "parallel"), vmem_limit_bytes=FFN_VMEM_LIMIT),
        name="post",
    )(x, y_hg, y_nsa, ada4, w_h, w_n, w1, w3, w2, ln1_g, ln1_b, ln2_g, ln2_b)


def _slope_table(group_slopes):
    g, hg = group_slopes.shape
    rest = (group_slopes * LOG2E).astype(np.float32)
    tab = np.zeros((g, hg, Q_BLOCK, LANES), jnp.bfloat16)
    for i in range(N_SPLIT):
        piece = rest.astype(jnp.bfloat16)
        rest = rest - piece.astype(np.float32)
        tab[..., NSA_DH + 2 * i] = piece[:, :, None]
        tab[..., NSA_DH + 2 * i + 1] = piece[:, :, None]
    return tab.reshape(g, hg * Q_BLOCK, LANES)


def _pad_slots(w, n_slots, slot):
    d = w.shape[1]
    w3 = w.reshape(n_slots, -1, d)
    return jnp.pad(w3, ((0, 0), (0, slot - w3.shape[1]), (0, 0))).reshape(n_slots * slot, d)


def _cmp_w1_expand(w1, g):
    hid = w1.shape[1]
    w1r = w1.reshape(2, CMP_STRIDE, 1, NSA_DH, hid)
    eye = jnp.eye(g, dtype=w1.dtype).reshape(g, 1, 1, g, 1, 1)
    return (eye * w1r[None]).reshape(g, 2 * CMP_STRIDE * g * NSA_DH, hid)


def _overlap_t(n_cmp_pad, n_cmp, n_slc):
    cs = np.arange(n_cmp_pad) * CMP_STRIDE
    ss = np.arange(n_slc) * SLC_BLOCK
    ov = np.clip(np.minimum(cs[:, None] + CMP_BLOCK, ss[None] + SLC_BLOCK)
                 - np.maximum(cs[:, None], ss[None]), 0, None).astype(np.float32) / CMP_BLOCK
    ov[n_cmp:] = 0.0
    return ov.T


def kernel(x, c, w_ada, b_ada, w_in, hg_lb_logits, hg_norm_g, cmp_pos_k, cmp_w1_k, cmp_w2_k,
           cmp_pos_v, cmp_w1_v, cmp_w2_v, w_out, ln1_g, ln1_b, ffn_w1, ffn_w3, ffn_w2, ln2_g, ln2_b):
    bsz, t, d = x.shape
    depth = w_ada.shape[0]
    g, hg, dh = NSA_GROUPS, NSA_HG, NSA_DH
    hgw = HG_HEADS * HG_D
    nsw = NSA_HEADS * dh
    kvw = g * dh
    n_mod = 6
    alpha = (2.0 * depth) ** 0.25
    scale = dh ** -0.5
    n_grp = t // CMP_STRIDE
    n_cmp = (t - CMP_BLOCK) // CMP_STRIDE + 1
    n_slc = t // SLC_BLOCK

    slopes = (2.0 ** (-8.0 * (np.arange(NSA_HEADS) + 1) / NSA_HEADS)).reshape(g, hg)
    slope_tab = jnp.asarray(_slope_table(slopes), BF16)
    ov_t = jnp.asarray(_overlap_t(n_grp, n_cmp, n_slc), BF16)

    c_pad = jnp.pad(c, ((0, 8 - bsz), (0, 0)))
    for l in range(depth):
        ada = _ada(c_pad, w_ada[l], b_ada[l][None])
        ada3 = ada[:bsz].reshape(bsz * n_mod, 1, d)

        w = w_in[l].T
        o = 4 * hgw
        w_q = w[o:o + nsw]
        w_kv = [w[o + nsw + i * kvw: o + nsw + (i + 1) * kvw] for i in range(6)]
        w_g = w[o + nsw + 6 * kvw: o + nsw + 6 * kvw + 3 * NSA_HEADS]
        w_main = jnp.concatenate(
            [w[:o], w_q * (scale * LOG2E), _pad_slots(w_kv[2], g, LANES),
             _pad_slots(w_kv[4], g, LANES), w_kv[0], w_kv[1]], axis=0).astype(BF16)
        w_t = jnp.concatenate([_pad_slots(w_kv[3], g, V_SLOT), _pad_slots(w_kv[5], g, V_SLOT),
                               _pad_slots(w_g, g, GATE_ROWS)], axis=0).astype(BF16)
        widths = (o, nsw, g * LANES, kvw, 2 * g * V_SLOT, g * GATE_ROWS)
        hg_in, q_slots, ks_aug, kw_aug, kc, vc, v_t, gate_t = _inproj(x, ada3, w_main, w_t, n_mod, widths)

        y_hg = _hgrn(hg_in, hg_lb_logits, hg_norm_g[l][None], l)

        grp = lambda a: a.reshape(bsz, n_grp, CMP_STRIDE * kvw)
        pos_e = lambda p: jnp.broadcast_to(
            p.reshape(2, CMP_STRIDE, 1, dh), (2, CMP_STRIDE, g, dh)).reshape(1, -1)
        w2k = jnp.pad(cmp_w2_k[l], ((0, 0), (0, LANES - dh))).astype(BF16)
        kc_aug = _compress(grp(kc), pos_e(cmp_pos_k[l]), _cmp_w1_expand(cmp_w1_k[l], g).astype(BF16),
                           w2k, True)
        vc_t = _compress(grp(vc), pos_e(cmp_pos_v[l]), _cmp_w1_expand(cmp_w1_v[l], g).astype(BF16),
                         jnp.pad(cmp_w2_v[l].T, ((0, V_SLOT - dh), (0, 0))).astype(BF16), False)

        y_nsa = _nsa(q_slots, gate_t, kc_aug, vc_t, ks_aug, kw_aug, v_t, ov_t, slope_tab)

        w_o = w_out[l].astype(BF16)
        x = _post(x, y_hg, y_nsa, ada3.reshape(bsz, n_mod, 1, d), w_o[:hgw], w_o[hgw:],
                  ffn_w1[l].astype(BF16), ffn_w3[l].astype(BF16), ffn_w2[l].astype(BF16),
                  ln1_g[l][None], ln1_b[l][None], ln2_g[l][None], ln2_b[l][None], alpha)
    return x
```

```python
import functools

import numpy as np
import jax
import jax.numpy as jnp
from jax import lax
from jax.experimental import pallas as pl
from jax.experimental.pallas import tpu as pltpu

F32 = jnp.float32
BF16 = jnp.bfloat16

HG_HEADS = 4
HG_D = 128
NSA_HEADS = 8
NSA_GROUPS = 2
NSA_HG = NSA_HEADS // NSA_GROUPS
NSA_DH = 64
CMP_BLOCK = 32
CMP_STRIDE = 16
CMP_HIDDEN = 256
SLC_BLOCK = 64
SLC_TOPN = 16
WINDOW = 512
Q_BLOCK = 256
NEG = -1e30
FORCE = 1e4
LN_EPS = 1e-5
RMS_EPS = 1e-6

LANES = 128
SUBLANES = 8
HG_BLK = 32
HG_SAFE_LOG2 = 100.0
POS_SPLIT = 256
GATE_ROWS = 16
SEL_TILE = 512
SEL_UNROLL = 8
RANK_SPLITS = 8
V_SLOT = 80
N_SPLIT = 3
LOG2E = 1.4426950408889634
MXU_DIM = 256
MIB = 1024 * 1024
ADA_VMEM_LIMIT = 12 * MIB
INPROJ_VMEM_LIMIT = 32 * MIB
HGRN_VMEM_LIMIT = 16 * MIB
CMP_VMEM_LIMIT = 8 * MIB
NSA_VMEM_LIMIT = 24 * MIB
FFN_VMEM_LIMIT = 56 * MIB


def _sigmoid(x):
    return 1.0 / (1.0 + jnp.exp(-x))


def _dot(a, b):
    return jnp.dot(a, b, preferred_element_type=F32)


def _dot_nt(a, b):
    return lax.dot_general(a, b, (((1,), (1,)), ((), ())), preferred_element_type=F32)


def _pos_columns(pos, col):
    c = col - NSA_DH
    lo = pos % POS_SPLIT
    val = jnp.where(c % 2 == 0, lo, pos - lo)
    return jnp.where((c >= 0) & (c < 2 * N_SPLIT), val, 0).astype(F32)


def _split2(x):
    hi = x.astype(BF16)
    return hi, (x - hi.astype(F32)).astype(BF16)


def _ada_kernel(c_ref, w_ref, b_ref, o_ref):
    c = c_ref[...]
    s = (c * _sigmoid(c)).astype(BF16)
    o_ref[...] = _dot(s, w_ref[...].astype(BF16)) + b_ref[...]


def _ada(c_pad, w, b):
    rows, d = c_pad.shape
    n = w.shape[1]
    tn = 1024
    return pl.pallas_call(
        _ada_kernel,
        out_shape=jax.ShapeDtypeStruct((rows, n), F32),
        grid=(n // tn,),
        in_specs=[
            pl.BlockSpec((rows, d), lambda j: (0, 0)),
            pl.BlockSpec((d, tn), lambda j: (0, j)),
            pl.BlockSpec((1, tn), lambda j: (0, j)),
        ],
        out_specs=pl.BlockSpec((rows, tn), lambda j: (0, j)),
        compiler_params=pltpu.CompilerParams(
            dimension_semantics=("parallel",), vmem_limit_bytes=ADA_VMEM_LIMIT),
        name="ada",
    )(c_pad, w, b)


def _inproj_kernel(x_ref, sc_ref, sh_ref, wm_ref, wt_ref,
                   hg_ref, q_ref, ks_ref, kw_ref, kc_ref, vc_ref, vt_ref, gt_ref, *, tm):
    h = (x_ref[0] * (1.0 + sc_ref[0]) + sh_ref[0]).astype(BF16)
    y = _dot(h, wm_ref[...])
    n_hg = hg_ref.shape[2]
    n_q = q_ref.shape[2]
    n_k = ks_ref.shape[2]
    n_c = kc_ref.shape[2]
    o = 0
    hg_ref[0] = y[:, o:o + n_hg]
    o += n_hg
    q_ref[0] = y[:, o:o + n_q].astype(BF16)
    o += n_q
    row = pl.program_id(1) * tm + lax.broadcasted_iota(jnp.int32, (tm, n_k), 0)
    pc = _pos_columns(row, lax.broadcasted_iota(jnp.int32, (tm, n_k), 1) % LANES)
    ks_ref[0] = (y[:, o:o + n_k] + pc).astype(BF16)
    o += n_k
    kw_ref[0] = (y[:, o:o + n_k] + pc).astype(BF16)
    o += n_k
    kc_ref[0] = y[:, o:o + n_c]
    o += n_c
    vc_ref[0] = y[:, o:o + n_c]
    yt = _dot_nt(wt_ref[...], h)
    n_vt = vt_ref.shape[1]
    ri = lax.broadcasted_iota(jnp.int32, (n_vt, tm), 0) % V_SLOT
    vt_ref[0] = (yt[:n_vt] + jnp.where(ri == NSA_DH, 1.0, 0.0)).astype(BF16)
    gt_ref[0] = yt[n_vt:]


def _inproj(x, ada3, w_main, w_t, n_mod, widths):
    b, t, d = x.shape
    n_hg, n_q, n_k, n_c, n_vt, n_gt = widths
    tm = 512
    row_spec = lambda n: pl.BlockSpec((1, tm, n), lambda i, j: (i, j, 0))
    col_spec = lambda n: pl.BlockSpec((1, n, tm), lambda i, j: (i, 0, j))
    sds = jax.ShapeDtypeStruct
    return pl.pallas_call(
        functools.partial(_inproj_kernel, tm=tm),
        out_shape=(sds((b, t, n_hg), F32), sds((b, t, n_q), BF16), sds((b, t, n_k), BF16),
                   sds((b, t, n_k), BF16), sds((b, t, n_c), F32), sds((b, t, n_c), F32),
                   sds((b, n_vt, t), BF16), sds((b, n_gt, t), F32)),
        grid=(b, t // tm),
        in_specs=[
            pl.BlockSpec((1, tm, d), lambda i, j: (i, j, 0)),
            pl.BlockSpec((1, 1, d), lambda i, j: (i * n_mod + 1, 0, 0)),
            pl.BlockSpec((1, 1, d), lambda i, j: (i * n_mod, 0, 0)),
            pl.BlockSpec(w_main.shape, lambda i, j: (0, 0), pipeline_mode=pl.Buffered(1)),
            pl.BlockSpec(w_t.shape, lambda i, j: (0, 0), pipeline_mode=pl.Buffered(1)),
        ],
        out_specs=(row_spec(n_hg), row_spec(n_q), row_spec(n_k), row_spec(n_k), row_spec(n_c),
                   row_spec(n_c), col_spec(n_vt), col_spec(n_gt)),
        compiler_params=pltpu.CompilerParams(
            dimension_semantics=("parallel", "parallel"), vmem_limit_bytes=INPROJ_VMEM_LIMIT),
        name="inproj",
    )(x, ada3, ada3, w_main, w_t)


def _hgrn_kernel(q_ref, f_ref, i_ref, g_ref, lbl_ref, ng_ref, e_ref, o_ref, st_ref, *, tc, layer):
    blk = HG_BLK
    per_grp = LANES // blk
    half = blk // 2

    @pl.when(pl.program_id(2) == 0)
    def _():
        st_ref[...] = jnp.zeros_like(st_ref)

    lbl = lbl_ref[...]
    e = jnp.exp(lbl - jnp.max(lbl, axis=0, keepdims=True))
    lb = jnp.sum(e[: layer + 1], axis=0, keepdims=True) / jnp.sum(e, axis=0, keepdims=True)

    r_i = lax.broadcasted_iota(jnp.int32, (LANES, LANES), 0)
    c_i = lax.broadcasted_iota(jnp.int32, (LANES, LANES), 1)
    same_blk = (r_i // blk) == (c_i // blk)
    causal_blk = same_blk & (c_i <= r_i)
    tri = jnp.where(causal_blk, 1.0, 0.0).astype(BF16)
    row_blk = r_i // blk
    nh = per_grp * half
    r_h = lax.broadcasted_iota(jnp.int32, (nh, LANES), 0)
    same_half = (r_h // half) == (lax.broadcasted_iota(jnp.int32, (nh, LANES), 1) // blk)
    row_in_half = r_h % half
    blocked = lambda a: a.reshape(per_grp, blk, LANES)

    def gates(g):
        sl = slice(g * LANES, (g + 1) * LANES)
        q = q_ref[0, sl, :]
        qs = q * _sigmoid(q)
        f = lb + (1.0 - lb) * _sigmoid(f_ref[0, sl, :])
        hi, lo = _split2(jnp.log2(f))
        b = _dot(tri, hi) + _dot(tri, lo)
        return qs, 1.0 - f, i_ref[0, sl, :], b

    def increments(qs, k, v, b):
        b3 = blocked(b)
        bl3 = b3[:, blk - 1:blk, :]
        qd = (qs * jnp.exp2(b)).astype(BF16)
        kd = (blocked(k) * jnp.exp2(bl3 - b3)).reshape(LANES, LANES)
        kexp = jnp.concatenate(
            [jnp.where(row_blk == j, kd, 0.0) for j in range(per_grp)], axis=1).astype(BF16)
        return qd, jnp.exp2(bl3), _dot(v.T.astype(BF16), kexp)

    def recur(st, qd, dl, ut):
        ois = []
        for j in range(per_grp):
            ois.append(_dot_nt(qd[j * blk:(j + 1) * blk], st.astype(BF16)))
            st = dl[j] * st + ut[:, j * LANES:(j + 1) * LANES]
        return st, jnp.concatenate(ois, axis=0)

    def pairwise(qs, k, b):
        c = b - jnp.log2(k)
        tiles = lambda a, off: jnp.concatenate(
            [a[j * blk + off:j * blk + off + half] for j in range(per_grp)], axis=0)
        q_lo, q_hi, b_lo, b_hi = tiles(qs, 0), tiles(qs, half), tiles(b, 0), tiles(b, half)
        p_lo, p_hi = [], []
        for s in range(blk):
            cs = jnp.concatenate([jnp.broadcast_to(c[j * blk + s:j * blk + s + 1], (half, LANES))
                                  for j in range(per_grp)], axis=0)
            if s < half:
                p_lo.append(jnp.where(row_in_half >= s, q_lo * jnp.exp2(b_lo - cs), 0.0).astype(BF16))
                p_hi.append((q_hi * jnp.exp2(b_hi - cs)).astype(BF16))
            else:
                p_hi.append(jnp.where(row_in_half >= s - half, q_hi * jnp.exp2(b_hi - cs), 0.0).astype(BF16))
        return jnp.concatenate(p_lo, axis=1), jnp.concatenate(p_hi, axis=1)

    def emit(g, o):
        o = o * lax.rsqrt(jnp.mean(o * o, axis=-1, keepdims=True) + RMS_EPS)
        sl = slice(g * LANES, (g + 1) * LANES)
        o_ref[0, sl, :] = (o * ng_ref[...] * _sigmoid(g_ref[0, sl, :])).astype(o_ref.dtype)

    ngrp = tc // LANES
    gs = [gates(g) for g in range(ngrp)]
    incs = [increments(*gt) for gt in gs]

    a_s = [_dot_nt(qd, (k * jnp.exp2(-b)).astype(BF16))
           for (_, k, _, b), (qd, _, _) in zip(gs, incs)]
    o_s = [_dot(jnp.where(causal_blk, a, 0.0).astype(BF16), gt[2].astype(BF16)) for a, gt in zip(a_s, gs)]

    st = st_ref[...]
    o_inters = []
    for g, inc in enumerate(incs):
        st, oi = recur(st, *inc)
        o_inters.append(oi)
        emit(g, o_s[g] + oi)
    st_ref[...] = st

    b_min = functools.reduce(jnp.minimum, [jnp.min(gt[3]) for gt in gs])

    @pl.when(b_min < -HG_SAFE_LOG2)
    def _():
        pws = [pairwise(qs, k, b) for qs, k, _, b in gs]
        a_lo = _dot(jnp.concatenate([p[0] for p in pws], axis=0), e_ref[:half * LANES, :])
        a_hi = _dot(jnp.concatenate([p[1] for p in pws], axis=0), e_ref[...])
        for g in range(ngrp):
            rows = slice(g * nh, (g + 1) * nh)
            vb = gs[g][2].astype(BF16)
            o_lo = _dot(jnp.where(same_half, a_lo[rows], 0.0).astype(BF16), vb)
            o_hi = _dot(jnp.where(same_half, a_hi[rows], 0.0).astype(BF16), vb)
            emit(g, jnp.concatenate([x[j * half:(j + 1) * half] for j in range(per_grp) for x in (o_lo, o_hi)],
                                    axis=0) + o_inters[g])


def _hgrn(proj, lb_logits, norm_g, layer):
    b, t, _ = proj.shape
    tc = 1024
    h = HG_HEADS
    nslots = lb_logits.shape[0]
    col = lambda seg: (lambda i, j, s: (i, s, seg * h + j))
    fold = np.zeros((HG_BLK, LANES, LANES), np.float32)
    for s in range(HG_BLK):
        fold[s, :, s::HG_BLK] = 1.0
    fold = jnp.asarray(fold.reshape(HG_BLK * LANES, LANES), BF16)
    return pl.pallas_call(
        functools.partial(_hgrn_kernel, tc=tc, layer=layer),
        out_shape=jax.ShapeDtypeStruct((b, t, h * HG_D), BF16),
        grid=(b, h, t // tc),
        in_specs=[
            pl.BlockSpec((1, tc, HG_D), col(0)),
            pl.BlockSpec((1, tc, HG_D), col(1)),
            pl.BlockSpec((1, tc, HG_D), col(2)),
            pl.BlockSpec((1, tc, HG_D), col(3)),
            pl.BlockSpec((nslots, HG_D), lambda i, j, s: (0, j)),
            pl.BlockSpec((1, HG_D), lambda i, j, s: (0, j)),
            pl.BlockSpec(fold.shape, lambda i, j, s: (0, 0)),
        ],
        out_specs=pl.BlockSpec((1, tc, HG_D), lambda i, j, s: (i, s, j)),
        scratch_shapes=[pltpu.VMEM((HG_D, HG_D), F32)],
        compiler_params=pltpu.CompilerParams(
            dimension_semantics=("parallel", "parallel", "arbitrary"), vmem_limit_bytes=HGRN_VMEM_LIMIT),
        name="hgrn",
    )(proj, proj, proj, proj, lb_logits, norm_g, fold)


def _cmp_kernel(h_ref, pos_ref, w1_ref, w2_ref, o_ref, *, key_side):
    h = h_ref[0]
    half = h.shape[1]
    pos = pos_ref[...]
    p0 = _dot((h + pos[:, :half]).astype(BF16), w1_ref[0, :half])
    p1 = _dot((h + pos[:, half:]).astype(BF16), w1_ref[0, half:])
    ng = h.shape[0]
    pre = p0 + pltpu.roll(p1, ng - 1, 0)
    gl = 0.5 * pre * (1.0 + jnp.tanh(np.sqrt(2.0 / np.pi) * (pre + 0.044715 * (pre * pre * pre))))
    gl = gl.astype(BF16)
    if key_side:
        cend = lax.broadcasted_iota(jnp.int32, (ng, LANES), 0) * CMP_STRIDE + (CMP_BLOCK - 1)
        pc = _pos_columns(cend, lax.broadcasted_iota(jnp.int32, (ng, LANES), 1))
        o_ref[0, 0] = (_dot(gl, w2_ref[...]) + pc).astype(BF16)
    else:
        vt = _dot_nt(w2_ref[...], gl)
        ones_row = lax.broadcasted_iota(jnp.int32, vt.shape, 0) == NSA_DH
        o_ref[0, 0] = jnp.where(ones_row, 1.0, vt).astype(BF16)


def _compress(hgrp, pos, w1, w2, key_side):
    b, ng, half = hgrp.shape
    g = w1.shape[0]
    out_tail = (ng, LANES) if key_side else (w2.shape[0], ng)
    return pl.pallas_call(
        functools.partial(_cmp_kernel, key_side=key_side),
        out_shape=jax.ShapeDtypeStruct((b, g) + out_tail, BF16),
        grid=(b, g),
        in_specs=[
            pl.BlockSpec((1, ng, half), lambda i, j: (i, 0, 0)),
            pl.BlockSpec(pos.shape, lambda i, j: (0, 0)),
            pl.BlockSpec((1,) + w1.shape[1:], lambda i, j: (j, 0, 0)),
            pl.BlockSpec(w2.shape, lambda i, j: (0, 0)),
        ],
        out_specs=pl.BlockSpec((1, 1) + out_tail, lambda i, j: (i, j, 0, 0)),
        compiler_params=pltpu.CompilerParams(
            dimension_semantics=("parallel", "parallel"), vmem_limit_bytes=CMP_VMEM_LIMIT),
        name="cmp_k" if key_side else "cmp_v",
    )(hgrp, pos, w1, w2)


def _nsa_kernel(q_ref, gate_ref, kc_ref, vct_ref, ks_ref, vst_ref, kw_ref, vwt_ref, ovt_ref, slope_ref,
                o_ref, bias_ref, *z_scratch):
    qb = pl.program_id(2)
    t0 = qb * Q_BLOCK
    npair = NSA_HG // 2
    za_refs, zb_refs = z_scratch[:npair], z_scratch[npair:]
    nq = 2 * Q_BLOCK
    qblk = q_ref[0].astype(F32)
    low = lax.broadcasted_iota(jnp.int32, (Q_BLOCK, LANES), 1) < NSA_DH
    qas = []
    for hp in range(npair):
        pair = qblk[:, hp * LANES:(hp + 1) * LANES]
        heads = [jnp.where(low, x, 0.0) for x in (pair, pltpu.roll(pair, NSA_DH, 1))]
        qas.append(jnp.concatenate(heads, axis=0).astype(BF16) + slope_ref[0, hp * nq:(hp + 1) * nq])
    tpos = t0 + (lax.broadcasted_iota(jnp.int32, (1, nq), 1) % Q_BLOCK)

    def normalised(acc):
        return acc[:NSA_DH] * (1.0 / acc[NSA_DH:NSA_DH + 1])

    per_step = SEL_TILE // SLC_BLOCK
    spos0 = lax.broadcasted_iota(jnp.int32, (SEL_TILE, 1), 0)
    n_slab = WINDOW // Q_BLOCK + 1
    slab_kt = [qb - (n_slab - 1) + i for i in range(n_slab)]
    slab_at = [pl.multiple_of(jnp.maximum(kt, 0) * Q_BLOCK, Q_BLOCK) for kt in slab_kt]
    kw = jnp.concatenate([kw_ref[0, pl.ds(s, Q_BLOCK), :] for s in slab_at], axis=0)

    def sel_scores(st, z_refs):
        ks = ks_ref[0, pl.ds(pl.multiple_of(st * SEL_TILE, SEL_TILE), SEL_TILE), :]
        for z_ref, qa in zip(z_refs, qas):
            z_ref[...] = _dot_nt(ks, qa)

    zcs = [_dot_nt(kc_ref[0, 0], qa) for qa in qas]
    zws = [_dot_nt(kw, qa) for qa in qas]

    ncp = kc_ref.shape[2]
    cend = lax.broadcasted_iota(jnp.int32, (ncp, 1), 0) * CMP_STRIDE + (CMP_BLOCK - 1)
    valid = cend <= tpos
    o_c, imp = [], None
    for zc in zcs:
        zm = jnp.where(valid, zc, NEG)
        mc = jnp.max(zm, axis=0, keepdims=True)
        eb = jnp.exp2(zm - mc).astype(BF16)
        acc = _dot(vct_ref[0, 0], eb)
        inv = jnp.where(mc > 0.5 * NEG, 1.0 / acc[NSA_DH:NSA_DH + 1], 0.0)
        o_c.append(acc[:NSA_DH] * inv)
        for i in range(2):
            sl = slice(i * Q_BLOCK, (i + 1) * Q_BLOCK)
            ih = _dot(ovt_ref[...], eb[:, sl]) * inv[:, sl]
            imp = ih if imp is None else imp + ih

    sel_scores(0, za_refs)

    ns = imp.shape[0]
    j_i = lax.broadcasted_iota(jnp.int32, (ns, 1), 0)
    t_abs = t0 + lax.broadcasted_iota(jnp.int32, (1, Q_BLOCK), 1)
    cur = t_abs // SLC_BLOCK
    forced = (j_i == 0) | (j_i == cur) | (j_i == cur - 1)
    x = jnp.where(forced, FORCE, imp)
    x = jnp.where(j_i * SLC_BLOCK <= t_abs, x, NEG)

    vw = jnp.concatenate([vwt_ref[0, :, pl.ds(s, Q_BLOCK)] for s in slab_at], axis=1)
    r_k = lax.broadcasted_iota(jnp.int32, (Q_BLOCK, nq), 0)
    r_q = lax.broadcasted_iota(jnp.int32, (Q_BLOCK, nq), 1) % Q_BLOCK
    o_w = []
    for zw in zws:
        slabs = []
        for i, kt in enumerate(slab_kt):
            zi = zw[i * Q_BLOCK:(i + 1) * Q_BLOCK]
            if i == 0:
                zi = jnp.where(r_k > r_q, zi, NEG)
            if i == n_slab - 1:
                zi = jnp.where(r_k <= r_q, zi, NEG)
            else:
                zi = zi + jnp.where(kt < 0, NEG, 0.0)
            slabs.append(zi)
        zw = jnp.concatenate(slabs, axis=0)
        pw = jnp.exp2(zw - jnp.max(zw, axis=0, keepdims=True))
        o_w.append(normalised(_dot(vw, pw.astype(BF16))))

    def rank_rows(n_rows):
        sub = lax.broadcasted_iota(jnp.int32, (SUBLANES, Q_BLOCK), 0)
        tiles = [x[v * SUBLANES:(v + 1) * SUBLANES] for v in range(n_rows // SUBLANES)]
        cnts = [jnp.zeros((SUBLANES, Q_BLOCK), F32) for _ in tiles]
        for jp in range(n_rows):
            r = x[jp:jp + 1, :]
            vj, rj = divmod(jp, SUBLANES)
            for v, xt in enumerate(tiles):
                if v < vj:
                    ahead = r > xt
                elif v > vj:
                    ahead = r >= xt
                else:
                    ahead = (r > xt) | ((r == xt) & (sub > rj))
                cnts[v] = cnts[v] + jnp.where(ahead, 1.0, 0.0)
        for v, cnt in enumerate(cnts):
            bias_ref[v * SUBLANES:(v + 1) * SUBLANES, :] = jnp.where(cnt < float(SLC_TOPN), 0.0, NEG)
        if n_rows < ns:
            bias_ref[n_rows:, :] = jnp.zeros((ns - n_rows, Q_BLOCK), F32)

    n_causal = (t0 + Q_BLOCK) // SLC_BLOCK
    sizes = [ns * (i + 1) // RANK_SPLITS for i in range(RANK_SPLITS)]
    lax.switch((n_causal * RANK_SPLITS - 1) // ns, [functools.partial(rank_rows, n) for n in sizes])

    def sel_update(st, z_refs, carry, causal):
        s0 = pl.multiple_of(st * SEL_TILE, SEL_TILE)
        vs = vst_ref[0, :, pl.ds(s0, SEL_TILE)]
        brow = bias_ref[pl.ds(pl.multiple_of(st * per_step, per_step), per_step), :]
        b2 = jnp.concatenate([brow] * 2, axis=1)[:, None, :]
        out = []
        for z_ref, (m, acc) in zip(z_refs, carry):
            z = (z_ref[...].reshape(per_step, SLC_BLOCK, nq) + b2).reshape(SEL_TILE, nq)
            if causal:
                z = jnp.where((s0 + spos0) <= tpos, z, NEG)
            mn = jnp.maximum(m, jnp.max(z, axis=0, keepdims=True))
            p = jnp.exp2(z - mn)
            out.append((mn, jnp.exp2(m - mn) * acc + _dot(vs, p.astype(BF16))))
        return tuple(out)

    def run_steps(first, count, carry, ends_causal):
        bufs = (za_refs, zb_refs)
        for i in range(count):
            if i + 1 < count or not ends_causal:
                sel_scores(first + i + 1, bufs[(i + 1) % 2])
            carry = sel_update(first + i, bufs[i % 2], carry, ends_causal and i + 1 == count)
        return carry

    init = tuple((jnp.full((1, nq), NEG, F32), jnp.zeros((V_SLOT, nq), F32)) for _ in range(npair))
    last = qb // (SEL_TILE // Q_BLOCK)
    n_loop = last // SEL_UNROLL
    carry = lax.fori_loop(0, n_loop, lambda i, c: run_steps(SEL_UNROLL * i, SEL_UNROLL, c, False), init)
    tails = [functools.partial(run_steps, SEL_UNROLL * n_loop, r + 1, ends_causal=True)
             for r in range(SEL_UNROLL)]
    carry = lax.switch(last % SEL_UNROLL, tails, carry)
    o_s = [normalised(acc) for _, acc in carry]

    gts = _sigmoid(gate_ref[0])
    pairs = []
    for hp in range(npair):
        halves = []
        for i in range(2):
            r = 3 * (2 * hp + i)
            sl = slice(i * Q_BLOCK, (i + 1) * Q_BLOCK)
            halves.append(gts[r:r + 1] * o_c[hp][:, sl] + gts[r + 1:r + 2] * o_s[hp][:, sl]
                          + gts[r + 2:r + 3] * o_w[hp][:, sl])
        pairs.append(jnp.concatenate(halves, axis=0).T)
    o_ref[0] = jnp.concatenate(pairs, axis=1).astype(o_ref.dtype)


def _nsa(q_slots, gate_t, kc_aug, vc_t, ks_aug, kw_aug, v_t, ov_t, slope_tab):
    b, t, _ = q_slots.shape
    g = NSA_GROUPS
    nqb = t // Q_BLOCK
    ncp = kc_aug.shape[2]
    ns = ov_t.shape[0]
    assert t % SEL_TILE == 0 and SEL_TILE % Q_BLOCK == 0 and t >= WINDOW + Q_BLOCK and WINDOW % Q_BLOCK == 0
    assert 3 * NSA_HG <= GATE_ROWS and NSA_HG % 2 == 0 and ns % (RANK_SPLITS * SUBLANES) == 0
    return pl.pallas_call(
        _nsa_kernel,
        out_shape=jax.ShapeDtypeStruct((b, t, NSA_HEADS * NSA_DH), BF16),
        grid=(b, g, nqb),
        in_specs=[
            pl.BlockSpec((1, Q_BLOCK, NSA_HG * NSA_DH), lambda i, j, s: (i, s, j)),
            pl.BlockSpec((1, GATE_ROWS, Q_BLOCK), lambda i, j, s: (i, j, s)),
            pl.BlockSpec((1, 1, ncp, LANES), lambda i, j, s: (i, j, 0, 0)),
            pl.BlockSpec((1, 1, V_SLOT, ncp), lambda i, j, s: (i, j, 0, 0)),
            pl.BlockSpec((1, t, LANES), lambda i, j, s: (i, 0, j)),
            pl.BlockSpec((1, V_SLOT, t), lambda i, j, s: (i, j, 0)),
            pl.BlockSpec((1, t, LANES), lambda i, j, s: (i, 0, j)),
            pl.BlockSpec((1, V_SLOT, t), lambda i, j, s: (i, g + j, 0)),
            pl.BlockSpec((ns, ncp), lambda i, j, s: (0, 0)),
            pl.BlockSpec((1, NSA_HG * Q_BLOCK, LANES), lambda i, j, s: (j, 0, 0)),
        ],
        out_specs=pl.BlockSpec((1, Q_BLOCK, NSA_HG * NSA_DH), lambda i, j, s: (i, s, j)),
        scratch_shapes=[pltpu.VMEM((ns, Q_BLOCK), F32)]
        + [pltpu.VMEM((SEL_TILE, 2 * Q_BLOCK), F32)] * NSA_HG,
        compiler_params=pltpu.CompilerParams(
            dimension_semantics=("parallel", "parallel", "arbitrary"), vmem_limit_bytes=NSA_VMEM_LIMIT),
        name="nsa",
    )(q_slots, gate_t, kc_aug, vc_t, ks_aug, v_t, kw_aug, v_t, ov_t, slope_tab)


def _layer_norm(z, g, b):
    mu = jnp.mean(z, axis=-1, keepdims=True)
    zc = z - mu
    var = jnp.mean(zc * zc, axis=-1, keepdims=True)
    return zc * lax.rsqrt(var + LN_EPS) * g + b


def _post_kernel(x_ref, yh_ref, yn_ref, mod_ref, wh_ref, wn_ref, w1_ref, w3_ref, w2_ref,
                 l1g_ref, l1b_ref, l2g_ref, l2b_ref, o_ref, *, alpha, chunks, row_parts):
    g1, sh2, sc2, g2 = mod_ref[0, 2], mod_ref[0, 3], mod_ref[0, 4], mod_ref[0, 5]
    tm = x_ref.shape[1]
    parts = [slice(i * tm // row_parts, (i + 1) * tm // row_parts) for i in range(row_parts)]
    mixes = [_dot(yh_ref[0, r, :], wh_ref[...]) + _dot(yn_ref[0, r, :], wn_ref[...]) for r in parts]
    x1s, ups = [], []
    for r, mix in zip(parts, mixes):
        x1 = _layer_norm(alpha * x_ref[0, r, :] + g1 * mix, l1g_ref[...], l1b_ref[...])
        h = (x1 * (1.0 + sc2) + sh2).astype(BF16)
        x1s.append(x1)
        ups.append([(_dot(h, w1_ref[:, lo:hi]), _dot(h, w3_ref[:, lo:hi])) for lo, hi in chunks])
    downs = []
    for up in ups:
        a = jnp.concatenate([(u * _sigmoid(u) * w).astype(BF16) for u, w in up], axis=1)
        downs.append(_dot(a, w2_ref[...]))
    for r, x1, down in zip(parts, x1s, downs):
        o_ref[0, r, :] = _layer_norm(alpha * x1 + g2 * down, l2g_ref[...], l2b_ref[...])


def _post(x, y_hg, y_nsa, ada4, w_h, w_n, w1, w3, w2, ln1_g, ln1_b, ln2_g, ln2_b, alpha):
    b, t, d = x.shape
    dff = w1.shape[1]
    tm = 1024
    step = 3 * MXU_DIM
    chunks = tuple((lo, min(lo + step, dff)) for lo in range(0, dff, step))
    resident = lambda a: pl.BlockSpec(a.shape, lambda i, j: (0,) * a.ndim, pipeline_mode=pl.Buffered(1))
    rows = lambda a: pl.BlockSpec((1, tm, a.shape[2]), lambda i, j: (i, j, 0))
    vec = pl.BlockSpec((1, d), lambda i, j: (0, 0))
    return pl.pallas_call(
        functools.partial(_post_kernel, alpha=alpha, chunks=chunks, row_parts=4),
        out_shape=jax.ShapeDtypeStruct((b, t, d), F32),
        grid=(b, t // tm),
        in_specs=[rows(x), rows(y_hg), rows(y_nsa),
                  pl.BlockSpec((1,) + ada4.shape[1:], lambda i, j: (i, 0, 0, 0)),
                  resident(w_h), resident(w_n), resident(w1), resident(w3), resident(w2),
                  vec, vec, vec, vec],
        out_specs=rows(x),
        compiler_params=pltpu.CompilerParams(
            dimension_semantics=("parallel", "parallel"), vmem_limit_bytes=FFN_VMEM_LIMIT),
        name="post",
    )(x, y_hg, y_nsa, ada4, w_h, w_n, w1, w3, w2, ln1_g, ln1_b, ln2_g, ln2_b)


def _slope_table(group_slopes):
    g, hg = group_slopes.shape
    rest = (group_slopes * LOG2E).astype(np.float32)
    tab = np.zeros((g, hg, Q_BLOCK, LANES), jnp.bfloat16)
    for i in range(N_SPLIT):
        piece = rest.astype(jnp.bfloat16)
        rest = rest - piece.astype(np.float32)
        tab[..., NSA_DH + 2 * i] = piece[:, :, None]
        tab[..., NSA_DH + 2 * i + 1] = piece[:, :, None]
    return tab.reshape(g, hg * Q_BLOCK, LANES)


def _pad_slots(w, n_slots):
    d = w.shape[0]
    w3 = w.reshape(d, n_slots, -1)
    return jnp.pad(w3, ((0, 0), (0, 0), (0, LANES - w3.shape[2]))).reshape(d, n_slots * LANES)


def _cmp_w1_expand(w1, g):
    hid = w1.shape[1]
    w1r = w1.reshape(2, CMP_STRIDE, 1, NSA_DH, hid)
    eye = jnp.eye(g, dtype=w1.dtype).reshape(g, 1, 1, g, 1, 1)
    return (eye * w1r[None]).reshape(g, 2 * CMP_STRIDE * g * NSA_DH, hid)


def _overlap_t(n_cmp_pad, n_cmp, n_slc):
    cs = np.arange(n_cmp_pad) * CMP_STRIDE
    ss = np.arange(n_slc) * SLC_BLOCK
    ov = np.clip(np.minimum(cs[:, None] + CMP_BLOCK, ss[None] + SLC_BLOCK)
                 - np.maximum(cs[:, None], ss[None]), 0, None).astype(np.float32) / CMP_BLOCK
    ov[n_cmp:] = 0.0
    return ov.T


def kernel(x, c, w_ada, b_ada, w_in, hg_lb_logits, hg_norm_g, cmp_pos_k, cmp_w1_k, cmp_w2_k,
           cmp_pos_v, cmp_w1_v, cmp_w2_v, w_out, ln1_g, ln1_b, ffn_w1, ffn_w3, ffn_w2, ln2_g, ln2_b):
    bsz, t, d = x.shape
    depth = w_ada.shape[0]
    g, hg, dh = NSA_GROUPS, NSA_HG, NSA_DH
    hgw = HG_HEADS * HG_D
    nsw = NSA_HEADS * dh
    kvw = g * dh
    n_mod = 6
    alpha = (2.0 * depth) ** 0.25
    scale = dh ** -0.5
    n_grp = t // CMP_STRIDE
    n_cmp = (t - CMP_BLOCK) // CMP_STRIDE + 1
    n_slc = t // SLC_BLOCK

    slopes = (2.0 ** (-8.0 * (np.arange(NSA_HEADS) + 1) / NSA_HEADS)).reshape(g, hg)
    slope_tab = jnp.asarray(_slope_table(slopes), BF16)
    ov_t = jnp.asarray(_overlap_t(n_grp, n_cmp, n_slc), BF16)

    c_pad = jnp.pad(c, ((0, 8 - bsz), (0, 0)))
    for l in range(depth):
        ada = _ada(c_pad, w_ada[l], b_ada[l][None])
        ada3 = ada[:bsz].reshape(bsz * n_mod, 1, d)

        w = w_in[l]
        o = 4 * hgw
        w_q = w[:, o:o + nsw]
        w_kv = [w[:, o + nsw + i * kvw: o + nsw + (i + 1) * kvw] for i in range(6)]
        w_g = w[:, o + nsw + 6 * kvw: o + nsw + 6 * kvw + 3 * NSA_HEADS]
        w_main = jnp.concatenate(
            [w[:, :o], w_q * (scale * LOG2E), _pad_slots(w_kv[2], g),
             _pad_slots(w_kv[4], g), w_kv[0], w_kv[1]], axis=1).astype(BF16)
        w_gt = jnp.pad(w_g.reshape(d, g, 3 * hg), ((0, 0), (0, 0), (0, GATE_ROWS - 3 * hg))).reshape(d, -1)
        v_slots = lambda a: jnp.pad(a.reshape(d, g, dh), ((0, 0), (0, 0), (0, V_SLOT - dh))).reshape(d, -1)
        w_t = jnp.concatenate([v_slots(w_kv[3]), v_slots(w_kv[5]), w_gt], axis=1).T.astype(BF16)
        widths = (o, nsw, g * LANES, kvw, 2 * g * V_SLOT, g * GATE_ROWS)
        hg_in, q_slots, ks_aug, kw_aug, kc, vc, v_t, gate_t = _inproj(x, ada3, w_main, w_t, n_mod, widths)

        y_hg = _hgrn(hg_in, hg_lb_logits, hg_norm_g[l][None], l)

        grp = lambda a: a.reshape(bsz, n_grp, CMP_STRIDE * kvw)
        pos_e = lambda p: jnp.broadcast_to(
            p.reshape(2, CMP_STRIDE, 1, dh), (2, CMP_STRIDE, g, dh)).reshape(1, -1)
        w2k = jnp.pad(cmp_w2_k[l], ((0, 0), (0, LANES - dh))).astype(BF16)
        kc_aug = _compress(grp(kc), pos_e(cmp_pos_k[l]), _cmp_w1_expand(cmp_w1_k[l], g).astype(BF16),
                           w2k, True)
        vc_t = _compress(grp(vc), pos_e(cmp_pos_v[l]), _cmp_w1_expand(cmp_w1_v[l], g).astype(BF16),
                         jnp.pad(cmp_w2_v[l].T, ((0, V_SLOT - dh), (0, 0))).astype(BF16), False)

        y_nsa = _nsa(q_slots, gate_t, kc_aug, vc_t, ks_aug, kw_aug, v_t, ov_t, slope_tab)

        w_o = w_out[l].astype(BF16)
        x = _post(x, y_hg, y_nsa, ada3.reshape(bsz, n_mod, 1, d), w_o[:hgw], w_o[hgw:],
                  ffn_w1[l].astype(BF16), ffn_w3[l].astype(BF16), ffn_w2[l].astype(BF16),
                  ln1_g[l][None], ln1_b[l][None], ln2_g[l][None], ln2_b[l][None], alpha)
    return x
```

```python
import functools

import numpy as np
import jax
import jax.numpy as jnp
from jax import lax
from jax.experimental import pallas as pl
from jax.experimental.pallas import tpu as pltpu

F32 = jnp.float32
BF16 = jnp.bfloat16

HG_HEADS = 4
HG_D = 128
NSA_HEADS = 8
NSA_GROUPS = 2
NSA_HG = NSA_HEADS // NSA_GROUPS
NSA_DH = 64
CMP_BLOCK = 32
CMP_STRIDE = 16
CMP_HIDDEN = 256
SLC_BLOCK = 64
SLC_TOPN = 16
WINDOW = 512
Q_BLOCK = 256
NEG = -1e30
FORCE = 1e4
LN_EPS = 1e-5
RMS_EPS = 1e-6

LANES = 128
SUBLANES = 8
HG_BLK = 32
HG_SAFE_LOG2 = 100.0
POS_SPLIT = 256
GATE_ROWS = 16
SEL_TILE = 512
SEL_UNROLL = 8
RANK_SPLITS = 8
V_SLOT = 80
N_SPLIT = 3
LOG2E = 1.4426950408889634
MXU_DIM = 256
VMEM_LIMIT = 48 * 1024 * 1024
FFN_VMEM_LIMIT = 56 * 1024 * 1024


def _sigmoid(x):
    return 1.0 / (1.0 + jnp.exp(-x))


def _dot(a, b):
    return jnp.dot(a, b, preferred_element_type=F32)


def _dot_nt(a, b):
    return lax.dot_general(a, b, (((1,), (1,)), ((), ())), preferred_element_type=F32)


def _pos_columns(pos, col):
    c = col - NSA_DH
    lo = pos % POS_SPLIT
    val = jnp.where(c % 2 == 0, lo, pos - lo)
    return jnp.where((c >= 0) & (c < 2 * N_SPLIT), val, 0).astype(F32)


def _split2(x):
    hi = x.astype(BF16)
    return hi, (x - hi.astype(F32)).astype(BF16)


def _ada_kernel(c_ref, w_ref, b_ref, o_ref):
    c = c_ref[...]
    s = (c * _sigmoid(c)).astype(BF16)
    o_ref[...] = _dot(s, w_ref[...].astype(BF16)) + b_ref[...]


def _ada(c_pad, w, b):
    rows, d = c_pad.shape
    n = w.shape[1]
    tn = 1024
    return pl.pallas_call(
        _ada_kernel,
        out_shape=jax.ShapeDtypeStruct((rows, n), F32),
        grid=(n // tn,),
        in_specs=[
            pl.BlockSpec((rows, d), lambda j: (0, 0)),
            pl.BlockSpec((d, tn), lambda j: (0, j)),
            pl.BlockSpec((1, tn), lambda j: (0, j)),
        ],
        out_specs=pl.BlockSpec((rows, tn), lambda j: (0, j)),
        compiler_params=pltpu.CompilerParams(
            dimension_semantics=("parallel",), vmem_limit_bytes=VMEM_LIMIT),
        name="ada",
    )(c_pad, w, b)


def _inproj_kernel(x_ref, sc_ref, sh_ref, wm_ref, wt_ref,
                   hg_ref, q_ref, ks_ref, kw_ref, kc_ref, vc_ref, vt_ref, gt_ref, *, tm):
    h = (x_ref[0] * (1.0 + sc_ref[0]) + sh_ref[0]).astype(BF16)
    y = _dot(h, wm_ref[...])
    n_hg = hg_ref.shape[2]
    n_q = q_ref.shape[2]
    n_k = ks_ref.shape[2]
    n_c = kc_ref.shape[2]
    o = 0
    hg_ref[0] = y[:, o:o + n_hg]
    o += n_hg
    q_ref[0] = y[:, o:o + n_q].astype(BF16)
    o += n_q
    row = pl.program_id(1) * tm + lax.broadcasted_iota(jnp.int32, (tm, n_k), 0)
    pc = _pos_columns(row, lax.broadcasted_iota(jnp.int32, (tm, n_k), 1) % LANES)
    ks_ref[0] = (y[:, o:o + n_k] + pc).astype(BF16)
    o += n_k
    kw_ref[0] = (y[:, o:o + n_k] + pc).astype(BF16)
    o += n_k
    kc_ref[0] = y[:, o:o + n_c]
    o += n_c
    vc_ref[0] = y[:, o:o + n_c]
    yt = _dot_nt(wt_ref[...], h)
    n_vt = vt_ref.shape[1]
    ri = lax.broadcasted_iota(jnp.int32, (n_vt, tm), 0) % V_SLOT
    vt_ref[0] = (yt[:n_vt] + jnp.where(ri == NSA_DH, 1.0, 0.0)).astype(BF16)
    gt_ref[0] = yt[n_vt:]


def _inproj(x, ada3, w_main, w_t, n_mod, widths):
    b, t, d = x.shape
    n_hg, n_q, n_k, n_c, n_vt, n_gt = widths
    tm = 512
    row_spec = lambda n: pl.BlockSpec((1, tm, n), lambda i, j: (i, j, 0))
    col_spec = lambda n: pl.BlockSpec((1, n, tm), lambda i, j: (i, 0, j))
    sds = jax.ShapeDtypeStruct
    return pl.pallas_call(
        functools.partial(_inproj_kernel, tm=tm),
        out_shape=(sds((b, t, n_hg), F32), sds((b, t, n_q), BF16), sds((b, t, n_k), BF16),
                   sds((b, t, n_k), BF16), sds((b, t, n_c), F32), sds((b, t, n_c), F32),
                   sds((b, n_vt, t), BF16), sds((b, n_gt, t), F32)),
        grid=(b, t // tm),
        in_specs=[
            pl.BlockSpec((1, tm, d), lambda i, j: (i, j, 0)),
            pl.BlockSpec((1, 1, d), lambda i, j: (i * n_mod + 1, 0, 0)),
            pl.BlockSpec((1, 1, d), lambda i, j: (i * n_mod, 0, 0)),
            pl.BlockSpec(w_main.shape, lambda i, j: (0, 0), pipeline_mode=pl.Buffered(1)),
            pl.BlockSpec(w_t.shape, lambda i, j: (0, 0), pipeline_mode=pl.Buffered(1)),
        ],
        out_specs=(row_spec(n_hg), row_spec(n_q), row_spec(n_k), row_spec(n_k), row_spec(n_c),
                   row_spec(n_c), col_spec(n_vt), col_spec(n_gt)),
        compiler_params=pltpu.CompilerParams(
            dimension_semantics=("parallel", "parallel"), vmem_limit_bytes=VMEM_LIMIT),
        name="inproj",
    )(x, ada3, ada3, w_main, w_t)


def _hgrn_kernel(q_ref, f_ref, i_ref, g_ref, lbl_ref, ng_ref, e_ref, o_ref, st_ref, *, tc, layer):
    blk = HG_BLK
    per_grp = LANES // blk
    half = blk // 2

    @pl.when(pl.program_id(2) == 0)
    def _():
        st_ref[...] = jnp.zeros_like(st_ref)

    lbl = lbl_ref[...]
    e = jnp.exp(lbl - jnp.max(lbl, axis=0, keepdims=True))
    lb = jnp.sum(e[: layer + 1], axis=0, keepdims=True) / jnp.sum(e, axis=0, keepdims=True)

    r_i = lax.broadcasted_iota(jnp.int32, (LANES, LANES), 0)
    c_i = lax.broadcasted_iota(jnp.int32, (LANES, LANES), 1)
    same_blk = (r_i // blk) == (c_i // blk)
    causal_blk = same_blk & (c_i <= r_i)
    tri = jnp.where(causal_blk, 1.0, 0.0).astype(BF16)
    row_blk = r_i // blk
    nh = per_grp * half
    r_h = lax.broadcasted_iota(jnp.int32, (nh, LANES), 0)
    same_half = (r_h // half) == (lax.broadcasted_iota(jnp.int32, (nh, LANES), 1) // blk)
    row_in_half = r_h % half
    blocked = lambda a: a.reshape(per_grp, blk, LANES)

    def gates(g):
        sl = slice(g * LANES, (g + 1) * LANES)
        q = q_ref[0, sl, :]
        qs = q * _sigmoid(q)
        f = lb + (1.0 - lb) * _sigmoid(f_ref[0, sl, :])
        hi, lo = _split2(jnp.log2(f))
        b = _dot(tri, hi) + _dot(tri, lo)
        return qs, 1.0 - f, i_ref[0, sl, :], b

    def increments(qs, k, v, b):
        b3 = blocked(b)
        bl3 = b3[:, blk - 1:blk, :]
        qd = (qs * jnp.exp2(b)).astype(BF16)
        kd = (blocked(k) * jnp.exp2(bl3 - b3)).reshape(LANES, LANES)
        kexp = jnp.concatenate(
            [jnp.where(row_blk == j, kd, 0.0) for j in range(per_grp)], axis=1).astype(BF16)
        return qd, jnp.exp2(bl3), _dot(v.T.astype(BF16), kexp)

    def recur(st, qd, dl, ut):
        ois = []
        for j in range(per_grp):
            ois.append(_dot_nt(qd[j * blk:(j + 1) * blk], st.astype(BF16)))
            st = dl[j] * st + ut[:, j * LANES:(j + 1) * LANES]
        return st, jnp.concatenate(ois, axis=0)

    def pairwise(qs, k, b):
        c = b - jnp.log2(k)
        tiles = lambda a, off: jnp.concatenate(
            [a[j * blk + off:j * blk + off + half] for j in range(per_grp)], axis=0)
        q_lo, q_hi, b_lo, b_hi = tiles(qs, 0), tiles(qs, half), tiles(b, 0), tiles(b, half)
        p_lo, p_hi = [], []
        for s in range(blk):
            cs = jnp.concatenate([jnp.broadcast_to(c[j * blk + s:j * blk + s + 1], (half, LANES))
                                  for j in range(per_grp)], axis=0)
            if s < half:
                p_lo.append(jnp.where(row_in_half >= s, q_lo * jnp.exp2(b_lo - cs), 0.0).astype(BF16))
                p_hi.append((q_hi * jnp.exp2(b_hi - cs)).astype(BF16))
            else:
                p_hi.append(jnp.where(row_in_half >= s - half, q_hi * jnp.exp2(b_hi - cs), 0.0).astype(BF16))
        return jnp.concatenate(p_lo, axis=1), jnp.concatenate(p_hi, axis=1)

    def emit(g, o):
        o = o * lax.rsqrt(jnp.mean(o * o, axis=-1, keepdims=True) + RMS_EPS)
        sl = slice(g * LANES, (g + 1) * LANES)
        o_ref[0, sl, :] = (o * ng_ref[...] * _sigmoid(g_ref[0, sl, :])).astype(o_ref.dtype)

    ngrp = tc // LANES
    gs = [gates(g) for g in range(ngrp)]
    incs = [increments(*gt) for gt in gs]

    a_s = [_dot_nt(qd, (k * jnp.exp2(-b)).astype(BF16))
           for (_, k, _, b), (qd, _, _) in zip(gs, incs)]
    o_s = [_dot(jnp.where(causal_blk, a, 0.0).astype(BF16), gt[2].astype(BF16)) for a, gt in zip(a_s, gs)]

    st = st_ref[...]
    o_inters = []
    for g, inc in enumerate(incs):
        st, oi = recur(st, *inc)
        o_inters.append(oi)
        emit(g, o_s[g] + oi)
    st_ref[...] = st

    b_min = functools.reduce(jnp.minimum, [jnp.min(gt[3]) for gt in gs])

    @pl.when(b_min < -HG_SAFE_LOG2)
    def _():
        pws = [pairwise(qs, k, b) for qs, k, _, b in gs]
        a_lo = _dot(jnp.concatenate([p[0] for p in pws], axis=0), e_ref[:half * LANES, :])
        a_hi = _dot(jnp.concatenate([p[1] for p in pws], axis=0), e_ref[...])
        for g in range(ngrp):
            rows = slice(g * nh, (g + 1) * nh)
            vb = gs[g][2].astype(BF16)
            o_lo = _dot(jnp.where(same_half, a_lo[rows], 0.0).astype(BF16), vb)
            o_hi = _dot(jnp.where(same_half, a_hi[rows], 0.0).astype(BF16), vb)
            emit(g, jnp.concatenate([x[j * half:(j + 1) * half] for j in range(per_grp) for x in (o_lo, o_hi)],
                                    axis=0) + o_inters[g])


def _hgrn(proj, lb_logits, norm_g, layer):
    b, t, _ = proj.shape
    tc = 1024
    h = HG_HEADS
    nslots = lb_logits.shape[0]
    col = lambda seg: (lambda i, j, s: (i, s, seg * h + j))
    fold = np.zeros((HG_BLK, LANES, LANES), np.float32)
    for s in range(HG_BLK):
        fold[s, :, s::HG_BLK] = 1.0
    fold = jnp.asarray(fold.reshape(HG_BLK * LANES, LANES), BF16)
    return pl.pallas_call(
        functools.partial(_hgrn_kernel, tc=tc, layer=layer),
        out_shape=jax.ShapeDtypeStruct((b, t, h * HG_D), BF16),
        grid=(b, h, t // tc),
        in_specs=[
            pl.BlockSpec((1, tc, HG_D), col(0)),
            pl.BlockSpec((1, tc, HG_D), col(1)),
            pl.BlockSpec((1, tc, HG_D), col(2)),
            pl.BlockSpec((1, tc, HG_D), col(3)),
            pl.BlockSpec((nslots, HG_D), lambda i, j, s: (0, j)),
            pl.BlockSpec((1, HG_D), lambda i, j, s: (0, j)),
            pl.BlockSpec(fold.shape, lambda i, j, s: (0, 0)),
        ],
        out_specs=pl.BlockSpec((1, tc, HG_D), lambda i, j, s: (i, s, j)),
        scratch_shapes=[pltpu.VMEM((HG_D, HG_D), F32)],
        compiler_params=pltpu.CompilerParams(
            dimension_semantics=("parallel", "parallel", "arbitrary"), vmem_limit_bytes=VMEM_LIMIT),
        name="hgrn",
    )(proj, proj, proj, proj, lb_logits, norm_g, fold)


def _cmp_kernel(hk_ref, hv_ref, pk_ref, pv_ref, w1k_ref, w1v_ref, w2k_ref, w2v_ref, ok_ref, ov_ref):
    ng, half = hk_ref.shape[1], hk_ref.shape[2]

    def first_layer(h_ref, pos_ref, w1_ref):
        h = h_ref[0]
        pos = pos_ref[...]
        return (_dot((h + pos[:, :half]).astype(BF16), w1_ref[0, :half]),
                _dot((h + pos[:, half:]).astype(BF16), w1_ref[0, half:]))

    def hidden(p0, p1):
        pre = p0 + pltpu.roll(p1, ng - 1, 0)
        gl = 0.5 * pre * (1.0 + jnp.tanh(np.sqrt(2.0 / np.pi) * (pre + 0.044715 * (pre * pre * pre))))
        return gl.astype(BF16)

    pk = first_layer(hk_ref, pk_ref, w1k_ref)
    pv = first_layer(hv_ref, pv_ref, w1v_ref)
    cend = lax.broadcasted_iota(jnp.int32, (ng, LANES), 0) * CMP_STRIDE + (CMP_BLOCK - 1)
    pc = _pos_columns(cend, lax.broadcasted_iota(jnp.int32, (ng, LANES), 1))
    ok_ref[0, 0] = (_dot(hidden(*pk), w2k_ref[...]) + pc).astype(BF16)
    vt = _dot_nt(w2v_ref[...], hidden(*pv))
    ones_row = lax.broadcasted_iota(jnp.int32, vt.shape, 0) == NSA_DH
    ov_ref[0, 0] = jnp.where(ones_row, 1.0, vt).astype(BF16)


def _compress(hk, hv, pos_k, pos_v, w1k, w1v, w2k, w2v):
    b, ng, half = hk.shape
    g = w1k.shape[0]
    rows = pl.BlockSpec((1, ng, half), lambda i, j: (i, 0, 0))
    whole = lambda a: pl.BlockSpec(a.shape, lambda i, j: (0,) * a.ndim)
    per_g = lambda a: pl.BlockSpec((1,) + a.shape[1:], lambda i, j: (j, 0, 0))
    vs = w2v.shape[0]
    return pl.pallas_call(
        _cmp_kernel,
        out_shape=(jax.ShapeDtypeStruct((b, g, ng, LANES), BF16), jax.ShapeDtypeStruct((b, g, vs, ng), BF16)),
        grid=(b, g),
        in_specs=[rows, rows, whole(pos_k), whole(pos_v), per_g(w1k), per_g(w1v), whole(w2k), whole(w2v)],
        out_specs=(pl.BlockSpec((1, 1, ng, LANES), lambda i, j: (i, j, 0, 0)),
                   pl.BlockSpec((1, 1, vs, ng), lambda i, j: (i, j, 0, 0))),
        compiler_params=pltpu.CompilerParams(
            dimension_semantics=("parallel", "parallel"), vmem_limit_bytes=VMEM_LIMIT),
        name="cmp",
    )(hk, hv, pos_k, pos_v, w1k, w1v, w2k, w2v)


def _nsa_kernel(q_ref, gate_ref, kc_ref, vct_ref, ks_ref, vst_ref, kw_ref, vwt_ref, ovt_ref, slope_ref,
                o_ref, bias_ref, *z_scratch):
    qb = pl.program_id(2)
    t0 = qb * Q_BLOCK
    npair = NSA_HG // 2
    za_refs, zb_refs = z_scratch[:npair], z_scratch[npair:]
    nq = 2 * Q_BLOCK
    qblk = q_ref[0].astype(F32)
    low = lax.broadcasted_iota(jnp.int32, (Q_BLOCK, LANES), 1) < NSA_DH
    qas = []
    for hp in range(npair):
        pair = qblk[:, hp * LANES:(hp + 1) * LANES]
        heads = [jnp.where(low, x, 0.0) for x in (pair, pltpu.roll(pair, NSA_DH, 1))]
        qas.append(jnp.concatenate(heads, axis=0).astype(BF16) + slope_ref[0, hp * nq:(hp + 1) * nq])
    tpos = t0 + (lax.broadcasted_iota(jnp.int32, (1, nq), 1) % Q_BLOCK)

    def normalised(acc):
        return acc[:NSA_DH] * (1.0 / acc[NSA_DH:NSA_DH + 1])

    per_step = SEL_TILE // SLC_BLOCK
    spos0 = lax.broadcasted_iota(jnp.int32, (SEL_TILE, 1), 0)
    n_slab = WINDOW // Q_BLOCK + 1
    slab_kt = [qb - (n_slab - 1) + i for i in range(n_slab)]
    slab_at = [pl.multiple_of(jnp.maximum(kt, 0) * Q_BLOCK, Q_BLOCK) for kt in slab_kt]
    kw = jnp.concatenate([kw_ref[0, pl.ds(s, Q_BLOCK), :] for s in slab_at], axis=0)

    def sel_scores(st, z_refs):
        ks = ks_ref[0, pl.ds(pl.multiple_of(st * SEL_TILE, SEL_TILE), SEL_TILE), :]
        for z_ref, qa in zip(z_refs, qas):
            z_ref[...] = _dot_nt(ks, qa)

    zcs = [_dot_nt(kc_ref[0, 0], qa) for qa in qas]
    zws = [_dot_nt(kw, qa) for qa in qas]

    ncp = kc_ref.shape[2]
    cend = lax.broadcasted_iota(jnp.int32, (ncp, 1), 0) * CMP_STRIDE + (CMP_BLOCK - 1)
    valid = cend <= tpos
    o_c, imp = [], None
    for zc in zcs:
        zm = jnp.where(valid, zc, NEG)
        mc = jnp.max(zm, axis=0, keepdims=True)
        eb = jnp.exp2(zm - mc).astype(BF16)
        acc = _dot(vct_ref[0, 0], eb)
        inv = jnp.where(mc > 0.5 * NEG, 1.0 / acc[NSA_DH:NSA_DH + 1], 0.0)
        o_c.append(acc[:NSA_DH] * inv)
        for i in range(2):
            sl = slice(i * Q_BLOCK, (i + 1) * Q_BLOCK)
            ih = _dot(ovt_ref[...], eb[:, sl]) * inv[:, sl]
            imp = ih if imp is None else imp + ih

    sel_scores(0, za_refs)

    ns = imp.shape[0]
    j_i = lax.broadcasted_iota(jnp.int32, (ns, 1), 0)
    t_abs = t0 + lax.broadcasted_iota(jnp.int32, (1, Q_BLOCK), 1)
    cur = t_abs // SLC_BLOCK
    forced = (j_i == 0) | (j_i == cur) | (j_i == cur - 1)
    x = jnp.where(forced, FORCE, imp)
    x = jnp.where(j_i * SLC_BLOCK <= t_abs, x, NEG)

    vw = jnp.concatenate([vwt_ref[0, :, pl.ds(s, Q_BLOCK)] for s in slab_at], axis=1)
    r_k = lax.broadcasted_iota(jnp.int32, (Q_BLOCK, nq), 0)
    r_q = lax.broadcasted_iota(jnp.int32, (Q_BLOCK, nq), 1) % Q_BLOCK
    o_w = []
    for zw in zws:
        slabs = []
        for i, kt in enumerate(slab_kt):
            zi = zw[i * Q_BLOCK:(i + 1) * Q_BLOCK]
            if i == 0:
                zi = jnp.where(r_k > r_q, zi, NEG)
            if i == n_slab - 1:
                zi = jnp.where(r_k <= r_q, zi, NEG)
            else:
                zi = zi + jnp.where(kt < 0, NEG, 0.0)
            slabs.append(zi)
        zw = jnp.concatenate(slabs, axis=0)
        pw = jnp.exp2(zw - jnp.max(zw, axis=0, keepdims=True))
        o_w.append(normalised(_dot(vw, pw.astype(BF16))))

    def rank_rows(n_rows):
        sub = lax.broadcasted_iota(jnp.int32, (SUBLANES, Q_BLOCK), 0)
        tiles = [x[v * SUBLANES:(v + 1) * SUBLANES] for v in range(n_rows // SUBLANES)]
        cnts = [jnp.zeros((SUBLANES, Q_BLOCK), F32) for _ in tiles]
        for jp in range(n_rows):
            r = x[jp:jp + 1, :]
            vj, rj = divmod(jp, SUBLANES)
            for v, xt in enumerate(tiles):
                if v < vj:
                    ahead = r > xt
                elif v > vj:
                    ahead = r >= xt
                else:
                    ahead = (r > xt) | ((r == xt) & (sub > rj))
                cnts[v] = cnts[v] + jnp.where(ahead, 1.0, 0.0)
        for v, cnt in enumerate(cnts):
            bias_ref[v * SUBLANES:(v + 1) * SUBLANES, :] = jnp.where(cnt < float(SLC_TOPN), 0.0, NEG)
        if n_rows < ns:
            bias_ref[n_rows:, :] = jnp.zeros((ns - n_rows, Q_BLOCK), F32)

    n_causal = (t0 + Q_BLOCK) // SLC_BLOCK
    sizes = [ns * (i + 1) // RANK_SPLITS for i in range(RANK_SPLITS)]
    lax.switch((n_causal * RANK_SPLITS - 1) // ns, [functools.partial(rank_rows, n) for n in sizes])

    def sel_update(st, z_refs, carry, causal):
        s0 = pl.multiple_of(st * SEL_TILE, SEL_TILE)
        vs = vst_ref[0, :, pl.ds(s0, SEL_TILE)]
        brow = bias_ref[pl.ds(pl.multiple_of(st * per_step, per_step), per_step), :]
        b2 = jnp.concatenate([brow] * 2, axis=1)[:, None, :]
        out = []
        for z_ref, (m, acc) in zip(z_refs, carry):
            z = (z_ref[...].reshape(per_step, SLC_BLOCK, nq) + b2).reshape(SEL_TILE, nq)
            if causal:
                z = jnp.where((s0 + spos0) <= tpos, z, NEG)
            mn = jnp.maximum(m, jnp.max(z, axis=0, keepdims=True))
            p = jnp.exp2(z - mn)
            out.append((mn, jnp.exp2(m - mn) * acc + _dot(vs, p.astype(BF16))))
        return tuple(out)

    def run_steps(first, count, carry, ends_causal):
        bufs = (za_refs, zb_refs)
        for i in range(count):
            if i + 1 < count or not ends_causal:
                sel_scores(first + i + 1, bufs[(i + 1) % 2])
            carry = sel_update(first + i, bufs[i % 2], carry, ends_causal and i + 1 == count)
        return carry

    init = tuple((jnp.full((1, nq), NEG, F32), jnp.zeros((V_SLOT, nq), F32)) for _ in range(npair))
    last = qb // (SEL_TILE // Q_BLOCK)
    n_loop = last // SEL_UNROLL
    carry = lax.fori_loop(0, n_loop, lambda i, c: run_steps(SEL_UNROLL * i, SEL_UNROLL, c, False), init)
    tails = [functools.partial(run_steps, SEL_UNROLL * n_loop, r + 1, ends_causal=True)
             for r in range(SEL_UNROLL)]
    carry = lax.switch(last % SEL_UNROLL, tails, carry)
    o_s = [normalised(acc) for _, acc in carry]

    gts = _sigmoid(gate_ref[0])
    pairs = []
    for hp in range(npair):
        halves = []
        for i in range(2):
            r = 3 * (2 * hp + i)
            sl = slice(i * Q_BLOCK, (i + 1) * Q_BLOCK)
            halves.append(gts[r:r + 1] * o_c[hp][:, sl] + gts[r + 1:r + 2] * o_s[hp][:, sl]
                          + gts[r + 2:r + 3] * o_w[hp][:, sl])
        pairs.append(jnp.concatenate(halves, axis=0).T)
    o_ref[0] = jnp.concatenate(pairs, axis=1).astype(o_ref.dtype)


def _nsa(q_slots, gate_t, kc_aug, vc_t, ks_aug, kw_aug, v_t, ov_t, slope_tab):
    b, t, _ = q_slots.shape
    g = NSA_GROUPS
    nqb = t // Q_BLOCK
    ncp = kc_aug.shape[2]
    ns = ov_t.shape[0]
    assert t % SEL_TILE == 0 and SEL_TILE % Q_BLOCK == 0 and t >= WINDOW + Q_BLOCK and WINDOW % Q_BLOCK == 0
    assert 3 * NSA_HG <= GATE_ROWS and NSA_HG % 2 == 0 and ns % (RANK_SPLITS * SUBLANES) == 0
    return pl.pallas_call(
        _nsa_kernel,
        out_shape=jax.ShapeDtypeStruct((b, t, NSA_HEADS * NSA_DH), BF16),
        grid=(b, g, nqb),
        in_specs=[
            pl.BlockSpec((1, Q_BLOCK, NSA_HG * NSA_DH), lambda i, j, s: (i, s, j)),
            pl.BlockSpec((1, GATE_ROWS, Q_BLOCK), lambda i, j, s: (i, j, s)),
            pl.BlockSpec((1, 1, ncp, LANES), lambda i, j, s: (i, j, 0, 0)),
            pl.BlockSpec((1, 1, V_SLOT, ncp), lambda i, j, s: (i, j, 0, 0)),
            pl.BlockSpec((1, t, LANES), lambda i, j, s: (i, 0, j)),
            pl.BlockSpec((1, V_SLOT, t), lambda i, j, s: (i, j, 0)),
            pl.BlockSpec((1, t, LANES), lambda i, j, s: (i, 0, j)),
            pl.BlockSpec((1, V_SLOT, t), lambda i, j, s: (i, g + j, 0)),
            pl.BlockSpec((ns, ncp), lambda i, j, s: (0, 0)),
            pl.BlockSpec((1, NSA_HG * Q_BLOCK, LANES), lambda i, j, s: (j, 0, 0)),
        ],
        out_specs=pl.BlockSpec((1, Q_BLOCK, NSA_HG * NSA_DH), lambda i, j, s: (i, s, j)),
        scratch_shapes=[pltpu.VMEM((ns, Q_BLOCK), F32)]
        + [pltpu.VMEM((SEL_TILE, 2 * Q_BLOCK), F32)] * NSA_HG,
        compiler_params=pltpu.CompilerParams(
            dimension_semantics=("parallel", "parallel", "arbitrary"), vmem_limit_bytes=VMEM_LIMIT),
        name="nsa",
    )(q_slots, gate_t, kc_aug, vc_t, ks_aug, v_t, kw_aug, v_t, ov_t, slope_tab)


def _layer_norm(z, g, b):
    mu = jnp.mean(z, axis=-1, keepdims=True)
    zc = z - mu
    var = jnp.mean(zc * zc, axis=-1, keepdims=True)
    return zc * lax.rsqrt(var + LN_EPS) * g + b


def _post_kernel(x_ref, yh_ref, yn_ref, mod_ref, wh_ref, wn_ref, w1_ref, w3_ref, w2_ref,
                 l1g_ref, l1b_ref, l2g_ref, l2b_ref, o_ref, *, alpha, chunks, row_parts):
    g1, sh2, sc2, g2 = mod_ref[0, 2], mod_ref[0, 3], mod_ref[0, 4], mod_ref[0, 5]
    tm = x_ref.shape[1]
    parts = [slice(i * tm // row_parts, (i + 1) * tm // row_parts) for i in range(row_parts)]
    mixes = [_dot(yh_ref[0, r, :], wh_ref[...]) + _dot(yn_ref[0, r, :], wn_ref[...]) for r in parts]
    x1s, ups = [], []
    for r, mix in zip(parts, mixes):
        x1 = _layer_norm(alpha * x_ref[0, r, :] + g1 * mix, l1g_ref[...], l1b_ref[...])
        h = (x1 * (1.0 + sc2) + sh2).astype(BF16)
        x1s.append(x1)
        ups.append([(_dot(h, w1_ref[:, lo:hi]), _dot(h, w3_ref[:, lo:hi])) for lo, hi in chunks])
    downs = []
    for up in ups:
        a = jnp.concatenate([(u * _sigmoid(u) * w).astype(BF16) for u, w in up], axis=1)
        downs.append(_dot(a, w2_ref[...]))
    for r, x1, down in zip(parts, x1s, downs):
        o_ref[0, r, :] = _layer_norm(alpha * x1 + g2 * down, l2g_ref[...], l2b_ref[...])


def _post(x, y_hg, y_nsa, ada4, w_h, w_n, w1, w3, w2, ln1_g, ln1_b, ln2_g, ln2_b, alpha):
    b, t, d = x.shape
    dff = w1.shape[1]
    tm = 1024
    step = 3 * MXU_DIM
    chunks = tuple((lo, min(lo + step, dff)) for lo in range(0, dff, step))
    resident = lambda a: pl.BlockSpec(a.shape, lambda i, j: (0,) * a.ndim, pipeline_mode=pl.Buffered(1))
    rows = lambda a: pl.BlockSpec((1, tm, a.shape[2]), lambda i, j: (i, j, 0))
    vec = pl.BlockSpec((1, d), lambda i, j: (0, 0))
    return pl.pallas_call(
        functools.partial(_post_kernel, alpha=alpha, chunks=chunks, row_parts=4),
        out_shape=jax.ShapeDtypeStruct((b, t, d), F32),
        grid=(b, t // tm),
        in_specs=[rows(x), rows(y_hg), rows(y_nsa),
                  pl.BlockSpec((1,) + ada4.shape[1:], lambda i, j: (i, 0, 0, 0)),
                  resident(w_h), resident(w_n), resident(w1), resident(w3), resident(w2),
                  vec, vec, vec, vec],
        out_specs=rows(x),
        compiler_params=pltpu.CompilerParams(
            dimension_semantics=("parallel", "parallel"), vmem_limit_bytes=FFN_VMEM_LIMIT),
        name="post",
    )(x, y_hg, y_nsa, ada4, w_h, w_n, w1, w3, w2, ln1_g, ln1_b, ln2_g, ln2_b)


def _slope_table(group_slopes):
    g, hg = group_slopes.shape
    rest = (group_slopes * LOG2E).astype(np.float32)
    tab = np.zeros((g, hg, Q_BLOCK, LANES), jnp.bfloat16)
    for i in range(N_SPLIT):
        piece = rest.astype(jnp.bfloat16)
        rest = rest - piece.astype(np.float32)
        tab[..., NSA_DH + 2 * i] = piece[:, :, None]
        tab[..., NSA_DH + 2 * i + 1] = piece[:, :, None]
    return tab.reshape(g, hg * Q_BLOCK, LANES)


def _pad_slots(w, n_slots):
    d = w.shape[0]
    w3 = w.reshape(d, n_slots, -1)
    return jnp.pad(w3, ((0, 0), (0, 0), (0, LANES - w3.shape[2]))).reshape(d, n_slots * LANES)


def _cmp_w1_expand(w1, g):
    hid = w1.shape[1]
    w1r = w1.reshape(2, CMP_STRIDE, 1, NSA_DH, hid)
    eye = jnp.eye(g, dtype=w1.dtype).reshape(g, 1, 1, g, 1, 1)
    return (eye * w1r[None]).reshape(g, 2 * CMP_STRIDE * g * NSA_DH, hid)


def _overlap_t(n_cmp_pad, n_cmp, n_slc):
    cs = np.arange(n_cmp_pad) * CMP_STRIDE
    ss = np.arange(n_slc) * SLC_BLOCK
    ov = np.clip(np.minimum(cs[:, None] + CMP_BLOCK, ss[None] + SLC_BLOCK)
                 - np.maximum(cs[:, None], ss[None]), 0, None).astype(np.float32) / CMP_BLOCK
    ov[n_cmp:] = 0.0
    return ov.T


def kernel(x, c, w_ada, b_ada, w_in, hg_lb_logits, hg_norm_g, cmp_pos_k, cmp_w1_k, cmp_w2_k,
           cmp_pos_v, cmp_w1_v, cmp_w2_v, w_out, ln1_g, ln1_b, ffn_w1, ffn_w3, ffn_w2, ln2_g, ln2_b):
    bsz, t, d = x.shape
    depth = w_ada.shape[0]
    g, hg, dh = NSA_GROUPS, NSA_HG, NSA_DH
    hgw = HG_HEADS * HG_D
    nsw = NSA_HEADS * dh
    kvw = g * dh
    n_mod = 6
    alpha = (2.0 * depth) ** 0.25
    scale = dh ** -0.5
    n_grp = t // CMP_STRIDE
    n_cmp = (t - CMP_BLOCK) // CMP_STRIDE + 1
    n_slc = t // SLC_BLOCK

    slopes = (2.0 ** (-8.0 * (np.arange(NSA_HEADS) + 1) / NSA_HEADS)).reshape(g, hg)
    slope_tab = jnp.asarray(_slope_table(slopes), BF16)
    ov_t = jnp.asarray(_overlap_t(n_grp, n_cmp, n_slc), BF16)

    c_pad = jnp.pad(c, ((0, 8 - bsz), (0, 0)))
    for l in range(depth):
        ada = _ada(c_pad, w_ada[l], b_ada[l][None])
        ada3 = ada[:bsz].reshape(bsz * n_mod, 1, d)

        w = w_in[l]
        o = 4 * hgw
        w_q = w[:, o:o + nsw]
        w_kv = [w[:, o + nsw + i * kvw: o + nsw + (i + 1) * kvw] for i in range(6)]
        w_g = w[:, o + nsw + 6 * kvw: o + nsw + 6 * kvw + 3 * NSA_HEADS]
        w_main = jnp.concatenate(
            [w[:, :o], w_q * (scale * LOG2E), _pad_slots(w_kv[2], g),
             _pad_slots(w_kv[4], g), w_kv[0], w_kv[1]], axis=1).astype(BF16)
        w_gt = jnp.pad(w_g.reshape(d, g, 3 * hg), ((0, 0), (0, 0), (0, GATE_ROWS - 3 * hg))).reshape(d, -1)
        v_slots = lambda a: jnp.pad(a.reshape(d, g, dh), ((0, 0), (0, 0), (0, V_SLOT - dh))).reshape(d, -1)
        w_t = jnp.concatenate([v_slots(w_kv[3]), v_slots(w_kv[5]), w_gt], axis=1).T.astype(BF16)
        widths = (o, nsw, g * LANES, kvw, 2 * g * V_SLOT, g * GATE_ROWS)
        hg_in, q_slots, ks_aug, kw_aug, kc, vc, v_t, gate_t = _inproj(x, ada3, w_main, w_t, n_mod, widths)

        y_hg = _hgrn(hg_in, hg_lb_logits, hg_norm_g[l][None], l)

        grp = lambda a: a.reshape(bsz, n_grp, CMP_STRIDE * kvw)
        pos_e = lambda p: jnp.broadcast_to(
            p.reshape(2, CMP_STRIDE, 1, dh), (2, CMP_STRIDE, g, dh)).reshape(1, -1)
        w2k = jnp.pad(cmp_w2_k[l], ((0, 0), (0, LANES - dh))).astype(BF16)
        w2v = jnp.pad(cmp_w2_v[l].T, ((0, V_SLOT - dh), (0, 0))).astype(BF16)
        kc_aug, vc_t = _compress(grp(kc), grp(vc), pos_e(cmp_pos_k[l]), pos_e(cmp_pos_v[l]),
                                 _cmp_w1_expand(cmp_w1_k[l], g).astype(BF16),
                                 _cmp_w1_expand(cmp_w1_v[l], g).astype(BF16), w2k, w2v)

        y_nsa = _nsa(q_slots, gate_t, kc_aug, vc_t, ks_aug, kw_aug, v_t, ov_t, slope_tab)

        w_o = w_out[l].astype(BF16)
        x = _post(x, y_hg, y_nsa, ada3.reshape(bsz, n_mod, 1, d), w_o[:hgw], w_o[hgw:],
                  ffn_w1[l].astype(BF16), ffn_w3[l].astype(BF16), ffn_w2[l].astype(BF16),
                  ln1_g[l][None], ln1_b[l][None], ln2_g[l][None], ln2_b[l][None], alpha)
    return x
```
